```python
import math
import jax, jax.numpy as jnp
from jax import lax
import numpy as np

D_MODEL = 2048
BATCH = 4
SEQ = 4096
DEPTH = 1

D_ATTN = D_MODEL // 2
D_LRU = D_MODEL - D_ATTN
HEAD_DIM = 64
N_HEADS = D_ATTN // HEAD_DIM
DILATED_PATTERNS = ((128, 1), (512, 4), (2048, 16))
LRU_BLOCKS = 16
LRU_BLOCK_DIM = D_LRU // LRU_BLOCKS
CONV_WIDTH = 4
RG_C = 8.0
D_PROJ = 3 * D_ATTN + 2 * D_LRU
N_GROUPS = 4
EXPERTS_PER_GROUP = 8
N_EXPERTS = N_GROUPS * EXPERTS_PER_GROUP
TOP_K = 2
D_EXPERT = 512
MOE_BLOCK = 128
ALPHA = (2.0 * DEPTH) ** 0.25
BETA = (8.0 * DEPTH) ** -0.25
LN_EPS = 1e-5
RMS_EPS = 1e-6

kernel_name = "hymba_dilated_rglru_hmoe_deepnorm"


def _layer_norm(x, g, b):
    xf = x.astype(jnp.float32)
    mu = jnp.mean(xf, axis=-1, keepdims=True)
    var = jnp.mean(jnp.square(xf - mu), axis=-1, keepdims=True)
    return ((xf - mu) * lax.rsqrt(var + LN_EPS) * g.astype(jnp.float32) + b.astype(jnp.float32)).astype(x.dtype)


def _rms_norm(x, g):
    xf = x.astype(jnp.float32)
    return xf * lax.rsqrt(jnp.mean(jnp.square(xf), axis=-1, keepdims=True) + RMS_EPS) * g.astype(jnp.float32)


def _dilated_band_attention(q, k, v, slopes, window, dilation):
    B, S, H, Dh = q.shape
    w = window // dilation
    L = S // dilation
    nb = -(-L // w)
    Lp = nb * w

    def to_sub(t):
        t = t.reshape(B, L, dilation, H, Dh).transpose(0, 2, 1, 3, 4).reshape(B * dilation, L, H, Dh)
        t = jnp.pad(t, ((0, 0), (0, Lp - L), (0, 0), (0, 0)))
        return t.reshape(B * dilation, nb, w, H, Dh)

    def with_prev(t):
        prev = jnp.pad(t[:, :-1], ((0, 0), (1, 0), (0, 0), (0, 0), (0, 0)))
        return jnp.concatenate([prev, t], axis=2)

    qb = to_sub(q)
    kk = with_prev(to_sub(k))
    vv = with_prev(to_sub(v))
    s = jnp.einsum('bnqhd,bnkhd->bnhqk', qb, kk)
    qi = jnp.arange(w)[:, None]
    ki = jnp.arange(2 * w)[None, :]
    dist = qi + w - ki
    blk = jnp.arange(nb)[:, None, None]
    valid = (dist >= 0) & (dist <= w) & (blk * w + ki - w >= 0)
    bias = -slopes[:, None, None] * (dist * dilation).astype(jnp.float32)[None]
    s = jnp.where(valid[None, :, None], s + bias[None, None], -jnp.inf)
    m = jnp.max(s, axis=-1, keepdims=True)
    p = jnp.exp(s - m)
    l = jnp.sum(p, axis=-1, keepdims=True)
    o = jnp.einsum('bnhqk,bnkhd->bnqhd', p / l, vv)
    lse = (m + jnp.log(l))[..., 0].transpose(0, 1, 3, 2)
    o = o.reshape(B, dilation, Lp, H, Dh)[:, :, :L].transpose(0, 2, 1, 3, 4).reshape(B, S, H, Dh)
    lse = lse.reshape(B, dilation, Lp, H)[:, :, :L].transpose(0, 2, 1, 3).reshape(B, S, H)
    return o, lse


def _causal_depthwise_conv(x, w, b):
    S = x.shape[1]
    xp = jnp.pad(x, ((0, 0), (CONV_WIDTH - 1, 0), (0, 0)))
    y = b
    for j in range(CONV_WIDTH):
        y = y + xp[:, j:j + S] * w[j]
    return y


def _rg_lru(x, wa, ba, wx, bx, lam):
    B, S, C = x.shape
    xb = x.reshape(B, S, LRU_BLOCKS, LRU_BLOCK_DIM)
    r = jax.nn.sigmoid(jnp.einsum('bsnd,nde->bsne', xb, wa) + ba).reshape(B, S, C)
    i = jax.nn.sigmoid(jnp.einsum('bsnd,nde->bsne', xb, wx) + bx).reshape(B, S, C)
    log_a = -RG_C * r * jax.nn.softplus(-lam)
    a = jnp.exp(log_a)
    u = jnp.sqrt(-jnp.expm1(2.0 * log_a)) * (i * x)

    def combine(c1, c2):
        a1, b1 = c1
        a2, b2 = c2
        return a1 * a2, a2 * b1 + b2

    _, h = lax.associative_scan(combine, (a, u), axis=1)
    return h


def _hybrid_mixer(x, w_in, conv_w, conv_b, lru_wa, lru_ba, lru_wx, lru_bx, lru_lambda,
                  attn_norm_g, lru_norm_g, w_out):
    B, S, _ = x.shape
    f32 = jnp.float32
    proj = jnp.einsum('bsd,de->bse', x, w_in)
    q, k, v, xr, xg = jnp.split(proj, [D_ATTN, 2 * D_ATTN, 3 * D_ATTN, 3 * D_ATTN + D_LRU], axis=-1)
    q = q.reshape(B, S, N_HEADS, HEAD_DIM).astype(f32) * (HEAD_DIM ** -0.5)
    k = k.reshape(B, S, N_HEADS, HEAD_DIM).astype(f32)
    v = v.reshape(B, S, N_HEADS, HEAD_DIM).astype(f32)
    slopes = jnp.exp2(-8.0 * jnp.arange(1, N_HEADS + 1, dtype=f32) / N_HEADS)
    outs, lses = [], []
    for window, dilation in DILATED_PATTERNS:
        o, l = _dilated_band_attention(q, k, v, slopes, window, dilation)
        outs.append(o)
        lses.append(l)
    wts = jax.nn.softmax(jnp.stack(lses), axis=0)
    attn = jnp.sum(wts[..., None] * jnp.stack(outs), axis=0).reshape(B, S, D_ATTN)
    xr = _causal_depthwise_conv(xr.astype(f32), conv_w.astype(f32), conv_b.astype(f32))
    h = _rg_lru(xr, lru_wa.astype(f32), lru_ba.astype(f32), lru_wx.astype(f32), lru_bx.astype(f32),
                lru_lambda.astype(f32))
    rec = h * jax.nn.gelu(xg.astype(f32))
    y = jnp.concatenate([_rms_norm(attn, attn_norm_g), _rms_norm(rec, lru_norm_g)], axis=-1).astype(x.dtype)
    return jnp.einsum('bse,ed->bsd', y, w_out)


def _hier_moe(x2d, grp_w, grp_b, exp_w, exp_b, w1, w3, w2):
    T, D = x2d.shape
    f32 = jnp.float32
    xf = x2d.astype(f32)
    p_grp = jax.nn.softmax(xf @ grp_w.astype(f32) + grp_b.astype(f32), axis=-1)
    g_idx = jnp.argmax(p_grp, axis=-1)
    g_gate = jnp.take_along_axis(p_grp, g_idx[:, None], axis=-1)
    e_logits = (xf @ exp_w.astype(f32) + exp_b.astype(f32)).reshape(T, N_GROUPS, EXPERTS_PER_GROUP)
    e_logits = jnp.take_along_axis(e_logits, g_idx[:, None, None], axis=1)[:, 0]
    top_v, top_i = lax.top_k(jax.nn.softmax(e_logits, axis=-1), TOP_K)
    weights = g_gate * top_v / jnp.sum(top_v, axis=-1, keepdims=True)
    e_flat = (g_idx[:, None] * EXPERTS_PER_GROUP + top_i).reshape(-1).astype(jnp.int32)
    w_flat = weights.reshape(-1).astype(x2d.dtype)
    tok_flat = jnp.repeat(jnp.arange(T, dtype=jnp.int32), TOP_K)
    order = jnp.argsort(e_flat)
    e_sorted, tok_sorted, w_sorted = e_flat[order], tok_flat[order], w_flat[order]
    counts = jnp.zeros((N_EXPERTS,), jnp.int32).at[e_flat].add(1)
    padded = (counts + MOE_BLOCK - 1) // MOE_BLOCK * MOE_BLOCK
    pad_end = jnp.cumsum(padded)
    pad_start = pad_end - padded
    start = jnp.cumsum(counts) - counts
    dest = pad_start[e_sorted] + (jnp.arange(T * TOP_K, dtype=jnp.int32) - start[e_sorted])
    P = ((T * TOP_K + MOE_BLOCK - 1) // MOE_BLOCK + N_EXPERTS) * MOE_BLOCK
    buf_tok = jnp.full((P,), T, jnp.int32).at[dest].set(tok_sorted)
    buf_w = jnp.zeros((P,), x2d.dtype).at[dest].set(w_sorted)
    n_blk = P // MOE_BLOCK
    blk_e = jnp.minimum(jnp.searchsorted(pad_end, jnp.arange(n_blk, dtype=jnp.int32) * MOE_BLOCK, side='right'),
                        N_EXPERTS - 1)
    x_pad = jnp.concatenate([x2d, jnp.zeros((1, D), x2d.dtype)], axis=0)
    xb = x_pad[buf_tok].reshape(n_blk, MOE_BLOCK, D)

    def expert_block(args):
        xblk, e = args
        return (jax.nn.silu(xblk @ w1[e]) * (xblk @ w3[e])) @ w2[e]

    yb = lax.map(expert_block, (xb, blk_e)).reshape(P, D)
    out = jnp.zeros((T + 1, D), x2d.dtype).at[buf_tok].add(yb * buf_w[:, None])
    return out[:T]


def setup_inputs(seed: int = 0) -> dict:
    key = jax.random.key(seed)
    ks = jax.random.split(key, 24)
    f32 = jnp.float32
    nrm = lambda k, shape, s: jax.random.normal(k, shape, f32) * s
    col_scale = jnp.concatenate([jnp.ones((2 * D_ATTN,), f32), jnp.full((D_ATTN,), BETA, f32),
                                 jnp.ones((2 * D_LRU,), f32)])
    a0 = jax.random.uniform(ks[8], (DEPTH, D_LRU), f32, 0.9, 0.999)
    sa = a0 ** (1.0 / RG_C)
    return {
        "x": nrm(ks[0], (BATCH, SEQ, D_MODEL), 1.0),
        "w_in": nrm(ks[1], (DEPTH, D_MODEL, D_PROJ), D_MODEL ** -0.5) * col_scale,
        "conv_w": nrm(ks[2], (DEPTH, CONV_WIDTH, D_LRU), CONV_WIDTH ** -0.5),
        "conv_b": nrm(ks[3], (DEPTH, D_LRU), 0.02),
        "lru_wa": nrm(ks[4], (DEPTH, LRU_BLOCKS, LRU_BLOCK_DIM, LRU_BLOCK_DIM), LRU_BLOCK_DIM ** -0.5),
        "lru_ba": nrm(ks[5], (DEPTH, LRU_BLOCKS, LRU_BLOCK_DIM), 0.02),
        "lru_wx": nrm(ks[6], (DEPTH, LRU_BLOCKS, LRU_BLOCK_DIM, LRU_BLOCK_DIM), LRU_BLOCK_DIM ** -0.5),
        "lru_bx": nrm(ks[7], (DEPTH, LRU_BLOCKS, LRU_BLOCK_DIM), 0.02),
        "lru_lambda": jnp.log(sa) - jnp.log1p(-sa),
        "attn_norm_g": 1.0 + nrm(ks[9], (DEPTH, D_ATTN), 0.02),
        "lru_norm_g": 1.0 + nrm(ks[10], (DEPTH, D_LRU), 0.02),
        "w_out": nrm(ks[11], (DEPTH, D_MODEL, D_MODEL), BETA * D_MODEL ** -0.5),
        "ln1_g": 1.0 + nrm(ks[12], (DEPTH, D_MODEL), 0.02),
        "ln1_b": nrm(ks[13], (DEPTH, D_MODEL), 0.02),
        "router_grp_w": nrm(ks[14], (DEPTH, D_MODEL, N_GROUPS), D_MODEL ** -0.5),
        "router_grp_b": nrm(ks[15], (DEPTH, N_GROUPS), 0.01),
        "router_exp_w": nrm(ks[16], (DEPTH, D_MODEL, N_EXPERTS), D_MODEL ** -0.5),
        "router_exp_b": nrm(ks[17], (DEPTH, N_EXPERTS), 0.01),
        "w1": nrm(ks[18], (DEPTH, N_EXPERTS, D_MODEL, D_EXPERT), D_MODEL ** -0.5),
        "w3": nrm(ks[19], (DEPTH, N_EXPERTS, D_MODEL, D_EXPERT), D_MODEL ** -0.5),
        "w2": nrm(ks[20], (DEPTH, N_EXPERTS, D_EXPERT, D_MODEL), BETA * D_EXPERT ** -0.5),
        "ln2_g": 1.0 + nrm(ks[21], (DEPTH, D_MODEL), 0.02),
        "ln2_b": nrm(ks[22], (DEPTH, D_MODEL), 0.02),
    }


def reference(x, w_in, conv_w, conv_b, lru_wa, lru_ba, lru_wx, lru_bx, lru_lambda, attn_norm_g, lru_norm_g,
              w_out, ln1_g, ln1_b, router_grp_w, router_grp_b, router_exp_w, router_exp_b, w1, w3, w2,
              ln2_g, ln2_b):
    B, S, D = x.shape
    for layer in range(DEPTH):
        mix = _hybrid_mixer(x, w_in[layer], conv_w[layer], conv_b[layer], lru_wa[layer], lru_ba[layer],
                            lru_wx[layer], lru_bx[layer], lru_lambda[layer], attn_norm_g[layer],
                            lru_norm_g[layer], w_out[layer])
        x = _layer_norm(ALPHA * x + mix, ln1_g[layer], ln1_b[layer])
        moe = _hier_moe(x.reshape(B * S, D), router_grp_w[layer], router_grp_b[layer], router_exp_w[layer],
                        router_exp_b[layer], w1[layer], w3[layer], w2[layer]).reshape(B, S, D)
        x = _layer_norm(ALPHA * x + moe, ln2_g[layer], ln2_b[layer])
    return x
```

```python
import functools

import jax
import jax.numpy as jnp
import numpy as np
from jax import lax
from jax.experimental import pallas as pl
from jax.experimental.pallas import tpu as pltpu

F32 = jnp.float32
BF16 = jnp.bfloat16

D_MODEL = 2048
D_ATTN = 1024
D_LRU = 1024
HEAD_DIM = 64
N_HEADS = 16
LANES = 128
N_PLANES = D_ATTN // LANES
DILATED_PATTERNS = ((128, 1), (512, 4), (2048, 16))
SUB_WINDOW = 128
LRU_BLOCKS = 16
LRU_BLOCK_DIM = 64
CONV_WIDTH = 4
RG_C = 8.0
N_GROUPS = 4
EXPERTS_PER_GROUP = 8
N_EXPERTS = 32
D_EXPERT = 512
ALPHA = 2.0 ** 0.25
LN_EPS = 1e-5
RMS_EPS = 1e-6
MASKED = -1e30

VMEM_LIMIT = 56 * 1024 * 1024

TM_PROJ = 256
TS_LRU = 256
TM_OUT = 256
MOE_ROWS = 256
TM_COMB = 256


def _params(*sem):
    return pltpu.CompilerParams(dimension_semantics=sem, vmem_limit_bytes=VMEM_LIMIT)


def _in_proj_kernel(x_ref, w_ref, qkv_ref, rg_ref):
    xb = x_ref[...].astype(BF16)
    for c in range(5):
        acc = jnp.dot(xb, w_ref[:, c * D_ATTN:(c + 1) * D_ATTN], preferred_element_type=F32)
        if c == 0:
            acc = acc * (HEAD_DIM ** -0.5)
        if c < 3:
            for p in range(N_PLANES):
                qkv_ref[c * N_PLANES + p] = acc[:, p * LANES:(p + 1) * LANES]
        else:
            rg_ref[c - 3] = acc


def _in_proj(x2d, w_in_b):
    T = x2d.shape[0]
    tm = TM_PROJ
    return pl.pallas_call(
        _in_proj_kernel,
        grid=(T // tm,),
        in_specs=[
            pl.BlockSpec((tm, D_MODEL), lambda i: (i, 0)),
            pl.BlockSpec((D_MODEL, 5 * D_ATTN), lambda i: (0, 0), pipeline_mode=pl.Buffered(1)),
        ],
        out_specs=[
            pl.BlockSpec((3 * N_PLANES, tm, LANES), lambda i: (0, i, 0)),
            pl.BlockSpec((2, tm, D_LRU), lambda i: (0, i, 0)),
        ],
        out_shape=[
            jax.ShapeDtypeStruct((3 * N_PLANES, T, LANES), F32),
            jax.ShapeDtypeStruct((2, T, D_LRU), F32),
        ],
        compiler_params=_params("parallel"),
        name="in_proj",
    )(x2d, w_in_b)


def _rows(start, size, stride):
    if stride == 1:
        return pl.ds(start, size)
    return pl.ds(start, size, stride=stride)


def _attn_kernel(slopes_ref, q_ref, k_ref, v_ref, nd_ref, o_ref, m_s, l_s, acc_s, *, seq):
    hp = pl.program_id(1)
    slope_a = slopes_ref[2 * hp]
    slope_b = slopes_ref[2 * hp + 1]
    w = SUB_WINDOW

    def head_a_lanes():
        return lax.broadcasted_iota(jnp.int32, (w, LANES), 1) < HEAD_DIM

    def attend(qstart, kstart, d, nk, pi):
        is_a = head_a_lanes()
        nd = nd_ref[pi] if nk == 2 * w else nd_ref[pi, :, w:]
        q = q_ref[_rows(qstart, w, d), :]
        k = k_ref[_rows(kstart, nk, d), :].astype(BF16)
        v = v_ref[_rows(kstart, nk, d), :].astype(BF16)
        zero = jnp.zeros_like(q)
        qa = jnp.where(is_a, q, zero).astype(BF16)
        qb = jnp.where(is_a, zero, q).astype(BF16)
        dn = (((1,), (1,)), ((), ()))
        outs = []
        for qh, slope in ((qa, slope_a), (qb, slope_b)):
            s = lax.dot_general(qh, k, dn, preferred_element_type=F32) + slope * nd
            m = jnp.max(s, axis=-1, keepdims=True)
            p = jnp.exp(s - m)
            l = jnp.sum(p, axis=-1, keepdims=True)
            o = jnp.dot(p.astype(BF16), v, preferred_element_type=F32)
            outs.append((m, l, o))
        (ma, la, oa), (mb, lb, ob) = outs
        m_c = jnp.where(is_a, ma, mb)
        l_c = jnp.where(is_a, la, lb)
        o_c = jnp.where(is_a, oa, ob)
        return m_c, l_c, o_c

    def merge(qstart, d, cur, first, last):
        m_c, l_c, o_c = cur
        idx = _rows(qstart, w, d)
        if first:
            m_s[idx, :] = m_c
            l_s[idx, :] = l_c
            acc_s[idx, :] = o_c
            return
        m_o = m_s[idx, :]
        m_n = jnp.maximum(m_o, m_c)
        e_o = jnp.exp(m_o - m_n)
        e_c = jnp.exp(m_c - m_n)
        l_n = l_s[idx, :] * e_o + l_c * e_c
        a_n = acc_s[idx, :] * e_o + o_c * e_c
        if last:
            o_ref[idx, :] = a_n / l_n
        else:
            m_s[idx, :] = m_n
            l_s[idx, :] = l_n
            acc_s[idx, :] = a_n

    n_pat = len(DILATED_PATTERNS)
    for pi, (window, d) in enumerate(DILATED_PATTERNS):
        assert window // d == w
        first, last = pi == 0, pi == n_pat - 1
        span = w * d
        nb = seq // span

        def head_block(r, carry, d=d, first=first, last=last, pi=pi):
            merge(r, d, attend(r, r, d, w, pi), first, last)
            return carry

        def later_block(i, carry, d=d, first=first, last=last, pi=pi, span=span):
            n = i // d + 1
            r = i % d
            qstart = n * span + r
            kstart = qstart - span
            if d == 1:
                qstart = pl.multiple_of(qstart, w)
                kstart = pl.multiple_of(kstart, w)
            merge(qstart, d, attend(qstart, kstart, d, 2 * w, pi), first, last)
            return carry

        if d == 1:
            head_block(0, 0)
        else:
            lax.fori_loop(0, d, head_block, 0)
        lax.fori_loop(0, d * (nb - 1), later_block, 0)


def _neg_distance_tables():
    w = SUB_WINDOW
    qi = np.arange(w)[:, None]
    ki = np.arange(2 * w)[None, :]
    dist = qi + w - ki
    valid = (dist >= 0) & (dist <= w)
    tabs = [np.where(valid, -(dist * d).astype(np.float32), np.float32(MASKED)) for _, d in DILATED_PATTERNS]
    return jnp.asarray(np.stack(tabs), dtype=F32)


def _attention(qkv, batch, seq):
    qkv4 = qkv.reshape(3 * N_PLANES, batch, seq, LANES)
    slopes = jnp.exp2(-8.0 * jnp.arange(1, N_HEADS + 1, dtype=F32) / N_HEADS)
    nd = _neg_distance_tables()
    plane = lambda off: pl.BlockSpec((None, None, seq, LANES), lambda b, h: (off + h, b, 0, 0))
    return pl.pallas_call(
        functools.partial(_attn_kernel, seq=seq),
        grid=(batch, N_PLANES),
        in_specs=[
            pl.BlockSpec(memory_space=pltpu.SMEM),
            plane(0), plane(N_PLANES), plane(2 * N_PLANES),
            pl.BlockSpec((len(DILATED_PATTERNS), SUB_WINDOW, 2 * SUB_WINDOW), lambda b, h: (0, 0, 0)),
        ],
        out_specs=pl.BlockSpec((None, None, seq, LANES), lambda b, h: (h, b, 0, 0)),
        scratch_shapes=[pltpu.VMEM((seq, LANES), F32)] * 3,
        out_shape=jax.ShapeDtypeStruct((N_PLANES, batch, seq, LANES), F32),
        compiler_params=_params("parallel", "parallel"),
        name="dilated_attention",
    )(slopes, qkv4, qkv4, qkv4, nd)


def _gelu_tanh(x):
    c = np.float32(np.sqrt(2.0 / np.pi))
    return 0.5 * x * (1.0 + jnp.tanh(c * (x + 0.044715 * (x * x * x))))


def _log1p(x):
    u = 1.0 + x
    return jnp.where(u == 1.0, x, jnp.log(u) * x / (u - 1.0))


def _expm1(x):
    u = jnp.exp(x)
    return jnp.where(u == 1.0, x, (u - 1.0) * x / jnp.log(u))


def _softplus(z):
    return jnp.maximum(z, 0.0) + _log1p(jnp.exp(-jnp.abs(z)))


def _lru_kernel(xr_ref, xg_ref, cw_ref, cb_ref, wbd_ref, ba_ref, bx_ref, lam_ref, g_ref, y_ref,
                xbuf, hc, a_s, b_s, h_s, *, ts):
    i = pl.program_id(1)
    pad = 8

    @pl.when(i == 0)
    def _():
        xbuf[0:pad, :] = jnp.zeros((pad, D_LRU), F32)
        hc[...] = jnp.zeros_like(hc)

    xbuf[pad:pad + ts, :] = xr_ref[...]
    y = jnp.broadcast_to(cb_ref[...], (ts, D_LRU))
    for j in range(CONV_WIDTH):
        off = pad - (CONV_WIDTH - 1) + j
        y = y + xbuf[off:off + ts, :] * cw_ref[j:j + 1, :]
    xbuf[0:pad, :] = xbuf[ts:ts + pad, :]

    yb = y.astype(BF16)
    gw = 4 * LRU_BLOCK_DIM
    r_parts, i_parts = [], []
    for j in range(D_LRU // gw):
        g = jnp.dot(yb[:, j * gw:(j + 1) * gw], wbd_ref[j], preferred_element_type=F32)
        r_parts.append(g[:, :gw])
        i_parts.append(g[:, gw:])
    r = jax.nn.sigmoid(jnp.concatenate(r_parts, axis=-1) + ba_ref[...])
    ig = jax.nn.sigmoid(jnp.concatenate(i_parts, axis=-1) + bx_ref[...])
    log_a = (-RG_C * r) * _softplus(-lam_ref[...])
    a = jnp.exp(log_a)
    u = jnp.sqrt(-_expm1(2.0 * log_a)) * (ig * y)

    rmod = lax.broadcasted_iota(jnp.int32, (ts, D_LRU), 0) & 7
    for sh in (1, 2, 4):
        a_sh = pltpu.roll(a, sh, 0)
        u_sh = pltpu.roll(u, sh, 0)
        take = rmod >= sh
        u = jnp.where(take, a * u_sh + u, u)
        a = jnp.where(take, a * a_sh, a)
    a_s[...] = a
    b_s[...] = u

    def group(gi, h):
        rows = pl.ds(pl.multiple_of(gi * 8, 8), 8)
        hg = a_s[rows, :] * h + b_s[rows, :]
        h_s[rows, :] = hg
        return jnp.broadcast_to(hg[7:8, :], (8, D_LRU))

    hc[...] = lax.fori_loop(0, ts // 8, group, hc[...])

    rec = h_s[...] * _gelu_tanh(xg_ref[...])
    ms = jnp.mean(rec * rec, axis=-1, keepdims=True)
    y_ref[...] = (rec * lax.rsqrt(ms + RMS_EPS) * g_ref[...]).astype(y_ref.dtype)


def _block_diag_gates(wa, wx):
    def bd(wm):
        wm = wm.reshape(4, 4, LRU_BLOCK_DIM, LRU_BLOCK_DIM)
        eye = jnp.eye(4, dtype=wm.dtype)
        full = jnp.einsum("gide,ij->gidje", wm, eye)
        return full.reshape(4, 4 * LRU_BLOCK_DIM, 4 * LRU_BLOCK_DIM)
    return jnp.concatenate([bd(wa), bd(wx)], axis=-1).astype(BF16)


def _recurrent(rg, conv_w, conv_b, lru_wa, lru_ba, lru_wx, lru_bx, lru_lambda, lru_norm_g, batch, seq):
    ts = TS_LRU
    rg4 = rg.reshape(2, batch, seq, D_LRU)
    wbd = _block_diag_gates(lru_wa, lru_wx)
    row = lambda a: a.reshape(1, D_LRU).astype(F32)
    vec = pl.BlockSpec((1, D_LRU), lambda b, i: (0, 0))
    return pl.pallas_call(
        functools.partial(_lru_kernel, ts=ts),
        grid=(batch, seq // ts),
        in_specs=[
            pl.BlockSpec((None, None, ts, D_LRU), lambda b, i: (0, b, i, 0)),
            pl.BlockSpec((None, None, ts, D_LRU), lambda b, i: (1, b, i, 0)),
            pl.BlockSpec((CONV_WIDTH, D_LRU), lambda b, i: (0, 0)),
            vec,
            pl.BlockSpec((4, 4 * LRU_BLOCK_DIM, 8 * LRU_BLOCK_DIM), lambda b, i: (0, 0, 0)),
            vec, vec, vec, vec,
        ],
        out_specs=pl.BlockSpec((None, ts, D_LRU), lambda b, i: (b, i, 0)),
        out_shape=jax.ShapeDtypeStruct((batch, seq, D_LRU), BF16),
        scratch_shapes=[
            pltpu.VMEM((ts + 8, D_LRU), F32),
            pltpu.VMEM((8, D_LRU), F32),
            pltpu.VMEM((ts, D_LRU), F32),
            pltpu.VMEM((ts, D_LRU), F32),
            pltpu.VMEM((ts, D_LRU), F32),
        ],
        compiler_params=_params("parallel", "arbitrary"),
        name="conv_rglru",
    )(rg4, rg4, conv_w.astype(F32), row(conv_b), wbd, row(lru_ba), row(lru_bx), row(lru_lambda),
      row(lru_norm_g))


def _layer_norm_rows(z, g, b):
    mu = jnp.mean(z, axis=-1, keepdims=True)
    zc = z - mu
    var = jnp.mean(zc * zc, axis=-1, keepdims=True)
    return zc * lax.rsqrt(var + LN_EPS) * g + b


def _split_bf16(a):
    hi = a.astype(BF16)
    lo = (a - hi.astype(F32)).astype(BF16)
    return hi, lo


def _out_router_kernel(attn_ref, yrec_ref, x_ref, ga_ref, wo_ref, g1_ref, b1_ref, wr_ref, br_ref,
                       x1_ref, ri_ref, rw_ref, *, tm):
    attn = jnp.concatenate([attn_ref[p] for p in range(N_PLANES)], axis=-1)
    ms = jnp.mean(attn * attn, axis=-1, keepdims=True)
    ya = (attn * lax.rsqrt(ms + RMS_EPS) * ga_ref[...]).astype(BF16)
    mix = jnp.dot(ya, wo_ref[0:D_ATTN, :], preferred_element_type=F32)
    mix = mix + jnp.dot(yrec_ref[...], wo_ref[D_ATTN:, :], preferred_element_type=F32)
    x1 = _layer_norm_rows(ALPHA * x_ref[...] + mix, g1_ref[...], b1_ref[...])
    x1_ref[...] = x1

    x_hi, x_lo = _split_bf16(x1)
    w_hi, w_lo = _split_bf16(wr_ref[...])
    logits = (jnp.dot(x_hi, w_hi, preferred_element_type=F32)
              + jnp.dot(x_lo, w_hi, preferred_element_type=F32)
              + jnp.dot(x_hi, w_lo, preferred_element_type=F32)) + br_ref[...]

    lane = lax.broadcasted_iota(jnp.int32, (tm, LANES), 1)
    big = jnp.int32(LANES)
    first_true = lambda c: jnp.min(jnp.where(c, lane, big), axis=-1, keepdims=True)

    in_g = lane < N_GROUPS
    gl = jnp.where(in_g, logits, MASKED)
    gmax = jnp.max(gl, axis=-1, keepdims=True)
    g_idx = first_true(gl == gmax)
    gsum = jnp.sum(jnp.where(in_g, jnp.exp(gl - gmax), 0.0), axis=-1, keepdims=True)
    g_gate = 1.0 / gsum

    lo = N_GROUPS + EXPERTS_PER_GROUP * g_idx
    in_e = (lane >= lo) & (lane < lo + EXPERTS_PER_GROUP)
    el = jnp.where(in_e, logits, MASKED)
    emax = jnp.max(el, axis=-1, keepdims=True)
    ee = jnp.where(in_e, jnp.exp(el - emax), 0.0)
    pe = ee / jnp.sum(ee, axis=-1, keepdims=True)
    cand = jnp.where(in_e, pe, -1.0)
    v1 = jnp.max(cand, axis=-1, keepdims=True)
    i1 = first_true(cand == v1)
    cand2 = jnp.where(lane == i1, -1.0, cand)
    v2 = jnp.max(cand2, axis=-1, keepdims=True)
    i2 = first_true(cand2 == v2)
    den = v1 + v2
    w1 = g_gate * v1 / den
    w2 = g_gate * v2 / den
    ri_ref[...] = jnp.where(lane == 0, i1 - N_GROUPS, jnp.where(lane == 1, i2 - N_GROUPS, 0))
    rw_ref[...] = jnp.where(lane == 0, w1, jnp.where(lane == 1, w2, 0.0))


def _out_router(attn, yrec2d, x2d, attn_norm_g, w_out_b, ln1_g, ln1_b, w_router, b_router):
    T = x2d.shape[0]
    tm = TM_OUT
    attn3 = attn.reshape(N_PLANES, T, LANES)
    const = lambda shape: pl.BlockSpec(shape, lambda i: (0,) * len(shape))
    return pl.pallas_call(
        functools.partial(_out_router_kernel, tm=tm),
        grid=(T // tm,),
        in_specs=[
            pl.BlockSpec((N_PLANES, tm, LANES), lambda i: (0, i, 0)),
            pl.BlockSpec((tm, D_LRU), lambda i: (i, 0)),
            pl.BlockSpec((tm, D_MODEL), lambda i: (i, 0)),
            const((1, D_ATTN)),
            pl.BlockSpec((D_MODEL, D_MODEL), lambda i: (0, 0), pipeline_mode=pl.Buffered(1)),
            const((1, D_MODEL)), const((1, D_MODEL)),
            const((D_MODEL, LANES)), const((1, LANES)),
        ],
        out_specs=[
            pl.BlockSpec((tm, D_MODEL), lambda i: (i, 0)),
            pl.BlockSpec((tm, LANES), lambda i: (i, 0)),
            pl.BlockSpec((tm, LANES), lambda i: (i, 0)),
        ],
        out_shape=[
            jax.ShapeDtypeStruct((T, D_MODEL), F32),
            jax.ShapeDtypeStruct((T, LANES), jnp.int32),
            jax.ShapeDtypeStruct((T, LANES), F32),
        ],
        compiler_params=_params("parallel"),
        name="out_proj_router",
    )(attn3, yrec2d, x2d, attn_norm_g.reshape(1, D_ATTN).astype(F32), w_out_b,
      ln1_g.reshape(1, D_MODEL).astype(F32), ln1_b.reshape(1, D_MODEL).astype(F32), w_router, b_router)


def _expert_kernel(tok_ref, blk_e_ref, n_used_ref, x_hbm, w1_ref, w3_ref, w2_ref, y_ref,
                   xg, sem, w1b, w3b, w2b, *, rows):
    i = pl.program_id(0)
    n_used = n_used_ref[0]
    slot = i % 2

    def row_copy(blk, slot_, j):
        t = tok_ref[blk * rows + j]
        return pltpu.make_async_copy(x_hbm.at[pl.ds(t, 1), :], xg.at[slot_, pl.ds(j, 1), :], sem.at[slot_])

    def start_gather(blk, slot_):
        def body(j, c):
            row_copy(blk, slot_, j).start()
            return c
        lax.fori_loop(0, rows, body, 0, unroll=8)

    def wait_gather(blk, slot_):
        def body(j, c):
            row_copy(blk, slot_, j).wait()
            return c
        lax.fori_loop(0, rows, body, 0, unroll=8)

    @pl.when(i == 0)
    def _():
        start_gather(0, 0)

    @pl.when(i + 1 < n_used)
    def _():
        start_gather(i + 1, 1 - slot)

    new_expert = jnp.logical_or(i == 0, blk_e_ref[i] != blk_e_ref[jnp.maximum(i - 1, 0)])

    @pl.when(jnp.logical_and(i < n_used, new_expert))
    def _():
        w1b[...] = w1_ref[...].astype(BF16)
        w3b[...] = w3_ref[...].astype(BF16)
        w2b[...] = w2_ref[...].astype(BF16)

    @pl.when(i < n_used)
    def _():
        wait_gather(i, slot)
        xb = xg[slot].astype(BF16)
        h1 = jnp.dot(xb, w1b[...], preferred_element_type=F32)
        h3 = jnp.dot(xb, w3b[...], preferred_element_type=F32)
        h = (jax.nn.silu(h1) * h3).astype(BF16)
        y_ref[...] = jnp.dot(h, w2b[...], preferred_element_type=F32)

    @pl.when(i >= n_used)
    def _():
        y_ref[...] = jnp.zeros_like(y_ref)


def _experts(buf_tok, blk_e, n_used, x1, w1, w3, w2):
    rows = MOE_ROWS
    n_blk = blk_e.shape[0]
    wspec = lambda shape: pl.BlockSpec((None,) + shape, lambda i, tok, be, nu: (be[i], 0, 0))
    grid_spec = pltpu.PrefetchScalarGridSpec(
        num_scalar_prefetch=3,
        grid=(n_blk,),
        in_specs=[
            pl.BlockSpec(memory_space=pl.ANY),
            wspec((D_MODEL, D_EXPERT)), wspec((D_MODEL, D_EXPERT)), wspec((D_EXPERT, D_MODEL)),
        ],
        out_specs=pl.BlockSpec((rows, D_MODEL), lambda i, tok, be, nu: (i, 0)),
        scratch_shapes=[
            pltpu.VMEM((2, rows, D_MODEL), F32),
            pltpu.SemaphoreType.DMA((2,)),
            pltpu.VMEM((D_MODEL, D_EXPERT), BF16),
            pltpu.VMEM((D_MODEL, D_EXPERT), BF16),
            pltpu.VMEM((D_EXPERT, D_MODEL), BF16),
        ],
    )
    return pl.pallas_call(
        functools.partial(_expert_kernel, rows=rows),
        grid_spec=grid_spec,
        out_shape=jax.ShapeDtypeStruct((n_blk * rows, D_MODEL), F32),
        compiler_params=_params("arbitrary"),
        name="moe_experts",
    )(buf_tok, blk_e, n_used, x1, w1, w3, w2)


def _combine_kernel(dest_ref, ys_hbm, x1_ref, rw_ref, g2_ref, b2_ref, o_ref, yg, sem, *, tm):
    i = pl.program_id(0)
    n = pl.num_programs(0)
    slot = i % 2

    def row_copy(blk, slot_, j):
        a = dest_ref[blk * (2 * tm) + j]
        r = (j % 2) * tm + j // 2
        return pltpu.make_async_copy(ys_hbm.at[pl.ds(a, 1), :], yg.at[slot_, pl.ds(r, 1), :], sem.at[slot_])

    def start_gather(blk, slot_):
        def body(j, c):
            row_copy(blk, slot_, j).start()
            return c
        lax.fori_loop(0, 2 * tm, body, 0, unroll=8)

    def wait_gather(blk, slot_):
        def body(j, c):
            row_copy(blk, slot_, j).wait()
            return c
        lax.fori_loop(0, 2 * tm, body, 0, unroll=8)

    @pl.when(i == 0)
    def _():
        start_gather(0, 0)

    @pl.when(i + 1 < n)
    def _():
        start_gather(i + 1, 1 - slot)

    wait_gather(i, slot)
    rw = rw_ref[...]
    moe = yg[slot, 0:tm, :] * rw[:, 0:1] + yg[slot, tm:2 * tm, :] * rw[:, 1:2]
    o_ref[...] = _layer_norm_rows(ALPHA * x1_ref[...] + moe, g2_ref[...], b2_ref[...])


def _combine(dest, ys, x1, rw, ln2_g, ln2_b):
    T = x1.shape[0]
    tm = TM_COMB
    grid_spec = pltpu.PrefetchScalarGridSpec(
        num_scalar_prefetch=1,
        grid=(T // tm,),
        in_specs=[
            pl.BlockSpec(memory_space=pl.ANY),
            pl.BlockSpec((tm, D_MODEL), lambda i, d: (i, 0)),
            pl.BlockSpec((tm, LANES), lambda i, d: (i, 0)),
            pl.BlockSpec((1, D_MODEL), lambda i, d: (0, 0)),
            pl.BlockSpec((1, D_MODEL), lambda i, d: (0, 0)),
        ],
        out_specs=pl.BlockSpec((tm, D_MODEL), lambda i, d: (i, 0)),
        scratch_shapes=[
            pltpu.VMEM((2, 2 * tm, D_MODEL), F32),
            pltpu.SemaphoreType.DMA((2,)),
        ],
    )
    return pl.pallas_call(
        functools.partial(_combine_kernel, tm=tm),
        grid_spec=grid_spec,
        out_shape=jax.ShapeDtypeStruct((T, D_MODEL), F32),
        compiler_params=_params("arbitrary"),
        name="moe_combine_ln",
    )(dest, ys, x1, rw, ln2_g.reshape(1, D_MODEL).astype(F32), ln2_b.reshape(1, D_MODEL).astype(F32))


def _dispatch_plan(e_tk, n_tokens):
    rows = MOE_ROWS
    n_assign = 2 * n_tokens
    e_flat = e_tk.reshape(-1)
    onehot = (e_flat[:, None] == jnp.arange(N_EXPERTS, dtype=jnp.int32)[None, :]).astype(jnp.int32)
    incl = jnp.cumsum(onehot, axis=0)
    counts = incl[-1]
    rank = jnp.sum((incl - onehot) * onehot, axis=1)
    padded = (counts + rows - 1) // rows * rows
    pad_end = jnp.cumsum(padded)
    pad_start = pad_end - padded
    dest = (pad_start[e_flat] + rank).astype(jnp.int32)
    n_blk = n_assign // rows + N_EXPERTS
    tok = jnp.arange(n_assign, dtype=jnp.int32) // 2
    buf_tok = jnp.zeros((n_blk * rows,), jnp.int32).at[dest].set(tok)
    blk_start = jnp.arange(n_blk, dtype=jnp.int32) * rows
    blk_e = jnp.minimum(jnp.searchsorted(pad_end, blk_start, side="right"), N_EXPERTS - 1).astype(jnp.int32)
    n_used = (pad_end[-1:] // rows).astype(jnp.int32)
    return dest, buf_tok, blk_e, n_used


def kernel(x, w_in, conv_w, conv_b, lru_wa, lru_ba, lru_wx, lru_bx, lru_lambda, attn_norm_g, lru_norm_g,
           w_out, ln1_g, ln1_b, router_grp_w, router_grp_b, router_exp_w, router_exp_b, w1, w3, w2,
           ln2_g, ln2_b):
    B, S, D = x.shape
    assert D == D_MODEL and S % DILATED_PATTERNS[-1][0] == 0 and w_in.shape[0] == 1
    T = B * S
    x2d = x.reshape(T, D)

    qkv, rg = _in_proj(x2d, w_in[0].astype(BF16))
    attn = _attention(qkv, B, S)
    yrec = _recurrent(rg, conv_w[0], conv_b[0], lru_wa[0], lru_ba[0], lru_wx[0], lru_bx[0], lru_lambda[0],
                      lru_norm_g[0], B, S)

    n_r = N_GROUPS + N_EXPERTS
    w_router = jnp.zeros((D, LANES), F32).at[:, :n_r].set(
        jnp.concatenate([router_grp_w[0], router_exp_w[0]], axis=-1).astype(F32))
    b_router = jnp.zeros((1, LANES), F32).at[0, :n_r].set(
        jnp.concatenate([router_grp_b[0], router_exp_b[0]], axis=-1).astype(F32))
    x1, ri, rw = _out_router(attn, yrec.reshape(T, D_LRU), x2d, attn_norm_g[0], w_out[0].astype(BF16),
                                  ln1_g[0], ln1_b[0], w_router, b_router)

    dest, buf_tok, blk_e, n_used = _dispatch_plan(ri[:, :2], T)
    ys = _experts(buf_tok, blk_e, n_used, x1, w1[0], w3[0], w2[0])
    out = _combine(dest, ys, x1, rw, ln2_g[0], ln2_b[0])
    return out.reshape(B, S, D)
```

```python
import functools

import jax
import jax.numpy as jnp
import numpy as np
from jax import lax
from jax.experimental import pallas as pl
from jax.experimental.pallas import tpu as pltpu

F32 = jnp.float32
BF16 = jnp.bfloat16

D_MODEL = 2048
D_ATTN = 1024
D_LRU = 1024
HEAD_DIM = 64
N_HEADS = 16
LANES = 128
N_PLANES = D_ATTN // LANES
DILATED_PATTERNS = ((128, 1), (512, 4), (2048, 16))
SUB_WINDOW = 128
LRU_BLOCKS = 16
LRU_BLOCK_DIM = 64
CONV_WIDTH = 4
RG_C = 8.0
N_GROUPS = 4
EXPERTS_PER_GROUP = 8
N_EXPERTS = 32
D_EXPERT = 512
ALPHA = 2.0 ** 0.25
LN_EPS = 1e-5
RMS_EPS = 1e-6
MASKED = -1e30

VMEM_LIMIT = 56 * 1024 * 1024

TM_PROJ = 256
TS_LRU = 256
TM_OUT = 256
MOE_ROWS = 256
TM_COMB = 256
ATTN_UNROLL = 8


def _params(*sem):
    return pltpu.CompilerParams(dimension_semantics=sem, vmem_limit_bytes=VMEM_LIMIT)


def _in_proj_kernel(x_ref, w_ref, qkv_ref, rg_ref):
    xb = x_ref[...].astype(BF16)
    for c in range(5):
        acc = jnp.dot(xb, w_ref[:, c * D_ATTN:(c + 1) * D_ATTN], preferred_element_type=F32)
        if c == 0:
            acc = acc * (HEAD_DIM ** -0.5)
        if c < 3:
            for p in range(N_PLANES):
                qkv_ref[c * N_PLANES + p] = acc[:, p * LANES:(p + 1) * LANES]
        else:
            rg_ref[c - 3] = acc


def _in_proj(x2d, w_in_b):
    T = x2d.shape[0]
    tm = TM_PROJ
    return pl.pallas_call(
        _in_proj_kernel,
        grid=(T // tm,),
        in_specs=[
            pl.BlockSpec((tm, D_MODEL), lambda i: (i, 0)),
            pl.BlockSpec((D_MODEL, 5 * D_ATTN), lambda i: (0, 0), pipeline_mode=pl.Buffered(1)),
        ],
        out_specs=[
            pl.BlockSpec((3 * N_PLANES, tm, LANES), lambda i: (0, i, 0)),
            pl.BlockSpec((2, tm, D_LRU), lambda i: (0, i, 0)),
        ],
        out_shape=[
            jax.ShapeDtypeStruct((3 * N_PLANES, T, LANES), F32),
            jax.ShapeDtypeStruct((2, T, D_LRU), F32),
        ],
        compiler_params=_params("parallel"),
        name="in_proj",
    )(x2d, w_in_b)


CLASSES = 16


def _attn_kernel(slopes_ref, q_ref, k_ref, v_ref, ndf_ref, ndh_ref, o_ref,
                 qs, ks, vs, os_, m_s, l_s, acc_s, bias_f, bias_h, *, seq):
    hp = pl.program_id(1)
    slope_a = slopes_ref[2 * hp]
    slope_b = slopes_ref[2 * hp + 1]
    w = SUB_WINDOW
    cl = seq // CLASSES
    n_pat = len(DILATED_PATTERNS)

    for r in range(CLASSES):
        rows = pl.ds(r * cl, cl)
        strided = pl.ds(r, cl, stride=CLASSES)
        qs[rows, :] = q_ref[strided, :]
        ks[rows, :] = k_ref[strided, :]
        vs[rows, :] = v_ref[strided, :]

    def gather(ref, starts, size):
        return jnp.concatenate([ref[pl.ds(s, size), :] for s in starts], axis=0)

    def scatter(ref, starts, size, val):
        for c, s in enumerate(starts):
            ref[pl.ds(s, size), :] = val[c * size:(c + 1) * size, :]

    for t in range(2 * n_pat):
        bias_f[t, 0:w, :] = slope_a * ndf_ref[t]
        bias_f[t, w:2 * w, :] = slope_b * ndf_ref[t]
    for t in range(n_pat):
        bias_h[t, 0:w, :] = slope_a * ndh_ref[t]
        bias_h[t, w:2 * w, :] = slope_b * ndh_ref[t]

    def block(c, n, d, pi, first, last, keys):
        qc = w * d // CLASSES
        aligned = lambda x: x if isinstance(x, int) else pl.multiple_of(x, 8)
        bases = [(c + d * j) * cl for j in range(CLASSES // d)]
        qstarts = [aligned(b + n * qc) for b in bases]
        if keys == "own":
            kstarts, kc, bias = qstarts, qc, bias_h[pi]
        elif keys == "prev+own":
            kstarts, kc, bias = [aligned(b + (n - 1) * qc) for b in bases], 2 * qc, bias_f[pi]
        else:
            kfirst = jnp.maximum(n - 1, 0) * qc
            kstarts, kc = [aligned(b + kfirst) for b in bases], 2 * qc
            bias = bias_f[jnp.where(n == 0, pi + n_pat, pi)]
        is_a = lax.broadcasted_iota(jnp.int32, (w, LANES), 1) < HEAD_DIM
        q = gather(qs, qstarts, qc)
        k = gather(ks, kstarts, kc).astype(BF16)
        v = gather(vs, kstarts, kc).astype(BF16)
        zero = jnp.zeros_like(q)
        q2 = jnp.concatenate([jnp.where(is_a, q, zero), jnp.where(is_a, zero, q)], axis=0).astype(BF16)
        s = lax.dot_general(q2, k, (((1,), (1,)), ((), ())), preferred_element_type=F32) + bias
        m = jnp.max(s, axis=-1, keepdims=True)
        p = jnp.exp(s - m)
        l = jnp.sum(p, axis=-1, keepdims=True)
        o = jnp.dot(p.astype(BF16), v, preferred_element_type=F32)
        m_c = jnp.where(is_a, m[:w], m[w:])
        l_c = jnp.where(is_a, l[:w], l[w:])
        o_c = jnp.where(is_a, o[:w], o[w:])
        if first:
            scatter(m_s, qstarts, qc, m_c)
            scatter(l_s, qstarts, qc, l_c)
            scatter(acc_s, qstarts, qc, o_c)
            return
        m_o = gather(m_s, qstarts, qc)
        m_n = jnp.maximum(m_o, m_c)
        e_o = jnp.exp(m_o - m_n)
        e_c = jnp.exp(m_c - m_n)
        l_n = gather(l_s, qstarts, qc) * e_o + l_c * e_c
        a_n = gather(acc_s, qstarts, qc) * e_o + o_c * e_c
        if last:
            scatter(os_, qstarts, qc, a_n / l_n)
        else:
            scatter(m_s, qstarts, qc, m_n)
            scatter(l_s, qstarts, qc, l_n)
            scatter(acc_s, qstarts, qc, a_n)

    for pi, (window, d) in enumerate(DILATED_PATTERNS):
        assert window // d == w and CLASSES % d == 0
        first, last = pi == 0, pi == n_pat - 1
        nb = seq // (w * d)

        if nb >= 4:
            def any_block(i, carry, d=d, pi=pi, first=first, last=last):
                block(i % d, i // d, d, pi, first, last, "any")
                return carry
            lax.fori_loop(0, d * nb, any_block, 0, unroll=ATTN_UNROLL)
        else:
            def class_blocks(c, carry, d=d, pi=pi, first=first, last=last, nb=nb):
                block(c, 0, d, pi, first, last, "own")
                for n in range(1, nb):
                    block(c, n, d, pi, first, last, "prev+own")
                return carry
            lax.fori_loop(0, d, class_blocks, 0, unroll=max(ATTN_UNROLL // nb, 1))

    for r in range(CLASSES):
        o_ref[pl.ds(r, cl, stride=CLASSES), :] = os_[pl.ds(r * cl, cl), :]


def _neg_distance_tables():
    w = SUB_WINDOW
    prev, nxt, own = [], [], []
    for _, d in DILATED_PATTERNS:
        n_cls = CLASSES // d
        qc = w // n_cls
        def pos(chunk_rows):
            j, l = np.divmod(np.arange(n_cls * chunk_rows), chunk_rows)
            return n_cls * l + j
        sq, sk2, sk1 = pos(qc)[:, None], pos(2 * qc)[None, :], pos(qc)[None, :]
        for out, dist in ((prev, sq + w - sk2), (nxt, sq - sk2), (own, sq - sk1)):
            valid = (dist >= 0) & (dist <= w)
            out.append(np.where(valid, -(dist * d).astype(np.float32), np.float32(MASKED)))
    return jnp.asarray(np.stack(prev + nxt), dtype=F32), jnp.asarray(np.stack(own), dtype=F32)


def _attention(qkv, batch, seq):
    qkv4 = qkv.reshape(3 * N_PLANES, batch, seq, LANES)
    slopes = jnp.exp2(-8.0 * jnp.arange(1, N_HEADS + 1, dtype=F32) / N_HEADS)
    nd_full, nd_head = _neg_distance_tables()
    n_pat = len(DILATED_PATTERNS)
    plane = lambda off: pl.BlockSpec((None, None, seq, LANES), lambda b, h: (off + h, b, 0, 0))
    return pl.pallas_call(
        functools.partial(_attn_kernel, seq=seq),
        grid=(batch, N_PLANES),
        in_specs=[
            pl.BlockSpec(memory_space=pltpu.SMEM),
            plane(0), plane(N_PLANES), plane(2 * N_PLANES),
            pl.BlockSpec((2 * n_pat, SUB_WINDOW, 2 * SUB_WINDOW), lambda b, h: (0, 0, 0)),
            pl.BlockSpec((n_pat, SUB_WINDOW, SUB_WINDOW), lambda b, h: (0, 0, 0)),
        ],
        out_specs=pl.BlockSpec((None, None, seq, LANES), lambda b, h: (h, b, 0, 0)),
        scratch_shapes=[pltpu.VMEM((seq, LANES), F32)] * 7 + [
            pltpu.VMEM((2 * n_pat, 2 * SUB_WINDOW, 2 * SUB_WINDOW), F32),
            pltpu.VMEM((n_pat, 2 * SUB_WINDOW, SUB_WINDOW), F32),
        ],
        out_shape=jax.ShapeDtypeStruct((N_PLANES, batch, seq, LANES), F32),
        compiler_params=_params("parallel", "parallel"),
        name="dilated_attention",
    )(slopes, qkv4, qkv4, qkv4, nd_full, nd_head)


def _gelu_tanh(x):
    c = np.float32(np.sqrt(2.0 / np.pi))
    return 0.5 * x * (1.0 + jnp.tanh(c * (x + 0.044715 * (x * x * x))))


def _log1p(x):
    u = 1.0 + x
    return jnp.where(u == 1.0, x, jnp.log(u) * x / (u - 1.0))


def _expm1(x):
    u = jnp.exp(x)
    return jnp.where(u == 1.0, x, (u - 1.0) * x / jnp.log(u))


def _softplus(z):
    return jnp.maximum(z, 0.0) + _log1p(jnp.exp(-jnp.abs(z)))


def _lru_kernel(xr_ref, xg_ref, cw_ref, cb_ref, wbd_ref, ba_ref, bx_ref, lam_ref, g_ref, y_ref,
                xbuf, hc, a_s, b_s, h_s, *, ts):
    i = pl.program_id(1)
    pad = 8

    @pl.when(i == 0)
    def _():
        xbuf[0:pad, :] = jnp.zeros((pad, D_LRU), F32)
        hc[...] = jnp.zeros_like(hc)

    xbuf[pad:pad + ts, :] = xr_ref[...]
    y = jnp.broadcast_to(cb_ref[...], (ts, D_LRU))
    for j in range(CONV_WIDTH):
        off = pad - (CONV_WIDTH - 1) + j
        y = y + xbuf[off:off + ts, :] * cw_ref[j:j + 1, :]
    xbuf[0:pad, :] = xbuf[ts:ts + pad, :]

    yb = y.astype(BF16)
    gw = 4 * LRU_BLOCK_DIM
    r_parts, i_parts = [], []
    for j in range(D_LRU // gw):
        g = jnp.dot(yb[:, j * gw:(j + 1) * gw], wbd_ref[j], preferred_element_type=F32)
        r_parts.append(g[:, :gw])
        i_parts.append(g[:, gw:])
    r = jax.nn.sigmoid(jnp.concatenate(r_parts, axis=-1) + ba_ref[...])
    ig = jax.nn.sigmoid(jnp.concatenate(i_parts, axis=-1) + bx_ref[...])
    log_a = (-RG_C * r) * _softplus(-lam_ref[...])
    a = jnp.exp(log_a)
    u = jnp.sqrt(-_expm1(2.0 * log_a)) * (ig * y)

    rmod = lax.broadcasted_iota(jnp.int32, (ts, D_LRU), 0) & 7
    for sh in (1, 2, 4):
        a_sh = pltpu.roll(a, sh, 0)
        u_sh = pltpu.roll(u, sh, 0)
        take = rmod >= sh
        u = jnp.where(take, a * u_sh + u, u)
        a = jnp.where(take, a * a_sh, a)
    a_s[...] = a
    b_s[...] = u

    def group(gi, h):
        rows = pl.ds(pl.multiple_of(gi * 8, 8), 8)
        hg = a_s[rows, :] * h + b_s[rows, :]
        h_s[rows, :] = hg
        return jnp.broadcast_to(hg[7:8, :], (8, D_LRU))

    hc[...] = lax.fori_loop(0, ts // 8, group, hc[...])

    rec = h_s[...] * _gelu_tanh(xg_ref[...])
    ms = jnp.mean(rec * rec, axis=-1, keepdims=True)
    y_ref[...] = (rec * lax.rsqrt(ms + RMS_EPS) * g_ref[...]).astype(y_ref.dtype)


def _block_diag_gates(wa, wx):
    def bd(wm):
        wm = wm.reshape(4, 4, LRU_BLOCK_DIM, LRU_BLOCK_DIM)
        eye = jnp.eye(4, dtype=wm.dtype)
        full = jnp.einsum("gide,ij->gidje", wm, eye)
        return full.reshape(4, 4 * LRU_BLOCK_DIM, 4 * LRU_BLOCK_DIM)
    return jnp.concatenate([bd(wa), bd(wx)], axis=-1).astype(BF16)


def _recurrent(rg, conv_w, conv_b, lru_wa, lru_ba, lru_wx, lru_bx, lru_lambda, lru_norm_g, batch, seq):
    ts = TS_LRU
    rg4 = rg.reshape(2, batch, seq, D_LRU)
    wbd = _block_diag_gates(lru_wa, lru_wx)
    row = lambda a: a.reshape(1, D_LRU).astype(F32)
    vec = pl.BlockSpec((1, D_LRU), lambda b, i: (0, 0))
    return pl.pallas_call(
        functools.partial(_lru_kernel, ts=ts),
        grid=(batch, seq // ts),
        in_specs=[
            pl.BlockSpec((None, None, ts, D_LRU), lambda b, i: (0, b, i, 0)),
            pl.BlockSpec((None, None, ts, D_LRU), lambda b, i: (1, b, i, 0)),
            pl.BlockSpec((CONV_WIDTH, D_LRU), lambda b, i: (0, 0)),
            vec,
            pl.BlockSpec((4, 4 * LRU_BLOCK_DIM, 8 * LRU_BLOCK_DIM), lambda b, i: (0, 0, 0)),
            vec, vec, vec, vec,
        ],
        out_specs=pl.BlockSpec((None, ts, D_LRU), lambda b, i: (b, i, 0)),
        out_shape=jax.ShapeDtypeStruct((batch, seq, D_LRU), BF16),
        scratch_shapes=[
            pltpu.VMEM((ts + 8, D_LRU), F32),
            pltpu.VMEM((8, D_LRU), F32),
            pltpu.VMEM((ts, D_LRU), F32),
            pltpu.VMEM((ts, D_LRU), F32),
            pltpu.VMEM((ts, D_LRU), F32),
        ],
        compiler_params=_params("parallel", "arbitrary"),
        name="conv_rglru",
    )(rg4, rg4, conv_w.astype(F32), row(conv_b), wbd, row(lru_ba), row(lru_bx), row(lru_lambda),
      row(lru_norm_g))


def _layer_norm_rows(z, g, b):
    mu = jnp.mean(z, axis=-1, keepdims=True)
    zc = z - mu
    var = jnp.mean(zc * zc, axis=-1, keepdims=True)
    return zc * lax.rsqrt(var + LN_EPS) * g + b


def _split_bf16(a):
    hi = a.astype(BF16)
    lo = (a - hi.astype(F32)).astype(BF16)
    return hi, lo


def _out_router_kernel(attn_ref, yrec_ref, x_ref, ga_ref, wo_ref, g1_ref, b1_ref, wr_ref, br_ref,
                       x1_ref, ri_ref, rw_ref, *, tm):
    attn = jnp.concatenate([attn_ref[p] for p in range(N_PLANES)], axis=-1)
    ms = jnp.mean(attn * attn, axis=-1, keepdims=True)
    ya = (attn * lax.rsqrt(ms + RMS_EPS) * ga_ref[...]).astype(BF16)
    mix = jnp.dot(ya, wo_ref[0:D_ATTN, :], preferred_element_type=F32)
    mix = mix + jnp.dot(yrec_ref[...], wo_ref[D_ATTN:, :], preferred_element_type=F32)
    x1 = _layer_norm_rows(ALPHA * x_ref[...] + mix, g1_ref[...], b1_ref[...])
    x1_ref[...] = x1

    x_hi, x_lo = _split_bf16(x1)
    w_hi, w_lo = _split_bf16(wr_ref[...])
    logits = (jnp.dot(x_hi, w_hi, preferred_element_type=F32)
              + jnp.dot(x_lo, w_hi, preferred_element_type=F32)
              + jnp.dot(x_hi, w_lo, preferred_element_type=F32)) + br_ref[...]

    lane = lax.broadcasted_iota(jnp.int32, (tm, LANES), 1)
    big = jnp.int32(LANES)
    first_true = lambda c: jnp.min(jnp.where(c, lane, big), axis=-1, keepdims=True)

    in_g = lane < N_GROUPS
    gl = jnp.where(in_g, logits, MASKED)
    gmax = jnp.max(gl, axis=-1, keepdims=True)
    g_idx = first_true(gl == gmax)
    gsum = jnp.sum(jnp.where(in_g, jnp.exp(gl - gmax), 0.0), axis=-1, keepdims=True)
    g_gate = 1.0 / gsum

    lo = N_GROUPS + EXPERTS_PER_GROUP * g_idx
    in_e = (lane >= lo) & (lane < lo + EXPERTS_PER_GROUP)
    el = jnp.where(in_e, logits, MASKED)
    emax = jnp.max(el, axis=-1, keepdims=True)
    ee = jnp.where(in_e, jnp.exp(el - emax), 0.0)
    pe = ee / jnp.sum(ee, axis=-1, keepdims=True)
    cand = jnp.where(in_e, pe, -1.0)
    v1 = jnp.max(cand, axis=-1, keepdims=True)
    i1 = first_true(cand == v1)
    cand2 = jnp.where(lane == i1, -1.0, cand)
    v2 = jnp.max(cand2, axis=-1, keepdims=True)
    i2 = first_true(cand2 == v2)
    den = v1 + v2
    w1 = g_gate * v1 / den
    w2 = g_gate * v2 / den
    ri_ref[...] = jnp.where(lane == 0, i1 - N_GROUPS, jnp.where(lane == 1, i2 - N_GROUPS, 0))
    rw_ref[...] = jnp.where(lane == 0, w1, jnp.where(lane == 1, w2, 0.0))


def _out_router(attn, yrec2d, x2d, attn_norm_g, w_out_b, ln1_g, ln1_b, w_router, b_router):
    T = x2d.shape[0]
    tm = TM_OUT
    attn3 = attn.reshape(N_PLANES, T, LANES)
    const = lambda shape: pl.BlockSpec(shape, lambda i: (0,) * len(shape))
    return pl.pallas_call(
        functools.partial(_out_router_kernel, tm=tm),
        grid=(T // tm,),
        in_specs=[
            pl.BlockSpec((N_PLANES, tm, LANES), lambda i: (0, i, 0)),
            pl.BlockSpec((tm, D_LRU), lambda i: (i, 0)),
            pl.BlockSpec((tm, D_MODEL), lambda i: (i, 0)),
            const((1, D_ATTN)),
            pl.BlockSpec((D_MODEL, D_MODEL), lambda i: (0, 0), pipeline_mode=pl.Buffered(1)),
            const((1, D_MODEL)), const((1, D_MODEL)),
            const((D_MODEL, LANES)), const((1, LANES)),
        ],
        out_specs=[
            pl.BlockSpec((tm, D_MODEL), lambda i: (i, 0)),
            pl.BlockSpec((tm, LANES), lambda i: (i, 0)),
            pl.BlockSpec((tm, LANES), lambda i: (i, 0)),
        ],
        out_shape=[
            jax.ShapeDtypeStruct((T, D_MODEL), F32),
            jax.ShapeDtypeStruct((T, LANES), jnp.int32),
            jax.ShapeDtypeStruct((T, LANES), F32),
        ],
        compiler_params=_params("parallel"),
        name="out_proj_router",
    )(attn3, yrec2d, x2d, attn_norm_g.reshape(1, D_ATTN).astype(F32), w_out_b,
      ln1_g.reshape(1, D_MODEL).astype(F32), ln1_b.reshape(1, D_MODEL).astype(F32), w_router, b_router)


def _expert_kernel(tok_ref, blk_e_ref, n_used_ref, x_hbm, w1_ref, w3_ref, w2_ref, y_ref,
                   xg, sem, w1b, w3b, w2b, *, rows):
    i = pl.program_id(0)
    n_used = n_used_ref[0]
    slot = i % 2

    def row_copy(blk, slot_, j):
        t = tok_ref[blk * rows + j]
        return pltpu.make_async_copy(x_hbm.at[pl.ds(t, 1), :], xg.at[slot_, pl.ds(j, 1), :], sem.at[slot_])

    def start_gather(blk, slot_):
        def body(j, c):
            row_copy(blk, slot_, j).start()
            return c
        lax.fori_loop(0, rows, body, 0, unroll=8)

    def wait_gather(blk, slot_):
        def body(j, c):
            row_copy(blk, slot_, j).wait()
            return c
        lax.fori_loop(0, rows, body, 0, unroll=8)

    @pl.when(i == 0)
    def _():
        start_gather(0, 0)

    @pl.when(i + 1 < n_used)
    def _():
        start_gather(i + 1, 1 - slot)

    new_expert = jnp.logical_or(i == 0, blk_e_ref[i] != blk_e_ref[jnp.maximum(i - 1, 0)])

    @pl.when(jnp.logical_and(i < n_used, new_expert))
    def _():
        w1b[...] = w1_ref[...].astype(BF16)
        w3b[...] = w3_ref[...].astype(BF16)
        w2b[...] = w2_ref[...].astype(BF16)

    @pl.when(i < n_used)
    def _():
        wait_gather(i, slot)
        xb = xg[slot].astype(BF16)
        h1 = jnp.dot(xb, w1b[...], preferred_element_type=F32)
        h3 = jnp.dot(xb, w3b[...], preferred_element_type=F32)
        h = (jax.nn.silu(h1) * h3).astype(BF16)
        y_ref[...] = jnp.dot(h, w2b[...], preferred_element_type=F32)

    @pl.when(i >= n_used)
    def _():
        y_ref[...] = jnp.zeros_like(y_ref)


def _experts(buf_tok, blk_e, n_used, x1, w1, w3, w2):
    rows = MOE_ROWS
    n_blk = blk_e.shape[0]
    wspec = lambda shape: pl.BlockSpec((None,) + shape, lambda i, tok, be, nu: (be[i], 0, 0))
    grid_spec = pltpu.PrefetchScalarGridSpec(
        num_scalar_prefetch=3,
        grid=(n_blk,),
        in_specs=[
            pl.BlockSpec(memory_space=pl.ANY),
            wspec((D_MODEL, D_EXPERT)), wspec((D_MODEL, D_EXPERT)), wspec((D_EXPERT, D_MODEL)),
        ],
        out_specs=pl.BlockSpec((rows, D_MODEL), lambda i, tok, be, nu: (i, 0)),
        scratch_shapes=[
            pltpu.VMEM((2, rows, D_MODEL), F32),
            pltpu.SemaphoreType.DMA((2,)),
            pltpu.VMEM((D_MODEL, D_EXPERT), BF16),
            pltpu.VMEM((D_MODEL, D_EXPERT), BF16),
            pltpu.VMEM((D_EXPERT, D_MODEL), BF16),
        ],
    )
    return pl.pallas_call(
        functools.partial(_expert_kernel, rows=rows),
        grid_spec=grid_spec,
        out_shape=jax.ShapeDtypeStruct((n_blk * rows, D_MODEL), F32),
        compiler_params=_params("arbitrary"),
        name="moe_experts",
    )(buf_tok, blk_e, n_used, x1, w1, w3, w2)


def _combine_kernel(dest_ref, ys_hbm, x1_ref, rw_ref, g2_ref, b2_ref, o_ref, yg, sem, *, tm):
    i = pl.program_id(0)
    n = pl.num_programs(0)
    slot = i % 2

    def row_copy(blk, slot_, j):
        a = dest_ref[blk * (2 * tm) + j]
        r = (j % 2) * tm + j // 2
        return pltpu.make_async_copy(ys_hbm.at[pl.ds(a, 1), :], yg.at[slot_, pl.ds(r, 1), :], sem.at[slot_])

    def start_gather(blk, slot_):
        def body(j, c):
            row_copy(blk, slot_, j).start()
            return c
        lax.fori_loop(0, 2 * tm, body, 0, unroll=8)

    def wait_gather(blk, slot_):
        def body(j, c):
            row_copy(blk, slot_, j).wait()
            return c
        lax.fori_loop(0, 2 * tm, body, 0, unroll=8)

    @pl.when(i == 0)
    def _():
        start_gather(0, 0)

    @pl.when(i + 1 < n)
    def _():
        start_gather(i + 1, 1 - slot)

    wait_gather(i, slot)
    rw = rw_ref[...]
    moe = yg[slot, 0:tm, :] * rw[:, 0:1] + yg[slot, tm:2 * tm, :] * rw[:, 1:2]
    o_ref[...] = _layer_norm_rows(ALPHA * x1_ref[...] + moe, g2_ref[...], b2_ref[...])


def _combine(dest, ys, x1, rw, ln2_g, ln2_b):
    T = x1.shape[0]
    tm = TM_COMB
    grid_spec = pltpu.PrefetchScalarGridSpec(
        num_scalar_prefetch=1,
        grid=(T // tm,),
        in_specs=[
            pl.BlockSpec(memory_space=pl.ANY),
            pl.BlockSpec((tm, D_MODEL), lambda i, d: (i, 0)),
            pl.BlockSpec((tm, LANES), lambda i, d: (i, 0)),
            pl.BlockSpec((1, D_MODEL), lambda i, d: (0, 0)),
            pl.BlockSpec((1, D_MODEL), lambda i, d: (0, 0)),
        ],
        out_specs=pl.BlockSpec((tm, D_MODEL), lambda i, d: (i, 0)),
        scratch_shapes=[
            pltpu.VMEM((2, 2 * tm, D_MODEL), F32),
            pltpu.SemaphoreType.DMA((2,)),
        ],
    )
    return pl.pallas_call(
        functools.partial(_combine_kernel, tm=tm),
        grid_spec=grid_spec,
        out_shape=jax.ShapeDtypeStruct((T, D_MODEL), F32),
        compiler_params=_params("arbitrary"),
        name="moe_combine_ln",
    )(dest, ys, x1, rw, ln2_g.reshape(1, D_MODEL).astype(F32), ln2_b.reshape(1, D_MODEL).astype(F32))


def _dispatch_plan(e_tk, n_tokens):
    rows = MOE_ROWS
    n_assign = 2 * n_tokens
    e_flat = e_tk.reshape(-1)
    onehot = (e_flat[:, None] == jnp.arange(N_EXPERTS, dtype=jnp.int32)[None, :]).astype(jnp.int32)
    incl = jnp.cumsum(onehot, axis=0)
    counts = incl[-1]
    rank = jnp.sum((incl - onehot) * onehot, axis=1)
    padded = (counts + rows - 1) // rows * rows
    pad_end = jnp.cumsum(padded)
    pad_start = pad_end - padded
    dest = (pad_start[e_flat] + rank).astype(jnp.int32)
    n_blk = n_assign // rows + N_EXPERTS
    tok = jnp.arange(n_assign, dtype=jnp.int32) // 2
    buf_tok = jnp.zeros((n_blk * rows,), jnp.int32).at[dest].set(tok)
    blk_start = jnp.arange(n_blk, dtype=jnp.int32) * rows
    blk_e = jnp.minimum(jnp.searchsorted(pad_end, blk_start, side="right"), N_EXPERTS - 1).astype(jnp.int32)
    n_used = (pad_end[-1:] // rows).astype(jnp.int32)
    return dest, buf_tok, blk_e, n_used


def kernel(x, w_in, conv_w, conv_b, lru_wa, lru_ba, lru_wx, lru_bx, lru_lambda, attn_norm_g, lru_norm_g,
           w_out, ln1_g, ln1_b, router_grp_w, router_grp_b, router_exp_w, router_exp_b, w1, w3, w2,
           ln2_g, ln2_b):
    B, S, D = x.shape
    assert D == D_MODEL and S % DILATED_PATTERNS[-1][0] == 0 and w_in.shape[0] == 1
    T = B * S
    x2d = x.reshape(T, D)

    qkv, rg = _in_proj(x2d, w_in[0].astype(BF16))
    attn = _attention(qkv, B, S)
    yrec = _recurrent(rg, conv_w[0], conv_b[0], lru_wa[0], lru_ba[0], lru_wx[0], lru_bx[0], lru_lambda[0],
                      lru_norm_g[0], B, S)

    n_r = N_GROUPS + N_EXPERTS
    w_router = jnp.zeros((D, LANES), F32).at[:, :n_r].set(
        jnp.concatenate([router_grp_w[0], router_exp_w[0]], axis=-1).astype(F32))
    b_router = jnp.zeros((1, LANES), F32).at[0, :n_r].set(
        jnp.concatenate([router_grp_b[0], router_exp_b[0]], axis=-1).astype(F32))
    x1, ri, rw = _out_router(attn, yrec.reshape(T, D_LRU), x2d, attn_norm_g[0], w_out[0].astype(BF16),
                                  ln1_g[0], ln1_b[0], w_router, b_router)

    dest, buf_tok, blk_e, n_used = _dispatch_plan(ri[:, :2], T)
    ys = _experts(buf_tok, blk_e, n_used, x1, w1[0], w3[0], w2[0])
    out = _combine(dest, ys, x1, rw, ln2_g[0], ln2_b[0])
    return out.reshape(B, S, D)
```

```python
import functools

import jax
import jax.numpy as jnp
import numpy as np
from jax import lax
from jax.experimental import pallas as pl
from jax.experimental.pallas import tpu as pltpu

F32 = jnp.float32
BF16 = jnp.bfloat16

D_MODEL = 2048
D_ATTN = 1024
D_LRU = 1024
HEAD_DIM = 64
N_HEADS = 16
LANES = 128
N_PLANES = D_ATTN // LANES
DILATED_PATTERNS = ((128, 1), (512, 4), (2048, 16))
SUB_WINDOW = 128
LRU_BLOCKS = 16
LRU_BLOCK_DIM = 64
CONV_WIDTH = 4
RG_C = 8.0
N_GROUPS = 4
EXPERTS_PER_GROUP = 8
N_EXPERTS = 32
D_EXPERT = 512
ALPHA = 2.0 ** 0.25
LN_EPS = 1e-5
RMS_EPS = 1e-6
MASKED = -1e30

VMEM_LIMIT = 56 * 1024 * 1024

TM_PROJ = 256
TS_LRU = 256
TM_OUT = 256
MOE_ROWS = 256
TM_COMB = 256
TM_DISP = 256
CHUNKS = D_MODEL // LANES
RANK_BITS = 16
RANK_SPAN = 1 << RANK_BITS
ATTN_UNROLL = 8


def _params(*sem):
    return pltpu.CompilerParams(dimension_semantics=sem, vmem_limit_bytes=VMEM_LIMIT)


def _in_proj_kernel(x_ref, w_ref, qkv_ref, rg_ref):
    xb = x_ref[...].astype(BF16)
    for c in range(5):
        acc = jnp.dot(xb, w_ref[:, c * D_ATTN:(c + 1) * D_ATTN], preferred_element_type=F32)
        if c == 0:
            acc = acc * (HEAD_DIM ** -0.5)
        if c < 3:
            for p in range(N_PLANES):
                qkv_ref[c * N_PLANES + p] = acc[:, p * LANES:(p + 1) * LANES]
        else:
            rg_ref[c - 3] = acc


def _in_proj(x2d, w_in_b):
    T = x2d.shape[0]
    tm = TM_PROJ
    return pl.pallas_call(
        _in_proj_kernel,
        grid=(T // tm,),
        in_specs=[
            pl.BlockSpec((tm, D_MODEL), lambda i: (i, 0)),
            pl.BlockSpec((D_MODEL, 5 * D_ATTN), lambda i: (0, 0), pipeline_mode=pl.Buffered(1)),
        ],
        out_specs=[
            pl.BlockSpec((3 * N_PLANES, tm, LANES), lambda i: (0, i, 0)),
            pl.BlockSpec((2, tm, D_LRU), lambda i: (0, i, 0)),
        ],
        out_shape=[
            jax.ShapeDtypeStruct((3 * N_PLANES, T, LANES), F32),
            jax.ShapeDtypeStruct((2, T, D_LRU), F32),
        ],
        compiler_params=_params("parallel"),
        name="in_proj",
    )(x2d, w_in_b)


CLASSES = 16


def _attn_kernel(slopes_ref, q_ref, k_ref, v_ref, ndf_ref, ndh_ref, o_ref,
                 qs, ks, vs, os_, m_s, l_s, acc_s, bias_f, bias_h, *, seq):
    hp = pl.program_id(1)
    slope_a = slopes_ref[2 * hp]
    slope_b = slopes_ref[2 * hp + 1]
    w = SUB_WINDOW
    cl = seq // CLASSES
    n_pat = len(DILATED_PATTERNS)

    for r in range(CLASSES):
        rows = pl.ds(r * cl, cl)
        strided = pl.ds(r, cl, stride=CLASSES)
        qs[rows, :] = q_ref[strided, :]
        ks[rows, :] = k_ref[strided, :]
        vs[rows, :] = v_ref[strided, :]

    def gather(ref, starts, size):
        return jnp.concatenate([ref[pl.ds(s, size), :] for s in starts], axis=0)

    def scatter(ref, starts, size, val):
        for c, s in enumerate(starts):
            ref[pl.ds(s, size), :] = val[c * size:(c + 1) * size, :]

    for t in range(2 * n_pat):
        bias_f[t, 0:w, :] = slope_a * ndf_ref[t]
        bias_f[t, w:2 * w, :] = slope_b * ndf_ref[t]
    for t in range(n_pat):
        bias_h[t, 0:w, :] = slope_a * ndh_ref[t]
        bias_h[t, w:2 * w, :] = slope_b * ndh_ref[t]

    def block(c, n, d, pi, first, last, keys):
        qc = w * d // CLASSES
        aligned = lambda x: x if isinstance(x, int) else pl.multiple_of(x, 8)
        bases = [(c + d * j) * cl for j in range(CLASSES // d)]
        qstarts = [aligned(b + n * qc) for b in bases]
        if keys == "own":
            kstarts, kc, bias = qstarts, qc, bias_h[pi]
        elif keys == "prev+own":
            kstarts, kc, bias = [aligned(b + (n - 1) * qc) for b in bases], 2 * qc, bias_f[pi]
        else:
            kfirst = jnp.maximum(n - 1, 0) * qc
            kstarts, kc = [aligned(b + kfirst) for b in bases], 2 * qc
            bias = bias_f[jnp.where(n == 0, pi + n_pat, pi)]
        is_a = lax.broadcasted_iota(jnp.int32, (w, LANES), 1) < HEAD_DIM
        q = gather(qs, qstarts, qc)
        k = gather(ks, kstarts, kc).astype(BF16)
        v = gather(vs, kstarts, kc).astype(BF16)
        zero = jnp.zeros_like(q)
        q2 = jnp.concatenate([jnp.where(is_a, q, zero), jnp.where(is_a, zero, q)], axis=0).astype(BF16)
        s = lax.dot_general(q2, k, (((1,), (1,)), ((), ())), preferred_element_type=F32) + bias
        m = jnp.max(s, axis=-1, keepdims=True)
        p = jnp.exp(s - m)
        l = jnp.sum(p, axis=-1, keepdims=True)
        o = jnp.dot(p.astype(BF16), v, preferred_element_type=F32)
        m_c = jnp.where(is_a, m[:w], m[w:])
        l_c = jnp.where(is_a, l[:w], l[w:])
        o_c = jnp.where(is_a, o[:w], o[w:])
        if first:
            scatter(m_s, qstarts, qc, m_c)
            scatter(l_s, qstarts, qc, l_c)
            scatter(acc_s, qstarts, qc, o_c)
            return
        m_o = gather(m_s, qstarts, qc)
        m_n = jnp.maximum(m_o, m_c)
        e_o = jnp.exp(m_o - m_n)
        e_c = jnp.exp(m_c - m_n)
        l_n = gather(l_s, qstarts, qc) * e_o + l_c * e_c
        a_n = gather(acc_s, qstarts, qc) * e_o + o_c * e_c
        if last:
            scatter(os_, qstarts, qc, a_n / l_n)
        else:
            scatter(m_s, qstarts, qc, m_n)
            scatter(l_s, qstarts, qc, l_n)
            scatter(acc_s, qstarts, qc, a_n)

    for pi, (window, d) in enumerate(DILATED_PATTERNS):
        assert window // d == w and CLASSES % d == 0
        first, last = pi == 0, pi == n_pat - 1
        nb = seq // (w * d)

        if nb >= 4:
            def any_block(i, carry, d=d, pi=pi, first=first, last=last):
                block(i % d, i // d, d, pi, first, last, "any")
                return carry
            lax.fori_loop(0, d * nb, any_block, 0, unroll=ATTN_UNROLL)
        else:
            def class_blocks(c, carry, d=d, pi=pi, first=first, last=last, nb=nb):
                block(c, 0, d, pi, first, last, "own")
                for n in range(1, nb):
                    block(c, n, d, pi, first, last, "prev+own")
                return carry
            lax.fori_loop(0, d, class_blocks, 0, unroll=max(ATTN_UNROLL // nb, 1))

    for r in range(CLASSES):
        o_ref[pl.ds(r, cl, stride=CLASSES), :] = os_[pl.ds(r * cl, cl), :]


def _neg_distance_tables():
    w = SUB_WINDOW
    prev, nxt, own = [], [], []
    for _, d in DILATED_PATTERNS:
        n_cls = CLASSES // d
        qc = w // n_cls
        def pos(chunk_rows):
            j, l = np.divmod(np.arange(n_cls * chunk_rows), chunk_rows)
            return n_cls * l + j
        sq, sk2, sk1 = pos(qc)[:, None], pos(2 * qc)[None, :], pos(qc)[None, :]
        for out, dist in ((prev, sq + w - sk2), (nxt, sq - sk2), (own, sq - sk1)):
            valid = (dist >= 0) & (dist <= w)
            out.append(np.where(valid, -(dist * d).astype(np.float32), np.float32(MASKED)))
    return jnp.asarray(np.stack(prev + nxt), dtype=F32), jnp.asarray(np.stack(own), dtype=F32)


def _attention(qkv, batch, seq):
    qkv4 = qkv.reshape(3 * N_PLANES, batch, seq, LANES)
    slopes = jnp.exp2(-8.0 * jnp.arange(1, N_HEADS + 1, dtype=F32) / N_HEADS)
    nd_full, nd_head = _neg_distance_tables()
    n_pat = len(DILATED_PATTERNS)
    plane = lambda off: pl.BlockSpec((None, None, seq, LANES), lambda b, h: (off + h, b, 0, 0))
    return pl.pallas_call(
        functools.partial(_attn_kernel, seq=seq),
        grid=(batch, N_PLANES),
        in_specs=[
            pl.BlockSpec(memory_space=pltpu.SMEM),
            plane(0), plane(N_PLANES), plane(2 * N_PLANES),
            pl.BlockSpec((2 * n_pat, SUB_WINDOW, 2 * SUB_WINDOW), lambda b, h: (0, 0, 0)),
            pl.BlockSpec((n_pat, SUB_WINDOW, SUB_WINDOW), lambda b, h: (0, 0, 0)),
        ],
        out_specs=pl.BlockSpec((None, None, seq, LANES), lambda b, h: (h, b, 0, 0)),
        scratch_shapes=[pltpu.VMEM((seq, LANES), F32)] * 7 + [
            pltpu.VMEM((2 * n_pat, 2 * SUB_WINDOW, 2 * SUB_WINDOW), F32),
            pltpu.VMEM((n_pat, 2 * SUB_WINDOW, SUB_WINDOW), F32),
        ],
        out_shape=jax.ShapeDtypeStruct((N_PLANES, batch, seq, LANES), F32),
        compiler_params=_params("parallel", "parallel"),
        name="dilated_attention",
    )(slopes, qkv4, qkv4, qkv4, nd_full, nd_head)


def _gelu_tanh(x):
    c = np.float32(np.sqrt(2.0 / np.pi))
    return 0.5 * x * (1.0 + jnp.tanh(c * (x + 0.044715 * (x * x * x))))


def _log1p(x):
    u = 1.0 + x
    return jnp.where(u == 1.0, x, jnp.log(u) * x / (u - 1.0))


def _expm1(x):
    u = jnp.exp(x)
    return jnp.where(u == 1.0, x, (u - 1.0) * x / jnp.log(u))


def _softplus(z):
    return jnp.maximum(z, 0.0) + _log1p(jnp.exp(-jnp.abs(z)))


def _lru_kernel(xr_ref, xg_ref, cw_ref, cb_ref, wbd_ref, ba_ref, bx_ref, lam_ref, g_ref, y_ref,
                xbuf, hc, a_s, b_s, h_s, *, ts):
    i = pl.program_id(1)
    pad = 8

    @pl.when(i == 0)
    def _():
        xbuf[0:pad, :] = jnp.zeros((pad, D_LRU), F32)
        hc[...] = jnp.zeros_like(hc)

    xbuf[pad:pad + ts, :] = xr_ref[...]
    y = jnp.broadcast_to(cb_ref[...], (ts, D_LRU))
    for j in range(CONV_WIDTH):
        off = pad - (CONV_WIDTH - 1) + j
        y = y + xbuf[off:off + ts, :] * cw_ref[j:j + 1, :]
    xbuf[0:pad, :] = xbuf[ts:ts + pad, :]

    yb = y.astype(BF16)
    gw = 4 * LRU_BLOCK_DIM
    r_parts, i_parts = [], []
    for j in range(D_LRU // gw):
        g = jnp.dot(yb[:, j * gw:(j + 1) * gw], wbd_ref[j], preferred_element_type=F32)
        r_parts.append(g[:, :gw])
        i_parts.append(g[:, gw:])
    r = jax.nn.sigmoid(jnp.concatenate(r_parts, axis=-1) + ba_ref[...])
    ig = jax.nn.sigmoid(jnp.concatenate(i_parts, axis=-1) + bx_ref[...])
    log_a = (-RG_C * r) * _softplus(-lam_ref[...])
    a = jnp.exp(log_a)
    u = jnp.sqrt(-_expm1(2.0 * log_a)) * (ig * y)

    rmod = lax.broadcasted_iota(jnp.int32, (ts, D_LRU), 0) & 7
    for sh in (1, 2, 4):
        a_sh = pltpu.roll(a, sh, 0)
        u_sh = pltpu.roll(u, sh, 0)
        take = rmod >= sh
        u = jnp.where(take, a * u_sh + u, u)
        a = jnp.where(take, a * a_sh, a)
    a_s[...] = a
    b_s[...] = u

    def group(gi, h):
        rows = pl.ds(pl.multiple_of(gi * 8, 8), 8)
        hg = a_s[rows, :] * h + b_s[rows, :]
        h_s[rows, :] = hg
        return jnp.broadcast_to(hg[7:8, :], (8, D_LRU))

    hc[...] = lax.fori_loop(0, ts // 8, group, hc[...])

    rec = h_s[...] * _gelu_tanh(xg_ref[...])
    ms = jnp.mean(rec * rec, axis=-1, keepdims=True)
    y_ref[...] = (rec * lax.rsqrt(ms + RMS_EPS) * g_ref[...]).astype(y_ref.dtype)


def _block_diag_gates(wa, wx):
    def bd(wm):
        wm = wm.reshape(4, 4, LRU_BLOCK_DIM, LRU_BLOCK_DIM)
        eye = jnp.eye(4, dtype=wm.dtype)
        full = jnp.einsum("gide,ij->gidje", wm, eye)
        return full.reshape(4, 4 * LRU_BLOCK_DIM, 4 * LRU_BLOCK_DIM)
    return jnp.concatenate([bd(wa), bd(wx)], axis=-1).astype(BF16)


def _recurrent(rg, conv_w, conv_b, lru_wa, lru_ba, lru_wx, lru_bx, lru_lambda, lru_norm_g, batch, seq):
    ts = TS_LRU
    rg4 = rg.reshape(2, batch, seq, D_LRU)
    wbd = _block_diag_gates(lru_wa, lru_wx)
    row = lambda a: a.reshape(1, D_LRU).astype(F32)
    vec = pl.BlockSpec((1, D_LRU), lambda b, i: (0, 0))
    return pl.pallas_call(
        functools.partial(_lru_kernel, ts=ts),
        grid=(batch, seq // ts),
        in_specs=[
            pl.BlockSpec((None, None, ts, D_LRU), lambda b, i: (0, b, i, 0)),
            pl.BlockSpec((None, None, ts, D_LRU), lambda b, i: (1, b, i, 0)),
            pl.BlockSpec((CONV_WIDTH, D_LRU), lambda b, i: (0, 0)),
            vec,
            pl.BlockSpec((4, 4 * LRU_BLOCK_DIM, 8 * LRU_BLOCK_DIM), lambda b, i: (0, 0, 0)),
            vec, vec, vec, vec,
        ],
        out_specs=pl.BlockSpec((None, ts, D_LRU), lambda b, i: (b, i, 0)),
        out_shape=jax.ShapeDtypeStruct((batch, seq, D_LRU), BF16),
        scratch_shapes=[
            pltpu.VMEM((ts + 8, D_LRU), F32),
            pltpu.VMEM((8, D_LRU), F32),
            pltpu.VMEM((ts, D_LRU), F32),
            pltpu.VMEM((ts, D_LRU), F32),
            pltpu.VMEM((ts, D_LRU), F32),
        ],
        compiler_params=_params("parallel", "arbitrary"),
        name="conv_rglru",
    )(rg4, rg4, conv_w.astype(F32), row(conv_b), wbd, row(lru_ba), row(lru_bx), row(lru_lambda),
      row(lru_norm_g))


def _layer_norm_rows(z, g, b):
    mu = jnp.mean(z, axis=-1, keepdims=True)
    zc = z - mu
    var = jnp.mean(zc * zc, axis=-1, keepdims=True)
    return zc * lax.rsqrt(var + LN_EPS) * g + b


def _split_bf16(a):
    hi = a.astype(BF16)
    lo = (a - hi.astype(F32)).astype(BF16)
    return hi, lo


def _to_token_major(ref, val, n_rows):
    for c in range(CHUNKS):
        ref[pl.ds(c, n_rows, stride=CHUNKS), :] = val[:, c * LANES:(c + 1) * LANES]


def _from_token_major(ref, first_row, n_rows):
    return jnp.concatenate(
        [ref[pl.ds(first_row * CHUNKS + c, n_rows, stride=CHUNKS), :] for c in range(CHUNKS)], axis=-1)


def _out_router_kernel(attn_ref, yrec_ref, x_ref, ga_ref, wo_ref, g1_ref, b1_ref, wr_ref, br_ref,
                       x1_ref, x1c_ref, pk_ref, rw_ref, cnt_ref, cnt_s, *, tm):
    attn = jnp.concatenate([attn_ref[p] for p in range(N_PLANES)], axis=-1)
    ms = jnp.mean(attn * attn, axis=-1, keepdims=True)
    ya = (attn * lax.rsqrt(ms + RMS_EPS) * ga_ref[...]).astype(BF16)
    mix = jnp.dot(ya, wo_ref[0:D_ATTN, :], preferred_element_type=F32)
    mix = mix + jnp.dot(yrec_ref[...], wo_ref[D_ATTN:, :], preferred_element_type=F32)
    x1 = _layer_norm_rows(ALPHA * x_ref[...] + mix, g1_ref[...], b1_ref[...])
    x1_ref[...] = x1
    _to_token_major(x1c_ref, x1, tm)

    x_hi, x_lo = _split_bf16(x1)
    w_hi, w_lo = _split_bf16(wr_ref[...])
    logits = (jnp.dot(x_hi, w_hi, preferred_element_type=F32)
              + jnp.dot(x_lo, w_hi, preferred_element_type=F32)
              + jnp.dot(x_hi, w_lo, preferred_element_type=F32)) + br_ref[...]

    lane = lax.broadcasted_iota(jnp.int32, (tm, LANES), 1)
    big = jnp.int32(LANES)
    first_true = lambda c: jnp.min(jnp.where(c, lane, big), axis=-1, keepdims=True)

    in_g = lane < N_GROUPS
    gl = jnp.where(in_g, logits, MASKED)
    gmax = jnp.max(gl, axis=-1, keepdims=True)
    g_idx = first_true(gl == gmax)
    gsum = jnp.sum(jnp.where(in_g, jnp.exp(gl - gmax), 0.0), axis=-1, keepdims=True)
    g_gate = 1.0 / gsum

    lo = N_GROUPS + EXPERTS_PER_GROUP * g_idx
    in_e = (lane >= lo) & (lane < lo + EXPERTS_PER_GROUP)
    el = jnp.where(in_e, logits, MASKED)
    emax = jnp.max(el, axis=-1, keepdims=True)
    ee = jnp.where(in_e, jnp.exp(el - emax), 0.0)
    pe = ee / jnp.sum(ee, axis=-1, keepdims=True)
    cand = jnp.where(in_e, pe, -1.0)
    v1 = jnp.max(cand, axis=-1, keepdims=True)
    i1 = first_true(cand == v1)
    cand2 = jnp.where(lane == i1, -1.0, cand)
    v2 = jnp.max(cand2, axis=-1, keepdims=True)
    i2 = first_true(cand2 == v2)
    den = v1 + v2
    w1 = g_gate * v1 / den
    w2 = g_gate * v2 / den
    e1 = i1 - N_GROUPS
    e2 = i2 - N_GROUPS
    rw_ref[...] = jnp.where(lane == 0, w1, jnp.where(lane == 1, w2, 0.0))

    @pl.when(pl.program_id(0) == 0)
    def _():
        cnt_s[...] = jnp.zeros_like(cnt_s)

    hot1 = lane == e1
    hot2 = lane == e2
    both = jnp.where(hot1 | hot2, 1.0, 0.0)
    earlier = (lax.broadcasted_iota(jnp.int32, (tm, tm), 1) < lax.broadcasted_iota(jnp.int32, (tm, tm), 0))
    before = jnp.dot(earlier.astype(BF16), both.astype(BF16), preferred_element_type=F32) + cnt_s[0:1, :]
    r1 = jnp.sum(jnp.where(hot1, before, 0.0), axis=-1, keepdims=True).astype(jnp.int32)
    r2 = jnp.sum(jnp.where(hot2, before, 0.0), axis=-1, keepdims=True).astype(jnp.int32)
    pk_ref[...] = jnp.where(lane == 0, e1 * RANK_SPAN + r1, jnp.where(lane == 1, e2 * RANK_SPAN + r2, 0))
    cnt_s[...] = cnt_s[...] + jnp.sum(both, axis=0, keepdims=True)
    cnt_ref[...] = cnt_s[...].astype(jnp.int32)


def _out_router(attn, yrec2d, x2d, attn_norm_g, w_out_b, ln1_g, ln1_b, w_router, b_router):
    T = x2d.shape[0]
    tm = TM_OUT
    assert 2 * T <= RANK_SPAN
    attn3 = attn.reshape(N_PLANES, T, LANES)
    const = lambda shape: pl.BlockSpec(shape, lambda i: (0,) * len(shape))
    return pl.pallas_call(
        functools.partial(_out_router_kernel, tm=tm),
        grid=(T // tm,),
        in_specs=[
            pl.BlockSpec((N_PLANES, tm, LANES), lambda i: (0, i, 0)),
            pl.BlockSpec((tm, D_LRU), lambda i: (i, 0)),
            pl.BlockSpec((tm, D_MODEL), lambda i: (i, 0)),
            const((1, D_ATTN)),
            pl.BlockSpec((D_MODEL, D_MODEL), lambda i: (0, 0), pipeline_mode=pl.Buffered(1)),
            const((1, D_MODEL)), const((1, D_MODEL)),
            const((D_MODEL, LANES)), const((1, LANES)),
        ],
        out_specs=[
            pl.BlockSpec((tm, D_MODEL), lambda i: (i, 0)),
            pl.BlockSpec((tm * CHUNKS, LANES), lambda i: (i, 0)),
            pl.BlockSpec((tm, LANES), lambda i: (i, 0)),
            pl.BlockSpec((tm, LANES), lambda i: (i, 0)),
            const((8, LANES)),
        ],
        out_shape=[
            jax.ShapeDtypeStruct((T, D_MODEL), F32),
            jax.ShapeDtypeStruct((T * CHUNKS, LANES), F32),
            jax.ShapeDtypeStruct((T, LANES), jnp.int32),
            jax.ShapeDtypeStruct((T, LANES), F32),
            jax.ShapeDtypeStruct((8, LANES), jnp.int32),
        ],
        scratch_shapes=[pltpu.VMEM((8, LANES), F32)],
        compiler_params=_params("arbitrary"),
        name="out_proj_router",
    )(attn3, yrec2d, x2d, attn_norm_g.reshape(1, D_ATTN).astype(F32), w_out_b,
      ln1_g.reshape(1, D_MODEL).astype(F32), ln1_b.reshape(1, D_MODEL).astype(F32), w_router, b_router)


def _slot_of(pk, pstart_ref):
    return pstart_ref[lax.shift_right_logical(pk, RANK_BITS)] + (pk & (RANK_SPAN - 1))


def _dispatch_kernel(pk_ref, pstart_ref, pend_ref, nused_ref, x1c_ref, xs_hbm, zbuf, sem, zsem,
                     *, tm, rows, n_blk):
    i = pl.program_id(0)
    blk = rows * CHUNKS

    @pl.when(i == 0)
    def _():
        zbuf[...] = jnp.zeros_like(zbuf)
        n_used = nused_ref[0]

        def zero_copy(b):
            return pltpu.make_async_copy(zbuf, xs_hbm.at[pl.ds(pl.multiple_of(b * blk, blk), blk), :], zsem)

        def for_each_zeroed_block(fn):
            for e in range(N_EXPERTS):
                @pl.when(pend_ref[e] > pstart_ref[e])
                def _(e=e):
                    fn(pend_ref[e] // rows - 1)

            def tail(b, c):
                fn(b)
                return c
            lax.fori_loop(n_used, n_blk, tail, 0)

        for_each_zeroed_block(lambda b: zero_copy(b).start())
        for_each_zeroed_block(lambda b: zero_copy(b).wait())

    def push(j, c):
        slot = _slot_of(pk_ref[i * (2 * tm) + j], pstart_ref)
        src = x1c_ref.at[pl.ds(pl.multiple_of((j >> 1) * CHUNKS, CHUNKS), CHUNKS), :]
        dst = xs_hbm.at[pl.ds(pl.multiple_of(slot * CHUNKS, CHUNKS), CHUNKS), :]
        pltpu.make_async_copy(src, dst, sem).start()
        return c
    lax.fori_loop(0, 2 * tm, push, 0, unroll=8)
    for _ in range(2):
        pltpu.make_async_copy(x1c_ref, xs_hbm.at[pl.ds(0, tm * CHUNKS), :], sem).wait()


def _dispatch(pk_flat, pad_start, pad_end, n_used, x1c, n_blk):
    tm, rows = TM_DISP, MOE_ROWS
    n_tok = x1c.shape[0] // CHUNKS
    grid_spec = pltpu.PrefetchScalarGridSpec(
        num_scalar_prefetch=4,
        grid=(n_tok // tm,),
        in_specs=[pl.BlockSpec((tm * CHUNKS, LANES), lambda i, *_: (i, 0))],
        out_specs=pl.BlockSpec(memory_space=pl.ANY),
        scratch_shapes=[
            pltpu.VMEM((rows * CHUNKS, LANES), F32),
            pltpu.SemaphoreType.DMA(()),
            pltpu.SemaphoreType.DMA(()),
        ],
    )
    return pl.pallas_call(
        functools.partial(_dispatch_kernel, tm=tm, rows=rows, n_blk=n_blk),
        grid_spec=grid_spec,
        out_shape=jax.ShapeDtypeStruct((n_blk * rows * CHUNKS, LANES), F32),
        compiler_params=_params("arbitrary"),
        name="moe_dispatch",
    )(pk_flat, pad_start, pad_end, n_used, x1c)


def _expert_kernel(blk_e_ref, n_used_ref, xs_ref, w1_ref, w3_ref, w2_ref, y_ref, w1b, w3b, w2b, *, rows):
    i = pl.program_id(0)
    n_used = n_used_ref[0]
    new_expert = jnp.logical_or(i == 0, blk_e_ref[i] != blk_e_ref[jnp.maximum(i - 1, 0)])

    @pl.when(jnp.logical_and(i < n_used, new_expert))
    def _():
        w1b[...] = w1_ref[...].astype(BF16)
        w3b[...] = w3_ref[...].astype(BF16)
        w2b[...] = w2_ref[...].astype(BF16)

    @pl.when(i < n_used)
    def _():
        xb = _from_token_major(xs_ref, 0, rows).astype(BF16)
        h1 = jnp.dot(xb, w1b[...], preferred_element_type=F32)
        h3 = jnp.dot(xb, w3b[...], preferred_element_type=F32)
        h = (jax.nn.silu(h1) * h3).astype(BF16)
        _to_token_major(y_ref, jnp.dot(h, w2b[...], preferred_element_type=F32), rows)

    @pl.when(i >= n_used)
    def _():
        y_ref[...] = jnp.zeros_like(y_ref)


def _experts(blk_e, n_used, xs, w1, w3, w2):
    rows = MOE_ROWS
    n_blk = blk_e.shape[0]
    wspec = lambda shape: pl.BlockSpec((None,) + shape, lambda i, be, nu: (be[i], 0, 0))
    grid_spec = pltpu.PrefetchScalarGridSpec(
        num_scalar_prefetch=2,
        grid=(n_blk,),
        in_specs=[
            pl.BlockSpec((rows * CHUNKS, LANES), lambda i, be, nu: (jnp.minimum(i, nu[0] - 1), 0)),
            wspec((D_MODEL, D_EXPERT)), wspec((D_MODEL, D_EXPERT)), wspec((D_EXPERT, D_MODEL)),
        ],
        out_specs=pl.BlockSpec((rows * CHUNKS, LANES), lambda i, be, nu: (i, 0)),
        scratch_shapes=[
            pltpu.VMEM((D_MODEL, D_EXPERT), BF16),
            pltpu.VMEM((D_MODEL, D_EXPERT), BF16),
            pltpu.VMEM((D_EXPERT, D_MODEL), BF16),
        ],
    )
    return pl.pallas_call(
        functools.partial(_expert_kernel, rows=rows),
        grid_spec=grid_spec,
        out_shape=jax.ShapeDtypeStruct((n_blk * rows * CHUNKS, LANES), F32),
        compiler_params=_params("arbitrary"),
        name="moe_experts",
    )(blk_e, n_used, xs, w1, w3, w2)


def _combine_kernel(pk_ref, pstart_ref, ys_hbm, x1_ref, rw_ref, g2_ref, b2_ref, o_ref, yg, sem, *, tm):
    i = pl.program_id(0)
    n = pl.num_programs(0)
    slot = i % 2
    half = tm * CHUNKS

    def start_gather(tile, slot_):
        def body(j, c):
            src_row = _slot_of(pk_ref[tile * (2 * tm) + j], pstart_ref) * CHUNKS
            dst_row = (j & 1) * half + (j >> 1) * CHUNKS
            pltpu.make_async_copy(ys_hbm.at[pl.ds(pl.multiple_of(src_row, CHUNKS), CHUNKS), :],
                                  yg.at[slot_, pl.ds(pl.multiple_of(dst_row, CHUNKS), CHUNKS), :],
                                  sem.at[slot_]).start()
            return c
        lax.fori_loop(0, 2 * tm, body, 0, unroll=8)

    @pl.when(i == 0)
    def _():
        start_gather(0, 0)

    @pl.when(i + 1 < n)
    def _():
        start_gather(i + 1, 1 - slot)

    pltpu.make_async_copy(ys_hbm.at[pl.ds(0, 2 * half), :], yg.at[slot], sem.at[slot]).wait()
    rw = rw_ref[...]
    buf = yg.at[slot]
    moe = _from_token_major(buf, 0, tm) * rw[:, 0:1] + _from_token_major(buf, tm, tm) * rw[:, 1:2]
    o_ref[...] = _layer_norm_rows(ALPHA * x1_ref[...] + moe, g2_ref[...], b2_ref[...])


def _combine(pk_flat, pad_start, ys, x1, rw, ln2_g, ln2_b):
    T = x1.shape[0]
    tm = TM_COMB
    grid_spec = pltpu.PrefetchScalarGridSpec(
        num_scalar_prefetch=2,
        grid=(T // tm,),
        in_specs=[
            pl.BlockSpec(memory_space=pl.ANY),
            pl.BlockSpec((tm, D_MODEL), lambda i, *_: (i, 0)),
            pl.BlockSpec((tm, LANES), lambda i, *_: (i, 0)),
            pl.BlockSpec((1, D_MODEL), lambda i, *_: (0, 0)),
            pl.BlockSpec((1, D_MODEL), lambda i, *_: (0, 0)),
        ],
        out_specs=pl.BlockSpec((tm, D_MODEL), lambda i, *_: (i, 0)),
        scratch_shapes=[
            pltpu.VMEM((2, 2 * tm * CHUNKS, LANES), F32),
            pltpu.SemaphoreType.DMA((2,)),
        ],
    )
    return pl.pallas_call(
        functools.partial(_combine_kernel, tm=tm),
        grid_spec=grid_spec,
        out_shape=jax.ShapeDtypeStruct((T, D_MODEL), F32),
        compiler_params=_params("arbitrary"),
        name="moe_combine_ln",
    )(pk_flat, pad_start, ys, x1, rw, ln2_g.reshape(1, D_MODEL).astype(F32), ln2_b.reshape(1, D_MODEL).astype(F32))


def _dispatch_plan(counts, n_tokens):
    rows = MOE_ROWS
    padded = (counts + rows - 1) // rows * rows
    pad_end = jnp.cumsum(padded).astype(jnp.int32)
    pad_start = (pad_end - padded).astype(jnp.int32)
    n_blk = 2 * n_tokens // rows + N_EXPERTS
    blk_start = jnp.arange(n_blk, dtype=jnp.int32) * rows
    blk_e = jnp.minimum(jnp.searchsorted(pad_end, blk_start, side="right"), N_EXPERTS - 1).astype(jnp.int32)
    n_used = (pad_end[-1:] // rows).astype(jnp.int32)
    return pad_start, pad_end, blk_e, n_used, n_blk


def kernel(x, w_in, conv_w, conv_b, lru_wa, lru_ba, lru_wx, lru_bx, lru_lambda, attn_norm_g, lru_norm_g,
           w_out, ln1_g, ln1_b, router_grp_w, router_grp_b, router_exp_w, router_exp_b, w1, w3, w2,
           ln2_g, ln2_b):
    B, S, D = x.shape
    assert D == D_MODEL and S % DILATED_PATTERNS[-1][0] == 0 and w_in.shape[0] == 1
    T = B * S
    x2d = x.reshape(T, D)

    qkv, rg = _in_proj(x2d, w_in[0].astype(BF16))
    attn = _attention(qkv, B, S)
    yrec = _recurrent(rg, conv_w[0], conv_b[0], lru_wa[0], lru_ba[0], lru_wx[0], lru_bx[0], lru_lambda[0],
                      lru_norm_g[0], B, S)

    n_r = N_GROUPS + N_EXPERTS
    w_router = jnp.zeros((D, LANES), F32).at[:, :n_r].set(
        jnp.concatenate([router_grp_w[0], router_exp_w[0]], axis=-1).astype(F32))
    b_router = jnp.zeros((1, LANES), F32).at[0, :n_r].set(
        jnp.concatenate([router_grp_b[0], router_exp_b[0]], axis=-1).astype(F32))
    x1, x1c, pk, rw, cnt = _out_router(attn, yrec.reshape(T, D_LRU), x2d, attn_norm_g[0], w_out[0].astype(BF16),
                                       ln1_g[0], ln1_b[0], w_router, b_router)

    pk_flat = pk[:, :2].reshape(-1)
    pad_start, pad_end, blk_e, n_used, n_blk = _dispatch_plan(cnt[0, :N_EXPERTS], T)
    xs = _dispatch(pk_flat, pad_start, pad_end, n_used, x1c, n_blk)
    ys = _experts(blk_e, n_used, xs, w1[0], w3[0], w2[0])
    out = _combine(pk_flat, pad_start, ys, x1, rw, ln2_g[0], ln2_b[0])
    return out.reshape(B, S, D)
```

```python
import functools

import jax
import jax.numpy as jnp
import numpy as np
from jax import lax
from jax.experimental import pallas as pl
from jax.experimental.pallas import tpu as pltpu

F32 = jnp.float32
BF16 = jnp.bfloat16

D_MODEL = 2048
D_ATTN = 1024
D_LRU = 1024
HEAD_DIM = 64
N_HEADS = 16
LANES = 128
N_PLANES = D_ATTN // LANES
DILATED_PATTERNS = ((128, 1), (512, 4), (2048, 16))
SUB_WINDOW = 128
LRU_BLOCKS = 16
LRU_BLOCK_DIM = 64
CONV_WIDTH = 4
RG_C = 8.0
N_GROUPS = 4
EXPERTS_PER_GROUP = 8
N_EXPERTS = 32
D_EXPERT = 512
ALPHA = 2.0 ** 0.25
LN_EPS = 1e-5
RMS_EPS = 1e-6
MASKED = -1e30

VMEM_LIMIT = 56 * 1024 * 1024

TM_PROJ = 256
TS_LRU = 256
TM_OUT = 256
MOE_ROWS = 256
TM_COMB = 256
TM_DISP = 256
CHUNKS = D_MODEL // LANES
PACKED_CHUNKS = CHUNKS // 2
RANK_BITS = 16
RANK_SPAN = 1 << RANK_BITS
ATTN_UNROLL = 8


def _params(*sem):
    return pltpu.CompilerParams(dimension_semantics=sem, vmem_limit_bytes=VMEM_LIMIT)


def _in_proj_kernel(x_ref, w_ref, qkv_ref, rg_ref):
    xb = x_ref[...].astype(BF16)
    for c in range(5):
        acc = jnp.dot(xb, w_ref[:, c * D_ATTN:(c + 1) * D_ATTN], preferred_element_type=F32)
        if c == 0:
            acc = acc * (HEAD_DIM ** -0.5)
        if c < 3:
            for p in range(N_PLANES):
                qkv_ref[c * N_PLANES + p] = acc[:, p * LANES:(p + 1) * LANES]
        else:
            rg_ref[c - 3] = acc


def _in_proj(x2d, w_in_b):
    T = x2d.shape[0]
    tm = TM_PROJ
    return pl.pallas_call(
        _in_proj_kernel,
        grid=(T // tm,),
        in_specs=[
            pl.BlockSpec((tm, D_MODEL), lambda i: (i, 0)),
            pl.BlockSpec((D_MODEL, 5 * D_ATTN), lambda i: (0, 0), pipeline_mode=pl.Buffered(1)),
        ],
        out_specs=[
            pl.BlockSpec((3 * N_PLANES, tm, LANES), lambda i: (0, i, 0)),
            pl.BlockSpec((2, tm, D_LRU), lambda i: (0, i, 0)),
        ],
        out_shape=[
            jax.ShapeDtypeStruct((3 * N_PLANES, T, LANES), F32),
            jax.ShapeDtypeStruct((2, T, D_LRU), F32),
        ],
        compiler_params=_params("parallel"),
        name="in_proj",
    )(x2d, w_in_b)


CLASSES = 16


def _attn_kernel(slopes_ref, q_ref, k_ref, v_ref, ndf_ref, ndh_ref, o_ref,
                 qs, ks, vs, os_, m_s, l_s, acc_s, bias_f, bias_h, *, seq):
    hp = pl.program_id(1)
    slope_a = slopes_ref[2 * hp]
    slope_b = slopes_ref[2 * hp + 1]
    w = SUB_WINDOW
    cl = seq // CLASSES
    n_pat = len(DILATED_PATTERNS)

    for r in range(CLASSES):
        rows = pl.ds(r * cl, cl)
        strided = pl.ds(r, cl, stride=CLASSES)
        qs[rows, :] = q_ref[strided, :]
        ks[rows, :] = k_ref[strided, :]
        vs[rows, :] = v_ref[strided, :]

    def gather(ref, starts, size):
        return jnp.concatenate([ref[pl.ds(s, size), :] for s in starts], axis=0)

    def scatter(ref, starts, size, val):
        for c, s in enumerate(starts):
            ref[pl.ds(s, size), :] = val[c * size:(c + 1) * size, :]

    for t in range(2 * n_pat):
        bias_f[t, 0:w, :] = slope_a * ndf_ref[t]
        bias_f[t, w:2 * w, :] = slope_b * ndf_ref[t]
    for t in range(n_pat):
        bias_h[t, 0:w, :] = slope_a * ndh_ref[t]
        bias_h[t, w:2 * w, :] = slope_b * ndh_ref[t]

    def block(c, n, d, pi, first, last, keys):
        qc = w * d // CLASSES
        aligned = lambda x: x if isinstance(x, int) else pl.multiple_of(x, 8)
        bases = [(c + d * j) * cl for j in range(CLASSES // d)]
        qstarts = [aligned(b + n * qc) for b in bases]
        if keys == "own":
            kstarts, kc, bias = qstarts, qc, bias_h[pi]
        elif keys == "prev+own":
            kstarts, kc, bias = [aligned(b + (n - 1) * qc) for b in bases], 2 * qc, bias_f[pi]
        else:
            kfirst = jnp.maximum(n - 1, 0) * qc
            kstarts, kc = [aligned(b + kfirst) for b in bases], 2 * qc
            bias = bias_f[jnp.where(n == 0, pi + n_pat, pi)]
        is_a = lax.broadcasted_iota(jnp.int32, (w, LANES), 1) < HEAD_DIM
        q = gather(qs, qstarts, qc)
        k = gather(ks, kstarts, kc).astype(BF16)
        v = gather(vs, kstarts, kc).astype(BF16)
        zero = jnp.zeros_like(q)
        q2 = jnp.concatenate([jnp.where(is_a, q, zero), jnp.where(is_a, zero, q)], axis=0).astype(BF16)
        s = lax.dot_general(q2, k, (((1,), (1,)), ((), ())), preferred_element_type=F32) + bias
        m = jnp.max(s, axis=-1, keepdims=True)
        p = jnp.exp(s - m)
        l = jnp.sum(p, axis=-1, keepdims=True)
        o = jnp.dot(p.astype(BF16), v, preferred_element_type=F32)
        m_c = jnp.where(is_a, m[:w], m[w:])
        l_c = jnp.where(is_a, l[:w], l[w:])
        o_c = jnp.where(is_a, o[:w], o[w:])
        if first:
            scatter(m_s, qstarts, qc, m_c)
            scatter(l_s, qstarts, qc, l_c)
            scatter(acc_s, qstarts, qc, o_c)
            return
        m_o = gather(m_s, qstarts, qc)
        m_n = jnp.maximum(m_o, m_c)
        e_o = jnp.exp(m_o - m_n)
        e_c = jnp.exp(m_c - m_n)
        l_n = gather(l_s, qstarts, qc) * e_o + l_c * e_c
        a_n = gather(acc_s, qstarts, qc) * e_o + o_c * e_c
        if last:
            scatter(os_, qstarts, qc, a_n / l_n)
        else:
            scatter(m_s, qstarts, qc, m_n)
            scatter(l_s, qstarts, qc, l_n)
            scatter(acc_s, qstarts, qc, a_n)

    for pi, (window, d) in enumerate(DILATED_PATTERNS):
        assert window // d == w and CLASSES % d == 0
        first, last = pi == 0, pi == n_pat - 1
        nb = seq // (w * d)

        if nb >= 4:
            def any_block(i, carry, d=d, pi=pi, first=first, last=last):
                block(i % d, i // d, d, pi, first, last, "any")
                return carry
            lax.fori_loop(0, d * nb, any_block, 0, unroll=ATTN_UNROLL)
        else:
            def class_blocks(c, carry, d=d, pi=pi, first=first, last=last, nb=nb):
                block(c, 0, d, pi, first, last, "own")
                for n in range(1, nb):
                    block(c, n, d, pi, first, last, "prev+own")
                return carry
            lax.fori_loop(0, d, class_blocks, 0, unroll=max(ATTN_UNROLL // nb, 1))

    for r in range(CLASSES):
        o_ref[pl.ds(r, cl, stride=CLASSES), :] = os_[pl.ds(r * cl, cl), :]


def _neg_distance_tables():
    w = SUB_WINDOW
    prev, nxt, own = [], [], []
    for _, d in DILATED_PATTERNS:
        n_cls = CLASSES // d
        qc = w // n_cls
        def pos(chunk_rows):
            j, l = np.divmod(np.arange(n_cls * chunk_rows), chunk_rows)
            return n_cls * l + j
        sq, sk2, sk1 = pos(qc)[:, None], pos(2 * qc)[None, :], pos(qc)[None, :]
        for out, dist in ((prev, sq + w - sk2), (nxt, sq - sk2), (own, sq - sk1)):
            valid = (dist >= 0) & (dist <= w)
            out.append(np.where(valid, -(dist * d).astype(np.float32), np.float32(MASKED)))
    return jnp.asarray(np.stack(prev + nxt), dtype=F32), jnp.asarray(np.stack(own), dtype=F32)


def _attention(qkv, batch, seq):
    qkv4 = qkv.reshape(3 * N_PLANES, batch, seq, LANES)
    slopes = jnp.exp2(-8.0 * jnp.arange(1, N_HEADS + 1, dtype=F32) / N_HEADS)
    nd_full, nd_head = _neg_distance_tables()
    n_pat = len(DILATED_PATTERNS)
    plane = lambda off: pl.BlockSpec((None, None, seq, LANES), lambda b, h: (off + h, b, 0, 0))
    return pl.pallas_call(
        functools.partial(_attn_kernel, seq=seq),
        grid=(batch, N_PLANES),
        in_specs=[
            pl.BlockSpec(memory_space=pltpu.SMEM),
            plane(0), plane(N_PLANES), plane(2 * N_PLANES),
            pl.BlockSpec((2 * n_pat, SUB_WINDOW, 2 * SUB_WINDOW), lambda b, h: (0, 0, 0)),
            pl.BlockSpec((n_pat, SUB_WINDOW, SUB_WINDOW), lambda b, h: (0, 0, 0)),
        ],
        out_specs=pl.BlockSpec((None, None, seq, LANES), lambda b, h: (h, b, 0, 0)),
        scratch_shapes=[pltpu.VMEM((seq, LANES), F32)] * 7 + [
            pltpu.VMEM((2 * n_pat, 2 * SUB_WINDOW, 2 * SUB_WINDOW), F32),
            pltpu.VMEM((n_pat, 2 * SUB_WINDOW, SUB_WINDOW), F32),
        ],
        out_shape=jax.ShapeDtypeStruct((N_PLANES, batch, seq, LANES), F32),
        compiler_params=_params("parallel", "parallel"),
        name="dilated_attention",
    )(slopes, qkv4, qkv4, qkv4, nd_full, nd_head)


def _gelu_tanh(x):
    c = np.float32(np.sqrt(2.0 / np.pi))
    return 0.5 * x * (1.0 + jnp.tanh(c * (x + 0.044715 * (x * x * x))))


def _log1p(x):
    u = 1.0 + x
    return jnp.where(u == 1.0, x, jnp.log(u) * x / (u - 1.0))


def _expm1(x):
    u = jnp.exp(x)
    return jnp.where(u == 1.0, x, (u - 1.0) * x / jnp.log(u))


def _softplus(z):
    return jnp.maximum(z, 0.0) + _log1p(jnp.exp(-jnp.abs(z)))


def _lru_kernel(xr_ref, xg_ref, cw_ref, cb_ref, wbd_ref, ba_ref, bx_ref, lam_ref, g_ref, y_ref,
                xbuf, hc, a_s, b_s, h_s, *, ts):
    i = pl.program_id(1)
    pad = 8

    @pl.when(i == 0)
    def _():
        xbuf[0:pad, :] = jnp.zeros((pad, D_LRU), F32)
        hc[...] = jnp.zeros_like(hc)

    xbuf[pad:pad + ts, :] = xr_ref[...]
    y = jnp.broadcast_to(cb_ref[...], (ts, D_LRU))
    for j in range(CONV_WIDTH):
        off = pad - (CONV_WIDTH - 1) + j
        y = y + xbuf[off:off + ts, :] * cw_ref[j:j + 1, :]
    xbuf[0:pad, :] = xbuf[ts:ts + pad, :]

    yb = y.astype(BF16)
    gw = 4 * LRU_BLOCK_DIM
    r_parts, i_parts = [], []
    for j in range(D_LRU // gw):
        g = jnp.dot(yb[:, j * gw:(j + 1) * gw], wbd_ref[j], preferred_element_type=F32)
        r_parts.append(g[:, :gw])
        i_parts.append(g[:, gw:])
    r = jax.nn.sigmoid(jnp.concatenate(r_parts, axis=-1) + ba_ref[...])
    ig = jax.nn.sigmoid(jnp.concatenate(i_parts, axis=-1) + bx_ref[...])
    log_a = (-RG_C * r) * _softplus(-lam_ref[...])
    a = jnp.exp(log_a)
    u = jnp.sqrt(-_expm1(2.0 * log_a)) * (ig * y)

    rmod = lax.broadcasted_iota(jnp.int32, (ts, D_LRU), 0) & 7
    for sh in (1, 2, 4):
        a_sh = pltpu.roll(a, sh, 0)
        u_sh = pltpu.roll(u, sh, 0)
        take = rmod >= sh
        u = jnp.where(take, a * u_sh + u, u)
        a = jnp.where(take, a * a_sh, a)
    a_s[...] = a
    b_s[...] = u

    def group(gi, h):
        rows = pl.ds(pl.multiple_of(gi * 8, 8), 8)
        hg = a_s[rows, :] * h + b_s[rows, :]
        h_s[rows, :] = hg
        return jnp.broadcast_to(hg[7:8, :], (8, D_LRU))

    hc[...] = lax.fori_loop(0, ts // 8, group, hc[...])

    rec = h_s[...] * _gelu_tanh(xg_ref[...])
    ms = jnp.mean(rec * rec, axis=-1, keepdims=True)
    y_ref[...] = (rec * lax.rsqrt(ms + RMS_EPS) * g_ref[...]).astype(y_ref.dtype)


def _block_diag_gates(wa, wx):
    def bd(wm):
        wm = wm.reshape(4, 4, LRU_BLOCK_DIM, LRU_BLOCK_DIM)
        eye = jnp.eye(4, dtype=wm.dtype)
        full = jnp.einsum("gide,ij->gidje", wm, eye)
        return full.reshape(4, 4 * LRU_BLOCK_DIM, 4 * LRU_BLOCK_DIM)
    return jnp.concatenate([bd(wa), bd(wx)], axis=-1).astype(BF16)


def _recurrent(rg, conv_w, conv_b, lru_wa, lru_ba, lru_wx, lru_bx, lru_lambda, lru_norm_g, batch, seq):
    ts = TS_LRU
    rg4 = rg.reshape(2, batch, seq, D_LRU)
    wbd = _block_diag_gates(lru_wa, lru_wx)
    row = lambda a: a.reshape(1, D_LRU).astype(F32)
    vec = pl.BlockSpec((1, D_LRU), lambda b, i: (0, 0))
    return pl.pallas_call(
        functools.partial(_lru_kernel, ts=ts),
        grid=(batch, seq // ts),
        in_specs=[
            pl.BlockSpec((None, None, ts, D_LRU), lambda b, i: (0, b, i, 0)),
            pl.BlockSpec((None, None, ts, D_LRU), lambda b, i: (1, b, i, 0)),
            pl.BlockSpec((CONV_WIDTH, D_LRU), lambda b, i: (0, 0)),
            vec,
            pl.BlockSpec((4, 4 * LRU_BLOCK_DIM, 8 * LRU_BLOCK_DIM), lambda b, i: (0, 0, 0)),
            vec, vec, vec, vec,
        ],
        out_specs=pl.BlockSpec((None, ts, D_LRU), lambda b, i: (b, i, 0)),
        out_shape=jax.ShapeDtypeStruct((batch, seq, D_LRU), BF16),
        scratch_shapes=[
            pltpu.VMEM((ts + 8, D_LRU), F32),
            pltpu.VMEM((8, D_LRU), F32),
            pltpu.VMEM((ts, D_LRU), F32),
            pltpu.VMEM((ts, D_LRU), F32),
            pltpu.VMEM((ts, D_LRU), F32),
        ],
        compiler_params=_params("parallel", "arbitrary"),
        name="conv_rglru",
    )(rg4, rg4, conv_w.astype(F32), row(conv_b), wbd, row(lru_ba), row(lru_bx), row(lru_lambda),
      row(lru_norm_g))


def _layer_norm_rows(z, g, b):
    mu = jnp.mean(z, axis=-1, keepdims=True)
    zc = z - mu
    var = jnp.mean(zc * zc, axis=-1, keepdims=True)
    return zc * lax.rsqrt(var + LN_EPS) * g + b


def _split_bf16(a):
    hi = a.astype(BF16)
    lo = (a - hi.astype(F32)).astype(BF16)
    return hi, lo


def _to_token_major(ref, val, n_rows):
    for c in range(CHUNKS):
        ref[pl.ds(c, n_rows, stride=CHUNKS), :] = val[:, c * LANES:(c + 1) * LANES]


def _from_token_major(ref, first_row, n_rows):
    return jnp.concatenate(
        [ref[pl.ds(first_row * CHUNKS + c, n_rows, stride=CHUNKS), :] for c in range(CHUNKS)], axis=-1)


U32 = jnp.uint32
HI_HALF = np.uint32(0xFFFF0000)


def _to_packed_token_major(ref, val, n_rows):
    bits = lambda a: lax.bitcast_convert_type(a.astype(BF16).astype(F32), U32)
    half = PACKED_CHUNKS * LANES
    for c in range(PACKED_CHUNKS):
        lo = bits(val[:, c * LANES:(c + 1) * LANES]) >> 16
        hi = bits(val[:, half + c * LANES:half + (c + 1) * LANES]) & HI_HALF
        ref[pl.ds(c, n_rows, stride=PACKED_CHUNKS), :] = lo | hi


def _from_packed_token_major(ref, n_rows):
    words = [ref[pl.ds(c, n_rows, stride=PACKED_CHUNKS), :] for c in range(PACKED_CHUNKS)]
    lo = [lax.bitcast_convert_type(wd << 16, F32) for wd in words]
    hi = [lax.bitcast_convert_type(wd & HI_HALF, F32) for wd in words]
    return jnp.concatenate(lo + hi, axis=-1).astype(BF16)


def _out_router_kernel(attn_ref, yrec_ref, x_ref, ga_ref, wo_ref, g1_ref, b1_ref, wr_ref, br_ref,
                       x1_ref, x1p_ref, pk_ref, rw_ref, cnt_ref, cnt_s, *, tm):
    attn = jnp.concatenate([attn_ref[p] for p in range(N_PLANES)], axis=-1)
    ms = jnp.mean(attn * attn, axis=-1, keepdims=True)
    ya = (attn * lax.rsqrt(ms + RMS_EPS) * ga_ref[...]).astype(BF16)
    mix = jnp.dot(ya, wo_ref[0:D_ATTN, :], preferred_element_type=F32)
    mix = mix + jnp.dot(yrec_ref[...], wo_ref[D_ATTN:, :], preferred_element_type=F32)
    x1 = _layer_norm_rows(ALPHA * x_ref[...] + mix, g1_ref[...], b1_ref[...])
    x1_ref[...] = x1
    _to_packed_token_major(x1p_ref, x1, tm)

    x_hi, x_lo = _split_bf16(x1)
    w_hi, w_lo = _split_bf16(wr_ref[...])
    logits = (jnp.dot(x_hi, w_hi, preferred_element_type=F32)
              + jnp.dot(x_lo, w_hi, preferred_element_type=F32)
              + jnp.dot(x_hi, w_lo, preferred_element_type=F32)) + br_ref[...]

    lane = lax.broadcasted_iota(jnp.int32, (tm, LANES), 1)
    big = jnp.int32(LANES)
    first_true = lambda c: jnp.min(jnp.where(c, lane, big), axis=-1, keepdims=True)

    in_g = lane < N_GROUPS
    gl = jnp.where(in_g, logits, MASKED)
    gmax = jnp.max(gl, axis=-1, keepdims=True)
    g_idx = first_true(gl == gmax)
    gsum = jnp.sum(jnp.where(in_g, jnp.exp(gl - gmax), 0.0), axis=-1, keepdims=True)
    g_gate = 1.0 / gsum

    lo = N_GROUPS + EXPERTS_PER_GROUP * g_idx
    in_e = (lane >= lo) & (lane < lo + EXPERTS_PER_GROUP)
    el = jnp.where(in_e, logits, MASKED)
    emax = jnp.max(el, axis=-1, keepdims=True)
    ee = jnp.where(in_e, jnp.exp(el - emax), 0.0)
    pe = ee / jnp.sum(ee, axis=-1, keepdims=True)
    cand = jnp.where(in_e, pe, -1.0)
    v1 = jnp.max(cand, axis=-1, keepdims=True)
    i1 = first_true(cand == v1)
    cand2 = jnp.where(lane == i1, -1.0, cand)
    v2 = jnp.max(cand2, axis=-1, keepdims=True)
    i2 = first_true(cand2 == v2)
    den = v1 + v2
    w1 = g_gate * v1 / den
    w2 = g_gate * v2 / den
    e1 = i1 - N_GROUPS
    e2 = i2 - N_GROUPS
    rw_ref[...] = jnp.where(lane == 0, w1, jnp.where(lane == 1, w2, 0.0))

    @pl.when(pl.program_id(0) == 0)
    def _():
        cnt_s[...] = jnp.zeros_like(cnt_s)

    hot1 = lane == e1
    hot2 = lane == e2
    both = jnp.where(hot1 | hot2, 1.0, 0.0)
    earlier = (lax.broadcasted_iota(jnp.int32, (tm, tm), 1) < lax.broadcasted_iota(jnp.int32, (tm, tm), 0))
    before = jnp.dot(earlier.astype(BF16), both.astype(BF16), preferred_element_type=F32) + cnt_s[0:1, :]
    r1 = jnp.sum(jnp.where(hot1, before, 0.0), axis=-1, keepdims=True).astype(jnp.int32)
    r2 = jnp.sum(jnp.where(hot2, before, 0.0), axis=-1, keepdims=True).astype(jnp.int32)
    pk_ref[...] = jnp.where(lane == 0, e1 * RANK_SPAN + r1, jnp.where(lane == 1, e2 * RANK_SPAN + r2, 0))
    cnt_s[...] = cnt_s[...] + jnp.sum(both, axis=0, keepdims=True)
    cnt_ref[...] = cnt_s[...].astype(jnp.int32)


def _out_router(attn, yrec2d, x2d, attn_norm_g, w_out_b, ln1_g, ln1_b, w_router, b_router):
    T = x2d.shape[0]
    tm = TM_OUT
    assert 2 * T <= RANK_SPAN
    attn3 = attn.reshape(N_PLANES, T, LANES)
    const = lambda shape: pl.BlockSpec(shape, lambda i: (0,) * len(shape))
    return pl.pallas_call(
        functools.partial(_out_router_kernel, tm=tm),
        grid=(T // tm,),
        in_specs=[
            pl.BlockSpec((N_PLANES, tm, LANES), lambda i: (0, i, 0)),
            pl.BlockSpec((tm, D_LRU), lambda i: (i, 0)),
            pl.BlockSpec((tm, D_MODEL), lambda i: (i, 0)),
            const((1, D_ATTN)),
            pl.BlockSpec((D_MODEL, D_MODEL), lambda i: (0, 0), pipeline_mode=pl.Buffered(1)),
            const((1, D_MODEL)), const((1, D_MODEL)),
            const((D_MODEL, LANES)), const((1, LANES)),
        ],
        out_specs=[
            pl.BlockSpec((tm, D_MODEL), lambda i: (i, 0)),
            pl.BlockSpec((tm * PACKED_CHUNKS, LANES), lambda i: (i, 0)),
            pl.BlockSpec((tm, LANES), lambda i: (i, 0)),
            pl.BlockSpec((tm, LANES), lambda i: (i, 0)),
            const((8, LANES)),
        ],
        out_shape=[
            jax.ShapeDtypeStruct((T, D_MODEL), F32),
            jax.ShapeDtypeStruct((T * PACKED_CHUNKS, LANES), U32),
            jax.ShapeDtypeStruct((T, LANES), jnp.int32),
            jax.ShapeDtypeStruct((T, LANES), F32),
            jax.ShapeDtypeStruct((8, LANES), jnp.int32),
        ],
        scratch_shapes=[pltpu.VMEM((8, LANES), F32)],
        compiler_params=_params("arbitrary"),
        name="out_proj_router",
    )(attn3, yrec2d, x2d, attn_norm_g.reshape(1, D_ATTN).astype(F32), w_out_b,
      ln1_g.reshape(1, D_MODEL).astype(F32), ln1_b.reshape(1, D_MODEL).astype(F32), w_router, b_router)


def _dispatch_kernel(slot_ref, pstart_ref, pend_ref, nused_ref, x1p_ref, xs_hbm, zbuf, sem, zsem,
                     *, tm, rows, n_blk):
    i = pl.program_id(0)
    slab = PACKED_CHUNKS
    blk = rows * slab

    @pl.when(i == 0)
    def _():
        zbuf[...] = jnp.zeros_like(zbuf)
        n_used = nused_ref[0]

        def zero_copy(b):
            return pltpu.make_async_copy(zbuf, xs_hbm.at[pl.ds(pl.multiple_of(b * blk, blk), blk), :], zsem)

        def for_each_zeroed_block(fn):
            for e in range(N_EXPERTS):
                @pl.when(pend_ref[e] > pstart_ref[e])
                def _(e=e):
                    fn(pend_ref[e] // rows - 1)

            def tail(b, c):
                fn(b)
                return c
            lax.fori_loop(n_used, n_blk, tail, 0)

        for_each_zeroed_block(lambda b: zero_copy(b).start())
        for_each_zeroed_block(lambda b: zero_copy(b).wait())

    group = 8

    def push(g, c):
        for t in range(group):
            src = x1p_ref.at[pl.ds(pl.multiple_of(g * (group * slab), group * slab) + t * slab, slab), :]
            for k in range(2):
                row = slot_ref[i * (2 * tm) + g * (2 * group) + 2 * t + k] * slab
                dst = xs_hbm.at[pl.ds(pl.multiple_of(row, slab), slab), :]
                pltpu.make_async_copy(src, dst, sem).start()
        return c
    lax.fori_loop(0, tm // group, push, 0)
    for _ in range(2):
        pltpu.make_async_copy(x1p_ref, xs_hbm.at[pl.ds(0, tm * slab), :], sem).wait()


def _dispatch(slot_flat, pad_start, pad_end, n_used, x1p, n_blk):
    tm, rows, slab = TM_DISP, MOE_ROWS, PACKED_CHUNKS
    n_tok = x1p.shape[0] // slab
    grid_spec = pltpu.PrefetchScalarGridSpec(
        num_scalar_prefetch=4,
        grid=(n_tok // tm,),
        in_specs=[pl.BlockSpec((tm * slab, LANES), lambda i, *_: (i, 0))],
        out_specs=pl.BlockSpec(memory_space=pl.ANY),
        scratch_shapes=[
            pltpu.VMEM((rows * slab, LANES), U32),
            pltpu.SemaphoreType.DMA(()),
            pltpu.SemaphoreType.DMA(()),
        ],
    )
    return pl.pallas_call(
        functools.partial(_dispatch_kernel, tm=tm, rows=rows, n_blk=n_blk),
        grid_spec=grid_spec,
        out_shape=jax.ShapeDtypeStruct((n_blk * rows * slab, LANES), U32),
        compiler_params=_params("arbitrary"),
        name="moe_dispatch",
    )(slot_flat, pad_start, pad_end, n_used, x1p)


def _expert_kernel(blk_e_ref, n_used_ref, xs_ref, w1_ref, w3_ref, w2_ref, y_ref, w1b, w3b, w2b, *, rows):
    i = pl.program_id(0)
    n_used = n_used_ref[0]
    new_expert = jnp.logical_or(i == 0, blk_e_ref[i] != blk_e_ref[jnp.maximum(i - 1, 0)])

    @pl.when(jnp.logical_and(i < n_used, new_expert))
    def _():
        w1b[...] = w1_ref[...].astype(BF16)
        w3b[...] = w3_ref[...].astype(BF16)
        w2b[...] = w2_ref[...].astype(BF16)

    @pl.when(i < n_used)
    def _():
        xb = _from_packed_token_major(xs_ref, rows)
        h1 = jnp.dot(xb, w1b[...], preferred_element_type=F32)
        h3 = jnp.dot(xb, w3b[...], preferred_element_type=F32)
        h = (jax.nn.silu(h1) * h3).astype(BF16)
        _to_token_major(y_ref, jnp.dot(h, w2b[...], preferred_element_type=F32), rows)

    @pl.when(i >= n_used)
    def _():
        y_ref[...] = jnp.zeros_like(y_ref)


def _experts(blk_e, n_used, xs, w1, w3, w2):
    rows = MOE_ROWS
    n_blk = blk_e.shape[0]
    wspec = lambda shape: pl.BlockSpec((None,) + shape, lambda i, be, nu: (be[i], 0, 0))
    grid_spec = pltpu.PrefetchScalarGridSpec(
        num_scalar_prefetch=2,
        grid=(n_blk,),
        in_specs=[
            pl.BlockSpec((rows * PACKED_CHUNKS, LANES), lambda i, be, nu: (jnp.minimum(i, nu[0] - 1), 0)),
            wspec((D_MODEL, D_EXPERT)), wspec((D_MODEL, D_EXPERT)), wspec((D_EXPERT, D_MODEL)),
        ],
        out_specs=pl.BlockSpec((rows * CHUNKS, LANES), lambda i, be, nu: (i, 0)),
        scratch_shapes=[
            pltpu.VMEM((D_MODEL, D_EXPERT), BF16),
            pltpu.VMEM((D_MODEL, D_EXPERT), BF16),
            pltpu.VMEM((D_EXPERT, D_MODEL), BF16),
        ],
    )
    return pl.pallas_call(
        functools.partial(_expert_kernel, rows=rows),
        grid_spec=grid_spec,
        out_shape=jax.ShapeDtypeStruct((n_blk * rows * CHUNKS, LANES), F32),
        compiler_params=_params("arbitrary"),
        name="moe_experts",
    )(blk_e, n_used, xs, w1, w3, w2)


def _combine_kernel(slot_ref, ys_hbm, x1_ref, rw_ref, g2_ref, b2_ref, o_ref, yg, sem, *, tm):
    i = pl.program_id(0)
    n = pl.num_programs(0)
    slot = i % 2
    half = tm * CHUNKS
    group = 8

    def start_gather(tile, slot_):
        def body(g, c):
            base = pl.multiple_of(g * (group * CHUNKS), group * CHUNKS)
            for t in range(group):
                for k in range(2):
                    src_row = slot_ref[tile * (2 * tm) + g * (2 * group) + 2 * t + k] * CHUNKS
                    pltpu.make_async_copy(
                        ys_hbm.at[pl.ds(pl.multiple_of(src_row, CHUNKS), CHUNKS), :],
                        yg.at[slot_, pl.ds(k * half + base + t * CHUNKS, CHUNKS), :],
                        sem.at[slot_]).start()
            return c
        lax.fori_loop(0, tm // group, body, 0)

    @pl.when(i == 0)
    def _():
        start_gather(0, 0)

    @pl.when(i + 1 < n)
    def _():
        start_gather(i + 1, 1 - slot)

    pltpu.make_async_copy(ys_hbm.at[pl.ds(0, 2 * half), :], yg.at[slot], sem.at[slot]).wait()
    rw = rw_ref[...]
    buf = yg.at[slot]
    moe = _from_token_major(buf, 0, tm) * rw[:, 0:1] + _from_token_major(buf, tm, tm) * rw[:, 1:2]
    o_ref[...] = _layer_norm_rows(ALPHA * x1_ref[...] + moe, g2_ref[...], b2_ref[...])


def _combine(slot_flat, ys, x1, rw, ln2_g, ln2_b):
    T = x1.shape[0]
    tm = TM_COMB
    grid_spec = pltpu.PrefetchScalarGridSpec(
        num_scalar_prefetch=1,
        grid=(T // tm,),
        in_specs=[
            pl.BlockSpec(memory_space=pl.ANY),
            pl.BlockSpec((tm, D_MODEL), lambda i, *_: (i, 0)),
            pl.BlockSpec((tm, LANES), lambda i, *_: (i, 0)),
            pl.BlockSpec((1, D_MODEL), lambda i, *_: (0, 0)),
            pl.BlockSpec((1, D_MODEL), lambda i, *_: (0, 0)),
        ],
        out_specs=pl.BlockSpec((tm, D_MODEL), lambda i, *_: (i, 0)),
        scratch_shapes=[
            pltpu.VMEM((2, 2 * tm * CHUNKS, LANES), F32),
            pltpu.SemaphoreType.DMA((2,)),
        ],
    )
    return pl.pallas_call(
        functools.partial(_combine_kernel, tm=tm),
        grid_spec=grid_spec,
        out_shape=jax.ShapeDtypeStruct((T, D_MODEL), F32),
        compiler_params=_params("arbitrary"),
        name="moe_combine_ln",
    )(slot_flat, ys, x1, rw, ln2_g.reshape(1, D_MODEL).astype(F32), ln2_b.reshape(1, D_MODEL).astype(F32))


def _dispatch_plan(counts, pk_flat, n_tokens):
    rows = MOE_ROWS
    experts = jnp.arange(N_EXPERTS, dtype=jnp.int32)
    padded = (counts + rows - 1) // rows * rows
    pad_end = jnp.cumsum(padded).astype(jnp.int32)
    pad_start = (pad_end - padded).astype(jnp.int32)
    n_blk = 2 * n_tokens // rows + N_EXPERTS
    blk_start = jnp.arange(n_blk, dtype=jnp.int32) * rows
    blk_e = jnp.minimum(jnp.sum(blk_start[:, None] >= pad_end[None, :], axis=1), N_EXPERTS - 1).astype(jnp.int32)
    n_used = (pad_end[-1:] // rows).astype(jnp.int32)
    e_flat = pk_flat >> RANK_BITS
    start_of = jnp.sum(jnp.where(e_flat[:, None] == experts[None, :], pad_start[None, :], 0), axis=1)
    slot_flat = (start_of + (pk_flat & (RANK_SPAN - 1))).astype(jnp.int32)
    return slot_flat, pad_start, pad_end, blk_e, n_used, n_blk


def kernel(x, w_in, conv_w, conv_b, lru_wa, lru_ba, lru_wx, lru_bx, lru_lambda, attn_norm_g, lru_norm_g,
           w_out, ln1_g, ln1_b, router_grp_w, router_grp_b, router_exp_w, router_exp_b, w1, w3, w2,
           ln2_g, ln2_b):
    B, S, D = x.shape
    assert D == D_MODEL and S % DILATED_PATTERNS[-1][0] == 0 and w_in.shape[0] == 1
    T = B * S
    x2d = x.reshape(T, D)

    qkv, rg = _in_proj(x2d, w_in[0].astype(BF16))
    attn = _attention(qkv, B, S)
    yrec = _recurrent(rg, conv_w[0], conv_b[0], lru_wa[0], lru_ba[0], lru_wx[0], lru_bx[0], lru_lambda[0],
                      lru_norm_g[0], B, S)

    n_r = N_GROUPS + N_EXPERTS
    w_router = jnp.zeros((D, LANES), F32).at[:, :n_r].set(
        jnp.concatenate([router_grp_w[0], router_exp_w[0]], axis=-1).astype(F32))
    b_router = jnp.zeros((1, LANES), F32).at[0, :n_r].set(
        jnp.concatenate([router_grp_b[0], router_exp_b[0]], axis=-1).astype(F32))
    x1, x1p, pk, rw, cnt = _out_router(attn, yrec.reshape(T, D_LRU), x2d, attn_norm_g[0], w_out[0].astype(BF16),
                                       ln1_g[0], ln1_b[0], w_router, b_router)

    pk_flat = pk[:, :2].reshape(-1)
    slot_flat, pad_start, pad_end, blk_e, n_used, n_blk = _dispatch_plan(cnt[0, :N_EXPERTS], pk_flat, T)
    xs = _dispatch(slot_flat, pad_start, pad_end, n_used, x1p, n_blk)
    ys = _experts(blk_e, n_used, xs, w1[0], w3[0], w2[0])
    out = _combine(slot_flat, ys, x1, rw, ln2_g[0], ln2_b[0])
    return out.reshape(B, S, D)
```

```python
import functools

import jax
import jax.numpy as jnp
import numpy as np
from jax import lax
from jax.experimental import pallas as pl
from jax.experimental.pallas import tpu as pltpu

F32 = jnp.float32
BF16 = jnp.bfloat16

D_MODEL = 2048
D_ATTN = 1024
D_LRU = 1024
HEAD_DIM = 64
N_HEADS = 16
LANES = 128
N_PLANES = D_ATTN // LANES
DILATED_PATTERNS = ((128, 1), (512, 4), (2048, 16))
SUB_WINDOW = 128
LRU_BLOCKS = 16
LRU_BLOCK_DIM = 64
CONV_WIDTH = 4
RG_C = 8.0
N_GROUPS = 4
EXPERTS_PER_GROUP = 8
N_EXPERTS = 32
D_EXPERT = 512
ALPHA = 2.0 ** 0.25
LN_EPS = 1e-5
RMS_EPS = 1e-6
MASKED = -1e30

VMEM_LIMIT = 56 * 1024 * 1024

TM_PROJ = 256
TS_LRU = 256
TM_OUT = 256
MOE_ROWS = 256
TM_COMB = 256
TM_DISP = 256
CHUNKS = D_MODEL // LANES
PACKED_CHUNKS = CHUNKS // 2
RANK_BITS = 16
RANK_SPAN = 1 << RANK_BITS
ATTN_UNROLL = 8


def _params(*sem):
    return pltpu.CompilerParams(dimension_semantics=sem, vmem_limit_bytes=VMEM_LIMIT)


def _in_proj_kernel(x_ref, w_ref, qkv_ref, rg_ref):
    xb = x_ref[...].astype(BF16)
    for c in range(5):
        acc = jnp.dot(xb, w_ref[:, c * D_ATTN:(c + 1) * D_ATTN], preferred_element_type=F32)
        if c == 0:
            acc = acc * (HEAD_DIM ** -0.5)
        out_ref, first = (qkv_ref, c * N_PLANES) if c < 3 else (rg_ref, (c - 3) * N_PLANES)
        for p in range(N_PLANES):
            out_ref[first + p] = acc[:, p * LANES:(p + 1) * LANES]


def _in_proj(x2d, w_in_b):
    T = x2d.shape[0]
    tm = TM_PROJ
    return pl.pallas_call(
        _in_proj_kernel,
        grid=(T // tm,),
        in_specs=[
            pl.BlockSpec((tm, D_MODEL), lambda i: (i, 0)),
            pl.BlockSpec((D_MODEL, 5 * D_ATTN), lambda i: (0, 0), pipeline_mode=pl.Buffered(1)),
        ],
        out_specs=[
            pl.BlockSpec((3 * N_PLANES, tm, LANES), lambda i: (0, i, 0)),
            pl.BlockSpec((2 * N_PLANES, tm, LANES), lambda i: (0, i, 0)),
        ],
        out_shape=[
            jax.ShapeDtypeStruct((3 * N_PLANES, T, LANES), F32),
            jax.ShapeDtypeStruct((2 * N_PLANES, T, LANES), F32),
        ],
        compiler_params=_params("parallel"),
        name="in_proj",
    )(x2d, w_in_b)


CLASSES = 16


def _attn_kernel(slopes_ref, q_ref, k_ref, v_ref, ndf_ref, ndh_ref, o_ref,
                 qs, ks, vs, os_, m_s, l_s, acc_s, bias_f, bias_h, *, seq):
    hp = pl.program_id(1)
    slope_a = slopes_ref[2 * hp]
    slope_b = slopes_ref[2 * hp + 1]
    w = SUB_WINDOW
    cl = seq // CLASSES
    n_pat = len(DILATED_PATTERNS)

    for r in range(CLASSES):
        rows = pl.ds(r * cl, cl)
        strided = pl.ds(r, cl, stride=CLASSES)
        qs[rows, :] = q_ref[strided, :]
        ks[rows, :] = k_ref[strided, :]
        vs[rows, :] = v_ref[strided, :]

    def gather(ref, starts, size):
        return jnp.concatenate([ref[pl.ds(s, size), :] for s in starts], axis=0)

    def scatter(ref, starts, size, val):
        for c, s in enumerate(starts):
            ref[pl.ds(s, size), :] = val[c * size:(c + 1) * size, :]

    for t in range(2 * n_pat):
        bias_f[t, 0:w, :] = slope_a * ndf_ref[t]
        bias_f[t, w:2 * w, :] = slope_b * ndf_ref[t]
    for t in range(n_pat):
        bias_h[t, 0:w, :] = slope_a * ndh_ref[t]
        bias_h[t, w:2 * w, :] = slope_b * ndh_ref[t]

    def block(c, n, d, pi, first, last, keys):
        qc = w * d // CLASSES
        aligned = lambda x: x if isinstance(x, int) else pl.multiple_of(x, 8)
        bases = [(c + d * j) * cl for j in range(CLASSES // d)]
        qstarts = [aligned(b + n * qc) for b in bases]
        if keys == "own":
            kstarts, kc, bias = qstarts, qc, bias_h[pi]
        elif keys == "prev+own":
            kstarts, kc, bias = [aligned(b + (n - 1) * qc) for b in bases], 2 * qc, bias_f[pi]
        else:
            kfirst = jnp.maximum(n - 1, 0) * qc
            kstarts, kc = [aligned(b + kfirst) for b in bases], 2 * qc
            bias = bias_f[jnp.where(n == 0, pi + n_pat, pi)]
        is_a = lax.broadcasted_iota(jnp.int32, (w, LANES), 1) < HEAD_DIM
        q = gather(qs, qstarts, qc)
        k = gather(ks, kstarts, kc).astype(BF16)
        v = gather(vs, kstarts, kc).astype(BF16)
        zero = jnp.zeros_like(q)
        q2 = jnp.concatenate([jnp.where(is_a, q, zero), jnp.where(is_a, zero, q)], axis=0).astype(BF16)
        s = lax.dot_general(q2, k, (((1,), (1,)), ((), ())), preferred_element_type=F32) + bias
        m = jnp.max(s, axis=-1, keepdims=True)
        p = jnp.exp(s - m)
        l = jnp.sum(p, axis=-1, keepdims=True)
        o = jnp.dot(p.astype(BF16), v, preferred_element_type=F32)
        m_c = jnp.where(is_a, m[:w], m[w:])
        l_c = jnp.where(is_a, l[:w], l[w:])
        o_c = jnp.where(is_a, o[:w], o[w:])
        if first:
            scatter(m_s, qstarts, qc, m_c)
            scatter(l_s, qstarts, qc, l_c)
            scatter(acc_s, qstarts, qc, o_c)
            return
        m_o = gather(m_s, qstarts, qc)
        m_n = jnp.maximum(m_o, m_c)
        e_o = jnp.exp(m_o - m_n)
        e_c = jnp.exp(m_c - m_n)
        l_n = gather(l_s, qstarts, qc) * e_o + l_c * e_c
        a_n = gather(acc_s, qstarts, qc) * e_o + o_c * e_c
        if last:
            scatter(os_, qstarts, qc, a_n / l_n)
        else:
            scatter(m_s, qstarts, qc, m_n)
            scatter(l_s, qstarts, qc, l_n)
            scatter(acc_s, qstarts, qc, a_n)

    for pi, (window, d) in enumerate(DILATED_PATTERNS):
        assert window // d == w and CLASSES % d == 0
        first, last = pi == 0, pi == n_pat - 1
        nb = seq // (w * d)

        if nb >= 4:
            def any_block(i, carry, d=d, pi=pi, first=first, last=last):
                block(i % d, i // d, d, pi, first, last, "any")
                return carry
            lax.fori_loop(0, d * nb, any_block, 0, unroll=ATTN_UNROLL)
        else:
            def class_blocks(c, carry, d=d, pi=pi, first=first, last=last, nb=nb):
                block(c, 0, d, pi, first, last, "own")
                for n in range(1, nb):
                    block(c, n, d, pi, first, last, "prev+own")
                return carry
            lax.fori_loop(0, d, class_blocks, 0, unroll=max(ATTN_UNROLL // nb, 1))

    for r in range(CLASSES):
        o_ref[pl.ds(r, cl, stride=CLASSES), :] = os_[pl.ds(r * cl, cl), :]


def _neg_distance_tables():
    w = SUB_WINDOW
    prev, nxt, own = [], [], []
    for _, d in DILATED_PATTERNS:
        n_cls = CLASSES // d
        qc = w // n_cls
        def pos(chunk_rows):
            j, l = np.divmod(np.arange(n_cls * chunk_rows), chunk_rows)
            return n_cls * l + j
        sq, sk2, sk1 = pos(qc)[:, None], pos(2 * qc)[None, :], pos(qc)[None, :]
        for out, dist in ((prev, sq + w - sk2), (nxt, sq - sk2), (own, sq - sk1)):
            valid = (dist >= 0) & (dist <= w)
            out.append(np.where(valid, -(dist * d).astype(np.float32), np.float32(MASKED)))
    return jnp.asarray(np.stack(prev + nxt), dtype=F32), jnp.asarray(np.stack(own), dtype=F32)


def _attention(qkv, batch, seq):
    qkv4 = qkv.reshape(3 * N_PLANES, batch, seq, LANES)
    slopes = jnp.exp2(-8.0 * jnp.arange(1, N_HEADS + 1, dtype=F32) / N_HEADS)
    nd_full, nd_head = _neg_distance_tables()
    n_pat = len(DILATED_PATTERNS)
    plane = lambda off: pl.BlockSpec((None, None, seq, LANES), lambda b, h: (off + h, b, 0, 0))
    return pl.pallas_call(
        functools.partial(_attn_kernel, seq=seq),
        grid=(batch, N_PLANES),
        in_specs=[
            pl.BlockSpec(memory_space=pltpu.SMEM),
            plane(0), plane(N_PLANES), plane(2 * N_PLANES),
            pl.BlockSpec((2 * n_pat, SUB_WINDOW, 2 * SUB_WINDOW), lambda b, h: (0, 0, 0)),
            pl.BlockSpec((n_pat, SUB_WINDOW, SUB_WINDOW), lambda b, h: (0, 0, 0)),
        ],
        out_specs=pl.BlockSpec((None, None, seq, LANES), lambda b, h: (h, b, 0, 0)),
        scratch_shapes=[pltpu.VMEM((seq, LANES), F32)] * 7 + [
            pltpu.VMEM((2 * n_pat, 2 * SUB_WINDOW, 2 * SUB_WINDOW), F32),
            pltpu.VMEM((n_pat, 2 * SUB_WINDOW, SUB_WINDOW), F32),
        ],
        out_shape=jax.ShapeDtypeStruct((N_PLANES, batch, seq, LANES), F32),
        compiler_params=_params("parallel", "parallel"),
        name="dilated_attention",
    )(slopes, qkv4, qkv4, qkv4, nd_full, nd_head)


def _gelu_tanh(x):
    c = np.float32(np.sqrt(2.0 / np.pi))
    return 0.5 * x * (1.0 + jnp.tanh(c * (x + 0.044715 * (x * x * x))))


def _log1p(x):
    u = 1.0 + x
    return jnp.where(u == 1.0, x, jnp.log(u) * x / (u - 1.0))


def _softplus(z):
    return jnp.maximum(z, 0.0) + _log1p(jnp.exp(-jnp.abs(z)))


PHASES = 8


def _lru_kernel(xr_ref, xg_ref, cw_ref, cb_ref, wbd_ref, ba_ref, bx_ref, lam_ref, g_ref, y_ref,
                xc, hc, *, ts):
    i = pl.program_id(1)
    G = ts // PHASES
    first_group = lax.broadcasted_iota(jnp.int32, (G, LANES), 0) == 0
    sub = lax.broadcasted_iota(jnp.int32, (G, LANES), 0)

    @pl.when(i == 0)
    def _():
        xc[...] = jnp.zeros_like(xc)
        hc[...] = jnp.zeros_like(hc)

    def prev_group(cur, carry_row):
        return jnp.where(first_group, carry_row, pltpu.roll(cur, 1, 0))

    conv = []
    for l in range(N_PLANES):
        x = [xr_ref[l, pl.ds(s, G, stride=PHASES), :] for s in range(PHASES)]
        back = {s: prev_group(x[s], xc[l, s:s + 1, :]) for s in range(PHASES - CONV_WIDTH + 1, PHASES)}
        phases = []
        for s in range(PHASES):
            acc = jnp.broadcast_to(cb_ref[l:l + 1, :], (G, LANES))
            for j in range(CONV_WIDTH):
                q = s - (CONV_WIDTH - 1) + j
                acc = acc + (x[q] if q >= 0 else back[q + PHASES]) * cw_ref[j, l:l + 1, :]
            phases.append(acc)
        conv.append(jnp.concatenate(phases, axis=0))
        xc[l] = xr_ref[l, ts - PHASES:ts, :]

    gates_r, gates_i = [], []
    for j in range(N_PLANES // 2):
        yb = jnp.concatenate([conv[2 * j], conv[2 * j + 1]], axis=-1).astype(BF16)
        g = jnp.dot(yb, wbd_ref[j], preferred_element_type=F32)
        gates_r += [g[:, 0:LANES], g[:, LANES:2 * LANES]]
        gates_i += [g[:, 2 * LANES:3 * LANES], g[:, 3 * LANES:4 * LANES]]

    rec, sq = [], None
    for l in range(N_PLANES):
        y = conv[l]
        r = jax.nn.sigmoid(gates_r[l] + ba_ref[l:l + 1, :])
        ig = jax.nn.sigmoid(gates_i[l] + bx_ref[l:l + 1, :])
        log_a = (-RG_C * r) * _softplus(-lam_ref[l:l + 1, :])
        a = jnp.exp(log_a)
        u = jnp.sqrt(1.0 - a * a) * (ig * y)

        piece = lambda v, s: v[s * G:(s + 1) * G, :]
        pa, pb = [piece(a, 0)], [piece(u, 0)]
        for s in range(1, PHASES):
            pa.append(piece(a, s) * pa[-1])
            pb.append(piece(a, s) * pb[-1] + piece(u, s))
        ga, gb = pa[-1], pb[-1]
        sh = 1
        while sh < G:
            take = sub >= sh
            gb = jnp.where(take, ga * pltpu.roll(gb, sh, 0) + gb, gb)
            ga = jnp.where(take, ga * pltpu.roll(ga, sh, 0), ga)
            sh *= 2
        h0 = hc[l, 0:1, :]
        h_end = ga * h0 + gb
        h_in = prev_group(h_end, h0)
        hc[l] = jnp.broadcast_to(h_end[G - 1:G, :], (PHASES, LANES))
        h = jnp.concatenate([pa[s] * h_in + pb[s] for s in range(PHASES)], axis=0)

        xg = jnp.concatenate([xg_ref[l, pl.ds(s, G, stride=PHASES), :] for s in range(PHASES)], axis=0)
        rl = h * _gelu_tanh(xg)
        rec.append(rl)
        part = jnp.sum(rl * rl, axis=-1, keepdims=True)
        sq = part if sq is None else sq + part

    scale = lax.rsqrt(sq * (1.0 / D_LRU) + RMS_EPS)
    for l in range(N_PLANES):
        out = rec[l] * scale * g_ref[l:l + 1, :]
        for s in range(PHASES):
            y_ref[l, pl.ds(s, G, stride=PHASES), :] = out[s * G:(s + 1) * G, :]


def _block_diag_gates(wa, wx):
    def bd(wm):
        wm = wm.reshape(4, 4, LRU_BLOCK_DIM, LRU_BLOCK_DIM)
        eye = jnp.eye(4, dtype=wm.dtype)
        full = jnp.einsum("gide,ij->gidje", wm, eye)
        return full.reshape(4, 4 * LRU_BLOCK_DIM, 4 * LRU_BLOCK_DIM)
    return jnp.concatenate([bd(wa), bd(wx)], axis=-1).astype(BF16)


def _recurrent(rg, conv_w, conv_b, lru_wa, lru_ba, lru_wx, lru_bx, lru_lambda, lru_norm_g, batch, seq):
    ts = TS_LRU
    rg5 = rg.reshape(2, N_PLANES, batch, seq, LANES)
    wbd = _block_diag_gates(lru_wa, lru_wx)
    planes = lambda a: a.reshape(N_PLANES, LANES).astype(F32)
    vec = pl.BlockSpec((N_PLANES, LANES), lambda b, i: (0, 0))
    return pl.pallas_call(
        functools.partial(_lru_kernel, ts=ts),
        grid=(batch, seq // ts),
        in_specs=[
            pl.BlockSpec((None, N_PLANES, None, ts, LANES), lambda b, i: (0, 0, b, i, 0)),
            pl.BlockSpec((None, N_PLANES, None, ts, LANES), lambda b, i: (1, 0, b, i, 0)),
            pl.BlockSpec((CONV_WIDTH, N_PLANES, LANES), lambda b, i: (0, 0, 0)),
            vec,
            pl.BlockSpec((4, 4 * LRU_BLOCK_DIM, 8 * LRU_BLOCK_DIM), lambda b, i: (0, 0, 0)),
            vec, vec, vec, vec,
        ],
        out_specs=pl.BlockSpec((N_PLANES, None, ts, LANES), lambda b, i: (0, b, i, 0)),
        out_shape=jax.ShapeDtypeStruct((N_PLANES, batch, seq, LANES), F32),
        scratch_shapes=[
            pltpu.VMEM((N_PLANES, PHASES, LANES), F32),
            pltpu.VMEM((N_PLANES, PHASES, LANES), F32),
        ],
        compiler_params=_params("parallel", "arbitrary"),
        name="conv_rglru",
    )(rg5, rg5, conv_w.astype(F32).reshape(CONV_WIDTH, N_PLANES, LANES), planes(conv_b), wbd, planes(lru_ba),
      planes(lru_bx), planes(lru_lambda), planes(lru_norm_g))


def _layer_norm_rows(z, g, b):
    mu = jnp.mean(z, axis=-1, keepdims=True)
    zc = z - mu
    var = jnp.mean(zc * zc, axis=-1, keepdims=True)
    return zc * lax.rsqrt(var + LN_EPS) * g + b


def _split_bf16(a):
    hi = a.astype(BF16)
    lo = (a - hi.astype(F32)).astype(BF16)
    return hi, lo


U32 = jnp.uint32
HI_HALF = np.uint32(0xFFFF0000)


def _to_packed_token_major(ref, val, n_rows):
    bits = lambda a: lax.bitcast_convert_type(a.astype(BF16).astype(F32), U32)
    half = PACKED_CHUNKS * LANES
    for c in range(PACKED_CHUNKS):
        lo = bits(val[:, c * LANES:(c + 1) * LANES]) >> 16
        hi = bits(val[:, half + c * LANES:half + (c + 1) * LANES]) & HI_HALF
        ref[pl.ds(c, n_rows, stride=PACKED_CHUNKS), :] = lo | hi


def _from_packed_token_major(ref, first_row, n_rows):
    words = [ref[pl.ds(first_row * PACKED_CHUNKS + c, n_rows, stride=PACKED_CHUNKS), :]
             for c in range(PACKED_CHUNKS)]
    lo = [lax.bitcast_convert_type(wd << 16, F32) for wd in words]
    hi = [lax.bitcast_convert_type(wd & HI_HALF, F32) for wd in words]
    return jnp.concatenate(lo + hi, axis=-1)


def _out_router_kernel(attn_ref, yrec_ref, x_ref, ga_ref, wo_ref, g1_ref, b1_ref, wr_ref, br_ref,
                       x1_ref, x1p_ref, pk_ref, rw_ref, cnt_ref, cnt_s, *, tm):
    attn = jnp.concatenate([attn_ref[p] for p in range(N_PLANES)], axis=-1)
    ms = jnp.mean(attn * attn, axis=-1, keepdims=True)
    ya = (attn * lax.rsqrt(ms + RMS_EPS) * ga_ref[...]).astype(BF16)
    mix = jnp.dot(ya, wo_ref[0:D_ATTN, :], preferred_element_type=F32)
    yr = jnp.concatenate([yrec_ref[p] for p in range(N_PLANES)], axis=-1).astype(BF16)
    mix = mix + jnp.dot(yr, wo_ref[D_ATTN:, :], preferred_element_type=F32)
    x1 = _layer_norm_rows(ALPHA * x_ref[...] + mix, g1_ref[...], b1_ref[...])
    x1_ref[...] = x1
    _to_packed_token_major(x1p_ref, x1, tm)

    x_hi, x_lo = _split_bf16(x1)
    w_hi, w_lo = _split_bf16(wr_ref[...])
    logits = (jnp.dot(x_hi, w_hi, preferred_element_type=F32)
              + jnp.dot(x_lo, w_hi, preferred_element_type=F32)
              + jnp.dot(x_hi, w_lo, preferred_element_type=F32)) + br_ref[...]

    lane = lax.broadcasted_iota(jnp.int32, (tm, LANES), 1)
    big = jnp.int32(LANES)
    first_true = lambda c: jnp.min(jnp.where(c, lane, big), axis=-1, keepdims=True)

    in_g = lane < N_GROUPS
    gl = jnp.where(in_g, logits, MASKED)
    gmax = jnp.max(gl, axis=-1, keepdims=True)
    g_idx = first_true(gl == gmax)
    gsum = jnp.sum(jnp.where(in_g, jnp.exp(gl - gmax), 0.0), axis=-1, keepdims=True)
    g_gate = 1.0 / gsum

    lo = N_GROUPS + EXPERTS_PER_GROUP * g_idx
    in_e = (lane >= lo) & (lane < lo + EXPERTS_PER_GROUP)
    el = jnp.where(in_e, logits, MASKED)
    emax = jnp.max(el, axis=-1, keepdims=True)
    ee = jnp.where(in_e, jnp.exp(el - emax), 0.0)
    pe = ee / jnp.sum(ee, axis=-1, keepdims=True)
    cand = jnp.where(in_e, pe, -1.0)
    v1 = jnp.max(cand, axis=-1, keepdims=True)
    i1 = first_true(cand == v1)
    cand2 = jnp.where(lane == i1, -1.0, cand)
    v2 = jnp.max(cand2, axis=-1, keepdims=True)
    i2 = first_true(cand2 == v2)
    den = v1 + v2
    w1 = g_gate * v1 / den
    w2 = g_gate * v2 / den
    e1 = i1 - N_GROUPS
    e2 = i2 - N_GROUPS
    rw_ref[...] = jnp.where(lane == 0, w1, jnp.where(lane == 1, w2, 0.0))

    @pl.when(pl.program_id(0) == 0)
    def _():
        cnt_s[...] = jnp.zeros_like(cnt_s)

    hot1 = lane == e1
    hot2 = lane == e2
    both = jnp.where(hot1 | hot2, 1.0, 0.0)
    earlier = (lax.broadcasted_iota(jnp.int32, (tm, tm), 1) < lax.broadcasted_iota(jnp.int32, (tm, tm), 0))
    before = jnp.dot(earlier.astype(BF16), both.astype(BF16), preferred_element_type=F32) + cnt_s[0:1, :]
    r1 = jnp.sum(jnp.where(hot1, before, 0.0), axis=-1, keepdims=True).astype(jnp.int32)
    r2 = jnp.sum(jnp.where(hot2, before, 0.0), axis=-1, keepdims=True).astype(jnp.int32)
    pk_ref[...] = jnp.where(lane == 0, e1 * RANK_SPAN + r1, jnp.where(lane == 1, e2 * RANK_SPAN + r2, 0))
    cnt_s[...] = cnt_s[...] + jnp.sum(both, axis=0, keepdims=True)
    cnt_ref[...] = cnt_s[...].astype(jnp.int32)


def _out_router(attn, yrec, x2d, attn_norm_g, w_out_b, ln1_g, ln1_b, w_router, b_router):
    T = x2d.shape[0]
    tm = TM_OUT
    assert 2 * T <= RANK_SPAN
    attn3 = attn.reshape(N_PLANES, T, LANES)
    yrec3 = yrec.reshape(N_PLANES, T, LANES)
    const = lambda shape: pl.BlockSpec(shape, lambda i: (0,) * len(shape))
    return pl.pallas_call(
        functools.partial(_out_router_kernel, tm=tm),
        grid=(T // tm,),
        in_specs=[
            pl.BlockSpec((N_PLANES, tm, LANES), lambda i: (0, i, 0)),
            pl.BlockSpec((N_PLANES, tm, LANES), lambda i: (0, i, 0)),
            pl.BlockSpec((tm, D_MODEL), lambda i: (i, 0)),
            const((1, D_ATTN)),
            pl.BlockSpec((D_MODEL, D_MODEL), lambda i: (0, 0), pipeline_mode=pl.Buffered(1)),
            const((1, D_MODEL)), const((1, D_MODEL)),
            const((D_MODEL, LANES)), const((1, LANES)),
        ],
        out_specs=[
            pl.BlockSpec((tm, D_MODEL), lambda i: (i, 0)),
            pl.BlockSpec((tm * PACKED_CHUNKS, LANES), lambda i: (i, 0)),
            pl.BlockSpec((tm, LANES), lambda i: (i, 0)),
            pl.BlockSpec((tm, LANES), lambda i: (i, 0)),
            const((8, LANES)),
        ],
        out_shape=[
            jax.ShapeDtypeStruct((T, D_MODEL), F32),
            jax.ShapeDtypeStruct((T * PACKED_CHUNKS, LANES), U32),
            jax.ShapeDtypeStruct((T, LANES), jnp.int32),
            jax.ShapeDtypeStruct((T, LANES), F32),
            jax.ShapeDtypeStruct((8, LANES), jnp.int32),
        ],
        scratch_shapes=[pltpu.VMEM((8, LANES), F32)],
        compiler_params=_params("arbitrary"),
        name="out_proj_router",
    )(attn3, yrec3, x2d, attn_norm_g.reshape(1, D_ATTN).astype(F32), w_out_b,
      ln1_g.reshape(1, D_MODEL).astype(F32), ln1_b.reshape(1, D_MODEL).astype(F32), w_router, b_router)


def _dispatch_kernel(slot_ref, pstart_ref, pend_ref, nused_ref, x1p_ref, xs_hbm, zbuf, sem, zsem,
                     *, tm, rows, n_blk):
    i = pl.program_id(0)
    slab = PACKED_CHUNKS
    blk = rows * slab

    @pl.when(i == 0)
    def _():
        zbuf[...] = jnp.zeros_like(zbuf)
        n_used = nused_ref[0]

        def zero_copy(b):
            return pltpu.make_async_copy(zbuf, xs_hbm.at[pl.ds(pl.multiple_of(b * blk, blk), blk), :], zsem)

        def for_each_zeroed_block(fn):
            for e in range(N_EXPERTS):
                @pl.when(pend_ref[e] > pstart_ref[e])
                def _(e=e):
                    fn(pend_ref[e] // rows - 1)

            def tail(b, c):
                fn(b)
                return c
            lax.fori_loop(n_used, n_blk, tail, 0)

        for_each_zeroed_block(lambda b: zero_copy(b).start())
        for_each_zeroed_block(lambda b: zero_copy(b).wait())

    group = 8

    def push(g, c):
        for t in range(group):
            src = x1p_ref.at[pl.ds(pl.multiple_of(g * (group * slab), group * slab) + t * slab, slab), :]
            for k in range(2):
                row = slot_ref[i * (2 * tm) + g * (2 * group) + 2 * t + k] * slab
                dst = xs_hbm.at[pl.ds(pl.multiple_of(row, slab), slab), :]
                pltpu.make_async_copy(src, dst, sem).start(priority=k)
        return c
    lax.fori_loop(0, tm // group, push, 0)
    for _ in range(2):
        pltpu.make_async_copy(x1p_ref, xs_hbm.at[pl.ds(0, tm * slab), :], sem).wait()


def _dispatch(slot_flat, pad_start, pad_end, n_used, x1p, n_blk):
    tm, rows, slab = TM_DISP, MOE_ROWS, PACKED_CHUNKS
    n_tok = x1p.shape[0] // slab
    grid_spec = pltpu.PrefetchScalarGridSpec(
        num_scalar_prefetch=4,
        grid=(n_tok // tm,),
        in_specs=[pl.BlockSpec((tm * slab, LANES), lambda i, *_: (i, 0))],
        out_specs=pl.BlockSpec(memory_space=pl.ANY),
        scratch_shapes=[
            pltpu.VMEM((rows * slab, LANES), U32),
            pltpu.SemaphoreType.DMA(()),
            pltpu.SemaphoreType.DMA(()),
        ],
    )
    return pl.pallas_call(
        functools.partial(_dispatch_kernel, tm=tm, rows=rows, n_blk=n_blk),
        grid_spec=grid_spec,
        out_shape=jax.ShapeDtypeStruct((n_blk * rows * slab, LANES), U32),
        compiler_params=_params("arbitrary"),
        name="moe_dispatch",
    )(slot_flat, pad_start, pad_end, n_used, x1p)


def _expert_kernel(blk_e_ref, n_used_ref, xs_ref, w1_ref, w3_ref, w2_ref, y_ref, w1b, w3b, w2b, *, rows):
    i = pl.program_id(0)
    n_used = n_used_ref[0]
    new_expert = jnp.logical_or(i == 0, blk_e_ref[i] != blk_e_ref[jnp.maximum(i - 1, 0)])

    @pl.when(jnp.logical_and(i < n_used, new_expert))
    def _():
        w1b[...] = w1_ref[...].astype(BF16)
        w3b[...] = w3_ref[...].astype(BF16)
        w2b[...] = w2_ref[...].astype(BF16)

    @pl.when(i < n_used)
    def _():
        xb = _from_packed_token_major(xs_ref, 0, rows).astype(BF16)
        h1 = jnp.dot(xb, w1b[...], preferred_element_type=F32)
        h3 = jnp.dot(xb, w3b[...], preferred_element_type=F32)
        h = (jax.nn.silu(h1) * h3).astype(BF16)
        _to_packed_token_major(y_ref, jnp.dot(h, w2b[...], preferred_element_type=F32), rows)

    @pl.when(i >= n_used)
    def _():
        y_ref[...] = jnp.zeros_like(y_ref)


def _experts(blk_e, n_used, xs, w1, w3, w2):
    rows = MOE_ROWS
    n_blk = blk_e.shape[0]
    wspec = lambda shape: pl.BlockSpec((None,) + shape, lambda i, be, nu: (be[i], 0, 0))
    grid_spec = pltpu.PrefetchScalarGridSpec(
        num_scalar_prefetch=2,
        grid=(n_blk,),
        in_specs=[
            pl.BlockSpec((rows * PACKED_CHUNKS, LANES), lambda i, be, nu: (jnp.minimum(i, nu[0] - 1), 0)),
            wspec((D_MODEL, D_EXPERT)), wspec((D_MODEL, D_EXPERT)), wspec((D_EXPERT, D_MODEL)),
        ],
        out_specs=pl.BlockSpec((rows * PACKED_CHUNKS, LANES), lambda i, be, nu: (i, 0)),
        scratch_shapes=[
            pltpu.VMEM((D_MODEL, D_EXPERT), BF16),
            pltpu.VMEM((D_MODEL, D_EXPERT), BF16),
            pltpu.VMEM((D_EXPERT, D_MODEL), BF16),
        ],
    )
    return pl.pallas_call(
        functools.partial(_expert_kernel, rows=rows),
        grid_spec=grid_spec,
        out_shape=jax.ShapeDtypeStruct((n_blk * rows * PACKED_CHUNKS, LANES), U32),
        compiler_params=_params("arbitrary"),
        name="moe_experts",
    )(blk_e, n_used, xs, w1, w3, w2)


def _combine_kernel(slot_ref, ys_hbm, x1_ref, rw_ref, g2_ref, b2_ref, o_ref, yg, sem, *, tm):
    i = pl.program_id(0)
    n = pl.num_programs(0)
    slot = i % 2
    slab = PACKED_CHUNKS
    half = tm * slab
    group = 8

    def start_gather(tile, slot_):
        def body(g, c):
            base = pl.multiple_of(g * (group * slab), group * slab)
            for t in range(group):
                for k in range(2):
                    src_row = slot_ref[tile * (2 * tm) + g * (2 * group) + 2 * t + k] * slab
                    pltpu.make_async_copy(
                        ys_hbm.at[pl.ds(pl.multiple_of(src_row, slab), slab), :],
                        yg.at[slot_, pl.ds(k * half + base + t * slab, slab), :],
                        sem.at[slot_]).start(priority=k)
            return c
        lax.fori_loop(0, tm // group, body, 0)

    @pl.when(i == 0)
    def _():
        start_gather(0, 0)

    @pl.when(i + 1 < n)
    def _():
        start_gather(i + 1, 1 - slot)

    pltpu.make_async_copy(ys_hbm.at[pl.ds(0, 2 * half), :], yg.at[slot], sem.at[slot]).wait()
    rw = rw_ref[...]
    buf = yg.at[slot]
    moe = (_from_packed_token_major(buf, 0, tm) * rw[:, 0:1]
           + _from_packed_token_major(buf, tm, tm) * rw[:, 1:2])
    o_ref[...] = _layer_norm_rows(ALPHA * x1_ref[...] + moe, g2_ref[...], b2_ref[...])


def _combine(slot_flat, ys, x1, rw, ln2_g, ln2_b):
    T = x1.shape[0]
    tm = TM_COMB
    grid_spec = pltpu.PrefetchScalarGridSpec(
        num_scalar_prefetch=1,
        grid=(T // tm,),
        in_specs=[
            pl.BlockSpec(memory_space=pl.ANY),
            pl.BlockSpec((tm, D_MODEL), lambda i, *_: (i, 0)),
            pl.BlockSpec((tm, LANES), lambda i, *_: (i, 0)),
            pl.BlockSpec((1, D_MODEL), lambda i, *_: (0, 0)),
            pl.BlockSpec((1, D_MODEL), lambda i, *_: (0, 0)),
        ],
        out_specs=pl.BlockSpec((tm, D_MODEL), lambda i, *_: (i, 0)),
        scratch_shapes=[
            pltpu.VMEM((2, 2 * tm * PACKED_CHUNKS, LANES), U32),
            pltpu.SemaphoreType.DMA((2,)),
        ],
    )
    return pl.pallas_call(
        functools.partial(_combine_kernel, tm=tm),
        grid_spec=grid_spec,
        out_shape=jax.ShapeDtypeStruct((T, D_MODEL), F32),
        compiler_params=_params("arbitrary"),
        name="moe_combine_ln",
    )(slot_flat, ys, x1, rw, ln2_g.reshape(1, D_MODEL).astype(F32), ln2_b.reshape(1, D_MODEL).astype(F32))


def _dispatch_plan(counts, pk_flat, n_tokens):
    rows = MOE_ROWS
    experts = jnp.arange(N_EXPERTS, dtype=jnp.int32)
    padded = (counts + rows - 1) // rows * rows
    pad_end = jnp.cumsum(padded).astype(jnp.int32)
    pad_start = (pad_end - padded).astype(jnp.int32)
    n_blk = 2 * n_tokens // rows + N_EXPERTS
    blk_start = jnp.arange(n_blk, dtype=jnp.int32) * rows
    blk_e = jnp.minimum(jnp.sum(blk_start[:, None] >= pad_end[None, :], axis=1), N_EXPERTS - 1).astype(jnp.int32)
    n_used = (pad_end[-1:] // rows).astype(jnp.int32)
    e_flat = pk_flat >> RANK_BITS
    start_of = jnp.sum(jnp.where(e_flat[:, None] == experts[None, :], pad_start[None, :], 0), axis=1)
    slot_flat = (start_of + (pk_flat & (RANK_SPAN - 1))).astype(jnp.int32)
    return slot_flat, pad_start, pad_end, blk_e, n_used, n_blk


def kernel(x, w_in, conv_w, conv_b, lru_wa, lru_ba, lru_wx, lru_bx, lru_lambda, attn_norm_g, lru_norm_g,
           w_out, ln1_g, ln1_b, router_grp_w, router_grp_b, router_exp_w, router_exp_b, w1, w3, w2,
           ln2_g, ln2_b):
    B, S, D = x.shape
    assert D == D_MODEL and S % DILATED_PATTERNS[-1][0] == 0 and w_in.shape[0] == 1
    T = B * S
    x2d = x.reshape(T, D)

    qkv, rg = _in_proj(x2d, w_in[0].astype(BF16))
    attn = _attention(qkv, B, S)
    yrec = _recurrent(rg, conv_w[0], conv_b[0], lru_wa[0], lru_ba[0], lru_wx[0], lru_bx[0], lru_lambda[0],
                      lru_norm_g[0], B, S)

    n_r = N_GROUPS + N_EXPERTS
    w_router = jnp.zeros((D, LANES), F32).at[:, :n_r].set(
        jnp.concatenate([router_grp_w[0], router_exp_w[0]], axis=-1).astype(F32))
    b_router = jnp.zeros((1, LANES), F32).at[0, :n_r].set(
        jnp.concatenate([router_grp_b[0], router_exp_b[0]], axis=-1).astype(F32))
    x1, x1p, pk, rw, cnt = _out_router(attn, yrec, x2d, attn_norm_g[0], w_out[0].astype(BF16),
                                       ln1_g[0], ln1_b[0], w_router, b_router)

    pk_flat = pk[:, :2].reshape(-1)
    slot_flat, pad_start, pad_end, blk_e, n_used, n_blk = _dispatch_plan(cnt[0, :N_EXPERTS], pk_flat, T)
    xs = _dispatch(slot_flat, pad_start, pad_end, n_used, x1p, n_blk)
    ys = _experts(blk_e, n_used, xs, w1[0], w3[0], w2[0])
    out = _combine(slot_flat, ys, x1, rw, ln2_g[0], ln2_b[0])
    return out.reshape(B, S, D)
```

```python
import functools

import jax
import jax.numpy as jnp
import numpy as np
from jax import lax
from jax.experimental import pallas as pl
from jax.experimental.pallas import tpu as pltpu

F32 = jnp.float32
BF16 = jnp.bfloat16

D_MODEL = 2048
D_ATTN = 1024
D_LRU = 1024
HEAD_DIM = 64
N_HEADS = 16
LANES = 128
N_PLANES = D_ATTN // LANES
DILATED_PATTERNS = ((128, 1), (512, 4), (2048, 16))
SUB_WINDOW = 128
LRU_BLOCKS = 16
LRU_BLOCK_DIM = 64
CONV_WIDTH = 4
RG_C = 8.0
N_GROUPS = 4
EXPERTS_PER_GROUP = 8
N_EXPERTS = 32
D_EXPERT = 512
ALPHA = 2.0 ** 0.25
LN_EPS = 1e-5
RMS_EPS = 1e-6
MASKED = -1e30

VMEM_LIMIT = 56 * 1024 * 1024

TM_PROJ = 256
TS_LRU = 256
TM_OUT = 256
MOE_ROWS = 256
TM_COMB = 256
TM_DISP = 256
CHUNKS = D_MODEL // LANES
PACKED_CHUNKS = CHUNKS // 2
RANK_BITS = 16
RANK_SPAN = 1 << RANK_BITS
ATTN_UNROLL = 8


def _params(*sem):
    return pltpu.CompilerParams(dimension_semantics=sem, vmem_limit_bytes=VMEM_LIMIT)


def _in_proj_kernel(x_ref, w_ref, qkv_ref, rg_ref):
    xb = x_ref[...].astype(BF16)
    for c in range(5):
        acc = jnp.dot(xb, w_ref[:, c * D_ATTN:(c + 1) * D_ATTN], preferred_element_type=F32)
        if c == 0:
            acc = acc * (HEAD_DIM ** -0.5)
        out_ref, first = (qkv_ref, c * N_PLANES) if c < 3 else (rg_ref, (c - 3) * N_PLANES)
        for p in range(N_PLANES):
            out_ref[first + p] = acc[:, p * LANES:(p + 1) * LANES]


def _in_proj(x2d, w_in_b):
    T = x2d.shape[0]
    tm = TM_PROJ
    return pl.pallas_call(
        _in_proj_kernel,
        grid=(T // tm,),
        in_specs=[
            pl.BlockSpec((tm, D_MODEL), lambda i: (i, 0)),
            pl.BlockSpec((D_MODEL, 5 * D_ATTN), lambda i: (0, 0), pipeline_mode=pl.Buffered(1)),
        ],
        out_specs=[
            pl.BlockSpec((3 * N_PLANES, tm, LANES), lambda i: (0, i, 0)),
            pl.BlockSpec((2 * N_PLANES, tm, LANES), lambda i: (0, i, 0)),
        ],
        out_shape=[
            jax.ShapeDtypeStruct((3 * N_PLANES, T, LANES), F32),
            jax.ShapeDtypeStruct((2 * N_PLANES, T, LANES), F32),
        ],
        compiler_params=_params("parallel"),
        name="in_proj",
    )(x2d, w_in_b)


CLASSES = 16


def _attn_kernel(slopes_ref, q_ref, k_ref, v_ref, ndf_ref, ndh_ref, o_ref,
                 qs, ks, vs, os_, m_s, l_s, acc_s, bias_f, bias_h, *, seq):
    hp = pl.program_id(1)
    slope_a = slopes_ref[2 * hp]
    slope_b = slopes_ref[2 * hp + 1]
    w = SUB_WINDOW
    cl = seq // CLASSES
    n_pat = len(DILATED_PATTERNS)

    for r in range(CLASSES):
        rows = pl.ds(r * cl, cl)
        strided = pl.ds(r, cl, stride=CLASSES)
        qs[rows, :] = q_ref[strided, :]
        ks[rows, :] = k_ref[strided, :]
        vs[rows, :] = v_ref[strided, :]

    def gather(ref, starts, size):
        return jnp.concatenate([ref[pl.ds(s, size), :] for s in starts], axis=0)

    def scatter(ref, starts, size, val):
        for c, s in enumerate(starts):
            ref[pl.ds(s, size), :] = val[c * size:(c + 1) * size, :]

    for t in range(2 * n_pat):
        bias_f[t, 0:w, :] = slope_a * ndf_ref[t]
        bias_f[t, w:2 * w, :] = slope_b * ndf_ref[t]
    for t in range(n_pat):
        bias_h[t, 0:w, :] = slope_a * ndh_ref[t]
        bias_h[t, w:2 * w, :] = slope_b * ndh_ref[t]

    def block(c, n, d, pi, first, last, keys):
        qc = w * d // CLASSES
        aligned = lambda x: x if isinstance(x, int) else pl.multiple_of(x, 8)
        bases = [(c + d * j) * cl for j in range(CLASSES // d)]
        qstarts = [aligned(b + n * qc) for b in bases]
        if keys == "own":
            kstarts, kc, bias = qstarts, qc, bias_h[pi]
        elif keys == "prev+own":
            kstarts, kc, bias = [aligned(b + (n - 1) * qc) for b in bases], 2 * qc, bias_f[pi]
        else:
            kfirst = jnp.maximum(n - 1, 0) * qc
            kstarts, kc = [aligned(b + kfirst) for b in bases], 2 * qc
            bias = bias_f[jnp.where(n == 0, pi + n_pat, pi)]
        is_a = lax.broadcasted_iota(jnp.int32, (w, LANES), 1) < HEAD_DIM
        q = gather(qs, qstarts, qc)
        k = gather(ks, kstarts, kc).astype(BF16)
        v = gather(vs, kstarts, kc).astype(BF16)
        zero = jnp.zeros_like(q)
        q2 = jnp.concatenate([jnp.where(is_a, q, zero), jnp.where(is_a, zero, q)], axis=0).astype(BF16)
        s = lax.dot_general(q2, k, (((1,), (1,)), ((), ())), preferred_element_type=F32) + bias
        m = jnp.max(s, axis=-1, keepdims=True)
        p = jnp.exp(s - m)
        l = jnp.sum(p, axis=-1, keepdims=True)
        o = jnp.dot(p.astype(BF16), v, preferred_element_type=F32)
        m_c = jnp.where(is_a, m[:w], m[w:])
        l_c = jnp.where(is_a, l[:w], l[w:])
        o_c = jnp.where(is_a, o[:w], o[w:])
        if first:
            scatter(m_s, qstarts, qc, m_c)
            scatter(l_s, qstarts, qc, l_c)
            scatter(acc_s, qstarts, qc, o_c)
            return
        m_o = gather(m_s, qstarts, qc)
        m_n = jnp.maximum(m_o, m_c)
        e_o = jnp.exp(m_o - m_n)
        e_c = jnp.exp(m_c - m_n)
        l_n = gather(l_s, qstarts, qc) * e_o + l_c * e_c
        a_n = gather(acc_s, qstarts, qc) * e_o + o_c * e_c
        if last:
            scatter(os_, qstarts, qc, a_n / l_n)
        else:
            scatter(m_s, qstarts, qc, m_n)
            scatter(l_s, qstarts, qc, l_n)
            scatter(acc_s, qstarts, qc, a_n)

    for pi, (window, d) in enumerate(DILATED_PATTERNS):
        assert window // d == w and CLASSES % d == 0
        first, last = pi == 0, pi == n_pat - 1
        nb = seq // (w * d)

        if nb >= 4:
            def any_block(i, carry, d=d, pi=pi, first=first, last=last):
                block(i % d, i // d, d, pi, first, last, "any")
                return carry
            lax.fori_loop(0, d * nb, any_block, 0, unroll=ATTN_UNROLL)
        else:
            def class_blocks(c, carry, d=d, pi=pi, first=first, last=last, nb=nb):
                block(c, 0, d, pi, first, last, "own")
                for n in range(1, nb):
                    block(c, n, d, pi, first, last, "prev+own")
                return carry
            lax.fori_loop(0, d, class_blocks, 0, unroll=max(ATTN_UNROLL // nb, 1))

    for r in range(CLASSES):
        o_ref[pl.ds(r, cl, stride=CLASSES), :] = os_[pl.ds(r * cl, cl), :]


def _neg_distance_tables():
    w = SUB_WINDOW
    prev, nxt, own = [], [], []
    for _, d in DILATED_PATTERNS:
        n_cls = CLASSES // d
        qc = w // n_cls
        def pos(chunk_rows):
            j, l = np.divmod(np.arange(n_cls * chunk_rows), chunk_rows)
            return n_cls * l + j
        sq, sk2, sk1 = pos(qc)[:, None], pos(2 * qc)[None, :], pos(qc)[None, :]
        for out, dist in ((prev, sq + w - sk2), (nxt, sq - sk2), (own, sq - sk1)):
            valid = (dist >= 0) & (dist <= w)
            out.append(np.where(valid, -(dist * d).astype(np.float32), np.float32(MASKED)))
    return jnp.asarray(np.stack(prev + nxt), dtype=F32), jnp.asarray(np.stack(own), dtype=F32)


def _attention(qkv, batch, seq):
    qkv4 = qkv.reshape(3 * N_PLANES, batch, seq, LANES)
    slopes = jnp.exp2(-8.0 * jnp.arange(1, N_HEADS + 1, dtype=F32) / N_HEADS)
    nd_full, nd_head = _neg_distance_tables()
    n_pat = len(DILATED_PATTERNS)
    plane = lambda off: pl.BlockSpec((None, None, seq, LANES), lambda b, h: (off + h, b, 0, 0))
    return pl.pallas_call(
        functools.partial(_attn_kernel, seq=seq),
        grid=(batch, N_PLANES),
        in_specs=[
            pl.BlockSpec(memory_space=pltpu.SMEM),
            plane(0), plane(N_PLANES), plane(2 * N_PLANES),
            pl.BlockSpec((2 * n_pat, SUB_WINDOW, 2 * SUB_WINDOW), lambda b, h: (0, 0, 0)),
            pl.BlockSpec((n_pat, SUB_WINDOW, SUB_WINDOW), lambda b, h: (0, 0, 0)),
        ],
        out_specs=pl.BlockSpec((None, None, seq, LANES), lambda b, h: (h, b, 0, 0)),
        scratch_shapes=[pltpu.VMEM((seq, LANES), F32)] * 7 + [
            pltpu.VMEM((2 * n_pat, 2 * SUB_WINDOW, 2 * SUB_WINDOW), F32),
            pltpu.VMEM((n_pat, 2 * SUB_WINDOW, SUB_WINDOW), F32),
        ],
        out_shape=jax.ShapeDtypeStruct((N_PLANES, batch, seq, LANES), F32),
        compiler_params=_params("parallel", "parallel"),
        name="dilated_attention",
    )(slopes, qkv4, qkv4, qkv4, nd_full, nd_head)


def _gelu_tanh(x):
    c = np.float32(np.sqrt(2.0 / np.pi))
    return 0.5 * x * (1.0 + jnp.tanh(c * (x + 0.044715 * (x * x * x))))


def _log1p(x):
    u = 1.0 + x
    return jnp.where(u == 1.0, x, jnp.log(u) * x / (u - 1.0))


def _softplus(z):
    return jnp.maximum(z, 0.0) + _log1p(jnp.exp(-jnp.abs(z)))


PHASES = 8


def _lru_kernel(xr_ref, xg_ref, cw_ref, cb_ref, wbd_ref, ba_ref, bx_ref, lam_ref, g_ref, y_ref,
                xc, hc, *, ts):
    i = pl.program_id(1)
    G = ts // PHASES
    first_group = lax.broadcasted_iota(jnp.int32, (G, LANES), 0) == 0
    sub = lax.broadcasted_iota(jnp.int32, (G, LANES), 0)

    @pl.when(i == 0)
    def _():
        xc[...] = jnp.zeros_like(xc)
        hc[...] = jnp.zeros_like(hc)

    def prev_group(cur, carry_row):
        return jnp.where(first_group, carry_row, pltpu.roll(cur, 1, 0))

    conv = []
    for l in range(N_PLANES):
        x = [xr_ref[l, pl.ds(s, G, stride=PHASES), :] for s in range(PHASES)]
        back = {s: prev_group(x[s], xc[l, s:s + 1, :]) for s in range(PHASES - CONV_WIDTH + 1, PHASES)}
        phases = []
        for s in range(PHASES):
            acc = jnp.broadcast_to(cb_ref[l:l + 1, :], (G, LANES))
            for j in range(CONV_WIDTH):
                q = s - (CONV_WIDTH - 1) + j
                acc = acc + (x[q] if q >= 0 else back[q + PHASES]) * cw_ref[j, l:l + 1, :]
            phases.append(acc)
        conv.append(jnp.concatenate(phases, axis=0))
        xc[l] = xr_ref[l, ts - PHASES:ts, :]

    gates_r, gates_i = [], []
    for j in range(N_PLANES // 2):
        yb = jnp.concatenate([conv[2 * j], conv[2 * j + 1]], axis=-1).astype(BF16)
        g = jnp.dot(yb, wbd_ref[j], preferred_element_type=F32)
        gates_r += [g[:, 0:LANES], g[:, LANES:2 * LANES]]
        gates_i += [g[:, 2 * LANES:3 * LANES], g[:, 3 * LANES:4 * LANES]]

    rec, sq = [], None
    for l in range(N_PLANES):
        y = conv[l]
        r = jax.nn.sigmoid(gates_r[l] + ba_ref[l:l + 1, :])
        ig = jax.nn.sigmoid(gates_i[l] + bx_ref[l:l + 1, :])
        log_a = (-RG_C * r) * _softplus(-lam_ref[l:l + 1, :])
        a = jnp.exp(log_a)
        u = jnp.sqrt(1.0 - a * a) * (ig * y)

        piece = lambda v, s: v[s * G:(s + 1) * G, :]
        pa, pb = [piece(a, 0)], [piece(u, 0)]
        for s in range(1, PHASES):
            pa.append(piece(a, s) * pa[-1])
            pb.append(piece(a, s) * pb[-1] + piece(u, s))
        ga, gb = pa[-1], pb[-1]
        sh = 1
        while sh < G:
            take = sub >= sh
            gb = jnp.where(take, ga * pltpu.roll(gb, sh, 0) + gb, gb)
            ga = jnp.where(take, ga * pltpu.roll(ga, sh, 0), ga)
            sh *= 2
        h0 = hc[l, 0:1, :]
        h_end = ga * h0 + gb
        h_in = prev_group(h_end, h0)
        hc[l] = jnp.broadcast_to(h_end[G - 1:G, :], (PHASES, LANES))
        h = jnp.concatenate([pa[s] * h_in + pb[s] for s in range(PHASES)], axis=0)

        xg = jnp.concatenate([xg_ref[l, pl.ds(s, G, stride=PHASES), :] for s in range(PHASES)], axis=0)
        rl = h * _gelu_tanh(xg)
        rec.append(rl)
        part = jnp.sum(rl * rl, axis=-1, keepdims=True)
        sq = part if sq is None else sq + part

    scale = lax.rsqrt(sq * (1.0 / D_LRU) + RMS_EPS)
    for l in range(N_PLANES):
        out = rec[l] * scale * g_ref[l:l + 1, :]
        for s in range(PHASES):
            y_ref[l, pl.ds(s, G, stride=PHASES), :] = out[s * G:(s + 1) * G, :]


def _block_diag_gates(wa, wx):
    def bd(wm):
        wm = wm.reshape(4, 4, LRU_BLOCK_DIM, LRU_BLOCK_DIM)
        eye = jnp.eye(4, dtype=wm.dtype)
        full = jnp.einsum("gide,ij->gidje", wm, eye)
        return full.reshape(4, 4 * LRU_BLOCK_DIM, 4 * LRU_BLOCK_DIM)
    return jnp.concatenate([bd(wa), bd(wx)], axis=-1).astype(BF16)


def _recurrent(rg, conv_w, conv_b, lru_wa, lru_ba, lru_wx, lru_bx, lru_lambda, lru_norm_g, batch, seq):
    ts = TS_LRU
    rg5 = rg.reshape(2, N_PLANES, batch, seq, LANES)
    wbd = _block_diag_gates(lru_wa, lru_wx)
    planes = lambda a: a.reshape(N_PLANES, LANES).astype(F32)
    vec = pl.BlockSpec((N_PLANES, LANES), lambda b, i: (0, 0))
    return pl.pallas_call(
        functools.partial(_lru_kernel, ts=ts),
        grid=(batch, seq // ts),
        in_specs=[
            pl.BlockSpec((None, N_PLANES, None, ts, LANES), lambda b, i: (0, 0, b, i, 0)),
            pl.BlockSpec((None, N_PLANES, None, ts, LANES), lambda b, i: (1, 0, b, i, 0)),
            pl.BlockSpec((CONV_WIDTH, N_PLANES, LANES), lambda b, i: (0, 0, 0)),
            vec,
            pl.BlockSpec((4, 4 * LRU_BLOCK_DIM, 8 * LRU_BLOCK_DIM), lambda b, i: (0, 0, 0)),
            vec, vec, vec, vec,
        ],
        out_specs=pl.BlockSpec((N_PLANES, None, ts, LANES), lambda b, i: (0, b, i, 0)),
        out_shape=jax.ShapeDtypeStruct((N_PLANES, batch, seq, LANES), F32),
        scratch_shapes=[
            pltpu.VMEM((N_PLANES, PHASES, LANES), F32),
            pltpu.VMEM((N_PLANES, PHASES, LANES), F32),
        ],
        compiler_params=_params("parallel", "arbitrary"),
        name="conv_rglru",
    )(rg5, rg5, conv_w.astype(F32).reshape(CONV_WIDTH, N_PLANES, LANES), planes(conv_b), wbd, planes(lru_ba),
      planes(lru_bx), planes(lru_lambda), planes(lru_norm_g))


def _layer_norm_rows(z, g, b):
    mu = jnp.mean(z, axis=-1, keepdims=True)
    zc = z - mu
    var = jnp.mean(zc * zc, axis=-1, keepdims=True)
    return zc * lax.rsqrt(var + LN_EPS) * g + b


def _split_bf16(a):
    hi = a.astype(BF16)
    lo = (a - hi.astype(F32)).astype(BF16)
    return hi, lo


U32 = jnp.uint32
HI_HALF = np.uint32(0xFFFF0000)


def _to_packed_token_major(ref, val, first_row, n_rows):
    bits = lambda a: lax.bitcast_convert_type(a.astype(BF16).astype(F32), U32)
    half = PACKED_CHUNKS * LANES
    for c in range(PACKED_CHUNKS):
        lo = bits(val[:, c * LANES:(c + 1) * LANES]) >> 16
        hi = bits(val[:, half + c * LANES:half + (c + 1) * LANES]) & HI_HALF
        ref[pl.ds(first_row * PACKED_CHUNKS + c, n_rows, stride=PACKED_CHUNKS), :] = lo | hi


def _from_packed_token_major(ref, first_row, n_rows):
    words = [ref[pl.ds(first_row * PACKED_CHUNKS + c, n_rows, stride=PACKED_CHUNKS), :]
             for c in range(PACKED_CHUNKS)]
    lo = [lax.bitcast_convert_type(wd << 16, F32) for wd in words]
    hi = [lax.bitcast_convert_type(wd & HI_HALF, F32) for wd in words]
    return jnp.concatenate(lo + hi, axis=-1)


def _out_router_kernel(*refs, tm):
    *io_refs, cnt_s, z_a, z_b = refs
    i = pl.program_id(0)

    @pl.when(i == 0)
    def _():
        cnt_s[...] = jnp.zeros_like(cnt_s)
        z_b[...] = jnp.zeros_like(z_b)

    @pl.when(i % 2 == 0)
    def _():
        _out_router_step(*io_refs, cnt_s, z_b, z_a, tm=tm)

    @pl.when(i % 2 == 1)
    def _():
        _out_router_step(*io_refs, cnt_s, z_a, z_b, tm=tm)


def _out_router_step(attn_ref, yrec_ref, x_ref, ga_ref, wo_ref, g1_ref, b1_ref, wr_ref, br_ref,
                     x1_ref, x1p_ref, pk_ref, rw_ref, cnt_ref, cnt_s, z_in, z_out, *, tm):
    i = pl.program_id(0)
    attn = jnp.concatenate([attn_ref[p] for p in range(N_PLANES)], axis=-1)
    ms = jnp.mean(attn * attn, axis=-1, keepdims=True)
    ya = (attn * lax.rsqrt(ms + RMS_EPS) * ga_ref[...]).astype(BF16)
    mix = jnp.dot(ya, wo_ref[0:D_ATTN, :], preferred_element_type=F32)
    yr = jnp.concatenate([yrec_ref[p] for p in range(N_PLANES)], axis=-1).astype(BF16)
    mix = mix + jnp.dot(yr, wo_ref[D_ATTN:, :], preferred_element_type=F32)
    z_out[...] = ALPHA * x_ref[...] + mix

    x1 = _layer_norm_rows(z_in[...], g1_ref[...], b1_ref[...])
    x1_ref[...] = x1
    _to_packed_token_major(x1p_ref, x1, 0, tm)

    x_hi, x_lo = _split_bf16(x1)
    w_hi, w_lo = _split_bf16(wr_ref[...])
    logits = (jnp.dot(x_hi, w_hi, preferred_element_type=F32)
              + jnp.dot(x_lo, w_hi, preferred_element_type=F32)
              + jnp.dot(x_hi, w_lo, preferred_element_type=F32)) + br_ref[...]

    lane = lax.broadcasted_iota(jnp.int32, (tm, LANES), 1)
    big = jnp.int32(LANES)
    first_true = lambda c: jnp.min(jnp.where(c, lane, big), axis=-1, keepdims=True)

    in_g = lane < N_GROUPS
    gl = jnp.where(in_g, logits, MASKED)
    gmax = jnp.max(gl, axis=-1, keepdims=True)
    g_idx = first_true(gl == gmax)
    gsum = jnp.sum(jnp.where(in_g, jnp.exp(gl - gmax), 0.0), axis=-1, keepdims=True)
    g_gate = 1.0 / gsum

    lo = N_GROUPS + EXPERTS_PER_GROUP * g_idx
    in_e = (lane >= lo) & (lane < lo + EXPERTS_PER_GROUP)
    el = jnp.where(in_e, logits, MASKED)
    emax = jnp.max(el, axis=-1, keepdims=True)
    ee = jnp.where(in_e, jnp.exp(el - emax), 0.0)
    esum = jnp.sum(ee, axis=-1, keepdims=True)
    i1 = first_true(el == emax)
    rest = jnp.where(in_e & (lane != i1), ee, -1.0)
    e2max = jnp.max(rest, axis=-1, keepdims=True)
    i2 = first_true(rest == e2max)
    v1 = 1.0 / esum
    v2 = e2max / esum
    den = v1 + v2
    w1 = g_gate * v1 / den
    w2 = g_gate * v2 / den
    e1 = i1 - N_GROUPS
    e2 = i2 - N_GROUPS
    rw_ref[...] = jnp.where(lane == 0, w1, jnp.where(lane == 1, w2, 0.0))

    hot1 = lane == e1
    hot2 = lane == e2
    both = jnp.where((hot1 | hot2) & (i > 0), 1.0, 0.0)
    earlier = (lax.broadcasted_iota(jnp.int32, (tm, tm), 1) < lax.broadcasted_iota(jnp.int32, (tm, tm), 0))
    before = jnp.dot(earlier.astype(BF16), both.astype(BF16), preferred_element_type=F32) + cnt_s[0:1, :]
    r1 = jnp.sum(jnp.where(hot1, before, 0.0), axis=-1, keepdims=True).astype(jnp.int32)
    r2 = jnp.sum(jnp.where(hot2, before, 0.0), axis=-1, keepdims=True).astype(jnp.int32)
    pk_ref[...] = jnp.where(lane == 0, e1 * RANK_SPAN + r1, jnp.where(lane == 1, e2 * RANK_SPAN + r2, 0))
    cnt_s[...] = cnt_s[...] + jnp.sum(both, axis=0, keepdims=True)
    cnt_ref[...] = cnt_s[...].astype(jnp.int32)


def _out_router(attn, yrec, x2d, attn_norm_g, w_out_b, ln1_g, ln1_b, w_router, b_router):
    T = x2d.shape[0]
    tm = TM_OUT
    assert 2 * T <= RANK_SPAN
    attn3 = attn.reshape(N_PLANES, T, LANES)
    yrec3 = yrec.reshape(N_PLANES, T, LANES)
    const = lambda shape: pl.BlockSpec(shape, lambda i: (0,) * len(shape))
    n = T // tm
    stage1 = lambda i: jnp.minimum(i, n - 1)
    stage2 = lambda i: jnp.maximum(i - 1, 0)
    return pl.pallas_call(
        functools.partial(_out_router_kernel, tm=tm),
        grid=(n + 1,),
        in_specs=[
            pl.BlockSpec((N_PLANES, tm, LANES), lambda i: (0, stage1(i), 0)),
            pl.BlockSpec((N_PLANES, tm, LANES), lambda i: (0, stage1(i), 0)),
            pl.BlockSpec((tm, D_MODEL), lambda i: (stage1(i), 0)),
            const((1, D_ATTN)),
            pl.BlockSpec((D_MODEL, D_MODEL), lambda i: (0, 0), pipeline_mode=pl.Buffered(1)),
            const((1, D_MODEL)), const((1, D_MODEL)),
            const((D_MODEL, LANES)), const((1, LANES)),
        ],
        out_specs=[
            pl.BlockSpec((tm, D_MODEL), lambda i: (stage2(i), 0)),
            pl.BlockSpec((tm * PACKED_CHUNKS, LANES), lambda i: (stage2(i), 0)),
            pl.BlockSpec((tm, LANES), lambda i: (stage2(i), 0)),
            pl.BlockSpec((tm, LANES), lambda i: (stage2(i), 0)),
            const((8, LANES)),
        ],
        out_shape=[
            jax.ShapeDtypeStruct((T, D_MODEL), F32),
            jax.ShapeDtypeStruct((T * PACKED_CHUNKS, LANES), U32),
            jax.ShapeDtypeStruct((T, LANES), jnp.int32),
            jax.ShapeDtypeStruct((T, LANES), F32),
            jax.ShapeDtypeStruct((8, LANES), jnp.int32),
        ],
        scratch_shapes=[pltpu.VMEM((8, LANES), F32), pltpu.VMEM((tm, D_MODEL), F32),
                        pltpu.VMEM((tm, D_MODEL), F32)],
        compiler_params=_params("arbitrary"),
        name="out_proj_router",
    )(attn3, yrec3, x2d, attn_norm_g.reshape(1, D_ATTN).astype(F32), w_out_b,
      ln1_g.reshape(1, D_MODEL).astype(F32), ln1_b.reshape(1, D_MODEL).astype(F32), w_router, b_router)


def _dispatch_kernel(slot_ref, pstart_ref, pend_ref, nused_ref, x1p_ref, xs_hbm, zbuf, sem, zsem,
                     *, tm, rows, n_blk):
    i = pl.program_id(0)
    slab = PACKED_CHUNKS
    blk = rows * slab

    @pl.when(i == 0)
    def _():
        zbuf[...] = jnp.zeros_like(zbuf)
        n_used = nused_ref[0]

        def zero_copy(b):
            return pltpu.make_async_copy(zbuf, xs_hbm.at[pl.ds(pl.multiple_of(b * blk, blk), blk), :], zsem)

        def for_each_zeroed_block(fn):
            for e in range(N_EXPERTS):
                @pl.when(pend_ref[e] > pstart_ref[e])
                def _(e=e):
                    fn(pend_ref[e] // rows - 1)

            def tail(b, c):
                fn(b)
                return c
            lax.fori_loop(n_used, n_blk, tail, 0)

        for_each_zeroed_block(lambda b: zero_copy(b).start())
        for_each_zeroed_block(lambda b: zero_copy(b).wait())

    group = 8

    def push(g, c):
        for t in range(group):
            src = x1p_ref.at[pl.ds(pl.multiple_of(g * (group * slab), group * slab) + t * slab, slab), :]
            for k in range(2):
                row = slot_ref[i * (2 * tm) + g * (2 * group) + 2 * t + k] * slab
                dst = xs_hbm.at[pl.ds(pl.multiple_of(row, slab), slab), :]
                pltpu.make_async_copy(src, dst, sem).start(priority=k)
        return c
    lax.fori_loop(0, tm // group, push, 0)
    for _ in range(2):
        pltpu.make_async_copy(x1p_ref, xs_hbm.at[pl.ds(0, tm * slab), :], sem).wait()


def _dispatch(slot_flat, pad_start, pad_end, n_used, x1p, n_blk):
    tm, rows, slab = TM_DISP, MOE_ROWS, PACKED_CHUNKS
    n_tok = x1p.shape[0] // slab
    grid_spec = pltpu.PrefetchScalarGridSpec(
        num_scalar_prefetch=4,
        grid=(n_tok // tm,),
        in_specs=[pl.BlockSpec((tm * slab, LANES), lambda i, *_: (i, 0))],
        out_specs=pl.BlockSpec(memory_space=pl.ANY),
        scratch_shapes=[
            pltpu.VMEM((rows * slab, LANES), U32),
            pltpu.SemaphoreType.DMA(()),
            pltpu.SemaphoreType.DMA(()),
        ],
    )
    return pl.pallas_call(
        functools.partial(_dispatch_kernel, tm=tm, rows=rows, n_blk=n_blk),
        grid_spec=grid_spec,
        out_shape=jax.ShapeDtypeStruct((n_blk * rows * slab, LANES), U32),
        compiler_params=_params("arbitrary"),
        name="moe_dispatch",
    )(slot_flat, pad_start, pad_end, n_used, x1p)


def _expert_kernel(blk_e_ref, n_used_ref, run_ref, next_e_ref, xs_ref, w1_hbm, w3_hbm, w2_hbm, y_ref,
                   wf1, wf3, wf2, wsem, w1b, w3b, w2b, *, rows):
    i = pl.program_id(0)
    n_used = n_used_ref[0]
    e = blk_e_ref[i]
    slot = run_ref[i] % 2
    new_run = jnp.logical_or(i == 0, e != blk_e_ref[jnp.maximum(i - 1, 0)])

    def weight_copies(expert, slot_):
        return [pltpu.make_async_copy(src.at[expert], dst.at[slot_], wsem.at[slot_])
                for src, dst in ((w1_hbm, wf1), (w3_hbm, wf3), (w2_hbm, wf2))]

    @pl.when(i == 0)
    def _():
        for cp in weight_copies(e, 0):
            cp.start()

    @pl.when(jnp.logical_and(i < n_used, new_run))
    def _():
        for cp in weight_copies(e, slot):
            cp.wait()
        nxt = next_e_ref[i]

        @pl.when(nxt >= 0)
        def _():
            for cp in weight_copies(nxt, 1 - slot):
                cp.start()

        w1b[...] = wf1[slot].astype(BF16)
        w3b[...] = wf3[slot].astype(BF16)
        w2b[...] = wf2[slot].astype(BF16)

    @pl.when(i < n_used)
    def _():
        xb = _from_packed_token_major(xs_ref, 0, rows).astype(BF16)
        h1 = jnp.dot(xb, w1b[...], preferred_element_type=F32)
        h3 = jnp.dot(xb, w3b[...], preferred_element_type=F32)
        h = (jax.nn.silu(h1) * h3).astype(BF16)
        _to_packed_token_major(y_ref, jnp.dot(h, w2b[...], preferred_element_type=F32), 0, rows)

    @pl.when(i >= n_used)
    def _():
        y_ref[...] = jnp.zeros_like(y_ref)


def _experts(blk_e, n_used, run_idx, next_e, xs, w1, w3, w2):
    rows = MOE_ROWS
    n_blk = blk_e.shape[0]
    hbm = pl.BlockSpec(memory_space=pl.ANY)
    grid_spec = pltpu.PrefetchScalarGridSpec(
        num_scalar_prefetch=4,
        grid=(n_blk,),
        in_specs=[
            pl.BlockSpec((rows * PACKED_CHUNKS, LANES), lambda i, be, nu, *_: (jnp.minimum(i, nu[0] - 1), 0)),
            hbm, hbm, hbm,
        ],
        out_specs=pl.BlockSpec((rows * PACKED_CHUNKS, LANES), lambda i, *_: (i, 0)),
        scratch_shapes=[
            pltpu.VMEM((2, D_MODEL, D_EXPERT), F32),
            pltpu.VMEM((2, D_MODEL, D_EXPERT), F32),
            pltpu.VMEM((2, D_EXPERT, D_MODEL), F32),
            pltpu.SemaphoreType.DMA((2,)),
            pltpu.VMEM((D_MODEL, D_EXPERT), BF16),
            pltpu.VMEM((D_MODEL, D_EXPERT), BF16),
            pltpu.VMEM((D_EXPERT, D_MODEL), BF16),
        ],
    )
    return pl.pallas_call(
        functools.partial(_expert_kernel, rows=rows),
        grid_spec=grid_spec,
        out_shape=jax.ShapeDtypeStruct((n_blk * rows * PACKED_CHUNKS, LANES), U32),
        compiler_params=_params("arbitrary"),
        name="moe_experts",
    )(blk_e, n_used, run_idx, next_e, xs, w1, w3, w2)


def _combine_kernel(slot_ref, ys_hbm, x1_ref, rw_ref, g2_ref, b2_ref, o_ref, yg, sem, *, tm):
    i = pl.program_id(0)
    n = pl.num_programs(0)
    slot = i % 2
    slab = PACKED_CHUNKS
    half = tm * slab
    group = 8

    def start_gather(tile, slot_):
        def body(g, c):
            base = pl.multiple_of(g * (group * slab), group * slab)
            for t in range(group):
                for k in range(2):
                    src_row = slot_ref[tile * (2 * tm) + g * (2 * group) + 2 * t + k] * slab
                    pltpu.make_async_copy(
                        ys_hbm.at[pl.ds(pl.multiple_of(src_row, slab), slab), :],
                        yg.at[slot_, pl.ds(k * half + base + t * slab, slab), :],
                        sem.at[slot_]).start(priority=k)
            return c
        lax.fori_loop(0, tm // group, body, 0)

    @pl.when(i == 0)
    def _():
        start_gather(0, 0)

    @pl.when(i + 1 < n)
    def _():
        start_gather(i + 1, 1 - slot)

    pltpu.make_async_copy(ys_hbm.at[pl.ds(0, 2 * half), :], yg.at[slot], sem.at[slot]).wait()
    rw = rw_ref[...]
    buf = yg.at[slot]
    moe = (_from_packed_token_major(buf, 0, tm) * rw[:, 0:1]
           + _from_packed_token_major(buf, tm, tm) * rw[:, 1:2])
    o_ref[...] = _layer_norm_rows(ALPHA * x1_ref[...] + moe, g2_ref[...], b2_ref[...])


def _combine(slot_flat, ys, x1, rw, ln2_g, ln2_b):
    T = x1.shape[0]
    tm = TM_COMB
    grid_spec = pltpu.PrefetchScalarGridSpec(
        num_scalar_prefetch=1,
        grid=(T // tm,),
        in_specs=[
            pl.BlockSpec(memory_space=pl.ANY),
            pl.BlockSpec((tm, D_MODEL), lambda i, *_: (i, 0)),
            pl.BlockSpec((tm, LANES), lambda i, *_: (i, 0)),
            pl.BlockSpec((1, D_MODEL), lambda i, *_: (0, 0)),
            pl.BlockSpec((1, D_MODEL), lambda i, *_: (0, 0)),
        ],
        out_specs=pl.BlockSpec((tm, D_MODEL), lambda i, *_: (i, 0)),
        scratch_shapes=[
            pltpu.VMEM((2, 2 * tm * PACKED_CHUNKS, LANES), U32),
            pltpu.SemaphoreType.DMA((2,)),
        ],
    )
    return pl.pallas_call(
        functools.partial(_combine_kernel, tm=tm),
        grid_spec=grid_spec,
        out_shape=jax.ShapeDtypeStruct((T, D_MODEL), F32),
        compiler_params=_params("arbitrary"),
        name="moe_combine_ln",
    )(slot_flat, ys, x1, rw, ln2_g.reshape(1, D_MODEL).astype(F32), ln2_b.reshape(1, D_MODEL).astype(F32))


def _dispatch_plan(counts, pk_flat, n_tokens):
    rows = MOE_ROWS
    experts = jnp.arange(N_EXPERTS, dtype=jnp.int32)
    padded = (counts + rows - 1) // rows * rows
    pad_end = jnp.cumsum(padded).astype(jnp.int32)
    pad_start = (pad_end - padded).astype(jnp.int32)
    n_blk = 2 * n_tokens // rows + N_EXPERTS
    blk_start = jnp.arange(n_blk, dtype=jnp.int32) * rows
    blk_e = jnp.minimum(jnp.sum(blk_start[:, None] >= pad_end[None, :], axis=1), N_EXPERTS - 1).astype(jnp.int32)
    n_used = (pad_end[-1:] // rows).astype(jnp.int32)
    new_run = jnp.concatenate([jnp.ones((1,), jnp.int32), (blk_e[1:] != blk_e[:-1]).astype(jnp.int32)])
    run_idx = (jnp.cumsum(new_run) - 1).astype(jnp.int32)
    later_active = (experts[None, :] > experts[:, None]) & (counts[None, :] > 0)
    next_active = jnp.min(jnp.where(later_active, experts[None, :], N_EXPERTS), axis=1)
    next_active = jnp.where(next_active == N_EXPERTS, -1, next_active).astype(jnp.int32)
    next_e = next_active[blk_e]
    e_flat = pk_flat >> RANK_BITS
    start_of = jnp.sum(jnp.where(e_flat[:, None] == experts[None, :], pad_start[None, :], 0), axis=1)
    slot_flat = (start_of + (pk_flat & (RANK_SPAN - 1))).astype(jnp.int32)
    return slot_flat, pad_start, pad_end, blk_e, n_used, run_idx, next_e, n_blk


def kernel(x, w_in, conv_w, conv_b, lru_wa, lru_ba, lru_wx, lru_bx, lru_lambda, attn_norm_g, lru_norm_g,
           w_out, ln1_g, ln1_b, router_grp_w, router_grp_b, router_exp_w, router_exp_b, w1, w3, w2,
           ln2_g, ln2_b):
    B, S, D = x.shape
    assert D == D_MODEL and S % DILATED_PATTERNS[-1][0] == 0 and w_in.shape[0] == 1
    T = B * S
    x2d = x.reshape(T, D)

    qkv, rg = _in_proj(x2d, w_in[0].astype(BF16))
    attn = _attention(qkv, B, S)
    yrec = _recurrent(rg, conv_w[0], conv_b[0], lru_wa[0], lru_ba[0], lru_wx[0], lru_bx[0], lru_lambda[0],
                      lru_norm_g[0], B, S)

    n_r = N_GROUPS + N_EXPERTS
    w_router = jnp.zeros((D, LANES), F32).at[:, :n_r].set(
        jnp.concatenate([router_grp_w[0], router_exp_w[0]], axis=-1).astype(F32))
    b_router = jnp.zeros((1, LANES), F32).at[0, :n_r].set(
        jnp.concatenate([router_grp_b[0], router_exp_b[0]], axis=-1).astype(F32))
    x1, x1p, pk, rw, cnt = _out_router(attn, yrec, x2d, attn_norm_g[0], w_out[0].astype(BF16),
                                       ln1_g[0], ln1_b[0], w_router, b_router)

    pk_flat = pk[:, :2].reshape(-1)
    slot_flat, pad_start, pad_end, blk_e, n_used, run_idx, next_e, n_blk = _dispatch_plan(
        cnt[0, :N_EXPERTS], pk_flat, T)
    xs = _dispatch(slot_flat, pad_start, pad_end, n_used, x1p, n_blk)
    ys = _experts(blk_e, n_used, run_idx, next_e, xs, w1[0], w3[0], w2[0])
    out = _combine(slot_flat, ys, x1, rw, ln2_g[0], ln2_b[0])
    return out.reshape(B, S, D)
```

```python
import functools

import jax
import jax.numpy as jnp
import numpy as np
from jax import lax
from jax.experimental import pallas as pl
from jax.experimental.pallas import tpu as pltpu

F32 = jnp.float32
BF16 = jnp.bfloat16

D_MODEL = 2048
D_ATTN = 1024
D_LRU = 1024
HEAD_DIM = 64
N_HEADS = 16
LANES = 128
N_PLANES = D_ATTN // LANES
DILATED_PATTERNS = ((128, 1), (512, 4), (2048, 16))
SUB_WINDOW = 128
LRU_BLOCKS = 16
LRU_BLOCK_DIM = 64
CONV_WIDTH = 4
RG_C = 8.0
N_GROUPS = 4
EXPERTS_PER_GROUP = 8
N_EXPERTS = 32
D_EXPERT = 512
ALPHA = 2.0 ** 0.25
LN_EPS = 1e-5
RMS_EPS = 1e-6
MASKED = -1e30
LOG2E = 1.4426950408889634

VMEM_LIMIT = 56 * 1024 * 1024

TM_PROJ = 256
TS_LRU = 256
TM_OUT = 256
MOE_ROWS = 256
TM_COMB = 256
TM_DISP = 256
CHUNKS = D_MODEL // LANES
PACKED_CHUNKS = CHUNKS // 2
RANK_BITS = 16
RANK_SPAN = 1 << RANK_BITS
ATTN_UNROLL = 8


def _params(*sem):
    return pltpu.CompilerParams(dimension_semantics=sem, vmem_limit_bytes=VMEM_LIMIT)


def _in_proj_kernel(x_ref, w_ref, qkv_ref, rg_ref, *, tm):
    xb = x_ref[...].astype(BF16)
    per_class = tm // CLASSES
    out_row = lax.broadcasted_iota(jnp.int32, (tm, tm), 0)
    src_row = (out_row % per_class) * CLASSES + out_row // per_class
    perm = (lax.broadcasted_iota(jnp.int32, (tm, tm), 1) == src_row).astype(BF16)
    xb_cm = jnp.dot(perm, xb, preferred_element_type=F32).astype(BF16)
    for c in range(5):
        lhs = xb_cm if c < 3 else xb
        acc = jnp.dot(lhs, w_ref[:, c * D_ATTN:(c + 1) * D_ATTN], preferred_element_type=F32)
        if c == 0:
            acc = acc * (HEAD_DIM ** -0.5 * LOG2E)
        for p in range(N_PLANES):
            plane = acc[:, p * LANES:(p + 1) * LANES]
            if c < 3:
                for r in range(CLASSES):
                    qkv_ref[c * N_PLANES + p, r] = plane[r * per_class:(r + 1) * per_class, :]
            else:
                rg_ref[(c - 3) * N_PLANES + p] = plane


def _in_proj(x2d, w_in_b, batch, seq):
    T = x2d.shape[0]
    tm = TM_PROJ
    tiles = seq // tm
    assert tm % CLASSES == 0 and seq % tm == 0
    return pl.pallas_call(
        functools.partial(_in_proj_kernel, tm=tm),
        grid=(T // tm,),
        in_specs=[
            pl.BlockSpec((tm, D_MODEL), lambda i: (i, 0)),
            pl.BlockSpec((D_MODEL, 5 * D_ATTN), lambda i: (0, 0), pipeline_mode=pl.Buffered(1)),
        ],
        out_specs=[
            pl.BlockSpec((3 * N_PLANES, None, CLASSES, tm // CLASSES, LANES),
                         lambda i: (0, i // tiles, 0, i % tiles, 0)),
            pl.BlockSpec((2 * N_PLANES, tm, LANES), lambda i: (0, i, 0)),
        ],
        out_shape=[
            jax.ShapeDtypeStruct((3 * N_PLANES, batch, CLASSES, seq // CLASSES, LANES), F32),
            jax.ShapeDtypeStruct((2 * N_PLANES, T, LANES), F32),
        ],
        compiler_params=_params("parallel"),
        name="in_proj",
    )(x2d, w_in_b)


CLASSES = 16


def _attn_kernel(slopes_ref, qs, ks, vs, ndf_ref, ndh_ref, o_ref,
                 os_, m_s, l_s, acc_s, bias_f, bias_h, *, seq):
    hp = pl.program_id(1)
    slope_a = slopes_ref[2 * hp] * LOG2E
    slope_b = slopes_ref[2 * hp + 1] * LOG2E
    w = SUB_WINDOW
    cl = seq // CLASSES
    n_pat = len(DILATED_PATTERNS)

    def gather(ref, starts, size):
        return jnp.concatenate([ref[pl.ds(s, size), :] for s in starts], axis=0)

    def scatter(ref, starts, size, val):
        for c, s in enumerate(starts):
            ref[pl.ds(s, size), :] = val[c * size:(c + 1) * size, :]

    for t in range(2 * n_pat):
        bias_f[t, 0:w, :] = slope_a * ndf_ref[t]
        bias_f[t, w:2 * w, :] = slope_b * ndf_ref[t]
    for t in range(n_pat):
        bias_h[t, 0:w, :] = slope_a * ndh_ref[t]
        bias_h[t, w:2 * w, :] = slope_b * ndh_ref[t]

    def block(c, n, d, pi, first, last, keys):
        qc = w * d // CLASSES
        aligned = lambda x: x if isinstance(x, int) else pl.multiple_of(x, 8)
        bases = [(c + d * j) * cl for j in range(CLASSES // d)]
        qstarts = [aligned(b + n * qc) for b in bases]
        if keys == "own":
            kstarts, kc, bias = qstarts, qc, bias_h[pi]
        elif keys == "prev+own":
            kstarts, kc, bias = [aligned(b + (n - 1) * qc) for b in bases], 2 * qc, bias_f[pi]
        else:
            kfirst = jnp.maximum(n - 1, 0) * qc
            kstarts, kc = [aligned(b + kfirst) for b in bases], 2 * qc
            bias = bias_f[jnp.where(n == 0, pi + n_pat, pi)]
        is_a = lax.broadcasted_iota(jnp.int32, (w, LANES), 1) < HEAD_DIM
        q = gather(qs, qstarts, qc)
        k = gather(ks, kstarts, kc).astype(BF16)
        v = gather(vs, kstarts, kc).astype(BF16)
        zero = jnp.zeros_like(q)
        q2 = jnp.concatenate([jnp.where(is_a, q, zero), jnp.where(is_a, zero, q)], axis=0).astype(BF16)
        s = lax.dot_general(q2, k, (((1,), (1,)), ((), ())), preferred_element_type=F32) + bias
        m = jnp.max(s, axis=-1, keepdims=True)
        p = jnp.exp2(s - m)
        l = jnp.sum(p, axis=-1, keepdims=True)
        o = jnp.dot(p.astype(BF16), v, preferred_element_type=F32)
        m_c = jnp.where(is_a, m[:w], m[w:])
        l_c = jnp.where(is_a, l[:w], l[w:])
        o_c = jnp.where(is_a, o[:w], o[w:])
        if first:
            scatter(m_s, qstarts, qc, m_c)
            scatter(l_s, qstarts, qc, l_c)
            scatter(acc_s, qstarts, qc, o_c)
            return
        m_o = gather(m_s, qstarts, qc)
        m_n = jnp.maximum(m_o, m_c)
        e_o = jnp.exp2(m_o - m_n)
        e_c = jnp.exp2(m_c - m_n)
        l_n = gather(l_s, qstarts, qc) * e_o + l_c * e_c
        a_n = gather(acc_s, qstarts, qc) * e_o + o_c * e_c
        if last:
            scatter(os_, qstarts, qc, a_n / l_n)
        else:
            scatter(m_s, qstarts, qc, m_n)
            scatter(l_s, qstarts, qc, l_n)
            scatter(acc_s, qstarts, qc, a_n)

    for pi, (window, d) in enumerate(DILATED_PATTERNS):
        assert window // d == w and CLASSES % d == 0
        first, last = pi == 0, pi == n_pat - 1
        nb = seq // (w * d)

        if nb >= 4:
            def any_block(i, carry, d=d, pi=pi, first=first, last=last):
                block(i % d, i // d, d, pi, first, last, "any")
                return carry
            lax.fori_loop(0, d * nb, any_block, 0, unroll=ATTN_UNROLL)
        else:
            def class_blocks(c, carry, d=d, pi=pi, first=first, last=last, nb=nb):
                block(c, 0, d, pi, first, last, "own")
                for n in range(1, nb):
                    block(c, n, d, pi, first, last, "prev+own")
                return carry
            lax.fori_loop(0, d, class_blocks, 0, unroll=max(ATTN_UNROLL // nb, 1))

    for r in range(CLASSES):
        o_ref[pl.ds(r, cl, stride=CLASSES), :] = os_[pl.ds(r * cl, cl), :]


def _neg_distance_tables():
    w = SUB_WINDOW
    prev, nxt, own = [], [], []
    for _, d in DILATED_PATTERNS:
        n_cls = CLASSES // d
        qc = w // n_cls
        def pos(chunk_rows):
            j, l = np.divmod(np.arange(n_cls * chunk_rows), chunk_rows)
            return n_cls * l + j
        sq, sk2, sk1 = pos(qc)[:, None], pos(2 * qc)[None, :], pos(qc)[None, :]
        for out, dist in ((prev, sq + w - sk2), (nxt, sq - sk2), (own, sq - sk1)):
            valid = (dist >= 0) & (dist <= w)
            out.append(np.where(valid, -(dist * d).astype(np.float32), np.float32(MASKED)))
    return jnp.asarray(np.stack(prev + nxt), dtype=F32), jnp.asarray(np.stack(own), dtype=F32)


def _attention(qkv, batch, seq):
    qkv4 = qkv.reshape(3 * N_PLANES, batch, seq, LANES)
    slopes = jnp.exp2(-8.0 * jnp.arange(1, N_HEADS + 1, dtype=F32) / N_HEADS)
    nd_full, nd_head = _neg_distance_tables()
    n_pat = len(DILATED_PATTERNS)
    plane = lambda off: pl.BlockSpec((None, None, seq, LANES), lambda b, h: (off + h, b, 0, 0))
    return pl.pallas_call(
        functools.partial(_attn_kernel, seq=seq),
        grid=(batch, N_PLANES),
        in_specs=[
            pl.BlockSpec(memory_space=pltpu.SMEM),
            plane(0), plane(N_PLANES), plane(2 * N_PLANES),
            pl.BlockSpec((2 * n_pat, SUB_WINDOW, 2 * SUB_WINDOW), lambda b, h: (0, 0, 0)),
            pl.BlockSpec((n_pat, SUB_WINDOW, SUB_WINDOW), lambda b, h: (0, 0, 0)),
        ],
        out_specs=pl.BlockSpec((None, None, seq, LANES), lambda b, h: (h, b, 0, 0)),
        scratch_shapes=[pltpu.VMEM((seq, LANES), F32)] * 4 + [
            pltpu.VMEM((2 * n_pat, 2 * SUB_WINDOW, 2 * SUB_WINDOW), F32),
            pltpu.VMEM((n_pat, 2 * SUB_WINDOW, SUB_WINDOW), F32),
        ],
        out_shape=jax.ShapeDtypeStruct((N_PLANES, batch, seq, LANES), F32),
        compiler_params=_params("parallel", "parallel"),
        name="dilated_attention",
    )(slopes, qkv4, qkv4, qkv4, nd_full, nd_head)


def _gelu_tanh(x):
    c = np.float32(np.sqrt(2.0 / np.pi))
    return 0.5 * x * (1.0 + jnp.tanh(c * (x + 0.044715 * (x * x * x))))


def _log1p(x):
    u = 1.0 + x
    return jnp.where(u == 1.0, x, jnp.log(u) * x / (u - 1.0))


def _softplus(z):
    return jnp.maximum(z, 0.0) + _log1p(jnp.exp(-jnp.abs(z)))


PHASES = 8


def _lru_kernel(xr_ref, xg_ref, cw_ref, cb_ref, wbd_ref, ba_ref, bx_ref, lam_ref, g_ref, y_ref,
                xc, hc, *, ts):
    i = pl.program_id(1)
    G = ts // PHASES
    first_group = lax.broadcasted_iota(jnp.int32, (G, LANES), 0) == 0
    sub = lax.broadcasted_iota(jnp.int32, (G, LANES), 0)

    @pl.when(i == 0)
    def _():
        xc[...] = jnp.zeros_like(xc)
        hc[...] = jnp.zeros_like(hc)

    def prev_group(cur, carry_row):
        return jnp.where(first_group, carry_row, pltpu.roll(cur, 1, 0))

    conv = []
    for l in range(N_PLANES):
        x = [xr_ref[l, pl.ds(s, G, stride=PHASES), :] for s in range(PHASES)]
        back = {s: prev_group(x[s], xc[l, s:s + 1, :]) for s in range(PHASES - CONV_WIDTH + 1, PHASES)}
        phases = []
        for s in range(PHASES):
            acc = jnp.broadcast_to(cb_ref[l:l + 1, :], (G, LANES))
            for j in range(CONV_WIDTH):
                q = s - (CONV_WIDTH - 1) + j
                acc = acc + (x[q] if q >= 0 else back[q + PHASES]) * cw_ref[j, l:l + 1, :]
            phases.append(acc)
        conv.append(jnp.concatenate(phases, axis=0))
        xc[l] = xr_ref[l, ts - PHASES:ts, :]

    gates_r, gates_i = [], []
    for j in range(N_PLANES // 2):
        yb = jnp.concatenate([conv[2 * j], conv[2 * j + 1]], axis=-1).astype(BF16)
        g = jnp.dot(yb, wbd_ref[j], preferred_element_type=F32)
        gates_r += [g[:, 0:LANES], g[:, LANES:2 * LANES]]
        gates_i += [g[:, 2 * LANES:3 * LANES], g[:, 3 * LANES:4 * LANES]]

    rec, sq = [], None
    for l in range(N_PLANES):
        y = conv[l]
        r = jax.nn.sigmoid(gates_r[l] + ba_ref[l:l + 1, :])
        ig = jax.nn.sigmoid(gates_i[l] + bx_ref[l:l + 1, :])
        log_a = (-RG_C * r) * _softplus(-lam_ref[l:l + 1, :])
        a = jnp.exp(log_a)
        u = jnp.sqrt(1.0 - a * a) * (ig * y)

        piece = lambda v, s: v[s * G:(s + 1) * G, :]
        pa, pb = [piece(a, 0)], [piece(u, 0)]
        for s in range(1, PHASES):
            pa.append(piece(a, s) * pa[-1])
            pb.append(piece(a, s) * pb[-1] + piece(u, s))
        ga, gb = pa[-1], pb[-1]
        sh = 1
        while sh < G:
            take = sub >= sh
            gb = jnp.where(take, ga * pltpu.roll(gb, sh, 0) + gb, gb)
            ga = jnp.where(take, ga * pltpu.roll(ga, sh, 0), ga)
            sh *= 2
        h0 = hc[l, 0:1, :]
        h_end = ga * h0 + gb
        h_in = prev_group(h_end, h0)
        hc[l] = jnp.broadcast_to(h_end[G - 1:G, :], (PHASES, LANES))
        h = jnp.concatenate([pa[s] * h_in + pb[s] for s in range(PHASES)], axis=0)

        xg = jnp.concatenate([xg_ref[l, pl.ds(s, G, stride=PHASES), :] for s in range(PHASES)], axis=0)
        rl = h * _gelu_tanh(xg)
        rec.append(rl)
        part = jnp.sum(rl * rl, axis=-1, keepdims=True)
        sq = part if sq is None else sq + part

    scale = lax.rsqrt(sq * (1.0 / D_LRU) + RMS_EPS)
    for l in range(N_PLANES):
        out = rec[l] * scale * g_ref[l:l + 1, :]
        for s in range(PHASES):
            y_ref[l, pl.ds(s, G, stride=PHASES), :] = out[s * G:(s + 1) * G, :]


def _block_diag_gates(wa, wx):
    def bd(wm):
        wm = wm.reshape(4, 4, LRU_BLOCK_DIM, LRU_BLOCK_DIM)
        eye = jnp.eye(4, dtype=wm.dtype)
        full = jnp.einsum("gide,ij->gidje", wm, eye)
        return full.reshape(4, 4 * LRU_BLOCK_DIM, 4 * LRU_BLOCK_DIM)
    return jnp.concatenate([bd(wa), bd(wx)], axis=-1).astype(BF16)


def _recurrent(rg, conv_w, conv_b, lru_wa, lru_ba, lru_wx, lru_bx, lru_lambda, lru_norm_g, batch, seq):
    ts = TS_LRU
    rg5 = rg.reshape(2, N_PLANES, batch, seq, LANES)
    wbd = _block_diag_gates(lru_wa, lru_wx)
    planes = lambda a: a.reshape(N_PLANES, LANES).astype(F32)
    vec = pl.BlockSpec((N_PLANES, LANES), lambda b, i: (0, 0))
    return pl.pallas_call(
        functools.partial(_lru_kernel, ts=ts),
        grid=(batch, seq // ts),
        in_specs=[
            pl.BlockSpec((None, N_PLANES, None, ts, LANES), lambda b, i: (0, 0, b, i, 0)),
            pl.BlockSpec((None, N_PLANES, None, ts, LANES), lambda b, i: (1, 0, b, i, 0)),
            pl.BlockSpec((CONV_WIDTH, N_PLANES, LANES), lambda b, i: (0, 0, 0)),
            vec,
            pl.BlockSpec((4, 4 * LRU_BLOCK_DIM, 8 * LRU_BLOCK_DIM), lambda b, i: (0, 0, 0)),
            vec, vec, vec, vec,
        ],
        out_specs=pl.BlockSpec((N_PLANES, None, ts, LANES), lambda b, i: (0, b, i, 0)),
        out_shape=jax.ShapeDtypeStruct((N_PLANES, batch, seq, LANES), F32),
        scratch_shapes=[
            pltpu.VMEM((N_PLANES, PHASES, LANES), F32),
            pltpu.VMEM((N_PLANES, PHASES, LANES), F32),
        ],
        compiler_params=_params("parallel", "arbitrary"),
        name="conv_rglru",
    )(rg5, rg5, conv_w.astype(F32).reshape(CONV_WIDTH, N_PLANES, LANES), planes(conv_b), wbd, planes(lru_ba),
      planes(lru_bx), planes(lru_lambda), planes(lru_norm_g))


def _layer_norm_rows(z, g, b):
    mu = jnp.mean(z, axis=-1, keepdims=True)
    zc = z - mu
    var = jnp.mean(zc * zc, axis=-1, keepdims=True)
    return zc * lax.rsqrt(var + LN_EPS) * g + b


def _split_bf16(a):
    hi = a.astype(BF16)
    lo = (a - hi.astype(F32)).astype(BF16)
    return hi, lo


U32 = jnp.uint32
HI_HALF = np.uint32(0xFFFF0000)


def _to_packed_token_major(ref, val, first_row, n_rows):
    bits = lambda a: lax.bitcast_convert_type(a.astype(BF16).astype(F32), U32)
    half = PACKED_CHUNKS * LANES
    for c in range(PACKED_CHUNKS):
        lo = bits(val[:, c * LANES:(c + 1) * LANES]) >> 16
        hi = bits(val[:, half + c * LANES:half + (c + 1) * LANES]) & HI_HALF
        ref[pl.ds(first_row * PACKED_CHUNKS + c, n_rows, stride=PACKED_CHUNKS), :] = lo | hi


def _from_packed_token_major(ref, first_row, n_rows):
    words = [ref[pl.ds(first_row * PACKED_CHUNKS + c, n_rows, stride=PACKED_CHUNKS), :]
             for c in range(PACKED_CHUNKS)]
    lo = [lax.bitcast_convert_type(wd << 16, F32) for wd in words]
    hi = [lax.bitcast_convert_type(wd & HI_HALF, F32) for wd in words]
    return jnp.concatenate(lo + hi, axis=-1)


def _out_router_kernel(*refs, tm):
    *io_refs, cnt_s, z_a, z_b = refs
    i = pl.program_id(0)

    @pl.when(i == 0)
    def _():
        cnt_s[...] = jnp.zeros_like(cnt_s)
        z_b[...] = jnp.zeros_like(z_b)

    @pl.when(i % 2 == 0)
    def _():
        _out_router_step(*io_refs, cnt_s, z_b, z_a, tm=tm)

    @pl.when(i % 2 == 1)
    def _():
        _out_router_step(*io_refs, cnt_s, z_a, z_b, tm=tm)


def _out_router_step(attn_ref, yrec_ref, x_ref, ga_ref, wo_ref, g1_ref, b1_ref, wr_ref, br_ref,
                     x1_ref, x1p_ref, pk_ref, rw_ref, cnt_ref, cnt_s, z_in, z_out, *, tm):
    i = pl.program_id(0)
    attn = jnp.concatenate([attn_ref[p] for p in range(N_PLANES)], axis=-1)
    ms = jnp.mean(attn * attn, axis=-1, keepdims=True)
    ya = (attn * lax.rsqrt(ms + RMS_EPS) * ga_ref[...]).astype(BF16)
    mix = jnp.dot(ya, wo_ref[0:D_ATTN, :], preferred_element_type=F32)
    yr = jnp.concatenate([yrec_ref[p] for p in range(N_PLANES)], axis=-1).astype(BF16)
    mix = mix + jnp.dot(yr, wo_ref[D_ATTN:, :], preferred_element_type=F32)
    z_out[...] = ALPHA * x_ref[...] + mix

    x1 = _layer_norm_rows(z_in[...], g1_ref[...], b1_ref[...])
    x1_ref[...] = x1
    _to_packed_token_major(x1p_ref, x1, 0, tm)

    x_hi, x_lo = _split_bf16(x1)
    w_hi, w_lo = _split_bf16(wr_ref[...])
    logits = (jnp.dot(x_hi, w_hi, preferred_element_type=F32)
              + jnp.dot(x_lo, w_hi, preferred_element_type=F32)
              + jnp.dot(x_hi, w_lo, preferred_element_type=F32)) + br_ref[...]

    lane = lax.broadcasted_iota(jnp.int32, (tm, LANES), 1)
    big = jnp.int32(LANES)
    first_true = lambda c: jnp.min(jnp.where(c, lane, big), axis=-1, keepdims=True)

    in_g = lane < N_GROUPS
    gl = jnp.where(in_g, logits, MASKED)
    gmax = jnp.max(gl, axis=-1, keepdims=True)
    g_idx = first_true(gl == gmax)
    gsum = jnp.sum(jnp.where(in_g, jnp.exp(gl - gmax), 0.0), axis=-1, keepdims=True)
    g_gate = 1.0 / gsum

    lo = N_GROUPS + EXPERTS_PER_GROUP * g_idx
    in_e = (lane >= lo) & (lane < lo + EXPERTS_PER_GROUP)
    el = jnp.where(in_e, logits, MASKED)
    emax = jnp.max(el, axis=-1, keepdims=True)
    ee = jnp.where(in_e, jnp.exp(el - emax), 0.0)
    esum = jnp.sum(ee, axis=-1, keepdims=True)
    i1 = first_true(el == emax)
    rest = jnp.where(in_e & (lane != i1), ee, -1.0)
    e2max = jnp.max(rest, axis=-1, keepdims=True)
    i2 = first_true(rest == e2max)
    v1 = 1.0 / esum
    v2 = e2max / esum
    den = v1 + v2
    w1 = g_gate * v1 / den
    w2 = g_gate * v2 / den
    e1 = i1 - N_GROUPS
    e2 = i2 - N_GROUPS
    rw_ref[...] = jnp.where(lane == 0, w1, jnp.where(lane == 1, w2, 0.0))

    hot1 = lane == e1
    hot2 = lane == e2
    both = jnp.where((hot1 | hot2) & (i > 0), 1.0, 0.0)
    earlier = (lax.broadcasted_iota(jnp.int32, (tm, tm), 1) < lax.broadcasted_iota(jnp.int32, (tm, tm), 0))
    before = jnp.dot(earlier.astype(BF16), both.astype(BF16), preferred_element_type=F32) + cnt_s[0:1, :]
    r1 = jnp.sum(jnp.where(hot1, before, 0.0), axis=-1, keepdims=True).astype(jnp.int32)
    r2 = jnp.sum(jnp.where(hot2, before, 0.0), axis=-1, keepdims=True).astype(jnp.int32)
    pk_ref[...] = jnp.where(lane == 0, e1 * RANK_SPAN + r1, jnp.where(lane == 1, e2 * RANK_SPAN + r2, 0))
    cnt_s[...] = cnt_s[...] + jnp.sum(both, axis=0, keepdims=True)
    cnt_ref[...] = cnt_s[...].astype(jnp.int32)


def _out_router(attn, yrec, x2d, attn_norm_g, w_out_b, ln1_g, ln1_b, w_router, b_router):
    T = x2d.shape[0]
    tm = TM_OUT
    assert 2 * T <= RANK_SPAN
    attn3 = attn.reshape(N_PLANES, T, LANES)
    yrec3 = yrec.reshape(N_PLANES, T, LANES)
    const = lambda shape: pl.BlockSpec(shape, lambda i: (0,) * len(shape))
    n = T // tm
    stage1 = lambda i: jnp.minimum(i, n - 1)
    stage2 = lambda i: jnp.maximum(i - 1, 0)
    return pl.pallas_call(
        functools.partial(_out_router_kernel, tm=tm),
        grid=(n + 1,),
        in_specs=[
            pl.BlockSpec((N_PLANES, tm, LANES), lambda i: (0, stage1(i), 0)),
            pl.BlockSpec((N_PLANES, tm, LANES), lambda i: (0, stage1(i), 0)),
            pl.BlockSpec((tm, D_MODEL), lambda i: (stage1(i), 0)),
            const((1, D_ATTN)),
            pl.BlockSpec((D_MODEL, D_MODEL), lambda i: (0, 0), pipeline_mode=pl.Buffered(1)),
            const((1, D_MODEL)), const((1, D_MODEL)),
            const((D_MODEL, LANES)), const((1, LANES)),
        ],
        out_specs=[
            pl.BlockSpec((tm, D_MODEL), lambda i: (stage2(i), 0)),
            pl.BlockSpec((tm * PACKED_CHUNKS, LANES), lambda i: (stage2(i), 0)),
            pl.BlockSpec((tm, LANES), lambda i: (stage2(i), 0)),
            pl.BlockSpec((tm, LANES), lambda i: (stage2(i), 0)),
            const((8, LANES)),
        ],
        out_shape=[
            jax.ShapeDtypeStruct((T, D_MODEL), F32),
            jax.ShapeDtypeStruct((T * PACKED_CHUNKS, LANES), U32),
            jax.ShapeDtypeStruct((T, LANES), jnp.int32),
            jax.ShapeDtypeStruct((T, LANES), F32),
            jax.ShapeDtypeStruct((8, LANES), jnp.int32),
        ],
        scratch_shapes=[pltpu.VMEM((8, LANES), F32), pltpu.VMEM((tm, D_MODEL), F32),
                        pltpu.VMEM((tm, D_MODEL), F32)],
        compiler_params=_params("arbitrary"),
        name="out_proj_router",
    )(attn3, yrec3, x2d, attn_norm_g.reshape(1, D_ATTN).astype(F32), w_out_b,
      ln1_g.reshape(1, D_MODEL).astype(F32), ln1_b.reshape(1, D_MODEL).astype(F32), w_router, b_router)


def _dispatch_kernel(slot_ref, pstart_ref, pend_ref, nused_ref, x1p_ref, xs_hbm, zbuf, sem, zsem,
                     *, tm, rows, n_blk):
    i = pl.program_id(0)
    slab = PACKED_CHUNKS
    blk = rows * slab

    @pl.when(i == 0)
    def _():
        zbuf[...] = jnp.zeros_like(zbuf)
        n_used = nused_ref[0]

        def zero_copy(b):
            return pltpu.make_async_copy(zbuf, xs_hbm.at[pl.ds(pl.multiple_of(b * blk, blk), blk), :], zsem)

        def for_each_zeroed_block(fn):
            for e in range(N_EXPERTS):
                @pl.when(pend_ref[e] > pstart_ref[e])
                def _(e=e):
                    fn(pend_ref[e] // rows - 1)

            def tail(b, c):
                fn(b)
                return c
            lax.fori_loop(n_used, n_blk, tail, 0)

        for_each_zeroed_block(lambda b: zero_copy(b).start())
        for_each_zeroed_block(lambda b: zero_copy(b).wait())

    group = 8

    def push(g, c):
        for t in range(group):
            src = x1p_ref.at[pl.ds(pl.multiple_of(g * (group * slab), group * slab) + t * slab, slab), :]
            for k in range(2):
                row = slot_ref[i * (2 * tm) + g * (2 * group) + 2 * t + k] * slab
                dst = xs_hbm.at[pl.ds(pl.multiple_of(row, slab), slab), :]
                pltpu.make_async_copy(src, dst, sem).start(priority=k)
        return c
    lax.fori_loop(0, tm // group, push, 0)
    for _ in range(2):
        pltpu.make_async_copy(x1p_ref, xs_hbm.at[pl.ds(0, tm * slab), :], sem).wait()


def _dispatch(slot_flat, pad_start, pad_end, n_used, x1p, n_blk):
    tm, rows, slab = TM_DISP, MOE_ROWS, PACKED_CHUNKS
    n_tok = x1p.shape[0] // slab
    grid_spec = pltpu.PrefetchScalarGridSpec(
        num_scalar_prefetch=4,
        grid=(n_tok // tm,),
        in_specs=[pl.BlockSpec((tm * slab, LANES), lambda i, *_: (i, 0))],
        out_specs=pl.BlockSpec(memory_space=pl.ANY),
        scratch_shapes=[
            pltpu.VMEM((rows * slab, LANES), U32),
            pltpu.SemaphoreType.DMA(()),
            pltpu.SemaphoreType.DMA(()),
        ],
    )
    return pl.pallas_call(
        functools.partial(_dispatch_kernel, tm=tm, rows=rows, n_blk=n_blk),
        grid_spec=grid_spec,
        out_shape=jax.ShapeDtypeStruct((n_blk * rows * slab, LANES), U32),
        compiler_params=_params("arbitrary"),
        name="moe_dispatch",
    )(slot_flat, pad_start, pad_end, n_used, x1p)


def _expert_kernel(blk_e_ref, n_used_ref, run_ref, next_e_ref, xs_ref, w1_hbm, w3_hbm, w2_hbm, y_ref,
                   wf1, wf3, wf2, wsem, w1b, w3b, w2b, *, rows):
    i = pl.program_id(0)
    n_used = n_used_ref[0]
    e = blk_e_ref[i]
    slot = run_ref[i] % 2
    new_run = jnp.logical_or(i == 0, e != blk_e_ref[jnp.maximum(i - 1, 0)])

    def weight_copies(expert, slot_):
        return [pltpu.make_async_copy(src.at[expert], dst.at[slot_], wsem.at[slot_])
                for src, dst in ((w1_hbm, wf1), (w3_hbm, wf3), (w2_hbm, wf2))]

    @pl.when(i == 0)
    def _():
        for cp in weight_copies(e, 0):
            cp.start()

    @pl.when(jnp.logical_and(i < n_used, new_run))
    def _():
        for cp in weight_copies(e, slot):
            cp.wait()
        nxt = next_e_ref[i]

        @pl.when(nxt >= 0)
        def _():
            for cp in weight_copies(nxt, 1 - slot):
                cp.start()

        w1b[...] = wf1[slot].astype(BF16)
        w3b[...] = wf3[slot].astype(BF16)
        w2b[...] = wf2[slot].astype(BF16)

    @pl.when(i < n_used)
    def _():
        xb = _from_packed_token_major(xs_ref, 0, rows).astype(BF16)
        h1 = jnp.dot(xb, w1b[...], preferred_element_type=F32)
        h3 = jnp.dot(xb, w3b[...], preferred_element_type=F32)
        h = (jax.nn.silu(h1) * h3).astype(BF16)
        _to_packed_token_major(y_ref, jnp.dot(h, w2b[...], preferred_element_type=F32), 0, rows)

    @pl.when(i >= n_used)
    def _():
        y_ref[...] = jnp.zeros_like(y_ref)


def _experts(blk_e, n_used, run_idx, next_e, xs, w1, w3, w2):
    rows = MOE_ROWS
    n_blk = blk_e.shape[0]
    hbm = pl.BlockSpec(memory_space=pl.ANY)
    grid_spec = pltpu.PrefetchScalarGridSpec(
        num_scalar_prefetch=4,
        grid=(n_blk,),
        in_specs=[
            pl.BlockSpec((rows * PACKED_CHUNKS, LANES), lambda i, be, nu, *_: (jnp.minimum(i, nu[0] - 1), 0)),
            hbm, hbm, hbm,
        ],
        out_specs=pl.BlockSpec((rows * PACKED_CHUNKS, LANES), lambda i, *_: (i, 0)),
        scratch_shapes=[
            pltpu.VMEM((2, D_MODEL, D_EXPERT), F32),
            pltpu.VMEM((2, D_MODEL, D_EXPERT), F32),
            pltpu.VMEM((2, D_EXPERT, D_MODEL), F32),
            pltpu.SemaphoreType.DMA((2,)),
            pltpu.VMEM((D_MODEL, D_EXPERT), BF16),
            pltpu.VMEM((D_MODEL, D_EXPERT), BF16),
            pltpu.VMEM((D_EXPERT, D_MODEL), BF16),
        ],
    )
    return pl.pallas_call(
        functools.partial(_expert_kernel, rows=rows),
        grid_spec=grid_spec,
        out_shape=jax.ShapeDtypeStruct((n_blk * rows * PACKED_CHUNKS, LANES), U32),
        compiler_params=_params("arbitrary"),
        name="moe_experts",
    )(blk_e, n_used, run_idx, next_e, xs, w1, w3, w2)


def _combine_kernel(slot_ref, ys_hbm, x1_ref, rw_ref, g2_ref, b2_ref, o_ref, yg, sem, *, tm):
    i = pl.program_id(0)
    n = pl.num_programs(0)
    slot = i % 2
    slab = PACKED_CHUNKS
    half = tm * slab
    group = 8

    def start_gather(tile, slot_):
        def body(g, c):
            base = pl.multiple_of(g * (group * slab), group * slab)
            for t in range(group):
                for k in range(2):
                    src_row = slot_ref[tile * (2 * tm) + g * (2 * group) + 2 * t + k] * slab
                    pltpu.make_async_copy(
                        ys_hbm.at[pl.ds(pl.multiple_of(src_row, slab), slab), :],
                        yg.at[slot_, pl.ds(k * half + base + t * slab, slab), :],
                        sem.at[slot_]).start(priority=k)
            return c
        lax.fori_loop(0, tm // group, body, 0)

    @pl.when(i == 0)
    def _():
        start_gather(0, 0)

    @pl.when(i + 1 < n)
    def _():
        start_gather(i + 1, 1 - slot)

    pltpu.make_async_copy(ys_hbm.at[pl.ds(0, 2 * half), :], yg.at[slot], sem.at[slot]).wait()
    rw = rw_ref[...]
    buf = yg.at[slot]
    moe = (_from_packed_token_major(buf, 0, tm) * rw[:, 0:1]
           + _from_packed_token_major(buf, tm, tm) * rw[:, 1:2])
    o_ref[...] = _layer_norm_rows(ALPHA * x1_ref[...] + moe, g2_ref[...], b2_ref[...])


def _combine(slot_flat, ys, x1, rw, ln2_g, ln2_b):
    T = x1.shape[0]
    tm = TM_COMB
    grid_spec = pltpu.PrefetchScalarGridSpec(
        num_scalar_prefetch=1,
        grid=(T // tm,),
        in_specs=[
            pl.BlockSpec(memory_space=pl.ANY),
            pl.BlockSpec((tm, D_MODEL), lambda i, *_: (i, 0)),
            pl.BlockSpec((tm, LANES), lambda i, *_: (i, 0)),
            pl.BlockSpec((1, D_MODEL), lambda i, *_: (0, 0)),
            pl.BlockSpec((1, D_MODEL), lambda i, *_: (0, 0)),
        ],
        out_specs=pl.BlockSpec((tm, D_MODEL), lambda i, *_: (i, 0)),
        scratch_shapes=[
            pltpu.VMEM((2, 2 * tm * PACKED_CHUNKS, LANES), U32),
            pltpu.SemaphoreType.DMA((2,)),
        ],
    )
    return pl.pallas_call(
        functools.partial(_combine_kernel, tm=tm),
        grid_spec=grid_spec,
        out_shape=jax.ShapeDtypeStruct((T, D_MODEL), F32),
        compiler_params=_params("arbitrary"),
        name="moe_combine_ln",
    )(slot_flat, ys, x1, rw, ln2_g.reshape(1, D_MODEL).astype(F32), ln2_b.reshape(1, D_MODEL).astype(F32))


def _dispatch_plan(counts, pk_flat, n_tokens):
    rows = MOE_ROWS
    experts = jnp.arange(N_EXPERTS, dtype=jnp.int32)
    padded = (counts + rows - 1) // rows * rows
    pad_end = jnp.cumsum(padded).astype(jnp.int32)
    pad_start = (pad_end - padded).astype(jnp.int32)
    n_blk = 2 * n_tokens // rows + N_EXPERTS
    blk_start = jnp.arange(n_blk, dtype=jnp.int32) * rows
    blk_e = jnp.minimum(jnp.sum(blk_start[:, None] >= pad_end[None, :], axis=1), N_EXPERTS - 1).astype(jnp.int32)
    n_used = (pad_end[-1:] // rows).astype(jnp.int32)
    new_run = jnp.concatenate([jnp.ones((1,), jnp.int32), (blk_e[1:] != blk_e[:-1]).astype(jnp.int32)])
    run_idx = (jnp.cumsum(new_run) - 1).astype(jnp.int32)
    later_active = (experts[None, :] > experts[:, None]) & (counts[None, :] > 0)
    next_active = jnp.min(jnp.where(later_active, experts[None, :], N_EXPERTS), axis=1)
    next_active = jnp.where(next_active == N_EXPERTS, -1, next_active).astype(jnp.int32)
    next_e = next_active[blk_e]
    e_flat = pk_flat >> RANK_BITS
    start_of = jnp.sum(jnp.where(e_flat[:, None] == experts[None, :], pad_start[None, :], 0), axis=1)
    slot_flat = (start_of + (pk_flat & (RANK_SPAN - 1))).astype(jnp.int32)
    return slot_flat, pad_start, pad_end, blk_e, n_used, run_idx, next_e, n_blk


def kernel(x, w_in, conv_w, conv_b, lru_wa, lru_ba, lru_wx, lru_bx, lru_lambda, attn_norm_g, lru_norm_g,
           w_out, ln1_g, ln1_b, router_grp_w, router_grp_b, router_exp_w, router_exp_b, w1, w3, w2,
           ln2_g, ln2_b):
    B, S, D = x.shape
    assert D == D_MODEL and S % DILATED_PATTERNS[-1][0] == 0 and w_in.shape[0] == 1
    T = B * S
    x2d = x.reshape(T, D)

    qkv, rg = _in_proj(x2d, w_in[0].astype(BF16), B, S)
    attn = _attention(qkv, B, S)
    yrec = _recurrent(rg, conv_w[0], conv_b[0], lru_wa[0], lru_ba[0], lru_wx[0], lru_bx[0], lru_lambda[0],
                      lru_norm_g[0], B, S)

    n_r = N_GROUPS + N_EXPERTS
    w_router = jnp.zeros((D, LANES), F32).at[:, :n_r].set(
        jnp.concatenate([router_grp_w[0], router_exp_w[0]], axis=-1).astype(F32))
    b_router = jnp.zeros((1, LANES), F32).at[0, :n_r].set(
        jnp.concatenate([router_grp_b[0], router_exp_b[0]], axis=-1).astype(F32))
    x1, x1p, pk, rw, cnt = _out_router(attn, yrec, x2d, attn_norm_g[0], w_out[0].astype(BF16),
                                       ln1_g[0], ln1_b[0], w_router, b_router)

    pk_flat = pk[:, :2].reshape(-1)
    slot_flat, pad_start, pad_end, blk_e, n_used, run_idx, next_e, n_blk = _dispatch_plan(
        cnt[0, :N_EXPERTS], pk_flat, T)
    xs = _dispatch(slot_flat, pad_start, pad_end, n_used, x1p, n_blk)
    ys = _experts(blk_e, n_used, run_idx, next_e, xs, w1[0], w3[0], w2[0])
    out = _combine(slot_flat, ys, x1, rw, ln2_g[0], ln2_b[0])
    return out.reshape(B, S, D)
```

```python
import functools

import jax
import jax.numpy as jnp
import numpy as np
from jax import lax
from jax.experimental import pallas as pl
from jax.experimental.pallas import tpu as pltpu

F32 = jnp.float32
BF16 = jnp.bfloat16

D_MODEL = 2048
D_ATTN = 1024
D_LRU = 1024
HEAD_DIM = 64
N_HEADS = 16
LANES = 128
N_PLANES = D_ATTN // LANES
DILATED_PATTERNS = ((128, 1), (512, 4), (2048, 16))
SUB_WINDOW = 128
LRU_BLOCKS = 16
LRU_BLOCK_DIM = 64
CONV_WIDTH = 4
RG_C = 8.0
N_GROUPS = 4
EXPERTS_PER_GROUP = 8
N_EXPERTS = 32
D_EXPERT = 512
ALPHA = 2.0 ** 0.25
LN_EPS = 1e-5
RMS_EPS = 1e-6
MASKED = -1e30
LOG2E = 1.4426950408889634

VMEM_LIMIT = 56 * 1024 * 1024

TM_PROJ = 256
TS_LRU = 256
TM_OUT = 256
MOE_ROWS = 256
TM_COMB = 256
TM_DISP = 256
CHUNKS = D_MODEL // LANES
PACKED_CHUNKS = CHUNKS // 2
RANK_BITS = 16
RANK_SPAN = 1 << RANK_BITS
ATTN_UNROLL = 32


def _params(*sem):
    return pltpu.CompilerParams(dimension_semantics=sem, vmem_limit_bytes=VMEM_LIMIT)


def _in_proj_kernel(x_ref, w_ref, qkv_ref, rg_ref, *, tm):
    xb = x_ref[...].astype(BF16)
    per_class = tm // CLASSES
    out_row = lax.broadcasted_iota(jnp.int32, (tm, tm), 0)
    src_row = (out_row % per_class) * CLASSES + out_row // per_class
    perm = (lax.broadcasted_iota(jnp.int32, (tm, tm), 1) == src_row).astype(BF16)
    xb_cm = jnp.dot(perm, xb, preferred_element_type=F32).astype(BF16)
    for c in range(5):
        lhs = xb_cm if c < 3 else xb
        acc = jnp.dot(lhs, w_ref[:, c * D_ATTN:(c + 1) * D_ATTN], preferred_element_type=F32)
        if c == 0:
            acc = acc * (HEAD_DIM ** -0.5 * LOG2E)
        for p in range(N_PLANES):
            plane = acc[:, p * LANES:(p + 1) * LANES]
            if c < 3:
                for r in range(CLASSES):
                    qkv_ref[c * N_PLANES + p, r] = plane[r * per_class:(r + 1) * per_class, :]
            else:
                rg_ref[(c - 3) * N_PLANES + p] = plane


def _in_proj(x2d, w_in_b, batch, seq):
    T = x2d.shape[0]
    tm = TM_PROJ
    tiles = seq // tm
    assert tm % CLASSES == 0 and seq % tm == 0
    return pl.pallas_call(
        functools.partial(_in_proj_kernel, tm=tm),
        grid=(T // tm,),
        in_specs=[
            pl.BlockSpec((tm, D_MODEL), lambda i: (i, 0)),
            pl.BlockSpec((D_MODEL, 5 * D_ATTN), lambda i: (0, 0), pipeline_mode=pl.Buffered(1)),
        ],
        out_specs=[
            pl.BlockSpec((3 * N_PLANES, None, CLASSES, tm // CLASSES, LANES),
                         lambda i: (0, i // tiles, 0, i % tiles, 0)),
            pl.BlockSpec((2 * N_PLANES, tm, LANES), lambda i: (0, i, 0)),
        ],
        out_shape=[
            jax.ShapeDtypeStruct((3 * N_PLANES, batch, CLASSES, seq // CLASSES, LANES), F32),
            jax.ShapeDtypeStruct((2 * N_PLANES, T, LANES), F32),
        ],
        compiler_params=_params("parallel"),
        name="in_proj",
    )(x2d, w_in_b)


CLASSES = 16


def _attn_kernel(slopes_ref, qs, ks, vs, ndf_ref, ndh_ref, o_ref,
                 os_, m_s, l_s, acc_s, bias_f, bias_h, *, seq):
    hp = pl.program_id(1)
    slope_a = slopes_ref[2 * hp] * LOG2E
    slope_b = slopes_ref[2 * hp + 1] * LOG2E
    w = SUB_WINDOW
    cl = seq // CLASSES
    n_pat = len(DILATED_PATTERNS)

    def gather(ref, starts, size):
        return jnp.concatenate([ref[pl.ds(s, size), :] for s in starts], axis=0)

    def scatter(ref, starts, size, val):
        for c, s in enumerate(starts):
            ref[pl.ds(s, size), :] = val[c * size:(c + 1) * size, :]

    for t in range(2 * n_pat):
        bias_f[t, 0:w, :] = slope_a * ndf_ref[t]
        bias_f[t, w:2 * w, :] = slope_b * ndf_ref[t]
    for t in range(n_pat):
        bias_h[t, 0:w, :] = slope_a * ndh_ref[t]
        bias_h[t, w:2 * w, :] = slope_b * ndh_ref[t]

    def block(c, n, d, pi, first, last, keys):
        qc = w * d // CLASSES
        aligned = lambda x: x if isinstance(x, int) else pl.multiple_of(x, 8)
        bases = [(c + d * j) * cl for j in range(CLASSES // d)]
        qstarts = [aligned(b + n * qc) for b in bases]
        if keys == "own":
            kstarts, kc, bias = qstarts, qc, bias_h[pi]
        elif keys == "prev+own":
            kstarts, kc, bias = [aligned(b + (n - 1) * qc) for b in bases], 2 * qc, bias_f[pi]
        else:
            kfirst = jnp.maximum(n - 1, 0) * qc
            kstarts, kc = [aligned(b + kfirst) for b in bases], 2 * qc
            bias = bias_f[jnp.where(n == 0, pi + n_pat, pi)]
        is_a = lax.broadcasted_iota(jnp.int32, (w, LANES), 1) < HEAD_DIM
        q = gather(qs, qstarts, qc)
        k = gather(ks, kstarts, kc).astype(BF16)
        v = gather(vs, kstarts, kc).astype(BF16)
        zero = jnp.zeros_like(q)
        q2 = jnp.concatenate([jnp.where(is_a, q, zero), jnp.where(is_a, zero, q)], axis=0).astype(BF16)
        s = lax.dot_general(q2, k, (((1,), (1,)), ((), ())), preferred_element_type=F32) + bias
        m = jnp.max(s, axis=-1, keepdims=True)
        p = jnp.exp2(s - m)
        l = jnp.sum(p, axis=-1, keepdims=True)
        o = jnp.dot(p.astype(BF16), v, preferred_element_type=F32)
        m_c = jnp.where(is_a, m[:w], m[w:])
        l_c = jnp.where(is_a, l[:w], l[w:])
        o_c = jnp.where(is_a, o[:w], o[w:])
        if first:
            scatter(m_s, qstarts, qc, m_c)
            scatter(l_s, qstarts, qc, l_c)
            scatter(acc_s, qstarts, qc, o_c)
            return
        m_o = gather(m_s, qstarts, qc)
        m_n = jnp.maximum(m_o, m_c)
        e_o = jnp.exp2(m_o - m_n)
        e_c = jnp.exp2(m_c - m_n)
        l_n = gather(l_s, qstarts, qc) * e_o + l_c * e_c
        a_n = gather(acc_s, qstarts, qc) * e_o + o_c * e_c
        if last:
            scatter(os_, qstarts, qc, a_n / l_n)
        else:
            scatter(m_s, qstarts, qc, m_n)
            scatter(l_s, qstarts, qc, l_n)
            scatter(acc_s, qstarts, qc, a_n)

    for pi, (window, d) in enumerate(DILATED_PATTERNS):
        assert window // d == w and CLASSES % d == 0
        first, last = pi == 0, pi == n_pat - 1
        nb = seq // (w * d)

        if nb >= 4:
            def any_block(i, carry, d=d, pi=pi, first=first, last=last):
                block(i % d, i // d, d, pi, first, last, "any")
                return carry
            lax.fori_loop(0, d * nb, any_block, 0, unroll=ATTN_UNROLL)
        else:
            def class_blocks(c, carry, d=d, pi=pi, first=first, last=last, nb=nb):
                block(c, 0, d, pi, first, last, "own")
                for n in range(1, nb):
                    block(c, n, d, pi, first, last, "prev+own")
                return carry
            lax.fori_loop(0, d, class_blocks, 0, unroll=max(ATTN_UNROLL // nb, 1))

    for r in range(CLASSES):
        o_ref[pl.ds(r, cl, stride=CLASSES), :] = os_[pl.ds(r * cl, cl), :]


def _neg_distance_tables():
    w = SUB_WINDOW
    prev, nxt, own = [], [], []
    for _, d in DILATED_PATTERNS:
        n_cls = CLASSES // d
        qc = w // n_cls
        def pos(chunk_rows):
            j, l = np.divmod(np.arange(n_cls * chunk_rows), chunk_rows)
            return n_cls * l + j
        sq, sk2, sk1 = pos(qc)[:, None], pos(2 * qc)[None, :], pos(qc)[None, :]
        for out, dist in ((prev, sq + w - sk2), (nxt, sq - sk2), (own, sq - sk1)):
            valid = (dist >= 0) & (dist <= w)
            out.append(np.where(valid, -(dist * d).astype(np.float32), np.float32(MASKED)))
    return jnp.asarray(np.stack(prev + nxt), dtype=F32), jnp.asarray(np.stack(own), dtype=F32)


def _attention(qkv, batch, seq):
    qkv4 = qkv.reshape(3 * N_PLANES, batch, seq, LANES)
    slopes = jnp.exp2(-8.0 * jnp.arange(1, N_HEADS + 1, dtype=F32) / N_HEADS)
    nd_full, nd_head = _neg_distance_tables()
    n_pat = len(DILATED_PATTERNS)
    plane = lambda off: pl.BlockSpec((None, None, seq, LANES), lambda b, h: (off + h, b, 0, 0))
    return pl.pallas_call(
        functools.partial(_attn_kernel, seq=seq),
        grid=(batch, N_PLANES),
        in_specs=[
            pl.BlockSpec(memory_space=pltpu.SMEM),
            plane(0), plane(N_PLANES), plane(2 * N_PLANES),
            pl.BlockSpec((2 * n_pat, SUB_WINDOW, 2 * SUB_WINDOW), lambda b, h: (0, 0, 0)),
            pl.BlockSpec((n_pat, SUB_WINDOW, SUB_WINDOW), lambda b, h: (0, 0, 0)),
        ],
        out_specs=pl.BlockSpec((None, None, seq, LANES), lambda b, h: (h, b, 0, 0)),
        scratch_shapes=[pltpu.VMEM((seq, LANES), F32)] * 4 + [
            pltpu.VMEM((2 * n_pat, 2 * SUB_WINDOW, 2 * SUB_WINDOW), F32),
            pltpu.VMEM((n_pat, 2 * SUB_WINDOW, SUB_WINDOW), F32),
        ],
        out_shape=jax.ShapeDtypeStruct((N_PLANES, batch, seq, LANES), F32),
        compiler_params=_params("parallel", "parallel"),
        name="dilated_attention",
    )(slopes, qkv4, qkv4, qkv4, nd_full, nd_head)


def _gelu_tanh(x):
    c = np.float32(np.sqrt(2.0 / np.pi))
    return 0.5 * x * (1.0 + jnp.tanh(c * (x + 0.044715 * (x * x * x))))


def _log1p(x):
    u = 1.0 + x
    return jnp.where(u == 1.0, x, jnp.log(u) * x / (u - 1.0))


def _softplus(z):
    return jnp.maximum(z, 0.0) + _log1p(jnp.exp(-jnp.abs(z)))


PHASES = 8


def _lru_kernel(xr_ref, xg_ref, cw_ref, cb_ref, wbd_ref, ba_ref, bx_ref, lam_ref, g_ref, y_ref,
                xc, hc, *, ts):
    i = pl.program_id(1)
    G = ts // PHASES
    first_group = lax.broadcasted_iota(jnp.int32, (G, LANES), 0) == 0
    sub = lax.broadcasted_iota(jnp.int32, (G, LANES), 0)

    @pl.when(i == 0)
    def _():
        xc[...] = jnp.zeros_like(xc)
        hc[...] = jnp.zeros_like(hc)

    def prev_group(cur, carry_row):
        return jnp.where(first_group, carry_row, pltpu.roll(cur, 1, 0))

    conv = []
    for l in range(N_PLANES):
        x = [xr_ref[l, pl.ds(s, G, stride=PHASES), :] for s in range(PHASES)]
        back = {s: prev_group(x[s], xc[l, s:s + 1, :]) for s in range(PHASES - CONV_WIDTH + 1, PHASES)}
        phases = []
        for s in range(PHASES):
            acc = jnp.broadcast_to(cb_ref[l:l + 1, :], (G, LANES))
            for j in range(CONV_WIDTH):
                q = s - (CONV_WIDTH - 1) + j
                acc = acc + (x[q] if q >= 0 else back[q + PHASES]) * cw_ref[j, l:l + 1, :]
            phases.append(acc)
        conv.append(jnp.concatenate(phases, axis=0))
        xc[l] = xr_ref[l, ts - PHASES:ts, :]

    gates_r, gates_i = [], []
    for j in range(N_PLANES // 2):
        yb = jnp.concatenate([conv[2 * j], conv[2 * j + 1]], axis=-1).astype(BF16)
        g = jnp.dot(yb, wbd_ref[j], preferred_element_type=F32)
        gates_r += [g[:, 0:LANES], g[:, LANES:2 * LANES]]
        gates_i += [g[:, 2 * LANES:3 * LANES], g[:, 3 * LANES:4 * LANES]]

    rec, sq = [], None
    for l in range(N_PLANES):
        y = conv[l]
        r = jax.nn.sigmoid(gates_r[l] + ba_ref[l:l + 1, :])
        ig = jax.nn.sigmoid(gates_i[l] + bx_ref[l:l + 1, :])
        log_a = (-RG_C * r) * _softplus(-lam_ref[l:l + 1, :])
        a = jnp.exp(log_a)
        z = 1.0 - a * a
        u = jnp.where(z > 0.0, z * lax.rsqrt(z), 0.0) * (ig * y)

        piece = lambda v, s: v[s * G:(s + 1) * G, :]
        pa, pb = [piece(a, 0)], [piece(u, 0)]
        for s in range(1, PHASES):
            pa.append(piece(a, s) * pa[-1])
            pb.append(piece(a, s) * pb[-1] + piece(u, s))
        ga, gb = pa[-1], pb[-1]
        sh = 1
        while sh < G:
            take = sub >= sh
            gb = jnp.where(take, ga * pltpu.roll(gb, sh, 0) + gb, gb)
            ga = jnp.where(take, ga * pltpu.roll(ga, sh, 0), ga)
            sh *= 2
        h0 = hc[l, 0:1, :]
        h_end = ga * h0 + gb
        h_in = prev_group(h_end, h0)
        hc[l] = jnp.broadcast_to(h_end[G - 1:G, :], (PHASES, LANES))
        h = jnp.concatenate([pa[s] * h_in + pb[s] for s in range(PHASES)], axis=0)

        xg = jnp.concatenate([xg_ref[l, pl.ds(s, G, stride=PHASES), :] for s in range(PHASES)], axis=0)
        rl = h * _gelu_tanh(xg)
        rec.append(rl)
        part = jnp.sum(rl * rl, axis=-1, keepdims=True)
        sq = part if sq is None else sq + part

    scale = lax.rsqrt(sq * (1.0 / D_LRU) + RMS_EPS)
    for l in range(N_PLANES):
        out = rec[l] * scale * g_ref[l:l + 1, :]
        for s in range(PHASES):
            y_ref[l, pl.ds(s, G, stride=PHASES), :] = out[s * G:(s + 1) * G, :]


def _block_diag_gates(wa, wx):
    def bd(wm):
        wm = wm.reshape(4, 4, LRU_BLOCK_DIM, LRU_BLOCK_DIM)
        eye = jnp.eye(4, dtype=wm.dtype)
        full = jnp.einsum("gide,ij->gidje", wm, eye)
        return full.reshape(4, 4 * LRU_BLOCK_DIM, 4 * LRU_BLOCK_DIM)
    return jnp.concatenate([bd(wa), bd(wx)], axis=-1).astype(BF16)


def _recurrent(rg, conv_w, conv_b, lru_wa, lru_ba, lru_wx, lru_bx, lru_lambda, lru_norm_g, batch, seq):
    ts = TS_LRU
    rg5 = rg.reshape(2, N_PLANES, batch, seq, LANES)
    wbd = _block_diag_gates(lru_wa, lru_wx)
    planes = lambda a: a.reshape(N_PLANES, LANES).astype(F32)
    vec = pl.BlockSpec((N_PLANES, LANES), lambda b, i: (0, 0))
    return pl.pallas_call(
        functools.partial(_lru_kernel, ts=ts),
        grid=(batch, seq // ts),
        in_specs=[
            pl.BlockSpec((None, N_PLANES, None, ts, LANES), lambda b, i: (0, 0, b, i, 0)),
            pl.BlockSpec((None, N_PLANES, None, ts, LANES), lambda b, i: (1, 0, b, i, 0)),
            pl.BlockSpec((CONV_WIDTH, N_PLANES, LANES), lambda b, i: (0, 0, 0)),
            vec,
            pl.BlockSpec((4, 4 * LRU_BLOCK_DIM, 8 * LRU_BLOCK_DIM), lambda b, i: (0, 0, 0)),
            vec, vec, vec, vec,
        ],
        out_specs=pl.BlockSpec((N_PLANES, None, ts, LANES), lambda b, i: (0, b, i, 0)),
        out_shape=jax.ShapeDtypeStruct((N_PLANES, batch, seq, LANES), F32),
        scratch_shapes=[
            pltpu.VMEM((N_PLANES, PHASES, LANES), F32),
            pltpu.VMEM((N_PLANES, PHASES, LANES), F32),
        ],
        compiler_params=_params("parallel", "arbitrary"),
        name="conv_rglru",
    )(rg5, rg5, conv_w.astype(F32).reshape(CONV_WIDTH, N_PLANES, LANES), planes(conv_b), wbd, planes(lru_ba),
      planes(lru_bx), planes(lru_lambda), planes(lru_norm_g))


def _layer_norm_rows(z, g, b):
    mu = jnp.mean(z, axis=-1, keepdims=True)
    zc = z - mu
    var = jnp.mean(zc * zc, axis=-1, keepdims=True)
    return zc * lax.rsqrt(var + LN_EPS) * g + b


def _split_bf16(a):
    hi = a.astype(BF16)
    lo = (a - hi.astype(F32)).astype(BF16)
    return hi, lo


U32 = jnp.uint32
HI_HALF = np.uint32(0xFFFF0000)


def _to_packed_token_major(ref, val, first_row, n_rows):
    bits = lambda a: lax.bitcast_convert_type(a.astype(BF16).astype(F32), U32)
    half = PACKED_CHUNKS * LANES
    for c in range(PACKED_CHUNKS):
        lo = bits(val[:, c * LANES:(c + 1) * LANES]) >> 16
        hi = bits(val[:, half + c * LANES:half + (c + 1) * LANES]) & HI_HALF
        ref[pl.ds(first_row * PACKED_CHUNKS + c, n_rows, stride=PACKED_CHUNKS), :] = lo | hi


def _from_packed_token_major(ref, first_row, n_rows):
    words = [ref[pl.ds(first_row * PACKED_CHUNKS + c, n_rows, stride=PACKED_CHUNKS), :]
             for c in range(PACKED_CHUNKS)]
    lo = [lax.bitcast_convert_type(wd << 16, F32) for wd in words]
    hi = [lax.bitcast_convert_type(wd & HI_HALF, F32) for wd in words]
    return jnp.concatenate(lo + hi, axis=-1)


def _out_router_kernel(*refs, tm):
    *io_refs, cnt_s, z_a, z_b = refs
    i = pl.program_id(0)

    @pl.when(i == 0)
    def _():
        cnt_s[...] = jnp.zeros_like(cnt_s)
        z_b[...] = jnp.zeros_like(z_b)

    @pl.when(i % 2 == 0)
    def _():
        _out_router_step(*io_refs, cnt_s, z_b, z_a, tm=tm)

    @pl.when(i % 2 == 1)
    def _():
        _out_router_step(*io_refs, cnt_s, z_a, z_b, tm=tm)


def _out_router_step(attn_ref, yrec_ref, x_ref, ga_ref, wo_ref, g1_ref, b1_ref, wr_ref, br_ref,
                     x1_ref, x1p_ref, pk_ref, rw_ref, cnt_ref, cnt_s, z_in, z_out, *, tm):
    i = pl.program_id(0)
    attn = jnp.concatenate([attn_ref[p] for p in range(N_PLANES)], axis=-1)
    ms = jnp.mean(attn * attn, axis=-1, keepdims=True)
    ya = (attn * lax.rsqrt(ms + RMS_EPS) * ga_ref[...]).astype(BF16)
    mix = jnp.dot(ya, wo_ref[0:D_ATTN, :], preferred_element_type=F32)
    yr = jnp.concatenate([yrec_ref[p] for p in range(N_PLANES)], axis=-1).astype(BF16)
    mix = mix + jnp.dot(yr, wo_ref[D_ATTN:, :], preferred_element_type=F32)
    z_out[...] = ALPHA * x_ref[...] + mix

    x1 = _layer_norm_rows(z_in[...], g1_ref[...], b1_ref[...])
    x1_ref[...] = x1
    _to_packed_token_major(x1p_ref, x1, 0, tm)

    x_hi, x_lo = _split_bf16(x1)
    w_hi, w_lo = _split_bf16(wr_ref[...])
    logits = (jnp.dot(x_hi, w_hi, preferred_element_type=F32)
              + jnp.dot(x_lo, w_hi, preferred_element_type=F32)
              + jnp.dot(x_hi, w_lo, preferred_element_type=F32)) + br_ref[...]

    lane = lax.broadcasted_iota(jnp.int32, (tm, LANES), 1)
    big = jnp.int32(LANES)
    first_true = lambda c: jnp.min(jnp.where(c, lane, big), axis=-1, keepdims=True)

    in_g = lane < N_GROUPS
    gl = jnp.where(in_g, logits, MASKED)
    gmax = jnp.max(gl, axis=-1, keepdims=True)
    g_idx = first_true(gl == gmax)
    gsum = jnp.sum(jnp.where(in_g, jnp.exp(gl - gmax), 0.0), axis=-1, keepdims=True)
    g_gate = 1.0 / gsum

    lo = N_GROUPS + EXPERTS_PER_GROUP * g_idx
    in_e = (lane >= lo) & (lane < lo + EXPERTS_PER_GROUP)
    el = jnp.where(in_e, logits, MASKED)
    emax = jnp.max(el, axis=-1, keepdims=True)
    ee = jnp.where(in_e, jnp.exp(el - emax), 0.0)
    esum = jnp.sum(ee, axis=-1, keepdims=True)
    i1 = first_true(el == emax)
    rest = jnp.where(in_e & (lane != i1), ee, -1.0)
    e2max = jnp.max(rest, axis=-1, keepdims=True)
    i2 = first_true(rest == e2max)
    v1 = 1.0 / esum
    v2 = e2max / esum
    den = v1 + v2
    w1 = g_gate * v1 / den
    w2 = g_gate * v2 / den
    e1 = i1 - N_GROUPS
    e2 = i2 - N_GROUPS
    rw_ref[...] = jnp.where(lane == 0, w1, jnp.where(lane == 1, w2, 0.0))

    hot1 = lane == e1
    hot2 = lane == e2
    both = jnp.where((hot1 | hot2) & (i > 0), 1.0, 0.0)
    earlier = (lax.broadcasted_iota(jnp.int32, (tm, tm), 1) < lax.broadcasted_iota(jnp.int32, (tm, tm), 0))
    before = jnp.dot(earlier.astype(BF16), both.astype(BF16), preferred_element_type=F32) + cnt_s[0:1, :]
    r1 = jnp.sum(jnp.where(hot1, before, 0.0), axis=-1, keepdims=True).astype(jnp.int32)
    r2 = jnp.sum(jnp.where(hot2, before, 0.0), axis=-1, keepdims=True).astype(jnp.int32)
    pk_ref[...] = jnp.where(lane == 0, e1 * RANK_SPAN + r1, jnp.where(lane == 1, e2 * RANK_SPAN + r2, 0))
    cnt_s[...] = cnt_s[...] + jnp.sum(both, axis=0, keepdims=True)
    cnt_ref[...] = cnt_s[...].astype(jnp.int32)


def _out_router(attn, yrec, x2d, attn_norm_g, w_out_b, ln1_g, ln1_b, w_router, b_router):
    T = x2d.shape[0]
    tm = TM_OUT
    assert 2 * T <= RANK_SPAN
    attn3 = attn.reshape(N_PLANES, T, LANES)
    yrec3 = yrec.reshape(N_PLANES, T, LANES)
    const = lambda shape: pl.BlockSpec(shape, lambda i: (0,) * len(shape))
    n = T // tm
    stage1 = lambda i: jnp.minimum(i, n - 1)
    stage2 = lambda i: jnp.maximum(i - 1, 0)
    return pl.pallas_call(
        functools.partial(_out_router_kernel, tm=tm),
        grid=(n + 1,),
        in_specs=[
            pl.BlockSpec((N_PLANES, tm, LANES), lambda i: (0, stage1(i), 0)),
            pl.BlockSpec((N_PLANES, tm, LANES), lambda i: (0, stage1(i), 0)),
            pl.BlockSpec((tm, D_MODEL), lambda i: (stage1(i), 0)),
            const((1, D_ATTN)),
            pl.BlockSpec((D_MODEL, D_MODEL), lambda i: (0, 0), pipeline_mode=pl.Buffered(1)),
            const((1, D_MODEL)), const((1, D_MODEL)),
            const((D_MODEL, LANES)), const((1, LANES)),
        ],
        out_specs=[
            pl.BlockSpec((tm, D_MODEL), lambda i: (stage2(i), 0)),
            pl.BlockSpec((tm * PACKED_CHUNKS, LANES), lambda i: (stage2(i), 0)),
            pl.BlockSpec((tm, LANES), lambda i: (stage2(i), 0)),
            pl.BlockSpec((tm, LANES), lambda i: (stage2(i), 0)),
            const((8, LANES)),
        ],
        out_shape=[
            jax.ShapeDtypeStruct((T, D_MODEL), F32),
            jax.ShapeDtypeStruct((T * PACKED_CHUNKS, LANES), U32),
            jax.ShapeDtypeStruct((T, LANES), jnp.int32),
            jax.ShapeDtypeStruct((T, LANES), F32),
            jax.ShapeDtypeStruct((8, LANES), jnp.int32),
        ],
        scratch_shapes=[pltpu.VMEM((8, LANES), F32), pltpu.VMEM((tm, D_MODEL), F32),
                        pltpu.VMEM((tm, D_MODEL), F32)],
        compiler_params=_params("arbitrary"),
        name="out_proj_router",
    )(attn3, yrec3, x2d, attn_norm_g.reshape(1, D_ATTN).astype(F32), w_out_b,
      ln1_g.reshape(1, D_MODEL).astype(F32), ln1_b.reshape(1, D_MODEL).astype(F32), w_router, b_router)


def _dispatch_kernel(slot_ref, pstart_ref, pend_ref, nused_ref, x1p_ref, xs_hbm, zbuf, sem, zsem,
                     *, tm, rows, n_blk):
    i = pl.program_id(0)
    slab = PACKED_CHUNKS
    blk = rows * slab

    @pl.when(i == 0)
    def _():
        zbuf[...] = jnp.zeros_like(zbuf)
        n_used = nused_ref[0]

        def zero_copy(b):
            return pltpu.make_async_copy(zbuf, xs_hbm.at[pl.ds(pl.multiple_of(b * blk, blk), blk), :], zsem)

        def for_each_zeroed_block(fn):
            for e in range(N_EXPERTS):
                @pl.when(pend_ref[e] > pstart_ref[e])
                def _(e=e):
                    fn(pend_ref[e] // rows - 1)

            def tail(b, c):
                fn(b)
                return c
            lax.fori_loop(n_used, n_blk, tail, 0)

        for_each_zeroed_block(lambda b: zero_copy(b).start())
        for_each_zeroed_block(lambda b: zero_copy(b).wait())

    group = 8

    def push(g, c):
        for t in range(group):
            src = x1p_ref.at[pl.ds(pl.multiple_of(g * (group * slab), group * slab) + t * slab, slab), :]
            for k in range(2):
                row = slot_ref[i * (2 * tm) + g * (2 * group) + 2 * t + k] * slab
                dst = xs_hbm.at[pl.ds(pl.multiple_of(row, slab), slab), :]
                pltpu.make_async_copy(src, dst, sem).start(priority=k)
        return c
    lax.fori_loop(0, tm // group, push, 0)
    for _ in range(2):
        pltpu.make_async_copy(x1p_ref, xs_hbm.at[pl.ds(0, tm * slab), :], sem).wait()


def _dispatch(slot_flat, pad_start, pad_end, n_used, x1p, n_blk):
    tm, rows, slab = TM_DISP, MOE_ROWS, PACKED_CHUNKS
    n_tok = x1p.shape[0] // slab
    grid_spec = pltpu.PrefetchScalarGridSpec(
        num_scalar_prefetch=4,
        grid=(n_tok // tm,),
        in_specs=[pl.BlockSpec((tm * slab, LANES), lambda i, *_: (i, 0))],
        out_specs=pl.BlockSpec(memory_space=pl.ANY),
        scratch_shapes=[
            pltpu.VMEM((rows * slab, LANES), U32),
            pltpu.SemaphoreType.DMA(()),
            pltpu.SemaphoreType.DMA(()),
        ],
    )
    return pl.pallas_call(
        functools.partial(_dispatch_kernel, tm=tm, rows=rows, n_blk=n_blk),
        grid_spec=grid_spec,
        out_shape=jax.ShapeDtypeStruct((n_blk * rows * slab, LANES), U32),
        compiler_params=_params("arbitrary"),
        name="moe_dispatch",
    )(slot_flat, pad_start, pad_end, n_used, x1p)


def _expert_kernel(blk_e_ref, n_used_ref, run_ref, next_e_ref, xs_ref, w1_hbm, w3_hbm, w2_hbm, y_ref,
                   wf1, wf3, wf2, wsem, w1b, w3b, w2b, *, rows):
    i = pl.program_id(0)
    n_used = n_used_ref[0]
    e = blk_e_ref[i]
    slot = run_ref[i] % 2
    new_run = jnp.logical_or(i == 0, e != blk_e_ref[jnp.maximum(i - 1, 0)])

    def weight_copies(expert, slot_):
        return [pltpu.make_async_copy(src.at[expert], dst.at[slot_], wsem.at[slot_])
                for src, dst in ((w1_hbm, wf1), (w3_hbm, wf3), (w2_hbm, wf2))]

    @pl.when(i == 0)
    def _():
        for cp in weight_copies(e, 0):
            cp.start()

    @pl.when(jnp.logical_and(i < n_used, new_run))
    def _():
        for cp in weight_copies(e, slot):
            cp.wait()
        nxt = next_e_ref[i]

        @pl.when(nxt >= 0)
        def _():
            for cp in weight_copies(nxt, 1 - slot):
                cp.start()

        w1b[...] = wf1[slot].astype(BF16)
        w3b[...] = wf3[slot].astype(BF16)
        w2b[...] = wf2[slot].astype(BF16)

    @pl.when(i < n_used)
    def _():
        xb = _from_packed_token_major(xs_ref, 0, rows).astype(BF16)
        h1 = jnp.dot(xb, w1b[...], preferred_element_type=F32)
        h3 = jnp.dot(xb, w3b[...], preferred_element_type=F32)
        h = (jax.nn.silu(h1) * h3).astype(BF16)
        _to_packed_token_major(y_ref, jnp.dot(h, w2b[...], preferred_element_type=F32), 0, rows)

    @pl.when(i >= n_used)
    def _():
        y_ref[...] = jnp.zeros_like(y_ref)


def _experts(blk_e, n_used, run_idx, next_e, xs, w1, w3, w2):
    rows = MOE_ROWS
    n_blk = blk_e.shape[0]
    hbm = pl.BlockSpec(memory_space=pl.ANY)
    grid_spec = pltpu.PrefetchScalarGridSpec(
        num_scalar_prefetch=4,
        grid=(n_blk,),
        in_specs=[
            pl.BlockSpec((rows * PACKED_CHUNKS, LANES), lambda i, be, nu, *_: (jnp.minimum(i, nu[0] - 1), 0)),
            hbm, hbm, hbm,
        ],
        out_specs=pl.BlockSpec((rows * PACKED_CHUNKS, LANES), lambda i, *_: (i, 0)),
        scratch_shapes=[
            pltpu.VMEM((2, D_MODEL, D_EXPERT), F32),
            pltpu.VMEM((2, D_MODEL, D_EXPERT), F32),
            pltpu.VMEM((2, D_EXPERT, D_MODEL), F32),
            pltpu.SemaphoreType.DMA((2,)),
            pltpu.VMEM((D_MODEL, D_EXPERT), BF16),
            pltpu.VMEM((D_MODEL, D_EXPERT), BF16),
            pltpu.VMEM((D_EXPERT, D_MODEL), BF16),
        ],
    )
    return pl.pallas_call(
        functools.partial(_expert_kernel, rows=rows),
        grid_spec=grid_spec,
        out_shape=jax.ShapeDtypeStruct((n_blk * rows * PACKED_CHUNKS, LANES), U32),
        compiler_params=_params("arbitrary"),
        name="moe_experts",
    )(blk_e, n_used, run_idx, next_e, xs, w1, w3, w2)


def _combine_kernel(slot_ref, ys_hbm, x1_ref, rw_ref, g2_ref, b2_ref, o_ref, yg, sem, *, tm):
    i = pl.program_id(0)
    n = pl.num_programs(0)
    slot = i % 2
    slab = PACKED_CHUNKS
    half = tm * slab
    group = 8

    def start_gather(tile, slot_):
        def body(g, c):
            base = pl.multiple_of(g * (group * slab), group * slab)
            for t in range(group):
                for k in range(2):
                    src_row = slot_ref[tile * (2 * tm) + g * (2 * group) + 2 * t + k] * slab
                    pltpu.make_async_copy(
                        ys_hbm.at[pl.ds(pl.multiple_of(src_row, slab), slab), :],
                        yg.at[slot_, pl.ds(k * half + base + t * slab, slab), :],
                        sem.at[slot_]).start(priority=k)
            return c
        lax.fori_loop(0, tm // group, body, 0)

    @pl.when(i == 0)
    def _():
        start_gather(0, 0)

    @pl.when(i + 1 < n)
    def _():
        start_gather(i + 1, 1 - slot)

    pltpu.make_async_copy(ys_hbm.at[pl.ds(0, 2 * half), :], yg.at[slot], sem.at[slot]).wait()
    rw = rw_ref[...]
    buf = yg.at[slot]
    moe = (_from_packed_token_major(buf, 0, tm) * rw[:, 0:1]
           + _from_packed_token_major(buf, tm, tm) * rw[:, 1:2])
    o_ref[...] = _layer_norm_rows(ALPHA * x1_ref[...] + moe, g2_ref[...], b2_ref[...])


def _combine(slot_flat, ys, x1, rw, ln2_g, ln2_b):
    T = x1.shape[0]
    tm = TM_COMB
    grid_spec = pltpu.PrefetchScalarGridSpec(
        num_scalar_prefetch=1,
        grid=(T // tm,),
        in_specs=[
            pl.BlockSpec(memory_space=pl.ANY),
            pl.BlockSpec((tm, D_MODEL), lambda i, *_: (i, 0)),
            pl.BlockSpec((tm, LANES), lambda i, *_: (i, 0)),
            pl.BlockSpec((1, D_MODEL), lambda i, *_: (0, 0)),
            pl.BlockSpec((1, D_MODEL), lambda i, *_: (0, 0)),
        ],
        out_specs=pl.BlockSpec((tm, D_MODEL), lambda i, *_: (i, 0)),
        scratch_shapes=[
            pltpu.VMEM((2, 2 * tm * PACKED_CHUNKS, LANES), U32),
            pltpu.SemaphoreType.DMA((2,)),
        ],
    )
    return pl.pallas_call(
        functools.partial(_combine_kernel, tm=tm),
        grid_spec=grid_spec,
        out_shape=jax.ShapeDtypeStruct((T, D_MODEL), F32),
        compiler_params=_params("arbitrary"),
        name="moe_combine_ln",
    )(slot_flat, ys, x1, rw, ln2_g.reshape(1, D_MODEL).astype(F32), ln2_b.reshape(1, D_MODEL).astype(F32))


def _dispatch_plan(counts, pk_flat, n_tokens):
    rows = MOE_ROWS
    experts = jnp.arange(N_EXPERTS, dtype=jnp.int32)
    padded = (counts + rows - 1) // rows * rows
    pad_end = jnp.cumsum(padded).astype(jnp.int32)
    pad_start = (pad_end - padded).astype(jnp.int32)
    n_blk = 2 * n_tokens // rows + N_EXPERTS
    blk_start = jnp.arange(n_blk, dtype=jnp.int32) * rows
    blk_e = jnp.minimum(jnp.sum(blk_start[:, None] >= pad_end[None, :], axis=1), N_EXPERTS - 1).astype(jnp.int32)
    n_used = (pad_end[-1:] // rows).astype(jnp.int32)
    new_run = jnp.concatenate([jnp.ones((1,), jnp.int32), (blk_e[1:] != blk_e[:-1]).astype(jnp.int32)])
    run_idx = (jnp.cumsum(new_run) - 1).astype(jnp.int32)
    later_active = (experts[None, :] > experts[:, None]) & (counts[None, :] > 0)
    next_active = jnp.min(jnp.where(later_active, experts[None, :], N_EXPERTS), axis=1)
    next_active = jnp.where(next_active == N_EXPERTS, -1, next_active).astype(jnp.int32)
    next_e = next_active[blk_e]
    e_flat = pk_flat >> RANK_BITS
    start_of = jnp.sum(jnp.where(e_flat[:, None] == experts[None, :], pad_start[None, :], 0), axis=1)
    slot_flat = (start_of + (pk_flat & (RANK_SPAN - 1))).astype(jnp.int32)
    return slot_flat, pad_start, pad_end, blk_e, n_used, run_idx, next_e, n_blk


def kernel(x, w_in, conv_w, conv_b, lru_wa, lru_ba, lru_wx, lru_bx, lru_lambda, attn_norm_g, lru_norm_g,
           w_out, ln1_g, ln1_b, router_grp_w, router_grp_b, router_exp_w, router_exp_b, w1, w3, w2,
           ln2_g, ln2_b):
    B, S, D = x.shape
    assert D == D_MODEL and S % DILATED_PATTERNS[-1][0] == 0 and w_in.shape[0] == 1
    T = B * S
    x2d = x.reshape(T, D)

    qkv, rg = _in_proj(x2d, w_in[0].astype(BF16), B, S)
    attn = _attention(qkv, B, S)
    yrec = _recurrent(rg, conv_w[0], conv_b[0], lru_wa[0], lru_ba[0], lru_wx[0], lru_bx[0], lru_lambda[0],
                      lru_norm_g[0], B, S)

    n_r = N_GROUPS + N_EXPERTS
    w_router = jnp.zeros((D, LANES), F32).at[:, :n_r].set(
        jnp.concatenate([router_grp_w[0], router_exp_w[0]], axis=-1).astype(F32))
    b_router = jnp.zeros((1, LANES), F32).at[0, :n_r].set(
        jnp.concatenate([router_grp_b[0], router_exp_b[0]], axis=-1).astype(F32))
    x1, x1p, pk, rw, cnt = _out_router(attn, yrec, x2d, attn_norm_g[0], w_out[0].astype(BF16),
                                       ln1_g[0], ln1_b[0], w_router, b_router)

    pk_flat = pk[:, :2].reshape(-1)
    slot_flat, pad_start, pad_end, blk_e, n_used, run_idx, next_e, n_blk = _dispatch_plan(
        cnt[0, :N_EXPERTS], pk_flat, T)
    xs = _dispatch(slot_flat, pad_start, pad_end, n_used, x1p, n_blk)
    ys = _experts(blk_e, n_used, run_idx, next_e, xs, w1[0], w3[0], w2[0])
    out = _combine(slot_flat, ys, x1, rw, ln2_g[0], ln2_b[0])
    return out.reshape(B, S, D)
```

```python
import functools

import jax
import jax.numpy as jnp
import numpy as np
from jax import lax
from jax.experimental import pallas as pl
from jax.experimental.pallas import tpu as pltpu

F32 = jnp.float32
BF16 = jnp.bfloat16

D_MODEL = 2048
D_ATTN = 1024
D_LRU = 1024
HEAD_DIM = 64
N_HEADS = 16
LANES = 128
N_PLANES = D_ATTN // LANES
DILATED_PATTERNS = ((128, 1), (512, 4), (2048, 16))
SUB_WINDOW = 128
LRU_BLOCKS = 16
LRU_BLOCK_DIM = 64
CONV_WIDTH = 4
RG_C = 8.0
N_GROUPS = 4
EXPERTS_PER_GROUP = 8
N_EXPERTS = 32
D_EXPERT = 512
ALPHA = 2.0 ** 0.25
LN_EPS = 1e-5
RMS_EPS = 1e-6
MASKED = -1e30
LOG2E = 1.4426950408889634

VMEM_LIMIT = 56 * 1024 * 1024

TM_PROJ = 256
TS_LRU = 256
TM_OUT = 256
MOE_ROWS = 256
TM_COMB = 256
TM_DISP = 256
CHUNKS = D_MODEL // LANES
PACKED_CHUNKS = CHUNKS // 2
RANK_BITS = 16
RANK_SPAN = 1 << RANK_BITS
ATTN_UNROLL = 32


def _params(*sem):
    return pltpu.CompilerParams(dimension_semantics=sem, vmem_limit_bytes=VMEM_LIMIT)


def _in_proj_kernel(x_ref, w_ref, qkv_ref, rg_ref, *, tm):
    xb = x_ref[...].astype(BF16)
    per_class = tm // CLASSES
    out_row = lax.broadcasted_iota(jnp.int32, (tm, tm), 0)
    src_row = (out_row % per_class) * CLASSES + out_row // per_class
    perm = (lax.broadcasted_iota(jnp.int32, (tm, tm), 1) == src_row).astype(BF16)
    xb_cm = jnp.dot(perm, xb, preferred_element_type=F32).astype(BF16)
    for c in range(5):
        lhs = xb_cm if c < 3 else xb
        acc = jnp.dot(lhs, w_ref[:, c * D_ATTN:(c + 1) * D_ATTN], preferred_element_type=F32)
        if c == 0:
            acc = acc * (HEAD_DIM ** -0.5 * LOG2E)
        for p in range(N_PLANES):
            plane = acc[:, p * LANES:(p + 1) * LANES]
            if c < 3:
                for r in range(CLASSES):
                    qkv_ref[c * N_PLANES + p, r] = plane[r * per_class:(r + 1) * per_class, :]
            else:
                rg_ref[(c - 3) * N_PLANES + p] = plane


def _in_proj(x2d, w_in_b, batch, seq):
    T = x2d.shape[0]
    tm = TM_PROJ
    tiles = seq // tm
    assert tm % CLASSES == 0 and seq % tm == 0
    return pl.pallas_call(
        functools.partial(_in_proj_kernel, tm=tm),
        grid=(T // tm,),
        in_specs=[
            pl.BlockSpec((tm, D_MODEL), lambda i: (i, 0)),
            pl.BlockSpec((D_MODEL, 5 * D_ATTN), lambda i: (0, 0), pipeline_mode=pl.Buffered(1)),
        ],
        out_specs=[
            pl.BlockSpec((3 * N_PLANES, None, CLASSES, tm // CLASSES, LANES),
                         lambda i: (0, i // tiles, 0, i % tiles, 0)),
            pl.BlockSpec((2 * N_PLANES, tm, LANES), lambda i: (0, i, 0)),
        ],
        out_shape=[
            jax.ShapeDtypeStruct((3 * N_PLANES, batch, CLASSES, seq // CLASSES, LANES), F32),
            jax.ShapeDtypeStruct((2 * N_PLANES, T, LANES), F32),
        ],
        compiler_params=_params("parallel"),
        name="in_proj",
    )(x2d, w_in_b)


CLASSES = 16


def _attn_kernel(slopes_ref, qs, ks, vs, ndf_ref, ndh_ref, o_ref,
                 os_, m_s, l_s, acc_s, bias_f, bias_h, *, seq):
    hp = pl.program_id(1)
    slope_a = slopes_ref[2 * hp] * LOG2E
    slope_b = slopes_ref[2 * hp + 1] * LOG2E
    w = SUB_WINDOW
    cl = seq // CLASSES
    n_pat = len(DILATED_PATTERNS)

    def gather(ref, starts, size):
        return jnp.concatenate([ref[pl.ds(s, size), :] for s in starts], axis=0)

    def scatter(ref, starts, size, val):
        for c, s in enumerate(starts):
            ref[pl.ds(s, size), :] = val[c * size:(c + 1) * size, :]

    for t in range(2 * n_pat):
        bias_f[t, 0:w, :] = slope_a * ndf_ref[t]
        bias_f[t, w:2 * w, :] = slope_b * ndf_ref[t]
    for t in range(n_pat):
        bias_h[t, 0:w, :] = slope_a * ndh_ref[t]
        bias_h[t, w:2 * w, :] = slope_b * ndh_ref[t]

    def block(c, n, d, pi, first, last, keys):
        qc = w * d // CLASSES
        aligned = lambda x: x if isinstance(x, int) else pl.multiple_of(x, 8)
        bases = [(c + d * j) * cl for j in range(CLASSES // d)]
        qstarts = [aligned(b + n * qc) for b in bases]
        if keys == "own":
            kstarts, kc, bias = qstarts, qc, bias_h[pi]
        elif keys == "prev+own":
            kstarts, kc, bias = [aligned(b + (n - 1) * qc) for b in bases], 2 * qc, bias_f[pi]
        else:
            kfirst = jnp.maximum(n - 1, 0) * qc
            kstarts, kc = [aligned(b + kfirst) for b in bases], 2 * qc
            bias = bias_f[jnp.where(n == 0, pi + n_pat, pi)]
        is_a = lax.broadcasted_iota(jnp.int32, (w, LANES), 1) < HEAD_DIM
        q = gather(qs, qstarts, qc)
        k = gather(ks, kstarts, kc).astype(BF16)
        v = gather(vs, kstarts, kc).astype(BF16)
        zero = jnp.zeros_like(q)
        q2 = jnp.concatenate([jnp.where(is_a, q, zero), jnp.where(is_a, zero, q)], axis=0).astype(BF16)
        s = lax.dot_general(q2, k, (((1,), (1,)), ((), ())), preferred_element_type=F32) + bias
        m = jnp.max(s, axis=-1, keepdims=True)
        p = jnp.exp2(s - m)
        l = jnp.sum(p, axis=-1, keepdims=True)
        o = jnp.dot(p.astype(BF16), v, preferred_element_type=F32)
        m_c = jnp.where(is_a, m[:w], m[w:])
        l_c = jnp.where(is_a, l[:w], l[w:])
        o_c = jnp.where(is_a, o[:w], o[w:])
        if first:
            scatter(m_s, qstarts, qc, m_c)
            scatter(l_s, qstarts, qc, l_c)
            scatter(acc_s, qstarts, qc, o_c)
            return
        m_o = gather(m_s, qstarts, qc)
        m_n = jnp.maximum(m_o, m_c)
        e_o = jnp.exp2(m_o - m_n)
        e_c = jnp.exp2(m_c - m_n)
        l_n = gather(l_s, qstarts, qc) * e_o + l_c * e_c
        a_n = gather(acc_s, qstarts, qc) * e_o + o_c * e_c
        if last:
            scatter(os_, qstarts, qc, a_n / l_n)
        else:
            scatter(m_s, qstarts, qc, m_n)
            scatter(l_s, qstarts, qc, l_n)
            scatter(acc_s, qstarts, qc, a_n)

    for pi, (window, d) in enumerate(DILATED_PATTERNS):
        assert window // d == w and CLASSES % d == 0
        first, last = pi == 0, pi == n_pat - 1
        nb = seq // (w * d)

        if nb >= 4:
            def any_block(i, carry, d=d, pi=pi, first=first, last=last):
                block(i % d, i // d, d, pi, first, last, "any")
                return carry
            lax.fori_loop(0, d * nb, any_block, 0, unroll=ATTN_UNROLL)
        else:
            def class_blocks(c, carry, d=d, pi=pi, first=first, last=last, nb=nb):
                block(c, 0, d, pi, first, last, "own")
                for n in range(1, nb):
                    block(c, n, d, pi, first, last, "prev+own")
                return carry
            lax.fori_loop(0, d, class_blocks, 0, unroll=max(ATTN_UNROLL // nb, 1))

    for r in range(CLASSES):
        o_ref[pl.ds(r, cl, stride=CLASSES), :] = os_[pl.ds(r * cl, cl), :]


def _neg_distance_tables():
    w = SUB_WINDOW
    prev, nxt, own = [], [], []
    for _, d in DILATED_PATTERNS:
        n_cls = CLASSES // d
        qc = w // n_cls
        def pos(chunk_rows):
            j, l = np.divmod(np.arange(n_cls * chunk_rows), chunk_rows)
            return n_cls * l + j
        sq, sk2, sk1 = pos(qc)[:, None], pos(2 * qc)[None, :], pos(qc)[None, :]
        for out, dist in ((prev, sq + w - sk2), (nxt, sq - sk2), (own, sq - sk1)):
            valid = (dist >= 0) & (dist <= w)
            out.append(np.where(valid, -(dist * d).astype(np.float32), np.float32(MASKED)))
    return jnp.asarray(np.stack(prev + nxt), dtype=F32), jnp.asarray(np.stack(own), dtype=F32)


def _attention(qkv, batch, seq):
    qkv4 = qkv.reshape(3 * N_PLANES, batch, seq, LANES)
    slopes = jnp.exp2(-8.0 * jnp.arange(1, N_HEADS + 1, dtype=F32) / N_HEADS)
    nd_full, nd_head = _neg_distance_tables()
    n_pat = len(DILATED_PATTERNS)
    plane = lambda off: pl.BlockSpec((None, None, seq, LANES), lambda b, h: (off + h, b, 0, 0))
    return pl.pallas_call(
        functools.partial(_attn_kernel, seq=seq),
        grid=(batch, N_PLANES),
        in_specs=[
            pl.BlockSpec(memory_space=pltpu.SMEM),
            plane(0), plane(N_PLANES), plane(2 * N_PLANES),
            pl.BlockSpec((2 * n_pat, SUB_WINDOW, 2 * SUB_WINDOW), lambda b, h: (0, 0, 0)),
            pl.BlockSpec((n_pat, SUB_WINDOW, SUB_WINDOW), lambda b, h: (0, 0, 0)),
        ],
        out_specs=pl.BlockSpec((None, None, seq, LANES), lambda b, h: (h, b, 0, 0)),
        scratch_shapes=[pltpu.VMEM((seq, LANES), F32)] * 4 + [
            pltpu.VMEM((2 * n_pat, 2 * SUB_WINDOW, 2 * SUB_WINDOW), F32),
            pltpu.VMEM((n_pat, 2 * SUB_WINDOW, SUB_WINDOW), F32),
        ],
        out_shape=jax.ShapeDtypeStruct((N_PLANES, batch, seq, LANES), F32),
        compiler_params=_params("parallel", "parallel"),
        name="dilated_attention",
    )(slopes, qkv4, qkv4, qkv4, nd_full, nd_head)


def _gelu_tanh(x):
    c = np.float32(np.sqrt(2.0 / np.pi))
    return 0.5 * x * (1.0 + jnp.tanh(c * (x + 0.044715 * (x * x * x))))


def _log1p(x):
    u = 1.0 + x
    return jnp.where(u == 1.0, x, jnp.log(u) * x / (u - 1.0))


def _softplus(z):
    return jnp.maximum(z, 0.0) + _log1p(jnp.exp(-jnp.abs(z)))


PHASES = 8


def _lru_kernel(xr_ref, xg_ref, cw_ref, cb_ref, wbd_ref, ba_ref, bx_ref, lam_ref, g_ref, y_ref,
                xc, hc, *, ts):
    i = pl.program_id(1)
    G = ts // PHASES
    first_group = lax.broadcasted_iota(jnp.int32, (G, LANES), 0) == 0
    sub = lax.broadcasted_iota(jnp.int32, (G, LANES), 0)

    @pl.when(i == 0)
    def _():
        xc[...] = jnp.zeros_like(xc)
        hc[...] = jnp.zeros_like(hc)

    def prev_group(cur, carry_row):
        return jnp.where(first_group, carry_row, pltpu.roll(cur, 1, 0))

    conv = []
    for l in range(N_PLANES):
        x = [xr_ref[l, pl.ds(s, G, stride=PHASES), :] for s in range(PHASES)]
        back = {s: prev_group(x[s], xc[l, s:s + 1, :]) for s in range(PHASES - CONV_WIDTH + 1, PHASES)}
        phases = []
        for s in range(PHASES):
            acc = jnp.broadcast_to(cb_ref[l:l + 1, :], (G, LANES))
            for j in range(CONV_WIDTH):
                q = s - (CONV_WIDTH - 1) + j
                acc = acc + (x[q] if q >= 0 else back[q + PHASES]) * cw_ref[j, l:l + 1, :]
            phases.append(acc)
        conv.append(jnp.concatenate(phases, axis=0))
        xc[l] = xr_ref[l, ts - PHASES:ts, :]

    gates_r, gates_i = [], []
    for j in range(N_PLANES // 2):
        yb = jnp.concatenate([conv[2 * j], conv[2 * j + 1]], axis=-1).astype(BF16)
        g = jnp.dot(yb, wbd_ref[j], preferred_element_type=F32)
        gates_r += [g[:, 0:LANES], g[:, LANES:2 * LANES]]
        gates_i += [g[:, 2 * LANES:3 * LANES], g[:, 3 * LANES:4 * LANES]]

    rec, sq = [], None
    for l in range(N_PLANES):
        y = conv[l]
        r = jax.nn.sigmoid(gates_r[l] + ba_ref[l:l + 1, :])
        ig = jax.nn.sigmoid(gates_i[l] + bx_ref[l:l + 1, :])
        log_a = (-RG_C * r) * _softplus(-lam_ref[l:l + 1, :])
        a = jnp.exp(log_a)
        z = 1.0 - a * a
        u = jnp.where(z > 0.0, z * lax.rsqrt(z), 0.0) * (ig * y)

        piece = lambda v, s: v[s * G:(s + 1) * G, :]
        pa, pb = [piece(a, 0)], [piece(u, 0)]
        for s in range(1, PHASES):
            pa.append(piece(a, s) * pa[-1])
            pb.append(piece(a, s) * pb[-1] + piece(u, s))
        ga, gb = pa[-1], pb[-1]
        sh = 1
        while sh < G:
            take = sub >= sh
            gb = jnp.where(take, ga * pltpu.roll(gb, sh, 0) + gb, gb)
            ga = jnp.where(take, ga * pltpu.roll(ga, sh, 0), ga)
            sh *= 2
        h0 = hc[l, 0:1, :]
        h_end = ga * h0 + gb
        h_in = prev_group(h_end, h0)
        hc[l] = jnp.broadcast_to(h_end[G - 1:G, :], (PHASES, LANES))
        h = jnp.concatenate([pa[s] * h_in + pb[s] for s in range(PHASES)], axis=0)

        xg = jnp.concatenate([xg_ref[l, pl.ds(s, G, stride=PHASES), :] for s in range(PHASES)], axis=0)
        rl = h * _gelu_tanh(xg)
        rec.append(rl)
        part = jnp.sum(rl * rl, axis=-1, keepdims=True)
        sq = part if sq is None else sq + part

    scale = lax.rsqrt(sq * (1.0 / D_LRU) + RMS_EPS)
    for l in range(N_PLANES):
        out = rec[l] * scale * g_ref[l:l + 1, :]
        for s in range(PHASES):
            y_ref[l, pl.ds(s, G, stride=PHASES), :] = out[s * G:(s + 1) * G, :]


def _block_diag_gates(wa, wx):
    def bd(wm):
        wm = wm.reshape(4, 4, LRU_BLOCK_DIM, LRU_BLOCK_DIM)
        eye = jnp.eye(4, dtype=wm.dtype)
        full = jnp.einsum("gide,ij->gidje", wm, eye)
        return full.reshape(4, 4 * LRU_BLOCK_DIM, 4 * LRU_BLOCK_DIM)
    return jnp.concatenate([bd(wa), bd(wx)], axis=-1).astype(BF16)


def _recurrent(rg, conv_w, conv_b, lru_wa, lru_ba, lru_wx, lru_bx, lru_lambda, lru_norm_g, batch, seq):
    ts = TS_LRU
    rg5 = rg.reshape(2, N_PLANES, batch, seq, LANES)
    wbd = _block_diag_gates(lru_wa, lru_wx)
    planes = lambda a: a.reshape(N_PLANES, LANES).astype(F32)
    vec = pl.BlockSpec((N_PLANES, LANES), lambda b, i: (0, 0))
    return pl.pallas_call(
        functools.partial(_lru_kernel, ts=ts),
        grid=(batch, seq // ts),
        in_specs=[
            pl.BlockSpec((None, N_PLANES, None, ts, LANES), lambda b, i: (0, 0, b, i, 0)),
            pl.BlockSpec((None, N_PLANES, None, ts, LANES), lambda b, i: (1, 0, b, i, 0)),
            pl.BlockSpec((CONV_WIDTH, N_PLANES, LANES), lambda b, i: (0, 0, 0)),
            vec,
            pl.BlockSpec((4, 4 * LRU_BLOCK_DIM, 8 * LRU_BLOCK_DIM), lambda b, i: (0, 0, 0)),
            vec, vec, vec, vec,
        ],
        out_specs=pl.BlockSpec((N_PLANES, None, ts, LANES), lambda b, i: (0, b, i, 0)),
        out_shape=jax.ShapeDtypeStruct((N_PLANES, batch, seq, LANES), F32),
        scratch_shapes=[
            pltpu.VMEM((N_PLANES, PHASES, LANES), F32),
            pltpu.VMEM((N_PLANES, PHASES, LANES), F32),
        ],
        compiler_params=_params("parallel", "arbitrary"),
        name="conv_rglru",
    )(rg5, rg5, conv_w.astype(F32).reshape(CONV_WIDTH, N_PLANES, LANES), planes(conv_b), wbd, planes(lru_ba),
      planes(lru_bx), planes(lru_lambda), planes(lru_norm_g))


def _layer_norm_rows(z, g, b):
    mu = jnp.mean(z, axis=-1, keepdims=True)
    zc = z - mu
    var = jnp.mean(zc * zc, axis=-1, keepdims=True)
    return zc * lax.rsqrt(var + LN_EPS) * g + b


def _split_bf16(a):
    hi = a.astype(BF16)
    lo = (a - hi.astype(F32)).astype(BF16)
    return hi, lo


U32 = jnp.uint32
HI_HALF = np.uint32(0xFFFF0000)


def _to_packed_token_major(ref, val, first_row, n_rows):
    bits = lambda a: lax.bitcast_convert_type(a.astype(BF16).astype(F32), U32)
    half = PACKED_CHUNKS * LANES
    for c in range(PACKED_CHUNKS):
        lo = bits(val[:, c * LANES:(c + 1) * LANES]) >> 16
        hi = bits(val[:, half + c * LANES:half + (c + 1) * LANES]) & HI_HALF
        ref[pl.ds(first_row * PACKED_CHUNKS + c, n_rows, stride=PACKED_CHUNKS), :] = lo | hi


def _from_packed_token_major(ref, first_row, n_rows):
    words = [ref[pl.ds(first_row * PACKED_CHUNKS + c, n_rows, stride=PACKED_CHUNKS), :]
             for c in range(PACKED_CHUNKS)]
    lo = [lax.bitcast_convert_type(wd << 16, F32) for wd in words]
    hi = [lax.bitcast_convert_type(wd & HI_HALF, F32) for wd in words]
    return jnp.concatenate(lo + hi, axis=-1)


def _out_router_kernel(*refs, tm):
    *io_refs, cnt_s, z_a, z_b = refs
    i = pl.program_id(0)

    @pl.when(i == 0)
    def _():
        cnt_s[...] = jnp.zeros_like(cnt_s)
        z_b[...] = jnp.zeros_like(z_b)

    @pl.when(i % 2 == 0)
    def _():
        _out_router_step(*io_refs, cnt_s, z_b, z_a, tm=tm)

    @pl.when(i % 2 == 1)
    def _():
        _out_router_step(*io_refs, cnt_s, z_a, z_b, tm=tm)


def _out_router_step(attn_ref, yrec_ref, x_ref, ga_ref, wo_ref, g1_ref, b1_ref, wr_ref, br_ref,
                     x1_ref, x1p_ref, pk_ref, rw_ref, cnt_ref, cnt_s, z_in, z_out, *, tm):
    i = pl.program_id(0)
    attn = jnp.concatenate([attn_ref[p] for p in range(N_PLANES)], axis=-1)
    ms = jnp.mean(attn * attn, axis=-1, keepdims=True)
    ya = (attn * lax.rsqrt(ms + RMS_EPS) * ga_ref[...]).astype(BF16)
    yr = jnp.concatenate([yrec_ref[p] for p in range(N_PLANES)], axis=-1).astype(BF16)
    n_col = 8
    cw = D_MODEL // n_col

    def project(j):
        cols = slice(j * cw, (j + 1) * cw)
        mix = jnp.dot(ya, wo_ref[0:D_ATTN, cols], preferred_element_type=F32)
        mix = mix + jnp.dot(yr, wo_ref[D_ATTN:, cols], preferred_element_type=F32)
        z_out[:, cols] = ALPHA * x_ref[:, cols] + mix

    def normalize(r0, n_rows):
        rows = slice(r0, r0 + n_rows)
        x1 = _layer_norm_rows(z_in[rows, :], g1_ref[...], b1_ref[...])
        x1_ref[rows, :] = x1
        _to_packed_token_major(x1p_ref, x1, r0, n_rows)
        return _split_bf16(x1)

    n_row = 4
    rh = tm // n_row
    parts = []
    for k in range(n_row):
        parts.append(normalize(k * rh, rh))
        project(k)

    x_hi = jnp.concatenate([p[0] for p in parts], axis=0)
    x_lo = jnp.concatenate([p[1] for p in parts], axis=0)
    w_hi, w_lo = _split_bf16(wr_ref[...])
    logits = (jnp.dot(x_hi, w_hi, preferred_element_type=F32)
              + jnp.dot(x_lo, w_hi, preferred_element_type=F32)
              + jnp.dot(x_hi, w_lo, preferred_element_type=F32)) + br_ref[...]
    project(4)
    project(5)
    _route(logits, pk_ref, rw_ref, cnt_ref, cnt_s, i, tm)
    project(6)
    project(7)


def _route(logits, pk_ref, rw_ref, cnt_ref, cnt_s, i, tm):
    lane = lax.broadcasted_iota(jnp.int32, (tm, LANES), 1)
    big = jnp.int32(LANES)
    first_true = lambda c: jnp.min(jnp.where(c, lane, big), axis=-1, keepdims=True)

    in_g = lane < N_GROUPS
    gl = jnp.where(in_g, logits, MASKED)
    gmax = jnp.max(gl, axis=-1, keepdims=True)
    g_idx = first_true(gl == gmax)
    gsum = jnp.sum(jnp.where(in_g, jnp.exp(gl - gmax), 0.0), axis=-1, keepdims=True)
    g_gate = 1.0 / gsum

    lo = N_GROUPS + EXPERTS_PER_GROUP * g_idx
    in_e = (lane >= lo) & (lane < lo + EXPERTS_PER_GROUP)
    el = jnp.where(in_e, logits, MASKED)
    emax = jnp.max(el, axis=-1, keepdims=True)
    ee = jnp.where(in_e, jnp.exp(el - emax), 0.0)
    esum = jnp.sum(ee, axis=-1, keepdims=True)
    i1 = first_true(el == emax)
    rest = jnp.where(in_e & (lane != i1), ee, -1.0)
    e2max = jnp.max(rest, axis=-1, keepdims=True)
    i2 = first_true(rest == e2max)
    v1 = 1.0 / esum
    v2 = e2max / esum
    den = v1 + v2
    w1 = g_gate * v1 / den
    w2 = g_gate * v2 / den
    e1 = i1 - N_GROUPS
    e2 = i2 - N_GROUPS
    rw_ref[...] = jnp.where(lane == 0, w1, jnp.where(lane == 1, w2, 0.0))

    hot1 = lane == e1
    hot2 = lane == e2
    both = jnp.where((hot1 | hot2) & (i > 0), 1.0, 0.0)
    earlier = (lax.broadcasted_iota(jnp.int32, (tm, tm), 1) < lax.broadcasted_iota(jnp.int32, (tm, tm), 0))
    before = jnp.dot(earlier.astype(BF16), both.astype(BF16), preferred_element_type=F32) + cnt_s[0:1, :]
    r1 = jnp.sum(jnp.where(hot1, before, 0.0), axis=-1, keepdims=True).astype(jnp.int32)
    r2 = jnp.sum(jnp.where(hot2, before, 0.0), axis=-1, keepdims=True).astype(jnp.int32)
    pk_ref[...] = jnp.where(lane == 0, e1 * RANK_SPAN + r1, jnp.where(lane == 1, e2 * RANK_SPAN + r2, 0))
    cnt_s[...] = cnt_s[...] + jnp.sum(both, axis=0, keepdims=True)
    cnt_ref[...] = cnt_s[...].astype(jnp.int32)


def _out_router(attn, yrec, x2d, attn_norm_g, w_out_b, ln1_g, ln1_b, w_router, b_router):
    T = x2d.shape[0]
    tm = TM_OUT
    assert 2 * T <= RANK_SPAN
    attn3 = attn.reshape(N_PLANES, T, LANES)
    yrec3 = yrec.reshape(N_PLANES, T, LANES)
    const = lambda shape: pl.BlockSpec(shape, lambda i: (0,) * len(shape))
    n = T // tm
    stage1 = lambda i: jnp.minimum(i, n - 1)
    stage2 = lambda i: jnp.maximum(i - 1, 0)
    return pl.pallas_call(
        functools.partial(_out_router_kernel, tm=tm),
        grid=(n + 1,),
        in_specs=[
            pl.BlockSpec((N_PLANES, tm, LANES), lambda i: (0, stage1(i), 0)),
            pl.BlockSpec((N_PLANES, tm, LANES), lambda i: (0, stage1(i), 0)),
            pl.BlockSpec((tm, D_MODEL), lambda i: (stage1(i), 0)),
            const((1, D_ATTN)),
            pl.BlockSpec((D_MODEL, D_MODEL), lambda i: (0, 0), pipeline_mode=pl.Buffered(1)),
            const((1, D_MODEL)), const((1, D_MODEL)),
            const((D_MODEL, LANES)), const((1, LANES)),
        ],
        out_specs=[
            pl.BlockSpec((tm, D_MODEL), lambda i: (stage2(i), 0)),
            pl.BlockSpec((tm * PACKED_CHUNKS, LANES), lambda i: (stage2(i), 0)),
            pl.BlockSpec((tm, LANES), lambda i: (stage2(i), 0)),
            pl.BlockSpec((tm, LANES), lambda i: (stage2(i), 0)),
            const((8, LANES)),
        ],
        out_shape=[
            jax.ShapeDtypeStruct((T, D_MODEL), F32),
            jax.ShapeDtypeStruct((T * PACKED_CHUNKS, LANES), U32),
            jax.ShapeDtypeStruct((T, LANES), jnp.int32),
            jax.ShapeDtypeStruct((T, LANES), F32),
            jax.ShapeDtypeStruct((8, LANES), jnp.int32),
        ],
        scratch_shapes=[pltpu.VMEM((8, LANES), F32), pltpu.VMEM((tm, D_MODEL), F32),
                        pltpu.VMEM((tm, D_MODEL), F32)],
        compiler_params=_params("arbitrary"),
        name="out_proj_router",
    )(attn3, yrec3, x2d, attn_norm_g.reshape(1, D_ATTN).astype(F32), w_out_b,
      ln1_g.reshape(1, D_MODEL).astype(F32), ln1_b.reshape(1, D_MODEL).astype(F32), w_router, b_router)


def _dispatch_kernel(slot_ref, pstart_ref, pend_ref, nused_ref, x1p_ref, xs_hbm, zbuf, sem, zsem,
                     *, tm, rows, n_blk):
    i = pl.program_id(0)
    slab = PACKED_CHUNKS
    blk = rows * slab

    @pl.when(i == 0)
    def _():
        zbuf[...] = jnp.zeros_like(zbuf)
        n_used = nused_ref[0]

        def zero_copy(b):
            return pltpu.make_async_copy(zbuf, xs_hbm.at[pl.ds(pl.multiple_of(b * blk, blk), blk), :], zsem)

        def for_each_zeroed_block(fn):
            for e in range(N_EXPERTS):
                @pl.when(pend_ref[e] > pstart_ref[e])
                def _(e=e):
                    fn(pend_ref[e] // rows - 1)

            def tail(b, c):
                fn(b)
                return c
            lax.fori_loop(n_used, n_blk, tail, 0)

        for_each_zeroed_block(lambda b: zero_copy(b).start())
        for_each_zeroed_block(lambda b: zero_copy(b).wait())

    group = 8

    def push(g, c):
        for t in range(group):
            src = x1p_ref.at[pl.ds(pl.multiple_of(g * (group * slab), group * slab) + t * slab, slab), :]
            for k in range(2):
                row = slot_ref[i * (2 * tm) + g * (2 * group) + 2 * t + k] * slab
                dst = xs_hbm.at[pl.ds(pl.multiple_of(row, slab), slab), :]
                pltpu.make_async_copy(src, dst, sem).start(priority=k)
        return c
    lax.fori_loop(0, tm // group, push, 0)
    for _ in range(2):
        pltpu.make_async_copy(x1p_ref, xs_hbm.at[pl.ds(0, tm * slab), :], sem).wait()


def _dispatch(slot_flat, pad_start, pad_end, n_used, x1p, n_blk):
    tm, rows, slab = TM_DISP, MOE_ROWS, PACKED_CHUNKS
    n_tok = x1p.shape[0] // slab
    grid_spec = pltpu.PrefetchScalarGridSpec(
        num_scalar_prefetch=4,
        grid=(n_tok // tm,),
        in_specs=[pl.BlockSpec((tm * slab, LANES), lambda i, *_: (i, 0))],
        out_specs=pl.BlockSpec(memory_space=pl.ANY),
        scratch_shapes=[
            pltpu.VMEM((rows * slab, LANES), U32),
            pltpu.SemaphoreType.DMA(()),
            pltpu.SemaphoreType.DMA(()),
        ],
    )
    return pl.pallas_call(
        functools.partial(_dispatch_kernel, tm=tm, rows=rows, n_blk=n_blk),
        grid_spec=grid_spec,
        out_shape=jax.ShapeDtypeStruct((n_blk * rows * slab, LANES), U32),
        compiler_params=_params("arbitrary"),
        name="moe_dispatch",
    )(slot_flat, pad_start, pad_end, n_used, x1p)


def _expert_kernel(blk_e_ref, n_used_ref, run_ref, next_e_ref, xs_ref, w1_hbm, w3_hbm, w2_hbm, y_ref,
                   wf1, wf3, wf2, wsem, w1b, w3b, w2b, *, rows):
    i = pl.program_id(0)
    n_used = n_used_ref[0]
    e = blk_e_ref[i]
    slot = run_ref[i] % 2
    new_run = jnp.logical_or(i == 0, e != blk_e_ref[jnp.maximum(i - 1, 0)])

    def weight_copies(expert, slot_):
        return [pltpu.make_async_copy(src.at[expert], dst.at[slot_], wsem.at[slot_])
                for src, dst in ((w1_hbm, wf1), (w3_hbm, wf3), (w2_hbm, wf2))]

    @pl.when(i == 0)
    def _():
        for cp in weight_copies(e, 0):
            cp.start()

    @pl.when(jnp.logical_and(i < n_used, new_run))
    def _():
        for cp in weight_copies(e, slot):
            cp.wait()
        nxt = next_e_ref[i]

        @pl.when(nxt >= 0)
        def _():
            for cp in weight_copies(nxt, 1 - slot):
                cp.start(priority=1)

        w1b[...] = wf1[slot].astype(BF16)
        w3b[...] = wf3[slot].astype(BF16)
        w2b[...] = wf2[slot].astype(BF16)

    @pl.when(i < n_used)
    def _():
        xb = _from_packed_token_major(xs_ref, 0, rows).astype(BF16)
        h1 = jnp.dot(xb, w1b[...], preferred_element_type=F32)
        h3 = jnp.dot(xb, w3b[...], preferred_element_type=F32)
        h = (jax.nn.silu(h1) * h3).astype(BF16)
        _to_packed_token_major(y_ref, jnp.dot(h, w2b[...], preferred_element_type=F32), 0, rows)

    @pl.when(i >= n_used)
    def _():
        y_ref[...] = jnp.zeros_like(y_ref)


def _experts(blk_e, n_used, run_idx, next_e, xs, w1, w3, w2):
    rows = MOE_ROWS
    n_blk = blk_e.shape[0]
    hbm = pl.BlockSpec(memory_space=pl.ANY)
    grid_spec = pltpu.PrefetchScalarGridSpec(
        num_scalar_prefetch=4,
        grid=(n_blk,),
        in_specs=[
            pl.BlockSpec((rows * PACKED_CHUNKS, LANES), lambda i, be, nu, *_: (jnp.minimum(i, nu[0] - 1), 0)),
            hbm, hbm, hbm,
        ],
        out_specs=pl.BlockSpec((rows * PACKED_CHUNKS, LANES), lambda i, *_: (i, 0)),
        scratch_shapes=[
            pltpu.VMEM((2, D_MODEL, D_EXPERT), F32),
            pltpu.VMEM((2, D_MODEL, D_EXPERT), F32),
            pltpu.VMEM((2, D_EXPERT, D_MODEL), F32),
            pltpu.SemaphoreType.DMA((2,)),
            pltpu.VMEM((D_MODEL, D_EXPERT), BF16),
            pltpu.VMEM((D_MODEL, D_EXPERT), BF16),
            pltpu.VMEM((D_EXPERT, D_MODEL), BF16),
        ],
    )
    return pl.pallas_call(
        functools.partial(_expert_kernel, rows=rows),
        grid_spec=grid_spec,
        out_shape=jax.ShapeDtypeStruct((n_blk * rows * PACKED_CHUNKS, LANES), U32),
        compiler_params=_params("arbitrary"),
        name="moe_experts",
    )(blk_e, n_used, run_idx, next_e, xs, w1, w3, w2)


def _combine_kernel(slot_ref, ys_hbm, x1_ref, rw_ref, g2_ref, b2_ref, o_ref, yg, sem, *, tm):
    i = pl.program_id(0)
    n = pl.num_programs(0)
    slot = i % 2
    slab = PACKED_CHUNKS
    half = tm * slab
    group = 8

    def start_gather(tile, slot_):
        def body(g, c):
            base = pl.multiple_of(g * (group * slab), group * slab)
            for t in range(group):
                for k in range(2):
                    src_row = slot_ref[tile * (2 * tm) + g * (2 * group) + 2 * t + k] * slab
                    pltpu.make_async_copy(
                        ys_hbm.at[pl.ds(pl.multiple_of(src_row, slab), slab), :],
                        yg.at[slot_, pl.ds(k * half + base + t * slab, slab), :],
                        sem.at[slot_]).start(priority=k)
            return c
        lax.fori_loop(0, tm // group, body, 0)

    @pl.when(i == 0)
    def _():
        start_gather(0, 0)

    @pl.when(i + 1 < n)
    def _():
        start_gather(i + 1, 1 - slot)

    pltpu.make_async_copy(ys_hbm.at[pl.ds(0, 2 * half), :], yg.at[slot], sem.at[slot]).wait()
    rw = rw_ref[...]
    buf = yg.at[slot]
    moe = (_from_packed_token_major(buf, 0, tm) * rw[:, 0:1]
           + _from_packed_token_major(buf, tm, tm) * rw[:, 1:2])
    o_ref[...] = _layer_norm_rows(ALPHA * x1_ref[...] + moe, g2_ref[...], b2_ref[...])


def _combine(slot_flat, ys, x1, rw, ln2_g, ln2_b):
    T = x1.shape[0]
    tm = TM_COMB
    grid_spec = pltpu.PrefetchScalarGridSpec(
        num_scalar_prefetch=1,
        grid=(T // tm,),
        in_specs=[
            pl.BlockSpec(memory_space=pl.ANY),
            pl.BlockSpec((tm, D_MODEL), lambda i, *_: (i, 0)),
            pl.BlockSpec((tm, LANES), lambda i, *_: (i, 0)),
            pl.BlockSpec((1, D_MODEL), lambda i, *_: (0, 0)),
            pl.BlockSpec((1, D_MODEL), lambda i, *_: (0, 0)),
        ],
        out_specs=pl.BlockSpec((tm, D_MODEL), lambda i, *_: (i, 0)),
        scratch_shapes=[
            pltpu.VMEM((2, 2 * tm * PACKED_CHUNKS, LANES), U32),
            pltpu.SemaphoreType.DMA((2,)),
        ],
    )
    return pl.pallas_call(
        functools.partial(_combine_kernel, tm=tm),
        grid_spec=grid_spec,
        out_shape=jax.ShapeDtypeStruct((T, D_MODEL), F32),
        compiler_params=_params("arbitrary"),
        name="moe_combine_ln",
    )(slot_flat, ys, x1, rw, ln2_g.reshape(1, D_MODEL).astype(F32), ln2_b.reshape(1, D_MODEL).astype(F32))


def _dispatch_plan(counts, pk_flat, n_tokens):
    rows = MOE_ROWS
    experts = jnp.arange(N_EXPERTS, dtype=jnp.int32)
    padded = (counts + rows - 1) // rows * rows
    pad_end = jnp.cumsum(padded).astype(jnp.int32)
    pad_start = (pad_end - padded).astype(jnp.int32)
    n_blk = 2 * n_tokens // rows + N_EXPERTS
    blk_start = jnp.arange(n_blk, dtype=jnp.int32) * rows
    blk_e = jnp.minimum(jnp.sum(blk_start[:, None] >= pad_end[None, :], axis=1), N_EXPERTS - 1).astype(jnp.int32)
    n_used = (pad_end[-1:] // rows).astype(jnp.int32)
    new_run = jnp.concatenate([jnp.ones((1,), jnp.int32), (blk_e[1:] != blk_e[:-1]).astype(jnp.int32)])
    run_idx = (jnp.cumsum(new_run) - 1).astype(jnp.int32)
    later_active = (experts[None, :] > experts[:, None]) & (counts[None, :] > 0)
    next_active = jnp.min(jnp.where(later_active, experts[None, :], N_EXPERTS), axis=1)
    next_active = jnp.where(next_active == N_EXPERTS, -1, next_active).astype(jnp.int32)
    next_e = next_active[blk_e]
    e_flat = pk_flat >> RANK_BITS
    start_of = jnp.sum(jnp.where(e_flat[:, None] == experts[None, :], pad_start[None, :], 0), axis=1)
    slot_flat = (start_of + (pk_flat & (RANK_SPAN - 1))).astype(jnp.int32)
    return slot_flat, pad_start, pad_end, blk_e, n_used, run_idx, next_e, n_blk


def kernel(x, w_in, conv_w, conv_b, lru_wa, lru_ba, lru_wx, lru_bx, lru_lambda, attn_norm_g, lru_norm_g,
           w_out, ln1_g, ln1_b, router_grp_w, router_grp_b, router_exp_w, router_exp_b, w1, w3, w2,
           ln2_g, ln2_b):
    B, S, D = x.shape
    assert D == D_MODEL and S % DILATED_PATTERNS[-1][0] == 0 and w_in.shape[0] == 1
    T = B * S
    x2d = x.reshape(T, D)

    qkv, rg = _in_proj(x2d, w_in[0].astype(BF16), B, S)
    attn = _attention(qkv, B, S)
    yrec = _recurrent(rg, conv_w[0], conv_b[0], lru_wa[0], lru_ba[0], lru_wx[0], lru_bx[0], lru_lambda[0],
                      lru_norm_g[0], B, S)

    n_r = N_GROUPS + N_EXPERTS
    w_router = jnp.zeros((D, LANES), F32).at[:, :n_r].set(
        jnp.concatenate([router_grp_w[0], router_exp_w[0]], axis=-1).astype(F32))
    b_router = jnp.zeros((1, LANES), F32).at[0, :n_r].set(
        jnp.concatenate([router_grp_b[0], router_exp_b[0]], axis=-1).astype(F32))
    x1, x1p, pk, rw, cnt = _out_router(attn, yrec, x2d, attn_norm_g[0], w_out[0].astype(BF16),
                                       ln1_g[0], ln1_b[0], w_router, b_router)

    pk_flat = pk[:, :2].reshape(-1)
    slot_flat, pad_start, pad_end, blk_e, n_used, run_idx, next_e, n_blk = _dispatch_plan(
        cnt[0, :N_EXPERTS], pk_flat, T)
    xs = _dispatch(slot_flat, pad_start, pad_end, n_used, x1p, n_blk)
    ys = _experts(blk_e, n_used, run_idx, next_e, xs, w1[0], w3[0], w2[0])
    out = _combine(slot_flat, ys, x1, rw, ln2_g[0], ln2_b[0])
    return out.reshape(B, S, D)
```

```python
import functools

import jax
import jax.numpy as jnp
import numpy as np
from jax import lax
from jax.experimental import pallas as pl
from jax.experimental.pallas import tpu as pltpu

F32 = jnp.float32
BF16 = jnp.bfloat16

D_MODEL = 2048
D_ATTN = 1024
D_LRU = 1024
HEAD_DIM = 64
N_HEADS = 16
LANES = 128
N_PLANES = D_ATTN // LANES
DILATED_PATTERNS = ((128, 1), (512, 4), (2048, 16))
SUB_WINDOW = 128
LRU_BLOCKS = 16
LRU_BLOCK_DIM = 64
CONV_WIDTH = 4
RG_C = 8.0
N_GROUPS = 4
EXPERTS_PER_GROUP = 8
N_EXPERTS = 32
D_EXPERT = 512
ALPHA = 2.0 ** 0.25
LN_EPS = 1e-5
RMS_EPS = 1e-6
MASKED = -1e30
LOG2E = 1.4426950408889634

VMEM_LIMIT = 56 * 1024 * 1024

TM_PROJ = 256
TS_LRU = 256
TM_OUT = 256
MOE_ROWS = 256
TM_COMB = 256
TM_DISP = 256
CHUNKS = D_MODEL // LANES
PACKED_CHUNKS = CHUNKS // 2
RANK_BITS = 16
RANK_SPAN = 1 << RANK_BITS
ATTN_UNROLL = 32


def _params(*sem):
    return pltpu.CompilerParams(dimension_semantics=sem, vmem_limit_bytes=VMEM_LIMIT)


def _in_proj_kernel(x_ref, w_ref, qkv_ref, rg_ref, *, tm):
    xb = x_ref[...].astype(BF16)
    per_class = tm // CLASSES
    out_row = lax.broadcasted_iota(jnp.int32, (tm, tm), 0)
    src_row = (out_row % per_class) * CLASSES + out_row // per_class
    perm = (lax.broadcasted_iota(jnp.int32, (tm, tm), 1) == src_row).astype(BF16)
    xb_cm = jnp.dot(perm, xb, preferred_element_type=F32).astype(BF16)
    for c in range(5):
        lhs = xb_cm if c < 3 else xb
        acc = jnp.dot(lhs, w_ref[:, c * D_ATTN:(c + 1) * D_ATTN], preferred_element_type=F32)
        if c == 0:
            acc = acc * (HEAD_DIM ** -0.5 * LOG2E)
        for p in range(N_PLANES):
            plane = acc[:, p * LANES:(p + 1) * LANES]
            if c < 3:
                for r in range(CLASSES):
                    qkv_ref[c * N_PLANES + p, r] = plane[r * per_class:(r + 1) * per_class, :]
            else:
                rg_ref[(c - 3) * N_PLANES + p] = plane


def _in_proj(x2d, w_in_b, batch, seq):
    T = x2d.shape[0]
    tm = TM_PROJ
    tiles = seq // tm
    assert tm % CLASSES == 0 and seq % tm == 0
    return pl.pallas_call(
        functools.partial(_in_proj_kernel, tm=tm),
        grid=(T // tm,),
        in_specs=[
            pl.BlockSpec((tm, D_MODEL), lambda i: (i, 0)),
            pl.BlockSpec((D_MODEL, 5 * D_ATTN), lambda i: (0, 0), pipeline_mode=pl.Buffered(1)),
        ],
        out_specs=[
            pl.BlockSpec((3 * N_PLANES, None, CLASSES, tm // CLASSES, LANES),
                         lambda i: (0, i // tiles, 0, i % tiles, 0)),
            pl.BlockSpec((2 * N_PLANES, tm, LANES), lambda i: (0, i, 0)),
        ],
        out_shape=[
            jax.ShapeDtypeStruct((3 * N_PLANES, batch, CLASSES, seq // CLASSES, LANES), F32),
            jax.ShapeDtypeStruct((2 * N_PLANES, T, LANES), F32),
        ],
        compiler_params=_params("parallel"),
        name="in_proj",
    )(x2d, w_in_b)


CLASSES = 16


def _attn_kernel(slopes_ref, qs, ks, vs, ndf_ref, ndh_ref, o_ref,
                 os_, m_s, l_s, acc_s, bias_f, bias_h, *, seq):
    hp = pl.program_id(1)
    slope_a = slopes_ref[2 * hp] * LOG2E
    slope_b = slopes_ref[2 * hp + 1] * LOG2E
    w = SUB_WINDOW
    cl = seq // CLASSES
    n_pat = len(DILATED_PATTERNS)

    def gather(ref, starts, size):
        return jnp.concatenate([ref[pl.ds(s, size), :] for s in starts], axis=0)

    def scatter(ref, starts, size, val):
        for c, s in enumerate(starts):
            ref[pl.ds(s, size), :] = val[c * size:(c + 1) * size, :]

    for t in range(2 * n_pat):
        bias_f[t, 0:w, :] = slope_a * ndf_ref[t]
        bias_f[t, w:2 * w, :] = slope_b * ndf_ref[t]
    for t in range(n_pat):
        bias_h[t, 0:w, :] = slope_a * ndh_ref[t]
        bias_h[t, w:2 * w, :] = slope_b * ndh_ref[t]

    def block(c, n, d, pi, first, last, keys):
        qc = w * d // CLASSES
        aligned = lambda x: x if isinstance(x, int) else pl.multiple_of(x, 8)
        bases = [(c + d * j) * cl for j in range(CLASSES // d)]
        qstarts = [aligned(b + n * qc) for b in bases]
        if keys == "own":
            kstarts, kc, bias = qstarts, qc, bias_h[pi]
        elif keys == "prev+own":
            kstarts, kc, bias = [aligned(b + (n - 1) * qc) for b in bases], 2 * qc, bias_f[pi]
        else:
            kfirst = jnp.maximum(n - 1, 0) * qc
            kstarts, kc = [aligned(b + kfirst) for b in bases], 2 * qc
            bias = bias_f[jnp.where(n == 0, pi + n_pat, pi)]
        is_a = lax.broadcasted_iota(jnp.int32, (w, LANES), 1) < HEAD_DIM
        q = gather(qs, qstarts, qc)
        k = gather(ks, kstarts, kc).astype(BF16)
        v = gather(vs, kstarts, kc).astype(BF16)
        zero = jnp.zeros_like(q)
        q2 = jnp.concatenate([jnp.where(is_a, q, zero), jnp.where(is_a, zero, q)], axis=0).astype(BF16)
        s = lax.dot_general(q2, k, (((1,), (1,)), ((), ())), preferred_element_type=F32) + bias
        m = jnp.max(s, axis=-1, keepdims=True)
        p = jnp.exp2(s - m)
        l = jnp.sum(p, axis=-1, keepdims=True)
        o = jnp.dot(p.astype(BF16), v, preferred_element_type=F32)
        m_c = jnp.where(is_a, m[:w], m[w:])
        l_c = jnp.where(is_a, l[:w], l[w:])
        o_c = jnp.where(is_a, o[:w], o[w:])
        if first:
            scatter(m_s, qstarts, qc, m_c)
            scatter(l_s, qstarts, qc, l_c)
            scatter(acc_s, qstarts, qc, o_c)
            return
        m_o = gather(m_s, qstarts, qc)
        m_n = jnp.maximum(m_o, m_c)
        e_o = jnp.exp2(m_o - m_n)
        e_c = jnp.exp2(m_c - m_n)
        l_n = gather(l_s, qstarts, qc) * e_o + l_c * e_c
        a_n = gather(acc_s, qstarts, qc) * e_o + o_c * e_c
        if last:
            scatter(os_, qstarts, qc, a_n / l_n)
        else:
            scatter(m_s, qstarts, qc, m_n)
            scatter(l_s, qstarts, qc, l_n)
            scatter(acc_s, qstarts, qc, a_n)

    for pi, (window, d) in enumerate(DILATED_PATTERNS):
        assert window // d == w and CLASSES % d == 0
        first, last = pi == 0, pi == n_pat - 1
        nb = seq // (w * d)

        if nb >= 4:
            def any_block(i, carry, d=d, pi=pi, first=first, last=last):
                block(i % d, i // d, d, pi, first, last, "any")
                return carry
            lax.fori_loop(0, d * nb, any_block, 0, unroll=ATTN_UNROLL)
        else:
            def class_blocks(c, carry, d=d, pi=pi, first=first, last=last, nb=nb):
                block(c, 0, d, pi, first, last, "own")
                for n in range(1, nb):
                    block(c, n, d, pi, first, last, "prev+own")
                return carry
            lax.fori_loop(0, d, class_blocks, 0, unroll=max(ATTN_UNROLL // nb, 1))

    for r in range(CLASSES):
        o_ref[pl.ds(r, cl, stride=CLASSES), :] = os_[pl.ds(r * cl, cl), :]


def _neg_distance_tables():
    w = SUB_WINDOW
    prev, nxt, own = [], [], []
    for _, d in DILATED_PATTERNS:
        n_cls = CLASSES // d
        qc = w // n_cls
        def pos(chunk_rows):
            j, l = np.divmod(np.arange(n_cls * chunk_rows), chunk_rows)
            return n_cls * l + j
        sq, sk2, sk1 = pos(qc)[:, None], pos(2 * qc)[None, :], pos(qc)[None, :]
        for out, dist in ((prev, sq + w - sk2), (nxt, sq - sk2), (own, sq - sk1)):
            valid = (dist >= 0) & (dist <= w)
            out.append(np.where(valid, -(dist * d).astype(np.float32), np.float32(MASKED)))
    return jnp.asarray(np.stack(prev + nxt), dtype=F32), jnp.asarray(np.stack(own), dtype=F32)


def _attention(qkv, batch, seq):
    qkv4 = qkv.reshape(3 * N_PLANES, batch, seq, LANES)
    slopes = jnp.exp2(-8.0 * jnp.arange(1, N_HEADS + 1, dtype=F32) / N_HEADS)
    nd_full, nd_head = _neg_distance_tables()
    n_pat = len(DILATED_PATTERNS)
    plane = lambda off: pl.BlockSpec((None, None, seq, LANES), lambda b, h: (off + h, b, 0, 0))
    return pl.pallas_call(
        functools.partial(_attn_kernel, seq=seq),
        grid=(batch, N_PLANES),
        in_specs=[
            pl.BlockSpec(memory_space=pltpu.SMEM),
            plane(0), plane(N_PLANES), plane(2 * N_PLANES),
            pl.BlockSpec((2 * n_pat, SUB_WINDOW, 2 * SUB_WINDOW), lambda b, h: (0, 0, 0)),
            pl.BlockSpec((n_pat, SUB_WINDOW, SUB_WINDOW), lambda b, h: (0, 0, 0)),
        ],
        out_specs=pl.BlockSpec((None, None, seq, LANES), lambda b, h: (h, b, 0, 0)),
        scratch_shapes=[pltpu.VMEM((seq, LANES), F32)] * 4 + [
            pltpu.VMEM((2 * n_pat, 2 * SUB_WINDOW, 2 * SUB_WINDOW), F32),
            pltpu.VMEM((n_pat, 2 * SUB_WINDOW, SUB_WINDOW), F32),
        ],
        out_shape=jax.ShapeDtypeStruct((N_PLANES, batch, seq, LANES), F32),
        compiler_params=_params("parallel", "parallel"),
        name="dilated_attention",
    )(slopes, qkv4, qkv4, qkv4, nd_full, nd_head)


def _gelu_tanh(x):
    c = np.float32(np.sqrt(2.0 / np.pi))
    return 0.5 * x * (1.0 + jnp.tanh(c * (x + 0.044715 * (x * x * x))))


def _log1p(x):
    u = 1.0 + x
    return jnp.where(u == 1.0, x, jnp.log(u) * x / (u - 1.0))


def _softplus(z):
    return jnp.maximum(z, 0.0) + _log1p(jnp.exp(-jnp.abs(z)))


PHASES = 8


def _lru_kernel(xr_ref, xg_ref, cw_ref, cb_ref, wbd_ref, ba_ref, bx_ref, lam_ref, g_ref, y_ref,
                xc, hc, *, ts):
    i = pl.program_id(1)
    G = ts // PHASES
    first_group = lax.broadcasted_iota(jnp.int32, (G, LANES), 0) == 0
    sub = lax.broadcasted_iota(jnp.int32, (G, LANES), 0)

    @pl.when(i == 0)
    def _():
        xc[...] = jnp.zeros_like(xc)
        hc[...] = jnp.zeros_like(hc)

    def prev_group(cur, carry_row):
        return jnp.where(first_group, carry_row, pltpu.roll(cur, 1, 0))

    conv = []
    for l in range(N_PLANES):
        x = [xr_ref[l, pl.ds(s, G, stride=PHASES), :] for s in range(PHASES)]
        back = {s: prev_group(x[s], xc[l, s:s + 1, :]) for s in range(PHASES - CONV_WIDTH + 1, PHASES)}
        phases = []
        for s in range(PHASES):
            acc = jnp.broadcast_to(cb_ref[l:l + 1, :], (G, LANES))
            for j in range(CONV_WIDTH):
                q = s - (CONV_WIDTH - 1) + j
                acc = acc + (x[q] if q >= 0 else back[q + PHASES]) * cw_ref[j, l:l + 1, :]
            phases.append(acc)
        conv.append(jnp.concatenate(phases, axis=0))
        xc[l] = xr_ref[l, ts - PHASES:ts, :]

    gates_r, gates_i = [], []
    for j in range(N_PLANES // 2):
        yb = jnp.concatenate([conv[2 * j], conv[2 * j + 1]], axis=-1).astype(BF16)
        g = jnp.dot(yb, wbd_ref[j], preferred_element_type=F32)
        gates_r += [g[:, 0:LANES], g[:, LANES:2 * LANES]]
        gates_i += [g[:, 2 * LANES:3 * LANES], g[:, 3 * LANES:4 * LANES]]

    rec, sq = [], None
    for l in range(N_PLANES):
        y = conv[l]
        r = jax.nn.sigmoid(gates_r[l] + ba_ref[l:l + 1, :])
        ig = jax.nn.sigmoid(gates_i[l] + bx_ref[l:l + 1, :])
        log_a = (-RG_C * r) * _softplus(-lam_ref[l:l + 1, :])
        a = jnp.exp(log_a)
        z = 1.0 - a * a
        u = jnp.where(z > 0.0, z * lax.rsqrt(z), 0.0) * (ig * y)

        piece = lambda v, s: v[s * G:(s + 1) * G, :]
        pa, pb = [piece(a, 0)], [piece(u, 0)]
        for s in range(1, PHASES):
            pa.append(piece(a, s) * pa[-1])
            pb.append(piece(a, s) * pb[-1] + piece(u, s))
        ga, gb = pa[-1], pb[-1]
        sh = 1
        while sh < G:
            take = sub >= sh
            gb = jnp.where(take, ga * pltpu.roll(gb, sh, 0) + gb, gb)
            ga = jnp.where(take, ga * pltpu.roll(ga, sh, 0), ga)
            sh *= 2
        h0 = hc[l, 0:1, :]
        h_end = ga * h0 + gb
        h_in = prev_group(h_end, h0)
        hc[l] = jnp.broadcast_to(h_end[G - 1:G, :], (PHASES, LANES))
        h = jnp.concatenate([pa[s] * h_in + pb[s] for s in range(PHASES)], axis=0)

        xg = jnp.concatenate([xg_ref[l, pl.ds(s, G, stride=PHASES), :] for s in range(PHASES)], axis=0)
        rl = h * _gelu_tanh(xg)
        rec.append(rl)
        part = jnp.sum(rl * rl, axis=-1, keepdims=True)
        sq = part if sq is None else sq + part

    scale = lax.rsqrt(sq * (1.0 / D_LRU) + RMS_EPS)
    for l in range(N_PLANES):
        out = rec[l] * scale * g_ref[l:l + 1, :]
        for s in range(PHASES):
            y_ref[l, pl.ds(s, G, stride=PHASES), :] = out[s * G:(s + 1) * G, :]


def _block_diag_gates(wa, wx):
    def bd(wm):
        wm = wm.reshape(4, 4, LRU_BLOCK_DIM, LRU_BLOCK_DIM)
        eye = jnp.eye(4, dtype=wm.dtype)
        full = jnp.einsum("gide,ij->gidje", wm, eye)
        return full.reshape(4, 4 * LRU_BLOCK_DIM, 4 * LRU_BLOCK_DIM)
    return jnp.concatenate([bd(wa), bd(wx)], axis=-1).astype(BF16)


def _recurrent(rg, conv_w, conv_b, lru_wa, lru_ba, lru_wx, lru_bx, lru_lambda, lru_norm_g, batch, seq):
    ts = TS_LRU
    rg5 = rg.reshape(2, N_PLANES, batch, seq, LANES)
    wbd = _block_diag_gates(lru_wa, lru_wx)
    planes = lambda a: a.reshape(N_PLANES, LANES).astype(F32)
    vec = pl.BlockSpec((N_PLANES, LANES), lambda b, i: (0, 0))
    return pl.pallas_call(
        functools.partial(_lru_kernel, ts=ts),
        grid=(batch, seq // ts),
        in_specs=[
            pl.BlockSpec((None, N_PLANES, None, ts, LANES), lambda b, i: (0, 0, b, i, 0)),
            pl.BlockSpec((None, N_PLANES, None, ts, LANES), lambda b, i: (1, 0, b, i, 0)),
            pl.BlockSpec((CONV_WIDTH, N_PLANES, LANES), lambda b, i: (0, 0, 0)),
            vec,
            pl.BlockSpec((4, 4 * LRU_BLOCK_DIM, 8 * LRU_BLOCK_DIM), lambda b, i: (0, 0, 0)),
            vec, vec, vec, vec,
        ],
        out_specs=pl.BlockSpec((N_PLANES, None, ts, LANES), lambda b, i: (0, b, i, 0)),
        out_shape=jax.ShapeDtypeStruct((N_PLANES, batch, seq, LANES), F32),
        scratch_shapes=[
            pltpu.VMEM((N_PLANES, PHASES, LANES), F32),
            pltpu.VMEM((N_PLANES, PHASES, LANES), F32),
        ],
        compiler_params=_params("parallel", "arbitrary"),
        name="conv_rglru",
    )(rg5, rg5, conv_w.astype(F32).reshape(CONV_WIDTH, N_PLANES, LANES), planes(conv_b), wbd, planes(lru_ba),
      planes(lru_bx), planes(lru_lambda), planes(lru_norm_g))


def _layer_norm_rows(z, g, b):
    mu = jnp.mean(z, axis=-1, keepdims=True)
    zc = z - mu
    var = jnp.mean(zc * zc, axis=-1, keepdims=True)
    return zc * lax.rsqrt(var + LN_EPS) * g + b


U32 = jnp.uint32
HI_HALF = np.uint32(0xFFFF0000)


def _to_packed_token_major(ref, val, first_row, n_rows):
    bits = lambda a: lax.bitcast_convert_type(a.astype(BF16).astype(F32), U32)
    half = PACKED_CHUNKS * LANES
    for c in range(PACKED_CHUNKS):
        lo = bits(val[:, c * LANES:(c + 1) * LANES]) >> 16
        hi = bits(val[:, half + c * LANES:half + (c + 1) * LANES]) & HI_HALF
        ref[pl.ds(first_row * PACKED_CHUNKS + c, n_rows, stride=PACKED_CHUNKS), :] = lo | hi


def _from_packed_token_major(ref, first_row, n_rows):
    words = [ref[pl.ds(first_row * PACKED_CHUNKS + c, n_rows, stride=PACKED_CHUNKS), :]
             for c in range(PACKED_CHUNKS)]
    lo = [lax.bitcast_convert_type(wd << 16, F32) for wd in words]
    hi = [lax.bitcast_convert_type(wd & HI_HALF, F32) for wd in words]
    return jnp.concatenate(lo + hi, axis=-1)


def _out_router_kernel(*refs, tm):
    *io_refs, cnt_s, z_a, z_b = refs
    i = pl.program_id(0)

    @pl.when(i == 0)
    def _():
        cnt_s[...] = jnp.zeros_like(cnt_s)
        z_b[...] = jnp.zeros_like(z_b)

    @pl.when(i % 2 == 0)
    def _():
        _out_router_step(*io_refs, cnt_s, z_b, z_a, tm=tm)

    @pl.when(i % 2 == 1)
    def _():
        _out_router_step(*io_refs, cnt_s, z_a, z_b, tm=tm)


def _out_router_step(attn_ref, yrec_ref, x_ref, ga_ref, wo_ref, g1_ref, b1_ref, wr_ref, br_ref,
                     x1_ref, x1p_ref, pk_ref, rw_ref, cnt_ref, cnt_s, z_in, z_out, *, tm):
    i = pl.program_id(0)
    attn = jnp.concatenate([attn_ref[p] for p in range(N_PLANES)], axis=-1)
    ms = jnp.mean(attn * attn, axis=-1, keepdims=True)
    ya = (attn * lax.rsqrt(ms + RMS_EPS) * ga_ref[...]).astype(BF16)
    yr = jnp.concatenate([yrec_ref[p] for p in range(N_PLANES)], axis=-1).astype(BF16)
    n_col = 8
    cw = D_MODEL // n_col

    def project(j):
        cols = slice(j * cw, (j + 1) * cw)
        mix = jnp.dot(ya, wo_ref[0:D_ATTN, cols], preferred_element_type=F32)
        mix = mix + jnp.dot(yr, wo_ref[D_ATTN:, cols], preferred_element_type=F32)
        z_out[:, cols] = ALPHA * x_ref[:, cols] + mix

    def normalize(r0, n_rows):
        rows = slice(r0, r0 + n_rows)
        x1 = _layer_norm_rows(z_in[rows, :], g1_ref[...], b1_ref[...])
        x1_ref[rows, :] = x1
        _to_packed_token_major(x1p_ref, x1, r0, n_rows)
        return x1.astype(BF16)

    n_row = 4
    rh = tm // n_row
    parts = []
    for k in range(n_row):
        parts.append(normalize(k * rh, rh))
        project(k)

    xb = jnp.concatenate(parts, axis=0)
    logits = jnp.dot(xb, wr_ref[...].astype(BF16), preferred_element_type=F32) + br_ref[...]
    project(4)
    project(5)
    _route(logits, pk_ref, rw_ref, cnt_ref, cnt_s, i, tm)
    project(6)
    project(7)


def _route(logits, pk_ref, rw_ref, cnt_ref, cnt_s, i, tm):
    lane = lax.broadcasted_iota(jnp.int32, (tm, LANES), 1)
    big = jnp.int32(LANES)
    first_true = lambda c: jnp.min(jnp.where(c, lane, big), axis=-1, keepdims=True)

    in_g = lane < N_GROUPS
    gl = jnp.where(in_g, logits, MASKED)
    gmax = jnp.max(gl, axis=-1, keepdims=True)
    g_idx = first_true(gl == gmax)
    gsum = jnp.sum(jnp.where(in_g, jnp.exp(gl - gmax), 0.0), axis=-1, keepdims=True)
    g_gate = 1.0 / gsum

    lo = N_GROUPS + EXPERTS_PER_GROUP * g_idx
    in_e = (lane >= lo) & (lane < lo + EXPERTS_PER_GROUP)
    el = jnp.where(in_e, logits, MASKED)
    emax = jnp.max(el, axis=-1, keepdims=True)
    ee = jnp.where(in_e, jnp.exp(el - emax), 0.0)
    esum = jnp.sum(ee, axis=-1, keepdims=True)
    i1 = first_true(el == emax)
    rest = jnp.where(in_e & (lane != i1), ee, -1.0)
    e2max = jnp.max(rest, axis=-1, keepdims=True)
    i2 = first_true(rest == e2max)
    v1 = 1.0 / esum
    v2 = e2max / esum
    den = v1 + v2
    w1 = g_gate * v1 / den
    w2 = g_gate * v2 / den
    e1 = i1 - N_GROUPS
    e2 = i2 - N_GROUPS
    rw_ref[...] = jnp.where(lane == 0, w1, jnp.where(lane == 1, w2, 0.0))

    hot1 = lane == e1
    hot2 = lane == e2
    both = jnp.where((hot1 | hot2) & (i > 0), 1.0, 0.0)
    earlier = (lax.broadcasted_iota(jnp.int32, (tm, tm), 1) < lax.broadcasted_iota(jnp.int32, (tm, tm), 0))
    before = jnp.dot(earlier.astype(BF16), both.astype(BF16), preferred_element_type=F32) + cnt_s[0:1, :]
    r1 = jnp.sum(jnp.where(hot1, before, 0.0), axis=-1, keepdims=True).astype(jnp.int32)
    r2 = jnp.sum(jnp.where(hot2, before, 0.0), axis=-1, keepdims=True).astype(jnp.int32)
    pk_ref[...] = jnp.where(lane == 0, e1 * RANK_SPAN + r1, jnp.where(lane == 1, e2 * RANK_SPAN + r2, 0))
    cnt_s[...] = cnt_s[...] + jnp.sum(both, axis=0, keepdims=True)
    cnt_ref[...] = cnt_s[...].astype(jnp.int32)


def _out_router(attn, yrec, x2d, attn_norm_g, w_out_b, ln1_g, ln1_b, w_router, b_router):
    T = x2d.shape[0]
    tm = TM_OUT
    assert 2 * T <= RANK_SPAN
    attn3 = attn.reshape(N_PLANES, T, LANES)
    yrec3 = yrec.reshape(N_PLANES, T, LANES)
    const = lambda shape: pl.BlockSpec(shape, lambda i: (0,) * len(shape))
    n = T // tm
    stage1 = lambda i: jnp.minimum(i, n - 1)
    stage2 = lambda i: jnp.maximum(i - 1, 0)
    return pl.pallas_call(
        functools.partial(_out_router_kernel, tm=tm),
        grid=(n + 1,),
        in_specs=[
            pl.BlockSpec((N_PLANES, tm, LANES), lambda i: (0, stage1(i), 0)),
            pl.BlockSpec((N_PLANES, tm, LANES), lambda i: (0, stage1(i), 0)),
            pl.BlockSpec((tm, D_MODEL), lambda i: (stage1(i), 0)),
            const((1, D_ATTN)),
            pl.BlockSpec((D_MODEL, D_MODEL), lambda i: (0, 0), pipeline_mode=pl.Buffered(1)),
            const((1, D_MODEL)), const((1, D_MODEL)),
            const((D_MODEL, LANES)), const((1, LANES)),
        ],
        out_specs=[
            pl.BlockSpec((tm, D_MODEL), lambda i: (stage2(i), 0)),
            pl.BlockSpec((tm * PACKED_CHUNKS, LANES), lambda i: (stage2(i), 0)),
            pl.BlockSpec((tm, LANES), lambda i: (stage2(i), 0)),
            pl.BlockSpec((tm, LANES), lambda i: (stage2(i), 0)),
            const((8, LANES)),
        ],
        out_shape=[
            jax.ShapeDtypeStruct((T, D_MODEL), F32),
            jax.ShapeDtypeStruct((T * PACKED_CHUNKS, LANES), U32),
            jax.ShapeDtypeStruct((T, LANES), jnp.int32),
            jax.ShapeDtypeStruct((T, LANES), F32),
            jax.ShapeDtypeStruct((8, LANES), jnp.int32),
        ],
        scratch_shapes=[pltpu.VMEM((8, LANES), F32), pltpu.VMEM((tm, D_MODEL), F32),
                        pltpu.VMEM((tm, D_MODEL), F32)],
        compiler_params=_params("arbitrary"),
        name="out_proj_router",
    )(attn3, yrec3, x2d, attn_norm_g.reshape(1, D_ATTN).astype(F32), w_out_b,
      ln1_g.reshape(1, D_MODEL).astype(F32), ln1_b.reshape(1, D_MODEL).astype(F32), w_router, b_router)


def _dispatch_kernel(slot_ref, pstart_ref, pend_ref, nused_ref, x1p_ref, xs_hbm, zbuf, sem, zsem,
                     *, tm, rows, n_blk):
    i = pl.program_id(0)
    slab = PACKED_CHUNKS
    blk = rows * slab

    @pl.when(i == 0)
    def _():
        zbuf[...] = jnp.zeros_like(zbuf)
        n_used = nused_ref[0]

        def zero_copy(b):
            return pltpu.make_async_copy(zbuf, xs_hbm.at[pl.ds(pl.multiple_of(b * blk, blk), blk), :], zsem)

        def for_each_zeroed_block(fn):
            for e in range(N_EXPERTS):
                @pl.when(pend_ref[e] > pstart_ref[e])
                def _(e=e):
                    fn(pend_ref[e] // rows - 1)

            def tail(b, c):
                fn(b)
                return c
            lax.fori_loop(n_used, n_blk, tail, 0)

        for_each_zeroed_block(lambda b: zero_copy(b).start())
        for_each_zeroed_block(lambda b: zero_copy(b).wait())

    group = 8

    def push(g, c):
        for t in range(group):
            src = x1p_ref.at[pl.ds(pl.multiple_of(g * (group * slab), group * slab) + t * slab, slab), :]
            for k in range(2):
                row = slot_ref[i * (2 * tm) + g * (2 * group) + 2 * t + k] * slab
                dst = xs_hbm.at[pl.ds(pl.multiple_of(row, slab), slab), :]
                pltpu.make_async_copy(src, dst, sem).start(priority=k)
        return c
    lax.fori_loop(0, tm // group, push, 0)
    for _ in range(2):
        pltpu.make_async_copy(x1p_ref, xs_hbm.at[pl.ds(0, tm * slab), :], sem).wait()


def _dispatch(slot_flat, pad_start, pad_end, n_used, x1p, n_blk):
    tm, rows, slab = TM_DISP, MOE_ROWS, PACKED_CHUNKS
    n_tok = x1p.shape[0] // slab
    grid_spec = pltpu.PrefetchScalarGridSpec(
        num_scalar_prefetch=4,
        grid=(n_tok // tm,),
        in_specs=[pl.BlockSpec((tm * slab, LANES), lambda i, *_: (i, 0))],
        out_specs=pl.BlockSpec(memory_space=pl.ANY),
        scratch_shapes=[
            pltpu.VMEM((rows * slab, LANES), U32),
            pltpu.SemaphoreType.DMA(()),
            pltpu.SemaphoreType.DMA(()),
        ],
    )
    return pl.pallas_call(
        functools.partial(_dispatch_kernel, tm=tm, rows=rows, n_blk=n_blk),
        grid_spec=grid_spec,
        out_shape=jax.ShapeDtypeStruct((n_blk * rows * slab, LANES), U32),
        compiler_params=_params("arbitrary"),
        name="moe_dispatch",
    )(slot_flat, pad_start, pad_end, n_used, x1p)


def _expert_kernel(blk_e_ref, n_used_ref, run_ref, next_e_ref, xs_ref, w1_hbm, w3_hbm, w2_hbm, y_ref,
                   wf1, wf3, wf2, wsem, w1b, w3b, w2b, *, rows):
    i = pl.program_id(0)
    n_used = n_used_ref[0]
    e = blk_e_ref[i]
    slot = run_ref[i] % 2
    new_run = jnp.logical_or(i == 0, e != blk_e_ref[jnp.maximum(i - 1, 0)])

    def weight_copies(expert, slot_):
        return [pltpu.make_async_copy(src.at[expert], dst.at[slot_], wsem.at[slot_])
                for src, dst in ((w1_hbm, wf1), (w3_hbm, wf3), (w2_hbm, wf2))]

    @pl.when(i == 0)
    def _():
        for cp in weight_copies(e, 0):
            cp.start()

    @pl.when(jnp.logical_and(i < n_used, new_run))
    def _():
        for cp in weight_copies(e, slot):
            cp.wait()
        nxt = next_e_ref[i]

        @pl.when(nxt >= 0)
        def _():
            for cp in weight_copies(nxt, 1 - slot):
                cp.start(priority=1)

        w1b[...] = wf1[slot].astype(BF16)
        w3b[...] = wf3[slot].astype(BF16)
        w2b[...] = wf2[slot].astype(BF16)

    @pl.when(i < n_used)
    def _():
        xb = _from_packed_token_major(xs_ref, 0, rows).astype(BF16)
        h1 = jnp.dot(xb, w1b[...], preferred_element_type=F32)
        h3 = jnp.dot(xb, w3b[...], preferred_element_type=F32)
        h = (jax.nn.silu(h1) * h3).astype(BF16)
        _to_packed_token_major(y_ref, jnp.dot(h, w2b[...], preferred_element_type=F32), 0, rows)

    @pl.when(i >= n_used)
    def _():
        y_ref[...] = jnp.zeros_like(y_ref)


def _experts(blk_e, n_used, run_idx, next_e, xs, w1, w3, w2):
    rows = MOE_ROWS
    n_blk = blk_e.shape[0]
    hbm = pl.BlockSpec(memory_space=pl.ANY)
    grid_spec = pltpu.PrefetchScalarGridSpec(
        num_scalar_prefetch=4,
        grid=(n_blk,),
        in_specs=[
            pl.BlockSpec((rows * PACKED_CHUNKS, LANES), lambda i, be, nu, *_: (jnp.minimum(i, nu[0] - 1), 0)),
            hbm, hbm, hbm,
        ],
        out_specs=pl.BlockSpec((rows * PACKED_CHUNKS, LANES), lambda i, *_: (i, 0)),
        scratch_shapes=[
            pltpu.VMEM((2, D_MODEL, D_EXPERT), F32),
            pltpu.VMEM((2, D_MODEL, D_EXPERT), F32),
            pltpu.VMEM((2, D_EXPERT, D_MODEL), F32),
            pltpu.SemaphoreType.DMA((2,)),
            pltpu.VMEM((D_MODEL, D_EXPERT), BF16),
            pltpu.VMEM((D_MODEL, D_EXPERT), BF16),
            pltpu.VMEM((D_EXPERT, D_MODEL), BF16),
        ],
    )
    return pl.pallas_call(
        functools.partial(_expert_kernel, rows=rows),
        grid_spec=grid_spec,
        out_shape=jax.ShapeDtypeStruct((n_blk * rows * PACKED_CHUNKS, LANES), U32),
        compiler_params=_params("arbitrary"),
        name="moe_experts",
    )(blk_e, n_used, run_idx, next_e, xs, w1, w3, w2)


def _combine_kernel(slot_ref, ys_hbm, x1_ref, rw_ref, g2_ref, b2_ref, o_ref, yg, sem, *, tm):
    i = pl.program_id(0)
    n = pl.num_programs(0)
    slot = i % 2
    slab = PACKED_CHUNKS
    half = tm * slab
    group = 8

    def start_gather(tile, slot_):
        def body(g, c):
            base = pl.multiple_of(g * (group * slab), group * slab)
            for t in range(group):
                for k in range(2):
                    src_row = slot_ref[tile * (2 * tm) + g * (2 * group) + 2 * t + k] * slab
                    pltpu.make_async_copy(
                        ys_hbm.at[pl.ds(pl.multiple_of(src_row, slab), slab), :],
                        yg.at[slot_, pl.ds(k * half + base + t * slab, slab), :],
                        sem.at[slot_]).start(priority=k)
            return c
        lax.fori_loop(0, tm // group, body, 0)

    @pl.when(i == 0)
    def _():
        start_gather(0, 0)

    @pl.when(i + 1 < n)
    def _():
        start_gather(i + 1, 1 - slot)

    pltpu.make_async_copy(ys_hbm.at[pl.ds(0, 2 * half), :], yg.at[slot], sem.at[slot]).wait()
    rw = rw_ref[...]
    buf = yg.at[slot]
    moe = (_from_packed_token_major(buf, 0, tm) * rw[:, 0:1]
           + _from_packed_token_major(buf, tm, tm) * rw[:, 1:2])
    o_ref[...] = _layer_norm_rows(ALPHA * x1_ref[...] + moe, g2_ref[...], b2_ref[...])


def _combine(slot_flat, ys, x1, rw, ln2_g, ln2_b):
    T = x1.shape[0]
    tm = TM_COMB
    grid_spec = pltpu.PrefetchScalarGridSpec(
        num_scalar_prefetch=1,
        grid=(T // tm,),
        in_specs=[
            pl.BlockSpec(memory_space=pl.ANY),
            pl.BlockSpec((tm, D_MODEL), lambda i, *_: (i, 0)),
            pl.BlockSpec((tm, LANES), lambda i, *_: (i, 0)),
            pl.BlockSpec((1, D_MODEL), lambda i, *_: (0, 0)),
            pl.BlockSpec((1, D_MODEL), lambda i, *_: (0, 0)),
        ],
        out_specs=pl.BlockSpec((tm, D_MODEL), lambda i, *_: (i, 0)),
        scratch_shapes=[
            pltpu.VMEM((2, 2 * tm * PACKED_CHUNKS, LANES), U32),
            pltpu.SemaphoreType.DMA((2,)),
        ],
    )
    return pl.pallas_call(
        functools.partial(_combine_kernel, tm=tm),
        grid_spec=grid_spec,
        out_shape=jax.ShapeDtypeStruct((T, D_MODEL), F32),
        compiler_params=_params("arbitrary"),
        name="moe_combine_ln",
    )(slot_flat, ys, x1, rw, ln2_g.reshape(1, D_MODEL).astype(F32), ln2_b.reshape(1, D_MODEL).astype(F32))


def _dispatch_plan(counts, pk_flat, n_tokens):
    rows = MOE_ROWS
    experts = jnp.arange(N_EXPERTS, dtype=jnp.int32)
    padded = (counts + rows - 1) // rows * rows
    pad_end = jnp.cumsum(padded).astype(jnp.int32)
    pad_start = (pad_end - padded).astype(jnp.int32)
    n_blk = 2 * n_tokens // rows + N_EXPERTS
    blk_start = jnp.arange(n_blk, dtype=jnp.int32) * rows
    blk_e = jnp.minimum(jnp.sum(blk_start[:, None] >= pad_end[None, :], axis=1), N_EXPERTS - 1).astype(jnp.int32)
    n_used = (pad_end[-1:] // rows).astype(jnp.int32)
    new_run = jnp.concatenate([jnp.ones((1,), jnp.int32), (blk_e[1:] != blk_e[:-1]).astype(jnp.int32)])
    run_idx = (jnp.cumsum(new_run) - 1).astype(jnp.int32)
    later_active = (experts[None, :] > experts[:, None]) & (counts[None, :] > 0)
    next_active = jnp.min(jnp.where(later_active, experts[None, :], N_EXPERTS), axis=1)
    next_active = jnp.where(next_active == N_EXPERTS, -1, next_active).astype(jnp.int32)
    next_e = next_active[blk_e]
    e_flat = pk_flat >> RANK_BITS
    start_of = jnp.sum(jnp.where(e_flat[:, None] == experts[None, :], pad_start[None, :], 0), axis=1)
    slot_flat = (start_of + (pk_flat & (RANK_SPAN - 1))).astype(jnp.int32)
    return slot_flat, pad_start, pad_end, blk_e, n_used, run_idx, next_e, n_blk


def kernel(x, w_in, conv_w, conv_b, lru_wa, lru_ba, lru_wx, lru_bx, lru_lambda, attn_norm_g, lru_norm_g,
           w_out, ln1_g, ln1_b, router_grp_w, router_grp_b, router_exp_w, router_exp_b, w1, w3, w2,
           ln2_g, ln2_b):
    B, S, D = x.shape
    assert D == D_MODEL and S % DILATED_PATTERNS[-1][0] == 0 and w_in.shape[0] == 1
    T = B * S
    x2d = x.reshape(T, D)

    qkv, rg = _in_proj(x2d, w_in[0].astype(BF16), B, S)
    attn = _attention(qkv, B, S)
    yrec = _recurrent(rg, conv_w[0], conv_b[0], lru_wa[0], lru_ba[0], lru_wx[0], lru_bx[0], lru_lambda[0],
                      lru_norm_g[0], B, S)

    n_r = N_GROUPS + N_EXPERTS
    w_router = jnp.zeros((D, LANES), F32).at[:, :n_r].set(
        jnp.concatenate([router_grp_w[0], router_exp_w[0]], axis=-1).astype(F32))
    b_router = jnp.zeros((1, LANES), F32).at[0, :n_r].set(
        jnp.concatenate([router_grp_b[0], router_exp_b[0]], axis=-1).astype(F32))
    x1, x1p, pk, rw, cnt = _out_router(attn, yrec, x2d, attn_norm_g[0], w_out[0].astype(BF16),
                                       ln1_g[0], ln1_b[0], w_router, b_router)

    pk_flat = pk[:, :2].reshape(-1)
    slot_flat, pad_start, pad_end, blk_e, n_used, run_idx, next_e, n_blk = _dispatch_plan(
        cnt[0, :N_EXPERTS], pk_flat, T)
    xs = _dispatch(slot_flat, pad_start, pad_end, n_used, x1p, n_blk)
    ys = _experts(blk_e, n_used, run_idx, next_e, xs, w1[0], w3[0], w2[0])
    out = _combine(slot_flat, ys, x1, rw, ln2_g[0], ln2_b[0])
    return out.reshape(B, S, D)
```

```python
import functools

import jax
import jax.numpy as jnp
import numpy as np
from jax import lax
from jax.experimental import pallas as pl
from jax.experimental.pallas import tpu as pltpu

F32 = jnp.float32
BF16 = jnp.bfloat16

D_MODEL = 2048
D_ATTN = 1024
D_LRU = 1024
HEAD_DIM = 64
N_HEADS = 16
LANES = 128
N_PLANES = D_ATTN // LANES
DILATED_PATTERNS = ((128, 1), (512, 4), (2048, 16))
SUB_WINDOW = 128
LRU_BLOCKS = 16
LRU_BLOCK_DIM = 64
CONV_WIDTH = 4
RG_C = 8.0
N_GROUPS = 4
EXPERTS_PER_GROUP = 8
N_EXPERTS = 32
D_EXPERT = 512
ALPHA = 2.0 ** 0.25
LN_EPS = 1e-5
RMS_EPS = 1e-6
MASKED = -1e30
LOG2E = 1.4426950408889634

VMEM_LIMIT = 56 * 1024 * 1024

TM_PROJ = 256
TM_OUT = 256
MOE_ROWS = 256
TM_COMB = 256
TM_DISP = 256
CHUNKS = D_MODEL // LANES
PACKED_CHUNKS = CHUNKS // 2
RANK_BITS = 16
RANK_SPAN = 1 << RANK_BITS
ATTN_UNROLL = 32


def _params(*sem):
    return pltpu.CompilerParams(dimension_semantics=sem, vmem_limit_bytes=VMEM_LIMIT)


def _in_proj_kernel(*refs, tm, tiles):
    *io_refs, xc, hc, rg_a, rg_b = refs
    i = pl.program_id(0)

    @pl.when(i == 0)
    def _():
        rg_b[...] = jnp.zeros_like(rg_b)

    @pl.when(jnp.logical_or(i == 0, (i - 1) % tiles == 0))
    def _():
        xc[...] = jnp.zeros_like(xc)
        hc[...] = jnp.zeros_like(hc)

    @pl.when(i % 2 == 0)
    def _():
        _in_proj_step(*io_refs, xc, hc, rg_b, rg_a, tm=tm)

    @pl.when(i % 2 == 1)
    def _():
        _in_proj_step(*io_refs, xc, hc, rg_a, rg_b, tm=tm)


def _in_proj_step(x_ref, w_ref, cw_ref, cb_ref, wbd_ref, ba_ref, bx_ref, lam_ref, g_ref, qkv_ref, y_ref,
                  xc, hc, rg_in, rg_out, *, tm):
    lru = _lru_pieces(rg_in, cw_ref, cb_ref, wbd_ref, ba_ref, bx_ref, lam_ref, g_ref, y_ref, xc, hc, tm)
    xb = x_ref[...].astype(BF16)
    per_class = tm // CLASSES
    out_row = lax.broadcasted_iota(jnp.int32, (tm, tm), 0)
    src_row = (out_row % per_class) * CLASSES + out_row // per_class
    perm = (lax.broadcasted_iota(jnp.int32, (tm, tm), 1) == src_row).astype(BF16)
    xb_cm = jnp.dot(perm, xb, preferred_element_type=F32).astype(BF16)
    cw = 2 * LANES
    for c in range(5):
        lhs = xb_cm if c < 3 else xb
        for q in range(D_ATTN // cw):
            col = c * D_ATTN + q * cw
            acc = jnp.dot(lhs, w_ref[:, col:col + cw], preferred_element_type=F32)
            if c == 0:
                acc = acc * (HEAD_DIM ** -0.5 * LOG2E)
            for t in range(2):
                p = 2 * q + t
                plane = acc[:, t * LANES:(t + 1) * LANES]
                if c < 3:
                    for r in range(CLASSES):
                        qkv_ref[c * N_PLANES + p, r] = plane[r * per_class:(r + 1) * per_class, :]
                else:
                    rg_out[(c - 3) * N_PLANES + p] = plane
            next(lru, None)
            if c < 2:
                next(lru, None)
    for _ in lru:
        pass


def _in_proj(x2d, w_in_b, conv_w, conv_b, lru_wa, lru_ba, lru_wx, lru_bx, lru_lambda, lru_norm_g, batch, seq):
    T = x2d.shape[0]
    tm = TM_PROJ
    tiles = seq // tm
    n = T // tm
    assert tm % CLASSES == 0 and seq % tm == 0
    wbd = _block_diag_gates(lru_wa, lru_wx)
    planes = lambda a: a.reshape(N_PLANES, LANES).astype(F32)
    vec = pl.BlockSpec((N_PLANES, LANES), lambda i: (0, 0))
    proj = lambda i: jnp.minimum(i, n - 1)
    rec = lambda i: jnp.maximum(i - 1, 0)
    return pl.pallas_call(
        functools.partial(_in_proj_kernel, tm=tm, tiles=tiles),
        grid=(n + 1,),
        in_specs=[
            pl.BlockSpec((tm, D_MODEL), lambda i: (proj(i), 0)),
            pl.BlockSpec((D_MODEL, 5 * D_ATTN), lambda i: (0, 0), pipeline_mode=pl.Buffered(1)),
            pl.BlockSpec((CONV_WIDTH, N_PLANES, LANES), lambda i: (0, 0, 0)),
            vec,
            pl.BlockSpec((4, 4 * LRU_BLOCK_DIM, 8 * LRU_BLOCK_DIM), lambda i: (0, 0, 0)),
            vec, vec, vec, vec,
        ],
        out_specs=[
            pl.BlockSpec((3 * N_PLANES, None, CLASSES, tm // CLASSES, LANES),
                         lambda i: (0, proj(i) // tiles, 0, proj(i) % tiles, 0)),
            pl.BlockSpec((N_PLANES, tm, LANES), lambda i: (0, rec(i), 0)),
        ],
        out_shape=[
            jax.ShapeDtypeStruct((3 * N_PLANES, batch, CLASSES, seq // CLASSES, LANES), F32),
            jax.ShapeDtypeStruct((N_PLANES, T, LANES), F32),
        ],
        scratch_shapes=[
            pltpu.VMEM((N_PLANES, PHASES, LANES), F32),
            pltpu.VMEM((N_PLANES, PHASES, LANES), F32),
            pltpu.VMEM((2 * N_PLANES, tm, LANES), F32),
            pltpu.VMEM((2 * N_PLANES, tm, LANES), F32),
        ],
        compiler_params=_params("arbitrary"),
        name="in_proj_rglru",
    )(x2d, w_in_b, conv_w.astype(F32).reshape(CONV_WIDTH, N_PLANES, LANES), planes(conv_b), wbd,
      planes(lru_ba), planes(lru_bx), planes(lru_lambda), planes(lru_norm_g))


CLASSES = 16


def _attn_kernel(slopes_ref, qs, ks, vs, ndf_ref, ndh_ref, o_ref,
                 os_, m_s, l_s, acc_s, bias_f, bias_h, *, seq):
    hp = pl.program_id(1)
    slope_a = slopes_ref[2 * hp] * LOG2E
    slope_b = slopes_ref[2 * hp + 1] * LOG2E
    w = SUB_WINDOW
    cl = seq // CLASSES
    n_pat = len(DILATED_PATTERNS)

    def gather(ref, starts, size):
        return jnp.concatenate([ref[pl.ds(s, size), :] for s in starts], axis=0)

    def scatter(ref, starts, size, val):
        for c, s in enumerate(starts):
            ref[pl.ds(s, size), :] = val[c * size:(c + 1) * size, :]

    for t in range(2 * n_pat):
        bias_f[t, 0:w, :] = slope_a * ndf_ref[t]
        bias_f[t, w:2 * w, :] = slope_b * ndf_ref[t]
    for t in range(n_pat):
        bias_h[t, 0:w, :] = slope_a * ndh_ref[t]
        bias_h[t, w:2 * w, :] = slope_b * ndh_ref[t]

    def block(c, n, d, pi, first, last, keys):
        qc = w * d // CLASSES
        aligned = lambda x: x if isinstance(x, int) else pl.multiple_of(x, 8)
        bases = [(c + d * j) * cl for j in range(CLASSES // d)]
        qstarts = [aligned(b + n * qc) for b in bases]
        if keys == "own":
            kstarts, kc, bias = qstarts, qc, bias_h[pi]
        elif keys == "prev+own":
            kstarts, kc, bias = [aligned(b + (n - 1) * qc) for b in bases], 2 * qc, bias_f[pi]
        else:
            kfirst = jnp.maximum(n - 1, 0) * qc
            kstarts, kc = [aligned(b + kfirst) for b in bases], 2 * qc
            bias = bias_f[jnp.where(n == 0, pi + n_pat, pi)]
        is_a = lax.broadcasted_iota(jnp.int32, (w, LANES), 1) < HEAD_DIM
        q = gather(qs, qstarts, qc)
        k = gather(ks, kstarts, kc).astype(BF16)
        v = gather(vs, kstarts, kc).astype(BF16)
        zero = jnp.zeros_like(q)
        q2 = jnp.concatenate([jnp.where(is_a, q, zero), jnp.where(is_a, zero, q)], axis=0).astype(BF16)
        s = lax.dot_general(q2, k, (((1,), (1,)), ((), ())), preferred_element_type=F32) + bias
        m = jnp.max(s, axis=-1, keepdims=True)
        p = jnp.exp2(s - m)
        l = jnp.sum(p, axis=-1, keepdims=True)
        o = jnp.dot(p.astype(BF16), v, preferred_element_type=F32)
        m_c = jnp.where(is_a, m[:w], m[w:])
        l_c = jnp.where(is_a, l[:w], l[w:])
        o_c = jnp.where(is_a, o[:w], o[w:])
        if first:
            scatter(m_s, qstarts, qc, m_c)
            scatter(l_s, qstarts, qc, l_c)
            scatter(acc_s, qstarts, qc, o_c)
            return
        m_o = gather(m_s, qstarts, qc)
        m_n = jnp.maximum(m_o, m_c)
        e_o = jnp.exp2(m_o - m_n)
        e_c = jnp.exp2(m_c - m_n)
        l_n = gather(l_s, qstarts, qc) * e_o + l_c * e_c
        a_n = gather(acc_s, qstarts, qc) * e_o + o_c * e_c
        if last:
            scatter(os_, qstarts, qc, a_n / l_n)
        else:
            scatter(m_s, qstarts, qc, m_n)
            scatter(l_s, qstarts, qc, l_n)
            scatter(acc_s, qstarts, qc, a_n)

    for pi, (window, d) in enumerate(DILATED_PATTERNS):
        assert window // d == w and CLASSES % d == 0
        first, last = pi == 0, pi == n_pat - 1
        nb = seq // (w * d)

        if nb >= 4:
            def any_block(i, carry, d=d, pi=pi, first=first, last=last):
                block(i % d, i // d, d, pi, first, last, "any")
                return carry
            lax.fori_loop(0, d * nb, any_block, 0, unroll=ATTN_UNROLL)
        else:
            def class_blocks(c, carry, d=d, pi=pi, first=first, last=last, nb=nb):
                block(c, 0, d, pi, first, last, "own")
                for n in range(1, nb):
                    block(c, n, d, pi, first, last, "prev+own")
                return carry
            lax.fori_loop(0, d, class_blocks, 0, unroll=max(ATTN_UNROLL // nb, 1))

    for r in range(CLASSES):
        o_ref[pl.ds(r, cl, stride=CLASSES), :] = os_[pl.ds(r * cl, cl), :]


def _neg_distance_tables():
    w = SUB_WINDOW
    prev, nxt, own = [], [], []
    for _, d in DILATED_PATTERNS:
        n_cls = CLASSES // d
        qc = w // n_cls
        def pos(chunk_rows):
            j, l = np.divmod(np.arange(n_cls * chunk_rows), chunk_rows)
            return n_cls * l + j
        sq, sk2, sk1 = pos(qc)[:, None], pos(2 * qc)[None, :], pos(qc)[None, :]
        for out, dist in ((prev, sq + w - sk2), (nxt, sq - sk2), (own, sq - sk1)):
            valid = (dist >= 0) & (dist <= w)
            out.append(np.where(valid, -(dist * d).astype(np.float32), np.float32(MASKED)))
    return jnp.asarray(np.stack(prev + nxt), dtype=F32), jnp.asarray(np.stack(own), dtype=F32)


def _attention(qkv, batch, seq):
    qkv4 = qkv.reshape(3 * N_PLANES, batch, seq, LANES)
    slopes = jnp.exp2(-8.0 * jnp.arange(1, N_HEADS + 1, dtype=F32) / N_HEADS)
    nd_full, nd_head = _neg_distance_tables()
    n_pat = len(DILATED_PATTERNS)
    plane = lambda off: pl.BlockSpec((None, None, seq, LANES), lambda b, h: (off + h, b, 0, 0))
    return pl.pallas_call(
        functools.partial(_attn_kernel, seq=seq),
        grid=(batch, N_PLANES),
        in_specs=[
            pl.BlockSpec(memory_space=pltpu.SMEM),
            plane(0), plane(N_PLANES), plane(2 * N_PLANES),
            pl.BlockSpec((2 * n_pat, SUB_WINDOW, 2 * SUB_WINDOW), lambda b, h: (0, 0, 0)),
            pl.BlockSpec((n_pat, SUB_WINDOW, SUB_WINDOW), lambda b, h: (0, 0, 0)),
        ],
        out_specs=pl.BlockSpec((None, None, seq, LANES), lambda b, h: (h, b, 0, 0)),
        scratch_shapes=[pltpu.VMEM((seq, LANES), F32)] * 4 + [
            pltpu.VMEM((2 * n_pat, 2 * SUB_WINDOW, 2 * SUB_WINDOW), F32),
            pltpu.VMEM((n_pat, 2 * SUB_WINDOW, SUB_WINDOW), F32),
        ],
        out_shape=jax.ShapeDtypeStruct((N_PLANES, batch, seq, LANES), F32),
        compiler_params=_params("parallel", "parallel"),
        name="dilated_attention",
    )(slopes, qkv4, qkv4, qkv4, nd_full, nd_head)


def _gelu_tanh(x):
    c = np.float32(np.sqrt(2.0 / np.pi))
    return 0.5 * x * (1.0 + jnp.tanh(c * (x + 0.044715 * (x * x * x))))


def _log1p(x):
    u = 1.0 + x
    return jnp.where(u == 1.0, x, jnp.log(u) * x / (u - 1.0))


def _softplus(z):
    return jnp.maximum(z, 0.0) + _log1p(jnp.exp(-jnp.abs(z)))


PHASES = 8


def _lru_pieces(rg_ref, cw_ref, cb_ref, wbd_ref, ba_ref, bx_ref, lam_ref, g_ref, y_ref, xc, hc, ts):
    G = ts // PHASES
    first_group = lax.broadcasted_iota(jnp.int32, (G, LANES), 0) == 0
    sub = lax.broadcasted_iota(jnp.int32, (G, LANES), 0)

    def prev_group(cur, carry_row):
        return jnp.where(first_group, carry_row, pltpu.roll(cur, 1, 0))

    conv = []
    for l in range(N_PLANES):
        x = [rg_ref[l, pl.ds(s, G, stride=PHASES), :] for s in range(PHASES)]
        back = {s: prev_group(x[s], xc[l, s:s + 1, :]) for s in range(PHASES - CONV_WIDTH + 1, PHASES)}
        phases = []
        for s in range(PHASES):
            acc = jnp.broadcast_to(cb_ref[l:l + 1, :], (G, LANES))
            for j in range(CONV_WIDTH):
                q = s - (CONV_WIDTH - 1) + j
                acc = acc + (x[q] if q >= 0 else back[q + PHASES]) * cw_ref[j, l:l + 1, :]
            phases.append(acc)
        conv.append(jnp.concatenate(phases, axis=0))
        xc[l] = rg_ref[l, ts - PHASES:ts, :]
        yield

    gates_r, gates_i = [], []
    for j in range(N_PLANES // 2):
        yb = jnp.concatenate([conv[2 * j], conv[2 * j + 1]], axis=-1).astype(BF16)
        g = jnp.dot(yb, wbd_ref[j], preferred_element_type=F32)
        gates_r += [g[:, 0:LANES], g[:, LANES:2 * LANES]]
        gates_i += [g[:, 2 * LANES:3 * LANES], g[:, 3 * LANES:4 * LANES]]
        yield

    rec, sq = [], None
    for l in range(N_PLANES):
        y = conv[l]
        r = jax.nn.sigmoid(gates_r[l] + ba_ref[l:l + 1, :])
        ig = jax.nn.sigmoid(gates_i[l] + bx_ref[l:l + 1, :])
        log_a = (-RG_C * r) * _softplus(-lam_ref[l:l + 1, :])
        a = jnp.exp(log_a)
        z = 1.0 - a * a
        u = jnp.where(z > 0.0, z * lax.rsqrt(z), 0.0) * (ig * y)

        piece = lambda v, s: v[s * G:(s + 1) * G, :]
        pa, pb = [piece(a, 0)], [piece(u, 0)]
        for s in range(1, PHASES):
            pa.append(piece(a, s) * pa[-1])
            pb.append(piece(a, s) * pb[-1] + piece(u, s))
        ga, gb = pa[-1], pb[-1]
        sh = 1
        while sh < G:
            take = sub >= sh
            gb = jnp.where(take, ga * pltpu.roll(gb, sh, 0) + gb, gb)
            ga = jnp.where(take, ga * pltpu.roll(ga, sh, 0), ga)
            sh *= 2
        h0 = hc[l, 0:1, :]
        h_end = ga * h0 + gb
        h_in = prev_group(h_end, h0)
        hc[l] = jnp.broadcast_to(h_end[G - 1:G, :], (PHASES, LANES))
        h = jnp.concatenate([pa[s] * h_in + pb[s] for s in range(PHASES)], axis=0)

        xg = jnp.concatenate(
            [rg_ref[N_PLANES + l, pl.ds(s, G, stride=PHASES), :] for s in range(PHASES)], axis=0)
        rl = h * _gelu_tanh(xg)
        rec.append(rl)
        part = jnp.sum(rl * rl, axis=-1, keepdims=True)
        sq = part if sq is None else sq + part
        yield

    scale = lax.rsqrt(sq * (1.0 / D_LRU) + RMS_EPS)
    for l in range(N_PLANES):
        out = rec[l] * scale * g_ref[l:l + 1, :]
        for s in range(PHASES):
            y_ref[l, pl.ds(s, G, stride=PHASES), :] = out[s * G:(s + 1) * G, :]
        yield


def _block_diag_gates(wa, wx):
    def bd(wm):
        wm = wm.reshape(4, 4, LRU_BLOCK_DIM, LRU_BLOCK_DIM)
        eye = jnp.eye(4, dtype=wm.dtype)
        full = jnp.einsum("gide,ij->gidje", wm, eye)
        return full.reshape(4, 4 * LRU_BLOCK_DIM, 4 * LRU_BLOCK_DIM)
    return jnp.concatenate([bd(wa), bd(wx)], axis=-1).astype(BF16)


def _layer_norm_rows(z, g, b):
    mu = jnp.mean(z, axis=-1, keepdims=True)
    zc = z - mu
    var = jnp.mean(zc * zc, axis=-1, keepdims=True)
    return zc * lax.rsqrt(var + LN_EPS) * g + b


U32 = jnp.uint32
HI_HALF = np.uint32(0xFFFF0000)


def _to_packed_token_major(ref, val, first_row, n_rows):
    bits = lambda a: lax.bitcast_convert_type(a.astype(BF16).astype(F32), U32)
    half = PACKED_CHUNKS * LANES
    for c in range(PACKED_CHUNKS):
        lo = bits(val[:, c * LANES:(c + 1) * LANES]) >> 16
        hi = bits(val[:, half + c * LANES:half + (c + 1) * LANES]) & HI_HALF
        ref[pl.ds(first_row * PACKED_CHUNKS + c, n_rows, stride=PACKED_CHUNKS), :] = lo | hi


def _from_packed_token_major(ref, first_row, n_rows):
    words = [ref[pl.ds(first_row * PACKED_CHUNKS + c, n_rows, stride=PACKED_CHUNKS), :]
             for c in range(PACKED_CHUNKS)]
    lo = [lax.bitcast_convert_type(wd << 16, F32) for wd in words]
    hi = [lax.bitcast_convert_type(wd & HI_HALF, F32) for wd in words]
    return jnp.concatenate(lo + hi, axis=-1)


def _out_router_kernel(*refs, tm):
    *io_refs, cnt_s, z_a, z_b = refs
    i = pl.program_id(0)

    @pl.when(i == 0)
    def _():
        cnt_s[...] = jnp.zeros_like(cnt_s)
        z_b[...] = jnp.zeros_like(z_b)

    @pl.when(i % 2 == 0)
    def _():
        _out_router_step(*io_refs, cnt_s, z_b, z_a, tm=tm)

    @pl.when(i % 2 == 1)
    def _():
        _out_router_step(*io_refs, cnt_s, z_a, z_b, tm=tm)


def _out_router_step(attn_ref, yrec_ref, x_ref, ga_ref, wo_ref, g1_ref, b1_ref, wr_ref, br_ref,
                     x1_ref, x1p_ref, pk_ref, rw_ref, cnt_ref, cnt_s, z_in, z_out, *, tm):
    i = pl.program_id(0)
    attn = jnp.concatenate([attn_ref[p] for p in range(N_PLANES)], axis=-1)
    ms = jnp.mean(attn * attn, axis=-1, keepdims=True)
    ya = (attn * lax.rsqrt(ms + RMS_EPS) * ga_ref[...]).astype(BF16)
    yr = jnp.concatenate([yrec_ref[p] for p in range(N_PLANES)], axis=-1).astype(BF16)
    n_col = 8
    cw = D_MODEL // n_col

    def project(j):
        cols = slice(j * cw, (j + 1) * cw)
        mix = jnp.dot(ya, wo_ref[0:D_ATTN, cols], preferred_element_type=F32)
        mix = mix + jnp.dot(yr, wo_ref[D_ATTN:, cols], preferred_element_type=F32)
        z_out[:, cols] = ALPHA * x_ref[:, cols] + mix

    def normalize(r0, n_rows):
        rows = slice(r0, r0 + n_rows)
        x1 = _layer_norm_rows(z_in[rows, :], g1_ref[...], b1_ref[...])
        x1_ref[rows, :] = x1
        _to_packed_token_major(x1p_ref, x1, r0, n_rows)
        return x1.astype(BF16)

    n_row = 4
    rh = tm // n_row
    parts = []
    for k in range(n_row):
        parts.append(normalize(k * rh, rh))
        project(k)

    xb = jnp.concatenate(parts, axis=0)
    logits = jnp.dot(xb, wr_ref[...].astype(BF16), preferred_element_type=F32) + br_ref[...]
    project(4)
    project(5)
    _route(logits, pk_ref, rw_ref, cnt_ref, cnt_s, i, tm)
    project(6)
    project(7)


def _route(logits, pk_ref, rw_ref, cnt_ref, cnt_s, i, tm):
    lane = lax.broadcasted_iota(jnp.int32, (tm, LANES), 1)
    big = jnp.int32(LANES)
    first_true = lambda c: jnp.min(jnp.where(c, lane, big), axis=-1, keepdims=True)

    in_g = lane < N_GROUPS
    gl = jnp.where(in_g, logits, MASKED)
    gmax = jnp.max(gl, axis=-1, keepdims=True)
    g_idx = first_true(gl == gmax)
    gsum = jnp.sum(jnp.where(in_g, jnp.exp(gl - gmax), 0.0), axis=-1, keepdims=True)
    g_gate = 1.0 / gsum

    lo = N_GROUPS + EXPERTS_PER_GROUP * g_idx
    in_e = (lane >= lo) & (lane < lo + EXPERTS_PER_GROUP)
    el = jnp.where(in_e, logits, MASKED)
    emax = jnp.max(el, axis=-1, keepdims=True)
    ee = jnp.where(in_e, jnp.exp(el - emax), 0.0)
    esum = jnp.sum(ee, axis=-1, keepdims=True)
    i1 = first_true(el == emax)
    rest = jnp.where(in_e & (lane != i1), ee, -1.0)
    e2max = jnp.max(rest, axis=-1, keepdims=True)
    i2 = first_true(rest == e2max)
    v1 = 1.0 / esum
    v2 = e2max / esum
    den = v1 + v2
    w1 = g_gate * v1 / den
    w2 = g_gate * v2 / den
    e1 = i1 - N_GROUPS
    e2 = i2 - N_GROUPS
    rw_ref[...] = jnp.where(lane == 0, w1, jnp.where(lane == 1, w2, 0.0))

    hot1 = lane == e1
    hot2 = lane == e2
    both = jnp.where((hot1 | hot2) & (i > 0), 1.0, 0.0)
    earlier = (lax.broadcasted_iota(jnp.int32, (tm, tm), 1) < lax.broadcasted_iota(jnp.int32, (tm, tm), 0))
    before = jnp.dot(earlier.astype(BF16), both.astype(BF16), preferred_element_type=F32) + cnt_s[0:1, :]
    r1 = jnp.sum(jnp.where(hot1, before, 0.0), axis=-1, keepdims=True).astype(jnp.int32)
    r2 = jnp.sum(jnp.where(hot2, before, 0.0), axis=-1, keepdims=True).astype(jnp.int32)
    pk_ref[...] = jnp.where(lane == 0, e1 * RANK_SPAN + r1, jnp.where(lane == 1, e2 * RANK_SPAN + r2, 0))
    cnt_s[...] = cnt_s[...] + jnp.sum(both, axis=0, keepdims=True)
    cnt_ref[...] = cnt_s[...].astype(jnp.int32)


def _out_router(attn, yrec, x2d, attn_norm_g, w_out_b, ln1_g, ln1_b, w_router, b_router):
    T = x2d.shape[0]
    tm = TM_OUT
    assert 2 * T <= RANK_SPAN
    attn3 = attn.reshape(N_PLANES, T, LANES)
    yrec3 = yrec.reshape(N_PLANES, T, LANES)
    const = lambda shape: pl.BlockSpec(shape, lambda i: (0,) * len(shape))
    n = T // tm
    stage1 = lambda i: jnp.minimum(i, n - 1)
    stage2 = lambda i: jnp.maximum(i - 1, 0)
    return pl.pallas_call(
        functools.partial(_out_router_kernel, tm=tm),
        grid=(n + 1,),
        in_specs=[
            pl.BlockSpec((N_PLANES, tm, LANES), lambda i: (0, stage1(i), 0)),
            pl.BlockSpec((N_PLANES, tm, LANES), lambda i: (0, stage1(i), 0)),
            pl.BlockSpec((tm, D_MODEL), lambda i: (stage1(i), 0)),
            const((1, D_ATTN)),
            pl.BlockSpec((D_MODEL, D_MODEL), lambda i: (0, 0), pipeline_mode=pl.Buffered(1)),
            const((1, D_MODEL)), const((1, D_MODEL)),
            const((D_MODEL, LANES)), const((1, LANES)),
        ],
        out_specs=[
            pl.BlockSpec((tm, D_MODEL), lambda i: (stage2(i), 0)),
            pl.BlockSpec((tm * PACKED_CHUNKS, LANES), lambda i: (stage2(i), 0)),
            pl.BlockSpec((tm, LANES), lambda i: (stage2(i), 0)),
            pl.BlockSpec((tm, LANES), lambda i: (stage2(i), 0)),
            const((8, LANES)),
        ],
        out_shape=[
            jax.ShapeDtypeStruct((T, D_MODEL), F32),
            jax.ShapeDtypeStruct((T * PACKED_CHUNKS, LANES), U32),
            jax.ShapeDtypeStruct((T, LANES), jnp.int32),
            jax.ShapeDtypeStruct((T, LANES), F32),
            jax.ShapeDtypeStruct((8, LANES), jnp.int32),
        ],
        scratch_shapes=[pltpu.VMEM((8, LANES), F32), pltpu.VMEM((tm, D_MODEL), F32),
                        pltpu.VMEM((tm, D_MODEL), F32)],
        compiler_params=_params("arbitrary"),
        name="out_proj_router",
    )(attn3, yrec3, x2d, attn_norm_g.reshape(1, D_ATTN).astype(F32), w_out_b,
      ln1_g.reshape(1, D_MODEL).astype(F32), ln1_b.reshape(1, D_MODEL).astype(F32), w_router, b_router)


def _dispatch_kernel(slot_ref, pstart_ref, pend_ref, nused_ref, x1p_ref, xs_hbm, zbuf, sem, zsem,
                     *, tm, rows, n_blk):
    i = pl.program_id(0)
    slab = PACKED_CHUNKS
    blk = rows * slab

    @pl.when(i == 0)
    def _():
        zbuf[...] = jnp.zeros_like(zbuf)
        n_used = nused_ref[0]

        def zero_copy(b):
            return pltpu.make_async_copy(zbuf, xs_hbm.at[pl.ds(pl.multiple_of(b * blk, blk), blk), :], zsem)

        def for_each_zeroed_block(fn):
            for e in range(N_EXPERTS):
                @pl.when(pend_ref[e] > pstart_ref[e])
                def _(e=e):
                    fn(pend_ref[e] // rows - 1)

            def tail(b, c):
                fn(b)
                return c
            lax.fori_loop(n_used, n_blk, tail, 0)

        for_each_zeroed_block(lambda b: zero_copy(b).start())
        for_each_zeroed_block(lambda b: zero_copy(b).wait())

    group = 8

    def push(g, c):
        for t in range(group):
            src = x1p_ref.at[pl.ds(pl.multiple_of(g * (group * slab), group * slab) + t * slab, slab), :]
            for k in range(2):
                row = slot_ref[i * (2 * tm) + g * (2 * group) + 2 * t + k] * slab
                dst = xs_hbm.at[pl.ds(pl.multiple_of(row, slab), slab), :]
                pltpu.make_async_copy(src, dst, sem).start(priority=k)
        return c
    lax.fori_loop(0, tm // group, push, 0)
    for _ in range(2):
        pltpu.make_async_copy(x1p_ref, xs_hbm.at[pl.ds(0, tm * slab), :], sem).wait()


def _dispatch(slot_flat, pad_start, pad_end, n_used, x1p, n_blk):
    tm, rows, slab = TM_DISP, MOE_ROWS, PACKED_CHUNKS
    n_tok = x1p.shape[0] // slab
    grid_spec = pltpu.PrefetchScalarGridSpec(
        num_scalar_prefetch=4,
        grid=(n_tok // tm,),
        in_specs=[pl.BlockSpec((tm * slab, LANES), lambda i, *_: (i, 0))],
        out_specs=pl.BlockSpec(memory_space=pl.ANY),
        scratch_shapes=[
            pltpu.VMEM((rows * slab, LANES), U32),
            pltpu.SemaphoreType.DMA(()),
            pltpu.SemaphoreType.DMA(()),
        ],
    )
    return pl.pallas_call(
        functools.partial(_dispatch_kernel, tm=tm, rows=rows, n_blk=n_blk),
        grid_spec=grid_spec,
        out_shape=jax.ShapeDtypeStruct((n_blk * rows * slab, LANES), U32),
        compiler_params=_params("arbitrary"),
        name="moe_dispatch",
    )(slot_flat, pad_start, pad_end, n_used, x1p)


def _expert_kernel(blk_e_ref, n_used_ref, run_ref, next_e_ref, xs_ref, w1_hbm, w3_hbm, w2_hbm, y_ref,
                   wf1, wf3, wf2, wsem, w1b, w3b, w2b, *, rows):
    i = pl.program_id(0)
    n_used = n_used_ref[0]
    e = blk_e_ref[i]
    slot = run_ref[i] % 2
    new_run = jnp.logical_or(i == 0, e != blk_e_ref[jnp.maximum(i - 1, 0)])

    def weight_copies(expert, slot_):
        return [pltpu.make_async_copy(src.at[expert], dst.at[slot_], wsem.at[slot_])
                for src, dst in ((w1_hbm, wf1), (w3_hbm, wf3), (w2_hbm, wf2))]

    @pl.when(i == 0)
    def _():
        for cp in weight_copies(e, 0):
            cp.start()

    @pl.when(jnp.logical_and(i < n_used, new_run))
    def _():
        for cp in weight_copies(e, slot):
            cp.wait()
        nxt = next_e_ref[i]

        @pl.when(nxt >= 0)
        def _():
            for cp in weight_copies(nxt, 1 - slot):
                cp.start(priority=1)

        w1b[...] = wf1[slot].astype(BF16)
        w3b[...] = wf3[slot].astype(BF16)
        w2b[...] = wf2[slot].astype(BF16)

    @pl.when(i < n_used)
    def _():
        xb = _from_packed_token_major(xs_ref, 0, rows).astype(BF16)
        h1 = jnp.dot(xb, w1b[...], preferred_element_type=F32)
        h3 = jnp.dot(xb, w3b[...], preferred_element_type=F32)
        h = (jax.nn.silu(h1) * h3).astype(BF16)
        _to_packed_token_major(y_ref, jnp.dot(h, w2b[...], preferred_element_type=F32), 0, rows)

    @pl.when(i >= n_used)
    def _():
        y_ref[...] = jnp.zeros_like(y_ref)


def _experts(blk_e, n_used, run_idx, next_e, xs, w1, w3, w2):
    rows = MOE_ROWS
    n_blk = blk_e.shape[0]
    hbm = pl.BlockSpec(memory_space=pl.ANY)
    grid_spec = pltpu.PrefetchScalarGridSpec(
        num_scalar_prefetch=4,
        grid=(n_blk,),
        in_specs=[
            pl.BlockSpec((rows * PACKED_CHUNKS, LANES), lambda i, be, nu, *_: (jnp.minimum(i, nu[0] - 1), 0)),
            hbm, hbm, hbm,
        ],
        out_specs=pl.BlockSpec((rows * PACKED_CHUNKS, LANES), lambda i, *_: (i, 0)),
        scratch_shapes=[
            pltpu.VMEM((2, D_MODEL, D_EXPERT), F32),
            pltpu.VMEM((2, D_MODEL, D_EXPERT), F32),
            pltpu.VMEM((2, D_EXPERT, D_MODEL), F32),
            pltpu.SemaphoreType.DMA((2,)),
            pltpu.VMEM((D_MODEL, D_EXPERT), BF16),
            pltpu.VMEM((D_MODEL, D_EXPERT), BF16),
            pltpu.VMEM((D_EXPERT, D_MODEL), BF16),
        ],
    )
    return pl.pallas_call(
        functools.partial(_expert_kernel, rows=rows),
        grid_spec=grid_spec,
        out_shape=jax.ShapeDtypeStruct((n_blk * rows * PACKED_CHUNKS, LANES), U32),
        compiler_params=_params("arbitrary"),
        name="moe_experts",
    )(blk_e, n_used, run_idx, next_e, xs, w1, w3, w2)


def _combine_kernel(slot_ref, ys_hbm, x1_ref, rw_ref, g2_ref, b2_ref, o_ref, yg, sem, *, tm):
    i = pl.program_id(0)
    n = pl.num_programs(0)
    slot = i % 2
    slab = PACKED_CHUNKS
    half = tm * slab
    group = 8

    def start_gather(tile, slot_):
        def body(g, c):
            base = pl.multiple_of(g * (group * slab), group * slab)
            for t in range(group):
                for k in range(2):
                    src_row = slot_ref[tile * (2 * tm) + g * (2 * group) + 2 * t + k] * slab
                    pltpu.make_async_copy(
                        ys_hbm.at[pl.ds(pl.multiple_of(src_row, slab), slab), :],
                        yg.at[slot_, pl.ds(k * half + base + t * slab, slab), :],
                        sem.at[slot_]).start(priority=k)
            return c
        lax.fori_loop(0, tm // group, body, 0)

    @pl.when(i == 0)
    def _():
        start_gather(0, 0)

    @pl.when(i + 1 < n)
    def _():
        start_gather(i + 1, 1 - slot)

    pltpu.make_async_copy(ys_hbm.at[pl.ds(0, 2 * half), :], yg.at[slot], sem.at[slot]).wait()
    rw = rw_ref[...]
    buf = yg.at[slot]
    moe = (_from_packed_token_major(buf, 0, tm) * rw[:, 0:1]
           + _from_packed_token_major(buf, tm, tm) * rw[:, 1:2])
    o_ref[...] = _layer_norm_rows(ALPHA * x1_ref[...] + moe, g2_ref[...], b2_ref[...])


def _combine(slot_flat, ys, x1, rw, ln2_g, ln2_b):
    T = x1.shape[0]
    tm = TM_COMB
    grid_spec = pltpu.PrefetchScalarGridSpec(
        num_scalar_prefetch=1,
        grid=(T // tm,),
        in_specs=[
            pl.BlockSpec(memory_space=pl.ANY),
            pl.BlockSpec((tm, D_MODEL), lambda i, *_: (i, 0)),
            pl.BlockSpec((tm, LANES), lambda i, *_: (i, 0)),
            pl.BlockSpec((1, D_MODEL), lambda i, *_: (0, 0)),
            pl.BlockSpec((1, D_MODEL), lambda i, *_: (0, 0)),
        ],
        out_specs=pl.BlockSpec((tm, D_MODEL), lambda i, *_: (i, 0)),
        scratch_shapes=[
            pltpu.VMEM((2, 2 * tm * PACKED_CHUNKS, LANES), U32),
            pltpu.SemaphoreType.DMA((2,)),
        ],
    )
    return pl.pallas_call(
        functools.partial(_combine_kernel, tm=tm),
        grid_spec=grid_spec,
        out_shape=jax.ShapeDtypeStruct((T, D_MODEL), F32),
        compiler_params=_params("arbitrary"),
        name="moe_combine_ln",
    )(slot_flat, ys, x1, rw, ln2_g.reshape(1, D_MODEL).astype(F32), ln2_b.reshape(1, D_MODEL).astype(F32))


def _dispatch_plan(counts, pk_flat, n_tokens):
    rows = MOE_ROWS
    experts = jnp.arange(N_EXPERTS, dtype=jnp.int32)
    padded = (counts + rows - 1) // rows * rows
    pad_end = jnp.cumsum(padded).astype(jnp.int32)
    pad_start = (pad_end - padded).astype(jnp.int32)
    n_blk = 2 * n_tokens // rows + N_EXPERTS
    blk_start = jnp.arange(n_blk, dtype=jnp.int32) * rows
    blk_e = jnp.minimum(jnp.sum(blk_start[:, None] >= pad_end[None, :], axis=1), N_EXPERTS - 1).astype(jnp.int32)
    n_used = (pad_end[-1:] // rows).astype(jnp.int32)
    new_run = jnp.concatenate([jnp.ones((1,), jnp.int32), (blk_e[1:] != blk_e[:-1]).astype(jnp.int32)])
    run_idx = (jnp.cumsum(new_run) - 1).astype(jnp.int32)
    later_active = (experts[None, :] > experts[:, None]) & (counts[None, :] > 0)
    next_active = jnp.min(jnp.where(later_active, experts[None, :], N_EXPERTS), axis=1)
    next_active = jnp.where(next_active == N_EXPERTS, -1, next_active).astype(jnp.int32)
    next_e = next_active[blk_e]
    e_flat = pk_flat >> RANK_BITS
    start_of = jnp.sum(jnp.where(e_flat[:, None] == experts[None, :], pad_start[None, :], 0), axis=1)
    slot_flat = (start_of + (pk_flat & (RANK_SPAN - 1))).astype(jnp.int32)
    return slot_flat, pad_start, pad_end, blk_e, n_used, run_idx, next_e, n_blk


def kernel(x, w_in, conv_w, conv_b, lru_wa, lru_ba, lru_wx, lru_bx, lru_lambda, attn_norm_g, lru_norm_g,
           w_out, ln1_g, ln1_b, router_grp_w, router_grp_b, router_exp_w, router_exp_b, w1, w3, w2,
           ln2_g, ln2_b):
    B, S, D = x.shape
    assert D == D_MODEL and S % DILATED_PATTERNS[-1][0] == 0 and w_in.shape[0] == 1
    T = B * S
    x2d = x.reshape(T, D)

    qkv, yrec = _in_proj(x2d, w_in[0].astype(BF16), conv_w[0], conv_b[0], lru_wa[0], lru_ba[0], lru_wx[0],
                         lru_bx[0], lru_lambda[0], lru_norm_g[0], B, S)
    attn = _attention(qkv, B, S)

    n_r = N_GROUPS + N_EXPERTS
    w_router = jnp.zeros((D, LANES), F32).at[:, :n_r].set(
        jnp.concatenate([router_grp_w[0], router_exp_w[0]], axis=-1).astype(F32))
    b_router = jnp.zeros((1, LANES), F32).at[0, :n_r].set(
        jnp.concatenate([router_grp_b[0], router_exp_b[0]], axis=-1).astype(F32))
    x1, x1p, pk, rw, cnt = _out_router(attn, yrec, x2d, attn_norm_g[0], w_out[0].astype(BF16),
                                       ln1_g[0], ln1_b[0], w_router, b_router)

    pk_flat = pk[:, :2].reshape(-1)
    slot_flat, pad_start, pad_end, blk_e, n_used, run_idx, next_e, n_blk = _dispatch_plan(
        cnt[0, :N_EXPERTS], pk_flat, T)
    xs = _dispatch(slot_flat, pad_start, pad_end, n_used, x1p, n_blk)
    ys = _experts(blk_e, n_used, run_idx, next_e, xs, w1[0], w3[0], w2[0])
    out = _combine(slot_flat, ys, x1, rw, ln2_g[0], ln2_b[0])
    return out.reshape(B, S, D)
```

```python
import functools

import jax
import jax.numpy as jnp
import numpy as np
from jax import lax
from jax.experimental import pallas as pl
from jax.experimental.pallas import tpu as pltpu

F32 = jnp.float32
BF16 = jnp.bfloat16

D_MODEL = 2048
D_ATTN = 1024
D_LRU = 1024
HEAD_DIM = 64
N_HEADS = 16
LANES = 128
N_PLANES = D_ATTN // LANES
DILATED_PATTERNS = ((128, 1), (512, 4), (2048, 16))
SUB_WINDOW = 128
LRU_BLOCKS = 16
LRU_BLOCK_DIM = 64
CONV_WIDTH = 4
RG_C = 8.0
N_GROUPS = 4
EXPERTS_PER_GROUP = 8
N_EXPERTS = 32
D_EXPERT = 512
ALPHA = 2.0 ** 0.25
LN_EPS = 1e-5
RMS_EPS = 1e-6
MASKED = -1e30
LOG2E = 1.4426950408889634

VMEM_LIMIT = 56 * 1024 * 1024

TM_PROJ = 256
TS_LRU = 256
TM_OUT = 256
MOE_ROWS = 256
TM_COMB = 512
TM_DISP = 512
CHUNKS = D_MODEL // LANES
PACKED_CHUNKS = CHUNKS // 2
RANK_BITS = 16
RANK_SPAN = 1 << RANK_BITS
ATTN_UNROLL = 32


def _params(*sem):
    return pltpu.CompilerParams(dimension_semantics=sem, vmem_limit_bytes=VMEM_LIMIT)


def _in_proj_kernel(x_ref, w_ref, qkv_ref, rg_ref, *, tm):
    xb = x_ref[...].astype(BF16)
    per_class = tm // CLASSES
    out_row = lax.broadcasted_iota(jnp.int32, (tm, tm), 0)
    src_row = (out_row % per_class) * CLASSES + out_row // per_class
    perm = (lax.broadcasted_iota(jnp.int32, (tm, tm), 1) == src_row).astype(BF16)
    xb_cm = jnp.dot(perm, xb, preferred_element_type=F32).astype(BF16)
    for c in range(5):
        lhs = xb_cm if c < 3 else xb
        acc = jnp.dot(lhs, w_ref[:, c * D_ATTN:(c + 1) * D_ATTN], preferred_element_type=F32)
        if c == 0:
            acc = acc * (HEAD_DIM ** -0.5 * LOG2E)
        for p in range(N_PLANES):
            plane = acc[:, p * LANES:(p + 1) * LANES]
            if c < 3:
                for r in range(CLASSES):
                    qkv_ref[c * N_PLANES + p, r] = plane[r * per_class:(r + 1) * per_class, :]
            else:
                rg_ref[(c - 3) * N_PLANES + p] = plane


def _in_proj(x2d, w_in_b, batch, seq):
    T = x2d.shape[0]
    tm = TM_PROJ
    tiles = seq // tm
    assert tm % CLASSES == 0 and seq % tm == 0
    return pl.pallas_call(
        functools.partial(_in_proj_kernel, tm=tm),
        grid=(T // tm,),
        in_specs=[
            pl.BlockSpec((tm, D_MODEL), lambda i: (i, 0)),
            pl.BlockSpec((D_MODEL, 5 * D_ATTN), lambda i: (0, 0), pipeline_mode=pl.Buffered(1)),
        ],
        out_specs=[
            pl.BlockSpec((3 * N_PLANES, None, CLASSES, tm // CLASSES, LANES),
                         lambda i: (0, i // tiles, 0, i % tiles, 0)),
            pl.BlockSpec((2 * N_PLANES, tm, LANES), lambda i: (0, i, 0)),
        ],
        out_shape=[
            jax.ShapeDtypeStruct((3 * N_PLANES, batch, CLASSES, seq // CLASSES, LANES), F32),
            jax.ShapeDtypeStruct((2 * N_PLANES, T, LANES), F32),
        ],
        compiler_params=_params("parallel"),
        name="in_proj",
    )(x2d, w_in_b)


CLASSES = 16


def _attn_kernel(slopes_ref, qs, ks, vs, ndf_ref, ndh_ref, o_ref,
                 os_, m_s, l_s, acc_s, bias_f, bias_h, *, seq):
    hp = pl.program_id(1)
    slope_a = slopes_ref[2 * hp] * LOG2E
    slope_b = slopes_ref[2 * hp + 1] * LOG2E
    w = SUB_WINDOW
    cl = seq // CLASSES
    n_pat = len(DILATED_PATTERNS)

    def gather(ref, starts, size):
        return jnp.concatenate([ref[pl.ds(s, size), :] for s in starts], axis=0)

    def scatter(ref, starts, size, val):
        for c, s in enumerate(starts):
            ref[pl.ds(s, size), :] = val[c * size:(c + 1) * size, :]

    for t in range(2 * n_pat):
        bias_f[t, 0:w, :] = slope_a * ndf_ref[t]
        bias_f[t, w:2 * w, :] = slope_b * ndf_ref[t]
    for t in range(n_pat):
        bias_h[t, 0:w, :] = slope_a * ndh_ref[t]
        bias_h[t, w:2 * w, :] = slope_b * ndh_ref[t]

    def block(c, n, d, pi, first, last, keys):
        qc = w * d // CLASSES
        aligned = lambda x: x if isinstance(x, int) else pl.multiple_of(x, 8)
        bases = [(c + d * j) * cl for j in range(CLASSES // d)]
        qstarts = [aligned(b + n * qc) for b in bases]
        if keys == "own":
            kstarts, kc, bias = qstarts, qc, bias_h[pi]
        elif keys == "prev+own":
            kstarts, kc, bias = [aligned(b + (n - 1) * qc) for b in bases], 2 * qc, bias_f[pi]
        else:
            kfirst = jnp.maximum(n - 1, 0) * qc
            kstarts, kc = [aligned(b + kfirst) for b in bases], 2 * qc
            bias = bias_f[jnp.where(n == 0, pi + n_pat, pi)]
        is_a = lax.broadcasted_iota(jnp.int32, (w, LANES), 1) < HEAD_DIM
        q = gather(qs, qstarts, qc)
        k = gather(ks, kstarts, kc).astype(BF16)
        v = gather(vs, kstarts, kc).astype(BF16)
        zero = jnp.zeros_like(q)
        q2 = jnp.concatenate([jnp.where(is_a, q, zero), jnp.where(is_a, zero, q)], axis=0).astype(BF16)
        s = lax.dot_general(q2, k, (((1,), (1,)), ((), ())), preferred_element_type=F32) + bias
        m = jnp.max(s, axis=-1, keepdims=True)
        p = jnp.exp2(s - m)
        l = jnp.sum(p, axis=-1, keepdims=True)
        o = jnp.dot(p.astype(BF16), v, preferred_element_type=F32)
        m_c = jnp.where(is_a, m[:w], m[w:])
        l_c = jnp.where(is_a, l[:w], l[w:])
        o_c = jnp.where(is_a, o[:w], o[w:])
        if first:
            scatter(m_s, qstarts, qc, m_c)
            scatter(l_s, qstarts, qc, l_c)
            scatter(acc_s, qstarts, qc, o_c)
            return
        m_o = gather(m_s, qstarts, qc)
        m_n = jnp.maximum(m_o, m_c)
        e_o = jnp.exp2(m_o - m_n)
        e_c = jnp.exp2(m_c - m_n)
        l_n = gather(l_s, qstarts, qc) * e_o + l_c * e_c
        a_n = gather(acc_s, qstarts, qc) * e_o + o_c * e_c
        if last:
            scatter(os_, qstarts, qc, a_n / l_n)
        else:
            scatter(m_s, qstarts, qc, m_n)
            scatter(l_s, qstarts, qc, l_n)
            scatter(acc_s, qstarts, qc, a_n)

    for pi, (window, d) in enumerate(DILATED_PATTERNS):
        assert window // d == w and CLASSES % d == 0
        first, last = pi == 0, pi == n_pat - 1
        nb = seq // (w * d)

        if nb >= 4:
            def any_block(i, carry, d=d, pi=pi, first=first, last=last):
                block(i % d, i // d, d, pi, first, last, "any")
                return carry
            lax.fori_loop(0, d * nb, any_block, 0, unroll=ATTN_UNROLL)
        else:
            def class_blocks(c, carry, d=d, pi=pi, first=first, last=last, nb=nb):
                block(c, 0, d, pi, first, last, "own")
                for n in range(1, nb):
                    block(c, n, d, pi, first, last, "prev+own")
                return carry
            lax.fori_loop(0, d, class_blocks, 0, unroll=max(ATTN_UNROLL // nb, 1))

    for r in range(CLASSES):
        o_ref[pl.ds(r, cl, stride=CLASSES), :] = os_[pl.ds(r * cl, cl), :]


def _neg_distance_tables():
    w = SUB_WINDOW
    prev, nxt, own = [], [], []
    for _, d in DILATED_PATTERNS:
        n_cls = CLASSES // d
        qc = w // n_cls
        def pos(chunk_rows):
            j, l = np.divmod(np.arange(n_cls * chunk_rows), chunk_rows)
            return n_cls * l + j
        sq, sk2, sk1 = pos(qc)[:, None], pos(2 * qc)[None, :], pos(qc)[None, :]
        for out, dist in ((prev, sq + w - sk2), (nxt, sq - sk2), (own, sq - sk1)):
            valid = (dist >= 0) & (dist <= w)
            out.append(np.where(valid, -(dist * d).astype(np.float32), np.float32(MASKED)))
    return jnp.asarray(np.stack(prev + nxt), dtype=F32), jnp.asarray(np.stack(own), dtype=F32)


def _attention(qkv, batch, seq):
    qkv4 = qkv.reshape(3 * N_PLANES, batch, seq, LANES)
    slopes = jnp.exp2(-8.0 * jnp.arange(1, N_HEADS + 1, dtype=F32) / N_HEADS)
    nd_full, nd_head = _neg_distance_tables()
    n_pat = len(DILATED_PATTERNS)
    plane = lambda off: pl.BlockSpec((None, None, seq, LANES), lambda b, h: (off + h, b, 0, 0))
    return pl.pallas_call(
        functools.partial(_attn_kernel, seq=seq),
        grid=(batch, N_PLANES),
        in_specs=[
            pl.BlockSpec(memory_space=pltpu.SMEM),
            plane(0), plane(N_PLANES), plane(2 * N_PLANES),
            pl.BlockSpec((2 * n_pat, SUB_WINDOW, 2 * SUB_WINDOW), lambda b, h: (0, 0, 0)),
            pl.BlockSpec((n_pat, SUB_WINDOW, SUB_WINDOW), lambda b, h: (0, 0, 0)),
        ],
        out_specs=pl.BlockSpec((None, None, seq, LANES), lambda b, h: (h, b, 0, 0)),
        scratch_shapes=[pltpu.VMEM((seq, LANES), F32)] * 4 + [
            pltpu.VMEM((2 * n_pat, 2 * SUB_WINDOW, 2 * SUB_WINDOW), F32),
            pltpu.VMEM((n_pat, 2 * SUB_WINDOW, SUB_WINDOW), F32),
        ],
        out_shape=jax.ShapeDtypeStruct((N_PLANES, batch, seq, LANES), F32),
        compiler_params=_params("parallel", "parallel"),
        name="dilated_attention",
    )(slopes, qkv4, qkv4, qkv4, nd_full, nd_head)


def _gelu_tanh(x):
    c = np.float32(np.sqrt(2.0 / np.pi))
    return 0.5 * x * (1.0 + jnp.tanh(c * (x + 0.044715 * (x * x * x))))


def _log1p(x):
    u = 1.0 + x
    return jnp.where(u == 1.0, x, jnp.log(u) * x / (u - 1.0))


def _softplus(z):
    return jnp.maximum(z, 0.0) + _log1p(jnp.exp(-jnp.abs(z)))


PHASES = 8


def _lru_kernel(xr_ref, xg_ref, cw_ref, cb_ref, wbd_ref, ba_ref, bx_ref, lam_ref, g_ref, y_ref,
                xc, hc, *, ts):
    i = pl.program_id(1)
    G = ts // PHASES
    first_group = lax.broadcasted_iota(jnp.int32, (G, LANES), 0) == 0
    sub = lax.broadcasted_iota(jnp.int32, (G, LANES), 0)

    @pl.when(i == 0)
    def _():
        xc[...] = jnp.zeros_like(xc)
        hc[...] = jnp.zeros_like(hc)

    def prev_group(cur, carry_row):
        return jnp.where(first_group, carry_row, pltpu.roll(cur, 1, 0))

    conv = []
    for l in range(N_PLANES):
        x = [xr_ref[l, pl.ds(s, G, stride=PHASES), :] for s in range(PHASES)]
        back = {s: prev_group(x[s], xc[l, s:s + 1, :]) for s in range(PHASES - CONV_WIDTH + 1, PHASES)}
        phases = []
        for s in range(PHASES):
            acc = jnp.broadcast_to(cb_ref[l:l + 1, :], (G, LANES))
            for j in range(CONV_WIDTH):
                q = s - (CONV_WIDTH - 1) + j
                acc = acc + (x[q] if q >= 0 else back[q + PHASES]) * cw_ref[j, l:l + 1, :]
            phases.append(acc)
        conv.append(jnp.concatenate(phases, axis=0))
        xc[l] = xr_ref[l, ts - PHASES:ts, :]

    gates_r, gates_i = [], []
    for j in range(N_PLANES // 2):
        yb = jnp.concatenate([conv[2 * j], conv[2 * j + 1]], axis=-1).astype(BF16)
        g = jnp.dot(yb, wbd_ref[j], preferred_element_type=F32)
        gates_r += [g[:, 0:LANES], g[:, LANES:2 * LANES]]
        gates_i += [g[:, 2 * LANES:3 * LANES], g[:, 3 * LANES:4 * LANES]]

    rec, sq = [], None
    for l in range(N_PLANES):
        y = conv[l]
        r = jax.nn.sigmoid(gates_r[l] + ba_ref[l:l + 1, :])
        ig = jax.nn.sigmoid(gates_i[l] + bx_ref[l:l + 1, :])
        log_a = (-RG_C * r) * _softplus(-lam_ref[l:l + 1, :])
        a = jnp.exp(log_a)
        z = 1.0 - a * a
        u = jnp.where(z > 0.0, z * lax.rsqrt(z), 0.0) * (ig * y)

        piece = lambda v, s: v[s * G:(s + 1) * G, :]
        pa, pb = [piece(a, 0)], [piece(u, 0)]
        for s in range(1, PHASES):
            pa.append(piece(a, s) * pa[-1])
            pb.append(piece(a, s) * pb[-1] + piece(u, s))
        ga, gb = pa[-1], pb[-1]
        sh = 1
        while sh < G:
            take = sub >= sh
            gb = jnp.where(take, ga * pltpu.roll(gb, sh, 0) + gb, gb)
            ga = jnp.where(take, ga * pltpu.roll(ga, sh, 0), ga)
            sh *= 2
        h0 = hc[l, 0:1, :]
        h_end = ga * h0 + gb
        h_in = prev_group(h_end, h0)
        hc[l] = jnp.broadcast_to(h_end[G - 1:G, :], (PHASES, LANES))
        h = jnp.concatenate([pa[s] * h_in + pb[s] for s in range(PHASES)], axis=0)

        xg = jnp.concatenate([xg_ref[l, pl.ds(s, G, stride=PHASES), :] for s in range(PHASES)], axis=0)
        rl = h * _gelu_tanh(xg)
        rec.append(rl)
        part = jnp.sum(rl * rl, axis=-1, keepdims=True)
        sq = part if sq is None else sq + part

    scale = lax.rsqrt(sq * (1.0 / D_LRU) + RMS_EPS)
    for l in range(N_PLANES):
        out = rec[l] * scale * g_ref[l:l + 1, :]
        for s in range(PHASES):
            y_ref[l, pl.ds(s, G, stride=PHASES), :] = out[s * G:(s + 1) * G, :]


def _block_diag_gates(wa, wx):
    def bd(wm):
        wm = wm.reshape(4, 4, LRU_BLOCK_DIM, LRU_BLOCK_DIM)
        eye = jnp.eye(4, dtype=wm.dtype)
        full = jnp.einsum("gide,ij->gidje", wm, eye)
        return full.reshape(4, 4 * LRU_BLOCK_DIM, 4 * LRU_BLOCK_DIM)
    return jnp.concatenate([bd(wa), bd(wx)], axis=-1).astype(BF16)


def _recurrent(rg, conv_w, conv_b, lru_wa, lru_ba, lru_wx, lru_bx, lru_lambda, lru_norm_g, batch, seq):
    ts = TS_LRU
    rg5 = rg.reshape(2, N_PLANES, batch, seq, LANES)
    wbd = _block_diag_gates(lru_wa, lru_wx)
    planes = lambda a: a.reshape(N_PLANES, LANES).astype(F32)
    vec = pl.BlockSpec((N_PLANES, LANES), lambda b, i: (0, 0))
    return pl.pallas_call(
        functools.partial(_lru_kernel, ts=ts),
        grid=(batch, seq // ts),
        in_specs=[
            pl.BlockSpec((None, N_PLANES, None, ts, LANES), lambda b, i: (0, 0, b, i, 0)),
            pl.BlockSpec((None, N_PLANES, None, ts, LANES), lambda b, i: (1, 0, b, i, 0)),
            pl.BlockSpec((CONV_WIDTH, N_PLANES, LANES), lambda b, i: (0, 0, 0)),
            vec,
            pl.BlockSpec((4, 4 * LRU_BLOCK_DIM, 8 * LRU_BLOCK_DIM), lambda b, i: (0, 0, 0)),
            vec, vec, vec, vec,
        ],
        out_specs=pl.BlockSpec((N_PLANES, None, ts, LANES), lambda b, i: (0, b, i, 0)),
        out_shape=jax.ShapeDtypeStruct((N_PLANES, batch, seq, LANES), F32),
        scratch_shapes=[
            pltpu.VMEM((N_PLANES, PHASES, LANES), F32),
            pltpu.VMEM((N_PLANES, PHASES, LANES), F32),
        ],
        compiler_params=_params("parallel", "arbitrary"),
        name="conv_rglru",
    )(rg5, rg5, conv_w.astype(F32).reshape(CONV_WIDTH, N_PLANES, LANES), planes(conv_b), wbd, planes(lru_ba),
      planes(lru_bx), planes(lru_lambda), planes(lru_norm_g))


def _layer_norm_rows(z, g, b):
    mu = jnp.mean(z, axis=-1, keepdims=True)
    zc = z - mu
    var = jnp.mean(zc * zc, axis=-1, keepdims=True)
    return zc * lax.rsqrt(var + LN_EPS) * g + b


U32 = jnp.uint32
HI_HALF = np.uint32(0xFFFF0000)


def _to_packed_token_major(ref, val, first_row, n_rows):
    bits = lambda a: lax.bitcast_convert_type(a.astype(BF16).astype(F32), U32)
    half = PACKED_CHUNKS * LANES
    for c in range(PACKED_CHUNKS):
        lo = bits(val[:, c * LANES:(c + 1) * LANES]) >> 16
        hi = bits(val[:, half + c * LANES:half + (c + 1) * LANES]) & HI_HALF
        ref[pl.ds(first_row * PACKED_CHUNKS + c, n_rows, stride=PACKED_CHUNKS), :] = lo | hi


def _from_packed_token_major(ref, first_row, n_rows):
    words = [ref[pl.ds(first_row * PACKED_CHUNKS + c, n_rows, stride=PACKED_CHUNKS), :]
             for c in range(PACKED_CHUNKS)]
    lo = [lax.bitcast_convert_type(wd << 16, F32) for wd in words]
    hi = [lax.bitcast_convert_type(wd & HI_HALF, F32) for wd in words]
    return jnp.concatenate(lo + hi, axis=-1)


def _out_router_kernel(*refs, tm):
    *io_refs, cnt_s, z_a, z_b = refs
    i = pl.program_id(0)

    @pl.when(i == 0)
    def _():
        cnt_s[...] = jnp.zeros_like(cnt_s)
        z_b[...] = jnp.zeros_like(z_b)

    @pl.when(i % 2 == 0)
    def _():
        _out_router_step(*io_refs, cnt_s, z_b, z_a, tm=tm)

    @pl.when(i % 2 == 1)
    def _():
        _out_router_step(*io_refs, cnt_s, z_a, z_b, tm=tm)


def _out_router_step(attn_ref, yrec_ref, x_ref, ga_ref, wo_ref, g1_ref, b1_ref, wr_ref, br_ref,
                     x1_ref, x1p_ref, pk_ref, rw_ref, cnt_ref, cnt_s, z_in, z_out, *, tm):
    i = pl.program_id(0)
    attn = jnp.concatenate([attn_ref[p] for p in range(N_PLANES)], axis=-1)
    ms = jnp.mean(attn * attn, axis=-1, keepdims=True)
    ya = (attn * lax.rsqrt(ms + RMS_EPS) * ga_ref[...]).astype(BF16)
    yr = jnp.concatenate([yrec_ref[p] for p in range(N_PLANES)], axis=-1).astype(BF16)
    n_col = 8
    cw = D_MODEL // n_col

    def project(j):
        cols = slice(j * cw, (j + 1) * cw)
        mix = jnp.dot(ya, wo_ref[0:D_ATTN, cols], preferred_element_type=F32)
        mix = mix + jnp.dot(yr, wo_ref[D_ATTN:, cols], preferred_element_type=F32)
        z_out[:, cols] = ALPHA * x_ref[:, cols] + mix

    def normalize(r0, n_rows):
        rows = slice(r0, r0 + n_rows)
        x1 = _layer_norm_rows(z_in[rows, :], g1_ref[...], b1_ref[...])
        x1_ref[rows, :] = x1
        _to_packed_token_major(x1p_ref, x1, r0, n_rows)
        return x1.astype(BF16)

    n_row = 4
    rh = tm // n_row
    parts = []
    for k in range(n_row):
        parts.append(normalize(k * rh, rh))
        project(k)

    xb = jnp.concatenate(parts, axis=0)
    logits = jnp.dot(xb, wr_ref[...].astype(BF16), preferred_element_type=F32) + br_ref[...]
    project(4)
    project(5)
    _route(logits, pk_ref, rw_ref, cnt_ref, cnt_s, i, tm)
    project(6)
    project(7)


def _route(logits, pk_ref, rw_ref, cnt_ref, cnt_s, i, tm):
    lane = lax.broadcasted_iota(jnp.int32, (tm, LANES), 1)
    big = jnp.int32(LANES)
    first_true = lambda c: jnp.min(jnp.where(c, lane, big), axis=-1, keepdims=True)

    in_g = lane < N_GROUPS
    gl = jnp.where(in_g, logits, MASKED)
    gmax = jnp.max(gl, axis=-1, keepdims=True)
    g_idx = first_true(gl == gmax)
    gsum = jnp.sum(jnp.where(in_g, jnp.exp(gl - gmax), 0.0), axis=-1, keepdims=True)
    g_gate = 1.0 / gsum

    lo = N_GROUPS + EXPERTS_PER_GROUP * g_idx
    in_e = (lane >= lo) & (lane < lo + EXPERTS_PER_GROUP)
    el = jnp.where(in_e, logits, MASKED)
    emax = jnp.max(el, axis=-1, keepdims=True)
    ee = jnp.where(in_e, jnp.exp(el - emax), 0.0)
    esum = jnp.sum(ee, axis=-1, keepdims=True)
    i1 = first_true(el == emax)
    rest = jnp.where(in_e & (lane != i1), ee, -1.0)
    e2max = jnp.max(rest, axis=-1, keepdims=True)
    i2 = first_true(rest == e2max)
    v1 = 1.0 / esum
    v2 = e2max / esum
    den = v1 + v2
    w1 = g_gate * v1 / den
    w2 = g_gate * v2 / den
    e1 = i1 - N_GROUPS
    e2 = i2 - N_GROUPS
    rw_ref[...] = jnp.where(lane == 0, w1, jnp.where(lane == 1, w2, 0.0))

    hot1 = lane == e1
    hot2 = lane == e2
    both = jnp.where((hot1 | hot2) & (i > 0), 1.0, 0.0)
    earlier = (lax.broadcasted_iota(jnp.int32, (tm, tm), 1) < lax.broadcasted_iota(jnp.int32, (tm, tm), 0))
    before = jnp.dot(earlier.astype(BF16), both.astype(BF16), preferred_element_type=F32) + cnt_s[0:1, :]
    r1 = jnp.sum(jnp.where(hot1, before, 0.0), axis=-1, keepdims=True).astype(jnp.int32)
    r2 = jnp.sum(jnp.where(hot2, before, 0.0), axis=-1, keepdims=True).astype(jnp.int32)
    pk_ref[...] = jnp.where(lane == 0, e1 * RANK_SPAN + r1, jnp.where(lane == 1, e2 * RANK_SPAN + r2, 0))
    cnt_s[...] = cnt_s[...] + jnp.sum(both, axis=0, keepdims=True)
    cnt_ref[...] = cnt_s[...].astype(jnp.int32)


def _out_router(attn, yrec, x2d, attn_norm_g, w_out_b, ln1_g, ln1_b, w_router, b_router):
    T = x2d.shape[0]
    tm = TM_OUT
    assert 2 * T <= RANK_SPAN
    attn3 = attn.reshape(N_PLANES, T, LANES)
    yrec3 = yrec.reshape(N_PLANES, T, LANES)
    const = lambda shape: pl.BlockSpec(shape, lambda i: (0,) * len(shape))
    n = T // tm
    stage1 = lambda i: jnp.minimum(i, n - 1)
    stage2 = lambda i: jnp.maximum(i - 1, 0)
    return pl.pallas_call(
        functools.partial(_out_router_kernel, tm=tm),
        grid=(n + 1,),
        in_specs=[
            pl.BlockSpec((N_PLANES, tm, LANES), lambda i: (0, stage1(i), 0)),
            pl.BlockSpec((N_PLANES, tm, LANES), lambda i: (0, stage1(i), 0)),
            pl.BlockSpec((tm, D_MODEL), lambda i: (stage1(i), 0)),
            const((1, D_ATTN)),
            pl.BlockSpec((D_MODEL, D_MODEL), lambda i: (0, 0), pipeline_mode=pl.Buffered(1)),
            const((1, D_MODEL)), const((1, D_MODEL)),
            const((D_MODEL, LANES)), const((1, LANES)),
        ],
        out_specs=[
            pl.BlockSpec((tm, D_MODEL), lambda i: (stage2(i), 0)),
            pl.BlockSpec((tm * PACKED_CHUNKS, LANES), lambda i: (stage2(i), 0)),
            pl.BlockSpec((tm, LANES), lambda i: (stage2(i), 0)),
            pl.BlockSpec((tm, LANES), lambda i: (stage2(i), 0)),
            const((8, LANES)),
        ],
        out_shape=[
            jax.ShapeDtypeStruct((T, D_MODEL), F32),
            jax.ShapeDtypeStruct((T * PACKED_CHUNKS, LANES), U32),
            jax.ShapeDtypeStruct((T, LANES), jnp.int32),
            jax.ShapeDtypeStruct((T, LANES), F32),
            jax.ShapeDtypeStruct((8, LANES), jnp.int32),
        ],
        scratch_shapes=[pltpu.VMEM((8, LANES), F32), pltpu.VMEM((tm, D_MODEL), F32),
                        pltpu.VMEM((tm, D_MODEL), F32)],
        compiler_params=_params("arbitrary"),
        name="out_proj_router",
    )(attn3, yrec3, x2d, attn_norm_g.reshape(1, D_ATTN).astype(F32), w_out_b,
      ln1_g.reshape(1, D_MODEL).astype(F32), ln1_b.reshape(1, D_MODEL).astype(F32), w_router, b_router)


def _dispatch_kernel(slot_ref, pstart_ref, pend_ref, nused_ref, x1p_ref, xs_hbm, zbuf, sem, zsem,
                     *, tm, rows, n_blk):
    i = pl.program_id(0)
    slab = PACKED_CHUNKS
    blk = rows * slab

    @pl.when(i == 0)
    def _():
        zbuf[...] = jnp.zeros_like(zbuf)
        n_used = nused_ref[0]

        def zero_copy(b):
            return pltpu.make_async_copy(zbuf, xs_hbm.at[pl.ds(pl.multiple_of(b * blk, blk), blk), :], zsem)

        def for_each_zeroed_block(fn):
            for e in range(N_EXPERTS):
                @pl.when(pend_ref[e] > pstart_ref[e])
                def _(e=e):
                    fn(pend_ref[e] // rows - 1)

            def tail(b, c):
                fn(b)
                return c
            lax.fori_loop(n_used, n_blk, tail, 0)

        for_each_zeroed_block(lambda b: zero_copy(b).start())
        for_each_zeroed_block(lambda b: zero_copy(b).wait())

    group = 8

    def push(g, c):
        for t in range(group):
            src = x1p_ref.at[pl.ds(pl.multiple_of(g * (group * slab), group * slab) + t * slab, slab), :]
            for k in range(2):
                row = slot_ref[i * (2 * tm) + g * (2 * group) + 2 * t + k] * slab
                dst = xs_hbm.at[pl.ds(pl.multiple_of(row, slab), slab), :]
                pltpu.make_async_copy(src, dst, sem).start(priority=k)
        return c
    lax.fori_loop(0, tm // group, push, 0)
    for _ in range(2):
        pltpu.make_async_copy(x1p_ref, xs_hbm.at[pl.ds(0, tm * slab), :], sem).wait()


def _dispatch(slot_flat, pad_start, pad_end, n_used, x1p, n_blk):
    tm, rows, slab = TM_DISP, MOE_ROWS, PACKED_CHUNKS
    n_tok = x1p.shape[0] // slab
    grid_spec = pltpu.PrefetchScalarGridSpec(
        num_scalar_prefetch=4,
        grid=(n_tok // tm,),
        in_specs=[pl.BlockSpec((tm * slab, LANES), lambda i, *_: (i, 0))],
        out_specs=pl.BlockSpec(memory_space=pl.ANY),
        scratch_shapes=[
            pltpu.VMEM((rows * slab, LANES), U32),
            pltpu.SemaphoreType.DMA(()),
            pltpu.SemaphoreType.DMA(()),
        ],
    )
    return pl.pallas_call(
        functools.partial(_dispatch_kernel, tm=tm, rows=rows, n_blk=n_blk),
        grid_spec=grid_spec,
        out_shape=jax.ShapeDtypeStruct((n_blk * rows * slab, LANES), U32),
        compiler_params=_params("arbitrary"),
        name="moe_dispatch",
    )(slot_flat, pad_start, pad_end, n_used, x1p)


def _expert_kernel(blk_e_ref, n_used_ref, run_ref, next_e_ref, xs_ref, w1_hbm, w3_hbm, w2_hbm, y_ref,
                   wf1, wf3, wf2, wsem, w1b, w3b, w2b, *, rows):
    i = pl.program_id(0)
    n_used = n_used_ref[0]
    e = blk_e_ref[i]
    slot = run_ref[i] % 2
    new_run = jnp.logical_or(i == 0, e != blk_e_ref[jnp.maximum(i - 1, 0)])

    def weight_copies(expert, slot_):
        return [pltpu.make_async_copy(src.at[expert], dst.at[slot_], wsem.at[slot_])
                for src, dst in ((w1_hbm, wf1), (w3_hbm, wf3), (w2_hbm, wf2))]

    @pl.when(i == 0)
    def _():
        for cp in weight_copies(e, 0):
            cp.start()

    @pl.when(jnp.logical_and(i < n_used, new_run))
    def _():
        for cp in weight_copies(e, slot):
            cp.wait()
        nxt = next_e_ref[i]

        @pl.when(nxt >= 0)
        def _():
            for cp in weight_copies(nxt, 1 - slot):
                cp.start(priority=1)

        w1b[...] = wf1[slot].astype(BF16)
        w3b[...] = wf3[slot].astype(BF16)
        w2b[...] = wf2[slot].astype(BF16)

    @pl.when(i < n_used)
    def _():
        xb = _from_packed_token_major(xs_ref, 0, rows).astype(BF16)
        h1 = jnp.dot(xb, w1b[...], preferred_element_type=F32)
        h3 = jnp.dot(xb, w3b[...], preferred_element_type=F32)
        h = (jax.nn.silu(h1) * h3).astype(BF16)
        _to_packed_token_major(y_ref, jnp.dot(h, w2b[...], preferred_element_type=F32), 0, rows)

    @pl.when(i >= n_used)
    def _():
        y_ref[...] = jnp.zeros_like(y_ref)


def _experts(blk_e, n_used, run_idx, next_e, xs, w1, w3, w2):
    rows = MOE_ROWS
    n_blk = blk_e.shape[0]
    hbm = pl.BlockSpec(memory_space=pl.ANY)
    grid_spec = pltpu.PrefetchScalarGridSpec(
        num_scalar_prefetch=4,
        grid=(n_blk,),
        in_specs=[
            pl.BlockSpec((rows * PACKED_CHUNKS, LANES), lambda i, be, nu, *_: (jnp.minimum(i, nu[0] - 1), 0)),
            hbm, hbm, hbm,
        ],
        out_specs=pl.BlockSpec((rows * PACKED_CHUNKS, LANES), lambda i, *_: (i, 0)),
        scratch_shapes=[
            pltpu.VMEM((2, D_MODEL, D_EXPERT), F32),
            pltpu.VMEM((2, D_MODEL, D_EXPERT), F32),
            pltpu.VMEM((2, D_EXPERT, D_MODEL), F32),
            pltpu.SemaphoreType.DMA((2,)),
            pltpu.VMEM((D_MODEL, D_EXPERT), BF16),
            pltpu.VMEM((D_MODEL, D_EXPERT), BF16),
            pltpu.VMEM((D_EXPERT, D_MODEL), BF16),
        ],
    )
    return pl.pallas_call(
        functools.partial(_expert_kernel, rows=rows),
        grid_spec=grid_spec,
        out_shape=jax.ShapeDtypeStruct((n_blk * rows * PACKED_CHUNKS, LANES), U32),
        compiler_params=_params("arbitrary"),
        name="moe_experts",
    )(blk_e, n_used, run_idx, next_e, xs, w1, w3, w2)


def _combine_kernel(slot_ref, ys_hbm, x1_ref, rw_ref, g2_ref, b2_ref, o_ref, yg, sem, *, tm):
    i = pl.program_id(0)
    n = pl.num_programs(0)
    slot = i % 2
    slab = PACKED_CHUNKS
    half = tm * slab
    group = 8

    def start_gather(tile, slot_):
        def body(g, c):
            base = pl.multiple_of(g * (group * slab), group * slab)
            for t in range(group):
                for k in range(2):
                    src_row = slot_ref[tile * (2 * tm) + g * (2 * group) + 2 * t + k] * slab
                    pltpu.make_async_copy(
                        ys_hbm.at[pl.ds(pl.multiple_of(src_row, slab), slab), :],
                        yg.at[slot_, pl.ds(k * half + base + t * slab, slab), :],
                        sem.at[slot_]).start(priority=k)
            return c
        lax.fori_loop(0, tm // group, body, 0)

    @pl.when(i == 0)
    def _():
        start_gather(0, 0)

    @pl.when(i + 1 < n)
    def _():
        start_gather(i + 1, 1 - slot)

    pltpu.make_async_copy(ys_hbm.at[pl.ds(0, 2 * half), :], yg.at[slot], sem.at[slot]).wait()
    rw = rw_ref[...]
    buf = yg.at[slot]
    moe = (_from_packed_token_major(buf, 0, tm) * rw[:, 0:1]
           + _from_packed_token_major(buf, tm, tm) * rw[:, 1:2])
    o_ref[...] = _layer_norm_rows(ALPHA * x1_ref[...] + moe, g2_ref[...], b2_ref[...])


def _combine(slot_flat, ys, x1, rw, ln2_g, ln2_b):
    T = x1.shape[0]
    tm = TM_COMB
    grid_spec = pltpu.PrefetchScalarGridSpec(
        num_scalar_prefetch=1,
        grid=(T // tm,),
        in_specs=[
            pl.BlockSpec(memory_space=pl.ANY),
            pl.BlockSpec((tm, D_MODEL), lambda i, *_: (i, 0)),
            pl.BlockSpec((tm, LANES), lambda i, *_: (i, 0)),
            pl.BlockSpec((1, D_MODEL), lambda i, *_: (0, 0)),
            pl.BlockSpec((1, D_MODEL), lambda i, *_: (0, 0)),
        ],
        out_specs=pl.BlockSpec((tm, D_MODEL), lambda i, *_: (i, 0)),
        scratch_shapes=[
            pltpu.VMEM((2, 2 * tm * PACKED_CHUNKS, LANES), U32),
            pltpu.SemaphoreType.DMA((2,)),
        ],
    )
    return pl.pallas_call(
        functools.partial(_combine_kernel, tm=tm),
        grid_spec=grid_spec,
        out_shape=jax.ShapeDtypeStruct((T, D_MODEL), F32),
        compiler_params=_params("arbitrary"),
        name="moe_combine_ln",
    )(slot_flat, ys, x1, rw, ln2_g.reshape(1, D_MODEL).astype(F32), ln2_b.reshape(1, D_MODEL).astype(F32))


def _dispatch_plan(counts, pk_flat, n_tokens):
    rows = MOE_ROWS
    experts = jnp.arange(N_EXPERTS, dtype=jnp.int32)
    padded = (counts + rows - 1) // rows * rows
    pad_end = jnp.cumsum(padded).astype(jnp.int32)
    pad_start = (pad_end - padded).astype(jnp.int32)
    n_blk = 2 * n_tokens // rows + N_EXPERTS
    blk_start = jnp.arange(n_blk, dtype=jnp.int32) * rows
    blk_e = jnp.minimum(jnp.sum(blk_start[:, None] >= pad_end[None, :], axis=1), N_EXPERTS - 1).astype(jnp.int32)
    n_used = (pad_end[-1:] // rows).astype(jnp.int32)
    new_run = jnp.concatenate([jnp.ones((1,), jnp.int32), (blk_e[1:] != blk_e[:-1]).astype(jnp.int32)])
    run_idx = (jnp.cumsum(new_run) - 1).astype(jnp.int32)
    later_active = (experts[None, :] > experts[:, None]) & (counts[None, :] > 0)
    next_active = jnp.min(jnp.where(later_active, experts[None, :], N_EXPERTS), axis=1)
    next_active = jnp.where(next_active == N_EXPERTS, -1, next_active).astype(jnp.int32)
    next_e = next_active[blk_e]
    e_flat = pk_flat >> RANK_BITS
    start_of = jnp.sum(jnp.where(e_flat[:, None] == experts[None, :], pad_start[None, :], 0), axis=1)
    slot_flat = (start_of + (pk_flat & (RANK_SPAN - 1))).astype(jnp.int32)
    return slot_flat, pad_start, pad_end, blk_e, n_used, run_idx, next_e, n_blk


def kernel(x, w_in, conv_w, conv_b, lru_wa, lru_ba, lru_wx, lru_bx, lru_lambda, attn_norm_g, lru_norm_g,
           w_out, ln1_g, ln1_b, router_grp_w, router_grp_b, router_exp_w, router_exp_b, w1, w3, w2,
           ln2_g, ln2_b):
    B, S, D = x.shape
    assert D == D_MODEL and S % DILATED_PATTERNS[-1][0] == 0 and w_in.shape[0] == 1
    T = B * S
    x2d = x.reshape(T, D)

    qkv, rg = _in_proj(x2d, w_in[0].astype(BF16), B, S)
    attn = _attention(qkv, B, S)
    yrec = _recurrent(rg, conv_w[0], conv_b[0], lru_wa[0], lru_ba[0], lru_wx[0], lru_bx[0], lru_lambda[0],
                      lru_norm_g[0], B, S)

    n_r = N_GROUPS + N_EXPERTS
    w_router = jnp.zeros((D, LANES), F32).at[:, :n_r].set(
        jnp.concatenate([router_grp_w[0], router_exp_w[0]], axis=-1).astype(F32))
    b_router = jnp.zeros((1, LANES), F32).at[0, :n_r].set(
        jnp.concatenate([router_grp_b[0], router_exp_b[0]], axis=-1).astype(F32))
    x1, x1p, pk, rw, cnt = _out_router(attn, yrec, x2d, attn_norm_g[0], w_out[0].astype(BF16),
                                       ln1_g[0], ln1_b[0], w_router, b_router)

    pk_flat = pk[:, :2].reshape(-1)
    slot_flat, pad_start, pad_end, blk_e, n_used, run_idx, next_e, n_blk = _dispatch_plan(
        cnt[0, :N_EXPERTS], pk_flat, T)
    xs = _dispatch(slot_flat, pad_start, pad_end, n_used, x1p, n_blk)
    ys = _experts(blk_e, n_used, run_idx, next_e, xs, w1[0], w3[0], w2[0])
    out = _combine(slot_flat, ys, x1, rw, ln2_g[0], ln2_b[0])
    return out.reshape(B, S, D)
```

```python
import functools

import jax
import jax.numpy as jnp
import numpy as np
from jax import lax
from jax.experimental import pallas as pl
from jax.experimental.pallas import tpu as pltpu

F32 = jnp.float32
BF16 = jnp.bfloat16

D_MODEL = 2048
D_ATTN = 1024
D_LRU = 1024
HEAD_DIM = 64
N_HEADS = 16
LANES = 128
N_PLANES = D_ATTN // LANES
DILATED_PATTERNS = ((128, 1), (512, 4), (2048, 16))
SUB_WINDOW = 128
LRU_BLOCKS = 16
LRU_BLOCK_DIM = 64
CONV_WIDTH = 4
RG_C = 8.0
N_GROUPS = 4
EXPERTS_PER_GROUP = 8
N_EXPERTS = 32
D_EXPERT = 512
ALPHA = 2.0 ** 0.25
LN_EPS = 1e-5
RMS_EPS = 1e-6
MASKED = -1e30
LOG2E = 1.4426950408889634

VMEM_LIMIT = 56 * 1024 * 1024

TM_PROJ = 256
TS_LRU = 256
TM_OUT = 256
MOE_ROWS = 256
TM_COMB = 256
TM_DISP = 2048
CHUNKS = D_MODEL // LANES
PACKED_CHUNKS = CHUNKS // 2
RANK_BITS = 16
RANK_SPAN = 1 << RANK_BITS
ATTN_UNROLL = 32


def _params(*sem):
    return pltpu.CompilerParams(dimension_semantics=sem, vmem_limit_bytes=VMEM_LIMIT)


def _in_proj_kernel(x_ref, w_ref, qkv_ref, rg_ref, *, tm):
    xb = x_ref[...].astype(BF16)
    per_class = tm // CLASSES
    out_row = lax.broadcasted_iota(jnp.int32, (tm, tm), 0)
    src_row = (out_row % per_class) * CLASSES + out_row // per_class
    perm = (lax.broadcasted_iota(jnp.int32, (tm, tm), 1) == src_row).astype(BF16)
    xb_cm = jnp.dot(perm, xb, preferred_element_type=F32).astype(BF16)
    for c in range(5):
        lhs = xb_cm if c < 3 else xb
        acc = jnp.dot(lhs, w_ref[:, c * D_ATTN:(c + 1) * D_ATTN], preferred_element_type=F32)
        if c == 0:
            acc = acc * (HEAD_DIM ** -0.5 * LOG2E)
        for p in range(N_PLANES):
            plane = acc[:, p * LANES:(p + 1) * LANES]
            if c < 3:
                for r in range(CLASSES):
                    qkv_ref[c * N_PLANES + p, r] = plane[r * per_class:(r + 1) * per_class, :]
            else:
                rg_ref[(c - 3) * N_PLANES + p] = plane


def _in_proj(x2d, w_in_b, batch, seq):
    T = x2d.shape[0]
    tm = TM_PROJ
    tiles = seq // tm
    assert tm % CLASSES == 0 and seq % tm == 0
    return pl.pallas_call(
        functools.partial(_in_proj_kernel, tm=tm),
        grid=(T // tm,),
        in_specs=[
            pl.BlockSpec((tm, D_MODEL), lambda i: (i, 0)),
            pl.BlockSpec((D_MODEL, 5 * D_ATTN), lambda i: (0, 0), pipeline_mode=pl.Buffered(1)),
        ],
        out_specs=[
            pl.BlockSpec((3 * N_PLANES, None, CLASSES, tm // CLASSES, LANES),
                         lambda i: (0, i // tiles, 0, i % tiles, 0)),
            pl.BlockSpec((2 * N_PLANES, tm, LANES), lambda i: (0, i, 0)),
        ],
        out_shape=[
            jax.ShapeDtypeStruct((3 * N_PLANES, batch, CLASSES, seq // CLASSES, LANES), F32),
            jax.ShapeDtypeStruct((2 * N_PLANES, T, LANES), F32),
        ],
        compiler_params=_params("parallel"),
        name="in_proj",
    )(x2d, w_in_b)


CLASSES = 16


def _attn_kernel(slopes_ref, qs, ks, vs, ndf_ref, ndh_ref, o_ref,
                 os_, m_s, l_s, acc_s, bias_f, bias_h, *, seq):
    hp = pl.program_id(1)
    slope_a = slopes_ref[2 * hp] * LOG2E
    slope_b = slopes_ref[2 * hp + 1] * LOG2E
    w = SUB_WINDOW
    cl = seq // CLASSES
    n_pat = len(DILATED_PATTERNS)

    def gather(ref, starts, size):
        return jnp.concatenate([ref[pl.ds(s, size), :] for s in starts], axis=0)

    def scatter(ref, starts, size, val):
        for c, s in enumerate(starts):
            ref[pl.ds(s, size), :] = val[c * size:(c + 1) * size, :]

    for t in range(2 * n_pat):
        bias_f[t, 0:w, :] = slope_a * ndf_ref[t]
        bias_f[t, w:2 * w, :] = slope_b * ndf_ref[t]
    for t in range(n_pat):
        bias_h[t, 0:w, :] = slope_a * ndh_ref[t]
        bias_h[t, w:2 * w, :] = slope_b * ndh_ref[t]

    def block(c, n, d, pi, first, last, keys):
        qc = w * d // CLASSES
        aligned = lambda x: x if isinstance(x, int) else pl.multiple_of(x, 8)
        bases = [(c + d * j) * cl for j in range(CLASSES // d)]
        qstarts = [aligned(b + n * qc) for b in bases]
        if keys == "own":
            kstarts, kc, bias = qstarts, qc, bias_h[pi]
        elif keys == "prev+own":
            kstarts, kc, bias = [aligned(b + (n - 1) * qc) for b in bases], 2 * qc, bias_f[pi]
        else:
            kfirst = jnp.maximum(n - 1, 0) * qc
            kstarts, kc = [aligned(b + kfirst) for b in bases], 2 * qc
            bias = bias_f[jnp.where(n == 0, pi + n_pat, pi)]
        is_a = lax.broadcasted_iota(jnp.int32, (w, LANES), 1) < HEAD_DIM
        q = gather(qs, qstarts, qc)
        k = gather(ks, kstarts, kc).astype(BF16)
        v = gather(vs, kstarts, kc).astype(BF16)
        zero = jnp.zeros_like(q)
        q2 = jnp.concatenate([jnp.where(is_a, q, zero), jnp.where(is_a, zero, q)], axis=0).astype(BF16)
        s = lax.dot_general(q2, k, (((1,), (1,)), ((), ())), preferred_element_type=F32) + bias
        m = jnp.max(s, axis=-1, keepdims=True)
        p = jnp.exp2(s - m)
        l = jnp.sum(p, axis=-1, keepdims=True)
        o = jnp.dot(p.astype(BF16), v, preferred_element_type=F32)
        m_c = jnp.where(is_a, m[:w], m[w:])
        l_c = jnp.where(is_a, l[:w], l[w:])
        o_c = jnp.where(is_a, o[:w], o[w:])
        if first:
            scatter(m_s, qstarts, qc, m_c)
            scatter(l_s, qstarts, qc, l_c)
            scatter(acc_s, qstarts, qc, o_c)
            return
        m_o = gather(m_s, qstarts, qc)
        m_n = jnp.maximum(m_o, m_c)
        e_o = jnp.exp2(m_o - m_n)
        e_c = jnp.exp2(m_c - m_n)
        l_n = gather(l_s, qstarts, qc) * e_o + l_c * e_c
        a_n = gather(acc_s, qstarts, qc) * e_o + o_c * e_c
        if last:
            scatter(os_, qstarts, qc, a_n / l_n)
        else:
            scatter(m_s, qstarts, qc, m_n)
            scatter(l_s, qstarts, qc, l_n)
            scatter(acc_s, qstarts, qc, a_n)

    for pi, (window, d) in enumerate(DILATED_PATTERNS):
        assert window // d == w and CLASSES % d == 0
        first, last = pi == 0, pi == n_pat - 1
        nb = seq // (w * d)

        if nb >= 4:
            def any_block(i, carry, d=d, pi=pi, first=first, last=last):
                block(i % d, i // d, d, pi, first, last, "any")
                return carry
            lax.fori_loop(0, d * nb, any_block, 0, unroll=ATTN_UNROLL)
        else:
            def class_blocks(c, carry, d=d, pi=pi, first=first, last=last, nb=nb):
                block(c, 0, d, pi, first, last, "own")
                for n in range(1, nb):
                    block(c, n, d, pi, first, last, "prev+own")
                return carry
            lax.fori_loop(0, d, class_blocks, 0, unroll=max(ATTN_UNROLL // nb, 1))

    for r in range(CLASSES):
        o_ref[pl.ds(r, cl, stride=CLASSES), :] = os_[pl.ds(r * cl, cl), :]


def _neg_distance_tables():
    w = SUB_WINDOW
    prev, nxt, own = [], [], []
    for _, d in DILATED_PATTERNS:
        n_cls = CLASSES // d
        qc = w // n_cls
        def pos(chunk_rows):
            j, l = np.divmod(np.arange(n_cls * chunk_rows), chunk_rows)
            return n_cls * l + j
        sq, sk2, sk1 = pos(qc)[:, None], pos(2 * qc)[None, :], pos(qc)[None, :]
        for out, dist in ((prev, sq + w - sk2), (nxt, sq - sk2), (own, sq - sk1)):
            valid = (dist >= 0) & (dist <= w)
            out.append(np.where(valid, -(dist * d).astype(np.float32), np.float32(MASKED)))
    return jnp.asarray(np.stack(prev + nxt), dtype=F32), jnp.asarray(np.stack(own), dtype=F32)


def _attention(qkv, batch, seq):
    qkv4 = qkv.reshape(3 * N_PLANES, batch, seq, LANES)
    slopes = jnp.exp2(-8.0 * jnp.arange(1, N_HEADS + 1, dtype=F32) / N_HEADS)
    nd_full, nd_head = _neg_distance_tables()
    n_pat = len(DILATED_PATTERNS)
    plane = lambda off: pl.BlockSpec((None, None, seq, LANES), lambda b, h: (off + h, b, 0, 0))
    return pl.pallas_call(
        functools.partial(_attn_kernel, seq=seq),
        grid=(batch, N_PLANES),
        in_specs=[
            pl.BlockSpec(memory_space=pltpu.SMEM),
            plane(0), plane(N_PLANES), plane(2 * N_PLANES),
            pl.BlockSpec((2 * n_pat, SUB_WINDOW, 2 * SUB_WINDOW), lambda b, h: (0, 0, 0)),
            pl.BlockSpec((n_pat, SUB_WINDOW, SUB_WINDOW), lambda b, h: (0, 0, 0)),
        ],
        out_specs=pl.BlockSpec((None, None, seq, LANES), lambda b, h: (h, b, 0, 0)),
        scratch_shapes=[pltpu.VMEM((seq, LANES), F32)] * 4 + [
            pltpu.VMEM((2 * n_pat, 2 * SUB_WINDOW, 2 * SUB_WINDOW), F32),
            pltpu.VMEM((n_pat, 2 * SUB_WINDOW, SUB_WINDOW), F32),
        ],
        out_shape=jax.ShapeDtypeStruct((N_PLANES, batch, seq, LANES), F32),
        compiler_params=_params("parallel", "parallel"),
        name="dilated_attention",
    )(slopes, qkv4, qkv4, qkv4, nd_full, nd_head)


def _gelu_tanh(x):
    c = np.float32(np.sqrt(2.0 / np.pi))
    return 0.5 * x * (1.0 + jnp.tanh(c * (x + 0.044715 * (x * x * x))))


def _log1p(x):
    u = 1.0 + x
    return jnp.where(u == 1.0, x, jnp.log(u) * x / (u - 1.0))


def _softplus(z):
    return jnp.maximum(z, 0.0) + _log1p(jnp.exp(-jnp.abs(z)))


PHASES = 8


def _lru_kernel(xr_ref, xg_ref, cw_ref, cb_ref, wbd_ref, ba_ref, bx_ref, lam_ref, g_ref, y_ref,
                xc, hc, *, ts):
    i = pl.program_id(1)
    G = ts // PHASES
    first_group = lax.broadcasted_iota(jnp.int32, (G, LANES), 0) == 0
    sub = lax.broadcasted_iota(jnp.int32, (G, LANES), 0)

    @pl.when(i == 0)
    def _():
        xc[...] = jnp.zeros_like(xc)
        hc[...] = jnp.zeros_like(hc)

    def prev_group(cur, carry_row):
        return jnp.where(first_group, carry_row, pltpu.roll(cur, 1, 0))

    conv = []
    for l in range(N_PLANES):
        x = [xr_ref[l, pl.ds(s, G, stride=PHASES), :] for s in range(PHASES)]
        back = {s: prev_group(x[s], xc[l, s:s + 1, :]) for s in range(PHASES - CONV_WIDTH + 1, PHASES)}
        phases = []
        for s in range(PHASES):
            acc = jnp.broadcast_to(cb_ref[l:l + 1, :], (G, LANES))
            for j in range(CONV_WIDTH):
                q = s - (CONV_WIDTH - 1) + j
                acc = acc + (x[q] if q >= 0 else back[q + PHASES]) * cw_ref[j, l:l + 1, :]
            phases.append(acc)
        conv.append(jnp.concatenate(phases, axis=0))
        xc[l] = xr_ref[l, ts - PHASES:ts, :]

    gates_r, gates_i = [], []
    for j in range(N_PLANES // 2):
        yb = jnp.concatenate([conv[2 * j], conv[2 * j + 1]], axis=-1).astype(BF16)
        g = jnp.dot(yb, wbd_ref[j], preferred_element_type=F32)
        gates_r += [g[:, 0:LANES], g[:, LANES:2 * LANES]]
        gates_i += [g[:, 2 * LANES:3 * LANES], g[:, 3 * LANES:4 * LANES]]

    rec, sq = [], None
    for l in range(N_PLANES):
        y = conv[l]
        r = jax.nn.sigmoid(gates_r[l] + ba_ref[l:l + 1, :])
        ig = jax.nn.sigmoid(gates_i[l] + bx_ref[l:l + 1, :])
        log_a = (-RG_C * r) * _softplus(-lam_ref[l:l + 1, :])
        a = jnp.exp(log_a)
        z = 1.0 - a * a
        u = jnp.where(z > 0.0, z * lax.rsqrt(z), 0.0) * (ig * y)

        piece = lambda v, s: v[s * G:(s + 1) * G, :]
        pa, pb = [piece(a, 0)], [piece(u, 0)]
        for s in range(1, PHASES):
            pa.append(piece(a, s) * pa[-1])
            pb.append(piece(a, s) * pb[-1] + piece(u, s))
        ga, gb = pa[-1], pb[-1]
        sh = 1
        while sh < G:
            take = sub >= sh
            gb = jnp.where(take, ga * pltpu.roll(gb, sh, 0) + gb, gb)
            ga = jnp.where(take, ga * pltpu.roll(ga, sh, 0), ga)
            sh *= 2
        h0 = hc[l, 0:1, :]
        h_end = ga * h0 + gb
        h_in = prev_group(h_end, h0)
        hc[l] = jnp.broadcast_to(h_end[G - 1:G, :], (PHASES, LANES))
        h = jnp.concatenate([pa[s] * h_in + pb[s] for s in range(PHASES)], axis=0)

        xg = jnp.concatenate([xg_ref[l, pl.ds(s, G, stride=PHASES), :] for s in range(PHASES)], axis=0)
        rl = h * _gelu_tanh(xg)
        rec.append(rl)
        part = jnp.sum(rl * rl, axis=-1, keepdims=True)
        sq = part if sq is None else sq + part

    scale = lax.rsqrt(sq * (1.0 / D_LRU) + RMS_EPS)
    for l in range(N_PLANES):
        out = rec[l] * scale * g_ref[l:l + 1, :]
        for s in range(PHASES):
            y_ref[l, pl.ds(s, G, stride=PHASES), :] = out[s * G:(s + 1) * G, :]


def _block_diag_gates(wa, wx):
    def bd(wm):
        wm = wm.reshape(4, 4, LRU_BLOCK_DIM, LRU_BLOCK_DIM)
        eye = jnp.eye(4, dtype=wm.dtype)
        full = jnp.einsum("gide,ij->gidje", wm, eye)
        return full.reshape(4, 4 * LRU_BLOCK_DIM, 4 * LRU_BLOCK_DIM)
    return jnp.concatenate([bd(wa), bd(wx)], axis=-1).astype(BF16)


def _recurrent(rg, conv_w, conv_b, lru_wa, lru_ba, lru_wx, lru_bx, lru_lambda, lru_norm_g, batch, seq):
    ts = TS_LRU
    rg5 = rg.reshape(2, N_PLANES, batch, seq, LANES)
    wbd = _block_diag_gates(lru_wa, lru_wx)
    planes = lambda a: a.reshape(N_PLANES, LANES).astype(F32)
    vec = pl.BlockSpec((N_PLANES, LANES), lambda b, i: (0, 0))
    return pl.pallas_call(
        functools.partial(_lru_kernel, ts=ts),
        grid=(batch, seq // ts),
        in_specs=[
            pl.BlockSpec((None, N_PLANES, None, ts, LANES), lambda b, i: (0, 0, b, i, 0)),
            pl.BlockSpec((None, N_PLANES, None, ts, LANES), lambda b, i: (1, 0, b, i, 0)),
            pl.BlockSpec((CONV_WIDTH, N_PLANES, LANES), lambda b, i: (0, 0, 0)),
            vec,
            pl.BlockSpec((4, 4 * LRU_BLOCK_DIM, 8 * LRU_BLOCK_DIM), lambda b, i: (0, 0, 0)),
            vec, vec, vec, vec,
        ],
        out_specs=pl.BlockSpec((N_PLANES, None, ts, LANES), lambda b, i: (0, b, i, 0)),
        out_shape=jax.ShapeDtypeStruct((N_PLANES, batch, seq, LANES), F32),
        scratch_shapes=[
            pltpu.VMEM((N_PLANES, PHASES, LANES), F32),
            pltpu.VMEM((N_PLANES, PHASES, LANES), F32),
        ],
        compiler_params=_params("parallel", "arbitrary"),
        name="conv_rglru",
    )(rg5, rg5, conv_w.astype(F32).reshape(CONV_WIDTH, N_PLANES, LANES), planes(conv_b), wbd, planes(lru_ba),
      planes(lru_bx), planes(lru_lambda), planes(lru_norm_g))


def _layer_norm_rows(z, g, b):
    mu = jnp.mean(z, axis=-1, keepdims=True)
    zc = z - mu
    var = jnp.mean(zc * zc, axis=-1, keepdims=True)
    return zc * lax.rsqrt(var + LN_EPS) * g + b


U32 = jnp.uint32
HI_HALF = np.uint32(0xFFFF0000)


def _to_packed_token_major(ref, val, first_row, n_rows):
    bits = lambda a: lax.bitcast_convert_type(a.astype(BF16).astype(F32), U32)
    half = PACKED_CHUNKS * LANES
    for c in range(PACKED_CHUNKS):
        lo = bits(val[:, c * LANES:(c + 1) * LANES]) >> 16
        hi = bits(val[:, half + c * LANES:half + (c + 1) * LANES]) & HI_HALF
        ref[pl.ds(first_row * PACKED_CHUNKS + c, n_rows, stride=PACKED_CHUNKS), :] = lo | hi


def _from_packed_token_major(ref, first_row, n_rows):
    words = [ref[pl.ds(first_row * PACKED_CHUNKS + c, n_rows, stride=PACKED_CHUNKS), :]
             for c in range(PACKED_CHUNKS)]
    lo = [lax.bitcast_convert_type(wd << 16, F32) for wd in words]
    hi = [lax.bitcast_convert_type(wd & HI_HALF, F32) for wd in words]
    return jnp.concatenate(lo + hi, axis=-1)


def _out_router_kernel(*refs, tm):
    *io_refs, cnt_s, z_a, z_b = refs
    i = pl.program_id(0)

    @pl.when(i == 0)
    def _():
        cnt_s[...] = jnp.zeros_like(cnt_s)
        z_b[...] = jnp.zeros_like(z_b)

    @pl.when(i % 2 == 0)
    def _():
        _out_router_step(*io_refs, cnt_s, z_b, z_a, tm=tm)

    @pl.when(i % 2 == 1)
    def _():
        _out_router_step(*io_refs, cnt_s, z_a, z_b, tm=tm)


def _out_router_step(attn_ref, yrec_ref, x_ref, ga_ref, wo_ref, g1_ref, b1_ref, wr_ref, br_ref,
                     x1_ref, x1p_ref, pk_ref, rw_ref, cnt_ref, cnt_s, z_in, z_out, *, tm):
    i = pl.program_id(0)
    attn = jnp.concatenate([attn_ref[p] for p in range(N_PLANES)], axis=-1)
    ms = jnp.mean(attn * attn, axis=-1, keepdims=True)
    ya = (attn * lax.rsqrt(ms + RMS_EPS) * ga_ref[...]).astype(BF16)
    yr = jnp.concatenate([yrec_ref[p] for p in range(N_PLANES)], axis=-1).astype(BF16)
    n_col = 8
    cw = D_MODEL // n_col

    def project(j):
        cols = slice(j * cw, (j + 1) * cw)
        mix = jnp.dot(ya, wo_ref[0:D_ATTN, cols], preferred_element_type=F32)
        mix = mix + jnp.dot(yr, wo_ref[D_ATTN:, cols], preferred_element_type=F32)
        z_out[:, cols] = ALPHA * x_ref[:, cols] + mix

    def normalize(r0, n_rows):
        rows = slice(r0, r0 + n_rows)
        x1 = _layer_norm_rows(z_in[rows, :], g1_ref[...], b1_ref[...])
        x1_ref[rows, :] = x1
        _to_packed_token_major(x1p_ref, x1, r0, n_rows)
        return x1.astype(BF16)

    n_row = 4
    rh = tm // n_row
    parts = []
    for k in range(n_row):
        parts.append(normalize(k * rh, rh))
        project(k)

    xb = jnp.concatenate(parts, axis=0)
    logits = jnp.dot(xb, wr_ref[...].astype(BF16), preferred_element_type=F32) + br_ref[...]
    project(4)
    project(5)
    _route(logits, pk_ref, rw_ref, cnt_ref, cnt_s, i, tm)
    project(6)
    project(7)


def _route(logits, pk_ref, rw_ref, cnt_ref, cnt_s, i, tm):
    lane = lax.broadcasted_iota(jnp.int32, (tm, LANES), 1)
    big = jnp.int32(LANES)
    first_true = lambda c: jnp.min(jnp.where(c, lane, big), axis=-1, keepdims=True)

    in_g = lane < N_GROUPS
    gl = jnp.where(in_g, logits, MASKED)
    gmax = jnp.max(gl, axis=-1, keepdims=True)
    g_idx = first_true(gl == gmax)
    gsum = jnp.sum(jnp.where(in_g, jnp.exp(gl - gmax), 0.0), axis=-1, keepdims=True)
    g_gate = 1.0 / gsum

    lo = N_GROUPS + EXPERTS_PER_GROUP * g_idx
    in_e = (lane >= lo) & (lane < lo + EXPERTS_PER_GROUP)
    el = jnp.where(in_e, logits, MASKED)
    emax = jnp.max(el, axis=-1, keepdims=True)
    ee = jnp.where(in_e, jnp.exp(el - emax), 0.0)
    esum = jnp.sum(ee, axis=-1, keepdims=True)
    i1 = first_true(el == emax)
    rest = jnp.where(in_e & (lane != i1), ee, -1.0)
    e2max = jnp.max(rest, axis=-1, keepdims=True)
    i2 = first_true(rest == e2max)
    v1 = 1.0 / esum
    v2 = e2max / esum
    den = v1 + v2
    w1 = g_gate * v1 / den
    w2 = g_gate * v2 / den
    e1 = i1 - N_GROUPS
    e2 = i2 - N_GROUPS
    rw_ref[...] = jnp.where(lane == 0, w1, jnp.where(lane == 1, w2, 0.0))

    hot1 = lane == e1
    hot2 = lane == e2
    both = jnp.where((hot1 | hot2) & (i > 0), 1.0, 0.0)
    earlier = (lax.broadcasted_iota(jnp.int32, (tm, tm), 1) < lax.broadcasted_iota(jnp.int32, (tm, tm), 0))
    before = jnp.dot(earlier.astype(BF16), both.astype(BF16), preferred_element_type=F32) + cnt_s[0:1, :]
    r1 = jnp.sum(jnp.where(hot1, before, 0.0), axis=-1, keepdims=True).astype(jnp.int32)
    r2 = jnp.sum(jnp.where(hot2, before, 0.0), axis=-1, keepdims=True).astype(jnp.int32)
    pk_ref[...] = jnp.where(lane == 0, e1 * RANK_SPAN + r1, jnp.where(lane == 1, e2 * RANK_SPAN + r2, 0))
    cnt_s[...] = cnt_s[...] + jnp.sum(both, axis=0, keepdims=True)
    cnt_ref[...] = cnt_s[...].astype(jnp.int32)


def _out_router(attn, yrec, x2d, attn_norm_g, w_out_b, ln1_g, ln1_b, w_router, b_router):
    T = x2d.shape[0]
    tm = TM_OUT
    assert 2 * T <= RANK_SPAN
    attn3 = attn.reshape(N_PLANES, T, LANES)
    yrec3 = yrec.reshape(N_PLANES, T, LANES)
    const = lambda shape: pl.BlockSpec(shape, lambda i: (0,) * len(shape))
    n = T // tm
    stage1 = lambda i: jnp.minimum(i, n - 1)
    stage2 = lambda i: jnp.maximum(i - 1, 0)
    return pl.pallas_call(
        functools.partial(_out_router_kernel, tm=tm),
        grid=(n + 1,),
        in_specs=[
            pl.BlockSpec((N_PLANES, tm, LANES), lambda i: (0, stage1(i), 0)),
            pl.BlockSpec((N_PLANES, tm, LANES), lambda i: (0, stage1(i), 0)),
            pl.BlockSpec((tm, D_MODEL), lambda i: (stage1(i), 0)),
            const((1, D_ATTN)),
            pl.BlockSpec((D_MODEL, D_MODEL), lambda i: (0, 0), pipeline_mode=pl.Buffered(1)),
            const((1, D_MODEL)), const((1, D_MODEL)),
            const((D_MODEL, LANES)), const((1, LANES)),
        ],
        out_specs=[
            pl.BlockSpec((tm, D_MODEL), lambda i: (stage2(i), 0)),
            pl.BlockSpec((tm * PACKED_CHUNKS, LANES), lambda i: (stage2(i), 0)),
            pl.BlockSpec((tm, LANES), lambda i: (stage2(i), 0)),
            pl.BlockSpec((tm, LANES), lambda i: (stage2(i), 0)),
            const((8, LANES)),
        ],
        out_shape=[
            jax.ShapeDtypeStruct((T, D_MODEL), F32),
            jax.ShapeDtypeStruct((T * PACKED_CHUNKS, LANES), U32),
            jax.ShapeDtypeStruct((T, LANES), jnp.int32),
            jax.ShapeDtypeStruct((T, LANES), F32),
            jax.ShapeDtypeStruct((8, LANES), jnp.int32),
        ],
        scratch_shapes=[pltpu.VMEM((8, LANES), F32), pltpu.VMEM((tm, D_MODEL), F32),
                        pltpu.VMEM((tm, D_MODEL), F32)],
        compiler_params=_params("arbitrary"),
        name="out_proj_router",
    )(attn3, yrec3, x2d, attn_norm_g.reshape(1, D_ATTN).astype(F32), w_out_b,
      ln1_g.reshape(1, D_MODEL).astype(F32), ln1_b.reshape(1, D_MODEL).astype(F32), w_router, b_router)


def _dispatch_kernel(slot_ref, pstart_ref, pend_ref, nused_ref, x1p_ref, xs_hbm, zbuf, sem, zsem,
                     *, tm, rows, n_blk):
    i = pl.program_id(0)
    slab = PACKED_CHUNKS
    blk = rows * slab

    @pl.when(i == 0)
    def _():
        zbuf[...] = jnp.zeros_like(zbuf)
        n_used = nused_ref[0]

        def zero_copy(b):
            return pltpu.make_async_copy(zbuf, xs_hbm.at[pl.ds(pl.multiple_of(b * blk, blk), blk), :], zsem)

        def for_each_zeroed_block(fn):
            for e in range(N_EXPERTS):
                @pl.when(pend_ref[e] > pstart_ref[e])
                def _(e=e):
                    fn(pend_ref[e] // rows - 1)

            def tail(b, c):
                fn(b)
                return c
            lax.fori_loop(n_used, n_blk, tail, 0)

        for_each_zeroed_block(lambda b: zero_copy(b).start())
        for_each_zeroed_block(lambda b: zero_copy(b).wait())

    group = 8

    def push(g, c):
        for t in range(group):
            src = x1p_ref.at[pl.ds(pl.multiple_of(g * (group * slab), group * slab) + t * slab, slab), :]
            for k in range(2):
                row = slot_ref[i * (2 * tm) + g * (2 * group) + 2 * t + k] * slab
                dst = xs_hbm.at[pl.ds(pl.multiple_of(row, slab), slab), :]
                pltpu.make_async_copy(src, dst, sem).start(priority=k)
        return c
    lax.fori_loop(0, tm // group, push, 0)
    for _ in range(2):
        pltpu.make_async_copy(x1p_ref, xs_hbm.at[pl.ds(0, tm * slab), :], sem).wait()


def _dispatch(slot_flat, pad_start, pad_end, n_used, x1p, n_blk):
    tm, rows, slab = TM_DISP, MOE_ROWS, PACKED_CHUNKS
    n_tok = x1p.shape[0] // slab
    grid_spec = pltpu.PrefetchScalarGridSpec(
        num_scalar_prefetch=4,
        grid=(n_tok // tm,),
        in_specs=[pl.BlockSpec((tm * slab, LANES), lambda i, *_: (i, 0))],
        out_specs=pl.BlockSpec(memory_space=pl.ANY),
        scratch_shapes=[
            pltpu.VMEM((rows * slab, LANES), U32),
            pltpu.SemaphoreType.DMA(()),
            pltpu.SemaphoreType.DMA(()),
        ],
    )
    return pl.pallas_call(
        functools.partial(_dispatch_kernel, tm=tm, rows=rows, n_blk=n_blk),
        grid_spec=grid_spec,
        out_shape=jax.ShapeDtypeStruct((n_blk * rows * slab, LANES), U32),
        compiler_params=_params("arbitrary"),
        name="moe_dispatch",
    )(slot_flat, pad_start, pad_end, n_used, x1p)


def _expert_kernel(blk_e_ref, n_used_ref, run_ref, next_e_ref, xs_ref, w1_hbm, w3_hbm, w2_hbm, y_ref,
                   wf1, wf3, wf2, wsem, w1b, w3b, w2b, *, rows):
    i = pl.program_id(0)
    n_used = n_used_ref[0]
    e = blk_e_ref[i]
    slot = run_ref[i] % 2
    new_run = jnp.logical_or(i == 0, e != blk_e_ref[jnp.maximum(i - 1, 0)])

    def weight_copies(expert, slot_):
        return [pltpu.make_async_copy(src.at[expert], dst.at[slot_], wsem.at[slot_])
                for src, dst in ((w1_hbm, wf1), (w3_hbm, wf3), (w2_hbm, wf2))]

    @pl.when(i == 0)
    def _():
        for cp in weight_copies(e, 0):
            cp.start()

    @pl.when(jnp.logical_and(i < n_used, new_run))
    def _():
        for cp in weight_copies(e, slot):
            cp.wait()
        nxt = next_e_ref[i]

        @pl.when(nxt >= 0)
        def _():
            for cp in weight_copies(nxt, 1 - slot):
                cp.start(priority=1)

        w1b[...] = wf1[slot].astype(BF16)
        w3b[...] = wf3[slot].astype(BF16)
        w2b[...] = wf2[slot].astype(BF16)

    @pl.when(i < n_used)
    def _():
        xb = _from_packed_token_major(xs_ref, 0, rows).astype(BF16)
        h1 = jnp.dot(xb, w1b[...], preferred_element_type=F32)
        h3 = jnp.dot(xb, w3b[...], preferred_element_type=F32)
        h = (jax.nn.silu(h1) * h3).astype(BF16)
        _to_packed_token_major(y_ref, jnp.dot(h, w2b[...], preferred_element_type=F32), 0, rows)

    @pl.when(i >= n_used)
    def _():
        y_ref[...] = jnp.zeros_like(y_ref)


def _experts(blk_e, n_used, run_idx, next_e, xs, w1, w3, w2):
    rows = MOE_ROWS
    n_blk = blk_e.shape[0]
    hbm = pl.BlockSpec(memory_space=pl.ANY)
    grid_spec = pltpu.PrefetchScalarGridSpec(
        num_scalar_prefetch=4,
        grid=(n_blk,),
        in_specs=[
            pl.BlockSpec((rows * PACKED_CHUNKS, LANES), lambda i, be, nu, *_: (jnp.minimum(i, nu[0] - 1), 0)),
            hbm, hbm, hbm,
        ],
        out_specs=pl.BlockSpec((rows * PACKED_CHUNKS, LANES), lambda i, *_: (i, 0)),
        scratch_shapes=[
            pltpu.VMEM((2, D_MODEL, D_EXPERT), F32),
            pltpu.VMEM((2, D_MODEL, D_EXPERT), F32),
            pltpu.VMEM((2, D_EXPERT, D_MODEL), F32),
            pltpu.SemaphoreType.DMA((2,)),
            pltpu.VMEM((D_MODEL, D_EXPERT), BF16),
            pltpu.VMEM((D_MODEL, D_EXPERT), BF16),
            pltpu.VMEM((D_EXPERT, D_MODEL), BF16),
        ],
    )
    return pl.pallas_call(
        functools.partial(_expert_kernel, rows=rows),
        grid_spec=grid_spec,
        out_shape=jax.ShapeDtypeStruct((n_blk * rows * PACKED_CHUNKS, LANES), U32),
        compiler_params=_params("arbitrary"),
        name="moe_experts",
    )(blk_e, n_used, run_idx, next_e, xs, w1, w3, w2)


def _combine_kernel(slot_ref, ys_hbm, x1_ref, rw_ref, g2_ref, b2_ref, o_ref, yg, sem, *, tm):
    i = pl.program_id(0)
    n = pl.num_programs(0)
    slot = i % 2
    slab = PACKED_CHUNKS
    half = tm * slab
    group = 8

    def start_gather(tile, slot_):
        def body(g, c):
            base = pl.multiple_of(g * (group * slab), group * slab)
            for t in range(group):
                for k in range(2):
                    src_row = slot_ref[tile * (2 * tm) + g * (2 * group) + 2 * t + k] * slab
                    pltpu.make_async_copy(
                        ys_hbm.at[pl.ds(pl.multiple_of(src_row, slab), slab), :],
                        yg.at[slot_, pl.ds(k * half + base + t * slab, slab), :],
                        sem.at[slot_]).start(priority=k)
            return c
        lax.fori_loop(0, tm // group, body, 0)

    @pl.when(i == 0)
    def _():
        start_gather(0, 0)

    @pl.when(i + 1 < n)
    def _():
        start_gather(i + 1, 1 - slot)

    pltpu.make_async_copy(ys_hbm.at[pl.ds(0, 2 * half), :], yg.at[slot], sem.at[slot]).wait()
    rw = rw_ref[...]
    buf = yg.at[slot]
    moe = (_from_packed_token_major(buf, 0, tm) * rw[:, 0:1]
           + _from_packed_token_major(buf, tm, tm) * rw[:, 1:2])
    o_ref[...] = _layer_norm_rows(ALPHA * x1_ref[...] + moe, g2_ref[...], b2_ref[...])


def _combine(slot_flat, ys, x1, rw, ln2_g, ln2_b):
    T = x1.shape[0]
    tm = TM_COMB
    grid_spec = pltpu.PrefetchScalarGridSpec(
        num_scalar_prefetch=1,
        grid=(T // tm,),
        in_specs=[
            pl.BlockSpec(memory_space=pl.ANY),
            pl.BlockSpec((tm, D_MODEL), lambda i, *_: (i, 0)),
            pl.BlockSpec((tm, LANES), lambda i, *_: (i, 0)),
            pl.BlockSpec((1, D_MODEL), lambda i, *_: (0, 0)),
            pl.BlockSpec((1, D_MODEL), lambda i, *_: (0, 0)),
        ],
        out_specs=pl.BlockSpec((tm, D_MODEL), lambda i, *_: (i, 0)),
        scratch_shapes=[
            pltpu.VMEM((2, 2 * tm * PACKED_CHUNKS, LANES), U32),
            pltpu.SemaphoreType.DMA((2,)),
        ],
    )
    return pl.pallas_call(
        functools.partial(_combine_kernel, tm=tm),
        grid_spec=grid_spec,
        out_shape=jax.ShapeDtypeStruct((T, D_MODEL), F32),
        compiler_params=_params("arbitrary"),
        name="moe_combine_ln",
    )(slot_flat, ys, x1, rw, ln2_g.reshape(1, D_MODEL).astype(F32), ln2_b.reshape(1, D_MODEL).astype(F32))


def _dispatch_plan(counts, pk_flat, n_tokens):
    rows = MOE_ROWS
    experts = jnp.arange(N_EXPERTS, dtype=jnp.int32)
    padded = (counts + rows - 1) // rows * rows
    pad_end = jnp.cumsum(padded).astype(jnp.int32)
    pad_start = (pad_end - padded).astype(jnp.int32)
    n_blk = 2 * n_tokens // rows + N_EXPERTS
    blk_start = jnp.arange(n_blk, dtype=jnp.int32) * rows
    blk_e = jnp.minimum(jnp.sum(blk_start[:, None] >= pad_end[None, :], axis=1), N_EXPERTS - 1).astype(jnp.int32)
    n_used = (pad_end[-1:] // rows).astype(jnp.int32)
    new_run = jnp.concatenate([jnp.ones((1,), jnp.int32), (blk_e[1:] != blk_e[:-1]).astype(jnp.int32)])
    run_idx = (jnp.cumsum(new_run) - 1).astype(jnp.int32)
    later_active = (experts[None, :] > experts[:, None]) & (counts[None, :] > 0)
    next_active = jnp.min(jnp.where(later_active, experts[None, :], N_EXPERTS), axis=1)
    next_active = jnp.where(next_active == N_EXPERTS, -1, next_active).astype(jnp.int32)
    next_e = next_active[blk_e]
    e_flat = pk_flat >> RANK_BITS
    start_of = jnp.sum(jnp.where(e_flat[:, None] == experts[None, :], pad_start[None, :], 0), axis=1)
    slot_flat = (start_of + (pk_flat & (RANK_SPAN - 1))).astype(jnp.int32)
    return slot_flat, pad_start, pad_end, blk_e, n_used, run_idx, next_e, n_blk


def kernel(x, w_in, conv_w, conv_b, lru_wa, lru_ba, lru_wx, lru_bx, lru_lambda, attn_norm_g, lru_norm_g,
           w_out, ln1_g, ln1_b, router_grp_w, router_grp_b, router_exp_w, router_exp_b, w1, w3, w2,
           ln2_g, ln2_b):
    B, S, D = x.shape
    assert D == D_MODEL and S % DILATED_PATTERNS[-1][0] == 0 and w_in.shape[0] == 1
    T = B * S
    x2d = x.reshape(T, D)

    qkv, rg = _in_proj(x2d, w_in[0].astype(BF16), B, S)
    attn = _attention(qkv, B, S)
    yrec = _recurrent(rg, conv_w[0], conv_b[0], lru_wa[0], lru_ba[0], lru_wx[0], lru_bx[0], lru_lambda[0],
                      lru_norm_g[0], B, S)

    n_r = N_GROUPS + N_EXPERTS
    w_router = jnp.zeros((D, LANES), F32).at[:, :n_r].set(
        jnp.concatenate([router_grp_w[0], router_exp_w[0]], axis=-1).astype(F32))
    b_router = jnp.zeros((1, LANES), F32).at[0, :n_r].set(
        jnp.concatenate([router_grp_b[0], router_exp_b[0]], axis=-1).astype(F32))
    x1, x1p, pk, rw, cnt = _out_router(attn, yrec, x2d, attn_norm_g[0], w_out[0].astype(BF16),
                                       ln1_g[0], ln1_b[0], w_router, b_router)

    pk_flat = pk[:, :2].reshape(-1)
    slot_flat, pad_start, pad_end, blk_e, n_used, run_idx, next_e, n_blk = _dispatch_plan(
        cnt[0, :N_EXPERTS], pk_flat, T)
    xs = _dispatch(slot_flat, pad_start, pad_end, n_used, x1p, n_blk)
    ys = _experts(blk_e, n_used, run_idx, next_e, xs, w1[0], w3[0], w2[0])
    out = _combine(slot_flat, ys, x1, rw, ln2_g[0], ln2_b[0])
    return out.reshape(B, S, D)
```

```python
import functools

import jax
import jax.numpy as jnp
import numpy as np
from jax import lax
from jax.experimental import pallas as pl
from jax.experimental.pallas import tpu as pltpu

F32 = jnp.float32
BF16 = jnp.bfloat16

D_MODEL = 2048
D_ATTN = 1024
D_LRU = 1024
HEAD_DIM = 64
N_HEADS = 16
LANES = 128
N_PLANES = D_ATTN // LANES
DILATED_PATTERNS = ((128, 1), (512, 4), (2048, 16))
SUB_WINDOW = 128
LRU_BLOCKS = 16
LRU_BLOCK_DIM = 64
CONV_WIDTH = 4
RG_C = 8.0
N_GROUPS = 4
EXPERTS_PER_GROUP = 8
N_EXPERTS = 32
D_EXPERT = 512
ALPHA = 2.0 ** 0.25
LN_EPS = 1e-5
RMS_EPS = 1e-6
MASKED = -1e30
LOG2E = 1.4426950408889634

VMEM_LIMIT = 56 * 1024 * 1024

TM_PROJ = 256
TS_LRU = 256
TM_OUT = 256
MOE_ROWS = 256
TM_COMB = 256
TM_DISP = 2048
CHUNKS = D_MODEL // LANES
PACKED_CHUNKS = CHUNKS // 2
RANK_BITS = 16
RANK_SPAN = 1 << RANK_BITS
ATTN_UNROLL = 32


def _params(*sem):
    return pltpu.CompilerParams(dimension_semantics=sem, vmem_limit_bytes=VMEM_LIMIT)


def _in_proj_kernel(x_ref, xc_ref, w_ref, qkv_ref, rg_ref):
    for c in range(5):
        lhs = (xc_ref if c < 3 else x_ref)[...].astype(BF16)
        acc = jnp.dot(lhs, w_ref[:, c * D_ATTN:(c + 1) * D_ATTN], preferred_element_type=F32)
        if c == 0:
            acc = acc * (HEAD_DIM ** -0.5 * LOG2E)
        out_ref, first = (qkv_ref, c * N_PLANES) if c < 3 else (rg_ref, (c - 3) * N_PLANES)
        for p in range(N_PLANES):
            out_ref[first + p] = acc[:, p * LANES:(p + 1) * LANES]


def _in_proj(x, w_in_b):
    batch, seq, _ = x.shape
    T = batch * seq
    tm = TM_PROJ
    per_class = seq // CLASSES
    assert seq % tm == 0 and per_class % tm == 0
    tiles = seq // tm
    cls_tiles = per_class // tm
    x2d = x.reshape(T, D_MODEL)
    x_cls = x.reshape(batch, per_class, CLASSES * D_MODEL)

    def cls_tile(i):
        j = i % tiles
        return i // tiles, j // cls_tiles, j % cls_tiles

    return pl.pallas_call(
        _in_proj_kernel,
        grid=(T // tm,),
        in_specs=[
            pl.BlockSpec((tm, D_MODEL), lambda i: (i, 0)),
            pl.BlockSpec((None, tm, D_MODEL), lambda i: (cls_tile(i)[0], cls_tile(i)[2], cls_tile(i)[1])),
            pl.BlockSpec((D_MODEL, 5 * D_ATTN), lambda i: (0, 0), pipeline_mode=pl.Buffered(1)),
        ],
        out_specs=[
            pl.BlockSpec((3 * N_PLANES, None, None, tm, LANES),
                         lambda i: (0, cls_tile(i)[0], cls_tile(i)[1], cls_tile(i)[2], 0)),
            pl.BlockSpec((2 * N_PLANES, tm, LANES), lambda i: (0, i, 0)),
        ],
        out_shape=[
            jax.ShapeDtypeStruct((3 * N_PLANES, batch, CLASSES, per_class, LANES), F32),
            jax.ShapeDtypeStruct((2 * N_PLANES, T, LANES), F32),
        ],
        compiler_params=_params("parallel"),
        name="in_proj",
    )(x2d, x_cls, w_in_b)


CLASSES = 16


def _attn_kernel(slopes_ref, qs, ks, vs, ndf_ref, ndh_ref, o_ref,
                 os_, m_s, l_s, acc_s, bias_f, bias_h, *, seq):
    hp = pl.program_id(1)
    slope_a = slopes_ref[2 * hp] * LOG2E
    slope_b = slopes_ref[2 * hp + 1] * LOG2E
    w = SUB_WINDOW
    cl = seq // CLASSES
    n_pat = len(DILATED_PATTERNS)

    def gather(ref, starts, size):
        return jnp.concatenate([ref[pl.ds(s, size), :] for s in starts], axis=0)

    def scatter(ref, starts, size, val):
        for c, s in enumerate(starts):
            ref[pl.ds(s, size), :] = val[c * size:(c + 1) * size, :]

    for t in range(2 * n_pat):
        bias_f[t, 0:w, :] = slope_a * ndf_ref[t]
        bias_f[t, w:2 * w, :] = slope_b * ndf_ref[t]
    for t in range(n_pat):
        bias_h[t, 0:w, :] = slope_a * ndh_ref[t]
        bias_h[t, w:2 * w, :] = slope_b * ndh_ref[t]

    def block(c, n, d, pi, first, last, keys):
        qc = w * d // CLASSES
        aligned = lambda x: x if isinstance(x, int) else pl.multiple_of(x, 8)
        bases = [(c + d * j) * cl for j in range(CLASSES // d)]
        qstarts = [aligned(b + n * qc) for b in bases]
        if keys == "own":
            kstarts, kc, bias = qstarts, qc, bias_h[pi]
        elif keys == "prev+own":
            kstarts, kc, bias = [aligned(b + (n - 1) * qc) for b in bases], 2 * qc, bias_f[pi]
        else:
            kfirst = jnp.maximum(n - 1, 0) * qc
            kstarts, kc = [aligned(b + kfirst) for b in bases], 2 * qc
            bias = bias_f[jnp.where(n == 0, pi + n_pat, pi)]
        is_a = lax.broadcasted_iota(jnp.int32, (w, LANES), 1) < HEAD_DIM
        q = gather(qs, qstarts, qc)
        k = gather(ks, kstarts, kc).astype(BF16)
        v = gather(vs, kstarts, kc).astype(BF16)
        zero = jnp.zeros_like(q)
        q2 = jnp.concatenate([jnp.where(is_a, q, zero), jnp.where(is_a, zero, q)], axis=0).astype(BF16)
        s = lax.dot_general(q2, k, (((1,), (1,)), ((), ())), preferred_element_type=F32) + bias
        m = jnp.max(s, axis=-1, keepdims=True)
        p = jnp.exp2(s - m)
        l = jnp.sum(p, axis=-1, keepdims=True)
        o = jnp.dot(p.astype(BF16), v, preferred_element_type=F32)
        m_c = jnp.where(is_a, m[:w], m[w:])
        l_c = jnp.where(is_a, l[:w], l[w:])
        o_c = jnp.where(is_a, o[:w], o[w:])
        if first:
            scatter(m_s, qstarts, qc, m_c)
            scatter(l_s, qstarts, qc, l_c)
            scatter(acc_s, qstarts, qc, o_c)
            return
        m_o = gather(m_s, qstarts, qc)
        m_n = jnp.maximum(m_o, m_c)
        e_o = jnp.exp2(m_o - m_n)
        e_c = jnp.exp2(m_c - m_n)
        l_n = gather(l_s, qstarts, qc) * e_o + l_c * e_c
        a_n = gather(acc_s, qstarts, qc) * e_o + o_c * e_c
        if last:
            scatter(os_, qstarts, qc, a_n / l_n)
        else:
            scatter(m_s, qstarts, qc, m_n)
            scatter(l_s, qstarts, qc, l_n)
            scatter(acc_s, qstarts, qc, a_n)

    for pi, (window, d) in enumerate(DILATED_PATTERNS):
        assert window // d == w and CLASSES % d == 0
        first, last = pi == 0, pi == n_pat - 1
        nb = seq // (w * d)

        if nb >= 4:
            def any_block(i, carry, d=d, pi=pi, first=first, last=last):
                block(i % d, i // d, d, pi, first, last, "any")
                return carry
            lax.fori_loop(0, d * nb, any_block, 0, unroll=ATTN_UNROLL)
        else:
            def class_blocks(c, carry, d=d, pi=pi, first=first, last=last, nb=nb):
                block(c, 0, d, pi, first, last, "own")
                for n in range(1, nb):
                    block(c, n, d, pi, first, last, "prev+own")
                return carry
            lax.fori_loop(0, d, class_blocks, 0, unroll=max(ATTN_UNROLL // nb, 1))

    for r in range(CLASSES):
        o_ref[pl.ds(r, cl, stride=CLASSES), :] = os_[pl.ds(r * cl, cl), :]


def _neg_distance_tables():
    w = SUB_WINDOW
    prev, nxt, own = [], [], []
    for _, d in DILATED_PATTERNS:
        n_cls = CLASSES // d
        qc = w // n_cls
        def pos(chunk_rows):
            j, l = np.divmod(np.arange(n_cls * chunk_rows), chunk_rows)
            return n_cls * l + j
        sq, sk2, sk1 = pos(qc)[:, None], pos(2 * qc)[None, :], pos(qc)[None, :]
        for out, dist in ((prev, sq + w - sk2), (nxt, sq - sk2), (own, sq - sk1)):
            valid = (dist >= 0) & (dist <= w)
            out.append(np.where(valid, -(dist * d).astype(np.float32), np.float32(MASKED)))
    return jnp.asarray(np.stack(prev + nxt), dtype=F32), jnp.asarray(np.stack(own), dtype=F32)


def _attention(qkv, batch, seq):
    qkv4 = qkv.reshape(3 * N_PLANES, batch, seq, LANES)
    slopes = jnp.exp2(-8.0 * jnp.arange(1, N_HEADS + 1, dtype=F32) / N_HEADS)
    nd_full, nd_head = _neg_distance_tables()
    n_pat = len(DILATED_PATTERNS)
    plane = lambda off: pl.BlockSpec((None, None, seq, LANES), lambda b, h: (off + h, b, 0, 0))
    return pl.pallas_call(
        functools.partial(_attn_kernel, seq=seq),
        grid=(batch, N_PLANES),
        in_specs=[
            pl.BlockSpec(memory_space=pltpu.SMEM),
            plane(0), plane(N_PLANES), plane(2 * N_PLANES),
            pl.BlockSpec((2 * n_pat, SUB_WINDOW, 2 * SUB_WINDOW), lambda b, h: (0, 0, 0)),
            pl.BlockSpec((n_pat, SUB_WINDOW, SUB_WINDOW), lambda b, h: (0, 0, 0)),
        ],
        out_specs=pl.BlockSpec((None, None, seq, LANES), lambda b, h: (h, b, 0, 0)),
        scratch_shapes=[pltpu.VMEM((seq, LANES), F32)] * 4 + [
            pltpu.VMEM((2 * n_pat, 2 * SUB_WINDOW, 2 * SUB_WINDOW), F32),
            pltpu.VMEM((n_pat, 2 * SUB_WINDOW, SUB_WINDOW), F32),
        ],
        out_shape=jax.ShapeDtypeStruct((N_PLANES, batch, seq, LANES), F32),
        compiler_params=_params("parallel", "parallel"),
        name="dilated_attention",
    )(slopes, qkv4, qkv4, qkv4, nd_full, nd_head)


def _gelu_tanh(x):
    c = np.float32(np.sqrt(2.0 / np.pi))
    return 0.5 * x * (1.0 + jnp.tanh(c * (x + 0.044715 * (x * x * x))))


def _log1p(x):
    u = 1.0 + x
    return jnp.where(u == 1.0, x, jnp.log(u) * x / (u - 1.0))


def _softplus(z):
    return jnp.maximum(z, 0.0) + _log1p(jnp.exp(-jnp.abs(z)))


PHASES = 8


def _lru_kernel(xr_ref, xg_ref, cw_ref, cb_ref, wbd_ref, ba_ref, bx_ref, lam_ref, g_ref, y_ref,
                xc, hc, *, ts):
    i = pl.program_id(1)
    G = ts // PHASES
    first_group = lax.broadcasted_iota(jnp.int32, (G, LANES), 0) == 0
    sub = lax.broadcasted_iota(jnp.int32, (G, LANES), 0)

    @pl.when(i == 0)
    def _():
        xc[...] = jnp.zeros_like(xc)
        hc[...] = jnp.zeros_like(hc)

    def prev_group(cur, carry_row):
        return jnp.where(first_group, carry_row, pltpu.roll(cur, 1, 0))

    conv = []
    for l in range(N_PLANES):
        x = [xr_ref[l, pl.ds(s, G, stride=PHASES), :] for s in range(PHASES)]
        back = {s: prev_group(x[s], xc[l, s:s + 1, :]) for s in range(PHASES - CONV_WIDTH + 1, PHASES)}
        phases = []
        for s in range(PHASES):
            acc = jnp.broadcast_to(cb_ref[l:l + 1, :], (G, LANES))
            for j in range(CONV_WIDTH):
                q = s - (CONV_WIDTH - 1) + j
                acc = acc + (x[q] if q >= 0 else back[q + PHASES]) * cw_ref[j, l:l + 1, :]
            phases.append(acc)
        conv.append(jnp.concatenate(phases, axis=0))
        xc[l] = xr_ref[l, ts - PHASES:ts, :]

    gates_r, gates_i = [], []
    for j in range(N_PLANES // 2):
        yb = jnp.concatenate([conv[2 * j], conv[2 * j + 1]], axis=-1).astype(BF16)
        g = jnp.dot(yb, wbd_ref[j], preferred_element_type=F32)
        gates_r += [g[:, 0:LANES], g[:, LANES:2 * LANES]]
        gates_i += [g[:, 2 * LANES:3 * LANES], g[:, 3 * LANES:4 * LANES]]

    rec, sq = [], None
    for l in range(N_PLANES):
        y = conv[l]
        r = jax.nn.sigmoid(gates_r[l] + ba_ref[l:l + 1, :])
        ig = jax.nn.sigmoid(gates_i[l] + bx_ref[l:l + 1, :])
        log_a = (-RG_C * r) * _softplus(-lam_ref[l:l + 1, :])
        a = jnp.exp(log_a)
        z = 1.0 - a * a
        u = jnp.where(z > 0.0, z * lax.rsqrt(z), 0.0) * (ig * y)

        piece = lambda v, s: v[s * G:(s + 1) * G, :]
        pa, pb = [piece(a, 0)], [piece(u, 0)]
        for s in range(1, PHASES):
            pa.append(piece(a, s) * pa[-1])
            pb.append(piece(a, s) * pb[-1] + piece(u, s))
        ga, gb = pa[-1], pb[-1]
        sh = 1
        while sh < G:
            take = sub >= sh
            gb = jnp.where(take, ga * pltpu.roll(gb, sh, 0) + gb, gb)
            ga = jnp.where(take, ga * pltpu.roll(ga, sh, 0), ga)
            sh *= 2
        h0 = hc[l, 0:1, :]
        h_end = ga * h0 + gb
        h_in = prev_group(h_end, h0)
        hc[l] = jnp.broadcast_to(h_end[G - 1:G, :], (PHASES, LANES))
        h = jnp.concatenate([pa[s] * h_in + pb[s] for s in range(PHASES)], axis=0)

        xg = jnp.concatenate([xg_ref[l, pl.ds(s, G, stride=PHASES), :] for s in range(PHASES)], axis=0)
        rl = h * _gelu_tanh(xg)
        rec.append(rl)
        part = jnp.sum(rl * rl, axis=-1, keepdims=True)
        sq = part if sq is None else sq + part

    scale = lax.rsqrt(sq * (1.0 / D_LRU) + RMS_EPS)
    for l in range(N_PLANES):
        out = rec[l] * scale * g_ref[l:l + 1, :]
        for s in range(PHASES):
            y_ref[l, pl.ds(s, G, stride=PHASES), :] = out[s * G:(s + 1) * G, :]


def _block_diag_gates(wa, wx):
    def bd(wm):
        wm = wm.reshape(4, 4, LRU_BLOCK_DIM, LRU_BLOCK_DIM)
        eye = jnp.eye(4, dtype=wm.dtype)
        full = jnp.einsum("gide,ij->gidje", wm, eye)
        return full.reshape(4, 4 * LRU_BLOCK_DIM, 4 * LRU_BLOCK_DIM)
    return jnp.concatenate([bd(wa), bd(wx)], axis=-1).astype(BF16)


def _recurrent(rg, conv_w, conv_b, lru_wa, lru_ba, lru_wx, lru_bx, lru_lambda, lru_norm_g, batch, seq):
    ts = TS_LRU
    rg5 = rg.reshape(2, N_PLANES, batch, seq, LANES)
    wbd = _block_diag_gates(lru_wa, lru_wx)
    planes = lambda a: a.reshape(N_PLANES, LANES).astype(F32)
    vec = pl.BlockSpec((N_PLANES, LANES), lambda b, i: (0, 0))
    return pl.pallas_call(
        functools.partial(_lru_kernel, ts=ts),
        grid=(batch, seq // ts),
        in_specs=[
            pl.BlockSpec((None, N_PLANES, None, ts, LANES), lambda b, i: (0, 0, b, i, 0)),
            pl.BlockSpec((None, N_PLANES, None, ts, LANES), lambda b, i: (1, 0, b, i, 0)),
            pl.BlockSpec((CONV_WIDTH, N_PLANES, LANES), lambda b, i: (0, 0, 0)),
            vec,
            pl.BlockSpec((4, 4 * LRU_BLOCK_DIM, 8 * LRU_BLOCK_DIM), lambda b, i: (0, 0, 0)),
            vec, vec, vec, vec,
        ],
        out_specs=pl.BlockSpec((N_PLANES, None, ts, LANES), lambda b, i: (0, b, i, 0)),
        out_shape=jax.ShapeDtypeStruct((N_PLANES, batch, seq, LANES), F32),
        scratch_shapes=[
            pltpu.VMEM((N_PLANES, PHASES, LANES), F32),
            pltpu.VMEM((N_PLANES, PHASES, LANES), F32),
        ],
        compiler_params=_params("parallel", "arbitrary"),
        name="conv_rglru",
    )(rg5, rg5, conv_w.astype(F32).reshape(CONV_WIDTH, N_PLANES, LANES), planes(conv_b), wbd, planes(lru_ba),
      planes(lru_bx), planes(lru_lambda), planes(lru_norm_g))


def _layer_norm_rows(z, g, b):
    mu = jnp.mean(z, axis=-1, keepdims=True)
    zc = z - mu
    var = jnp.mean(zc * zc, axis=-1, keepdims=True)
    return zc * lax.rsqrt(var + LN_EPS) * g + b


U32 = jnp.uint32
HI_HALF = np.uint32(0xFFFF0000)


def _to_packed_token_major(ref, val, first_row, n_rows):
    bits = lambda a: lax.bitcast_convert_type(a.astype(BF16).astype(F32), U32)
    half = PACKED_CHUNKS * LANES
    for c in range(PACKED_CHUNKS):
        lo = bits(val[:, c * LANES:(c + 1) * LANES]) >> 16
        hi = bits(val[:, half + c * LANES:half + (c + 1) * LANES]) & HI_HALF
        ref[pl.ds(first_row * PACKED_CHUNKS + c, n_rows, stride=PACKED_CHUNKS), :] = lo | hi


def _from_packed_token_major(ref, first_row, n_rows):
    words = [ref[pl.ds(first_row * PACKED_CHUNKS + c, n_rows, stride=PACKED_CHUNKS), :]
             for c in range(PACKED_CHUNKS)]
    lo = [lax.bitcast_convert_type(wd << 16, F32) for wd in words]
    hi = [lax.bitcast_convert_type(wd & HI_HALF, F32) for wd in words]
    return jnp.concatenate(lo + hi, axis=-1)


def _out_router_kernel(*refs, tm):
    *io_refs, cnt_s, z_a, z_b = refs
    i = pl.program_id(0)

    @pl.when(i == 0)
    def _():
        cnt_s[...] = jnp.zeros_like(cnt_s)
        z_b[...] = jnp.zeros_like(z_b)

    @pl.when(i % 2 == 0)
    def _():
        _out_router_step(*io_refs, cnt_s, z_b, z_a, tm=tm)

    @pl.when(i % 2 == 1)
    def _():
        _out_router_step(*io_refs, cnt_s, z_a, z_b, tm=tm)


def _out_router_step(attn_ref, yrec_ref, x_ref, ga_ref, wo_ref, g1_ref, b1_ref, wr_ref, br_ref,
                     x1_ref, x1p_ref, pk_ref, rw_ref, cnt_ref, cnt_s, z_in, z_out, *, tm):
    i = pl.program_id(0)
    attn = jnp.concatenate([attn_ref[p] for p in range(N_PLANES)], axis=-1)
    ms = jnp.mean(attn * attn, axis=-1, keepdims=True)
    ya = (attn * lax.rsqrt(ms + RMS_EPS) * ga_ref[...]).astype(BF16)
    yr = jnp.concatenate([yrec_ref[p] for p in range(N_PLANES)], axis=-1).astype(BF16)
    n_col = 8
    cw = D_MODEL // n_col

    def project(j):
        cols = slice(j * cw, (j + 1) * cw)
        mix = jnp.dot(ya, wo_ref[0:D_ATTN, cols], preferred_element_type=F32)
        mix = mix + jnp.dot(yr, wo_ref[D_ATTN:, cols], preferred_element_type=F32)
        z_out[:, cols] = ALPHA * x_ref[:, cols] + mix

    def normalize(r0, n_rows):
        rows = slice(r0, r0 + n_rows)
        x1 = _layer_norm_rows(z_in[rows, :], g1_ref[...], b1_ref[...])
        x1_ref[rows, :] = x1
        _to_packed_token_major(x1p_ref, x1, r0, n_rows)
        return x1.astype(BF16)

    n_row = 4
    rh = tm // n_row
    parts = []
    for k in range(n_row):
        parts.append(normalize(k * rh, rh))
        project(k)

    xb = jnp.concatenate(parts, axis=0)
    logits = jnp.dot(xb, wr_ref[...].astype(BF16), preferred_element_type=F32) + br_ref[...]
    project(4)
    project(5)
    _route(logits, pk_ref, rw_ref, cnt_ref, cnt_s, i, tm)
    project(6)
    project(7)


def _route(logits, pk_ref, rw_ref, cnt_ref, cnt_s, i, tm):
    lane = lax.broadcasted_iota(jnp.int32, (tm, LANES), 1)
    big = jnp.int32(LANES)
    first_true = lambda c: jnp.min(jnp.where(c, lane, big), axis=-1, keepdims=True)

    in_g = lane < N_GROUPS
    gl = jnp.where(in_g, logits, MASKED)
    gmax = jnp.max(gl, axis=-1, keepdims=True)
    g_idx = first_true(gl == gmax)
    gsum = jnp.sum(jnp.where(in_g, jnp.exp(gl - gmax), 0.0), axis=-1, keepdims=True)
    g_gate = 1.0 / gsum

    lo = N_GROUPS + EXPERTS_PER_GROUP * g_idx
    in_e = (lane >= lo) & (lane < lo + EXPERTS_PER_GROUP)
    el = jnp.where(in_e, logits, MASKED)
    emax = jnp.max(el, axis=-1, keepdims=True)
    ee = jnp.where(in_e, jnp.exp(el - emax), 0.0)
    esum = jnp.sum(ee, axis=-1, keepdims=True)
    i1 = first_true(el == emax)
    rest = jnp.where(in_e & (lane != i1), ee, -1.0)
    e2max = jnp.max(rest, axis=-1, keepdims=True)
    i2 = first_true(rest == e2max)
    v1 = 1.0 / esum
    v2 = e2max / esum
    den = v1 + v2
    w1 = g_gate * v1 / den
    w2 = g_gate * v2 / den
    e1 = i1 - N_GROUPS
    e2 = i2 - N_GROUPS
    rw_ref[...] = jnp.where(lane == 0, w1, jnp.where(lane == 1, w2, 0.0))

    hot1 = lane == e1
    hot2 = lane == e2
    both = jnp.where((hot1 | hot2) & (i > 0), 1.0, 0.0)
    earlier = (lax.broadcasted_iota(jnp.int32, (tm, tm), 1) < lax.broadcasted_iota(jnp.int32, (tm, tm), 0))
    before = jnp.dot(earlier.astype(BF16), both.astype(BF16), preferred_element_type=F32) + cnt_s[0:1, :]
    r1 = jnp.sum(jnp.where(hot1, before, 0.0), axis=-1, keepdims=True).astype(jnp.int32)
    r2 = jnp.sum(jnp.where(hot2, before, 0.0), axis=-1, keepdims=True).astype(jnp.int32)
    pk_ref[...] = jnp.where(lane == 0, e1 * RANK_SPAN + r1, jnp.where(lane == 1, e2 * RANK_SPAN + r2, 0))
    cnt_s[...] = cnt_s[...] + jnp.sum(both, axis=0, keepdims=True)
    cnt_ref[...] = cnt_s[...].astype(jnp.int32)


def _out_router(attn, yrec, x2d, attn_norm_g, w_out_b, ln1_g, ln1_b, w_router, b_router):
    T = x2d.shape[0]
    tm = TM_OUT
    assert 2 * T <= RANK_SPAN
    attn3 = attn.reshape(N_PLANES, T, LANES)
    yrec3 = yrec.reshape(N_PLANES, T, LANES)
    const = lambda shape: pl.BlockSpec(shape, lambda i: (0,) * len(shape))
    n = T // tm
    stage1 = lambda i: jnp.minimum(i, n - 1)
    stage2 = lambda i: jnp.maximum(i - 1, 0)
    return pl.pallas_call(
        functools.partial(_out_router_kernel, tm=tm),
        grid=(n + 1,),
        in_specs=[
            pl.BlockSpec((N_PLANES, tm, LANES), lambda i: (0, stage1(i), 0)),
            pl.BlockSpec((N_PLANES, tm, LANES), lambda i: (0, stage1(i), 0)),
            pl.BlockSpec((tm, D_MODEL), lambda i: (stage1(i), 0)),
            const((1, D_ATTN)),
            pl.BlockSpec((D_MODEL, D_MODEL), lambda i: (0, 0), pipeline_mode=pl.Buffered(1)),
            const((1, D_MODEL)), const((1, D_MODEL)),
            const((D_MODEL, LANES)), const((1, LANES)),
        ],
        out_specs=[
            pl.BlockSpec((tm, D_MODEL), lambda i: (stage2(i), 0)),
            pl.BlockSpec((tm * PACKED_CHUNKS, LANES), lambda i: (stage2(i), 0)),
            pl.BlockSpec((tm, LANES), lambda i: (stage2(i), 0)),
            pl.BlockSpec((tm, LANES), lambda i: (stage2(i), 0)),
            const((8, LANES)),
        ],
        out_shape=[
            jax.ShapeDtypeStruct((T, D_MODEL), F32),
            jax.ShapeDtypeStruct((T * PACKED_CHUNKS, LANES), U32),
            jax.ShapeDtypeStruct((T, LANES), jnp.int32),
            jax.ShapeDtypeStruct((T, LANES), F32),
            jax.ShapeDtypeStruct((8, LANES), jnp.int32),
        ],
        scratch_shapes=[pltpu.VMEM((8, LANES), F32), pltpu.VMEM((tm, D_MODEL), F32),
                        pltpu.VMEM((tm, D_MODEL), F32)],
        compiler_params=_params("arbitrary"),
        name="out_proj_router",
    )(attn3, yrec3, x2d, attn_norm_g.reshape(1, D_ATTN).astype(F32), w_out_b,
      ln1_g.reshape(1, D_MODEL).astype(F32), ln1_b.reshape(1, D_MODEL).astype(F32), w_router, b_router)


def _dispatch_kernel(slot_ref, pstart_ref, pend_ref, nused_ref, x1p_ref, xs_hbm, zbuf, sem, zsem,
                     *, tm, rows, n_blk):
    i = pl.program_id(0)
    slab = PACKED_CHUNKS
    blk = rows * slab

    @pl.when(i == 0)
    def _():
        zbuf[...] = jnp.zeros_like(zbuf)
        n_used = nused_ref[0]

        def zero_copy(b):
            return pltpu.make_async_copy(zbuf, xs_hbm.at[pl.ds(pl.multiple_of(b * blk, blk), blk), :], zsem)

        def for_each_zeroed_block(fn):
            for e in range(N_EXPERTS):
                @pl.when(pend_ref[e] > pstart_ref[e])
                def _(e=e):
                    fn(pend_ref[e] // rows - 1)

            def tail(b, c):
                fn(b)
                return c
            lax.fori_loop(n_used, n_blk, tail, 0)

        for_each_zeroed_block(lambda b: zero_copy(b).start())
        for_each_zeroed_block(lambda b: zero_copy(b).wait())

    group = 8

    def push(g, c):
        for t in range(group):
            src = x1p_ref.at[pl.ds(pl.multiple_of(g * (group * slab), group * slab) + t * slab, slab), :]
            for k in range(2):
                row = slot_ref[i * (2 * tm) + g * (2 * group) + 2 * t + k] * slab
                dst = xs_hbm.at[pl.ds(pl.multiple_of(row, slab), slab), :]
                pltpu.make_async_copy(src, dst, sem).start(priority=k)
        return c
    lax.fori_loop(0, tm // group, push, 0)
    for _ in range(2):
        pltpu.make_async_copy(x1p_ref, xs_hbm.at[pl.ds(0, tm * slab), :], sem).wait()


def _dispatch(slot_flat, pad_start, pad_end, n_used, x1p, n_blk):
    tm, rows, slab = TM_DISP, MOE_ROWS, PACKED_CHUNKS
    n_tok = x1p.shape[0] // slab
    grid_spec = pltpu.PrefetchScalarGridSpec(
        num_scalar_prefetch=4,
        grid=(n_tok // tm,),
        in_specs=[pl.BlockSpec((tm * slab, LANES), lambda i, *_: (i, 0))],
        out_specs=pl.BlockSpec(memory_space=pl.ANY),
        scratch_shapes=[
            pltpu.VMEM((rows * slab, LANES), U32),
            pltpu.SemaphoreType.DMA(()),
            pltpu.SemaphoreType.DMA(()),
        ],
    )
    return pl.pallas_call(
        functools.partial(_dispatch_kernel, tm=tm, rows=rows, n_blk=n_blk),
        grid_spec=grid_spec,
        out_shape=jax.ShapeDtypeStruct((n_blk * rows * slab, LANES), U32),
        compiler_params=_params("arbitrary"),
        name="moe_dispatch",
    )(slot_flat, pad_start, pad_end, n_used, x1p)


def _expert_kernel(blk_e_ref, n_used_ref, run_ref, next_e_ref, xs_ref, w1_hbm, w3_hbm, w2_hbm, y_ref,
                   wf1, wf3, wf2, wsem, w1b, w3b, w2b, *, rows):
    i = pl.program_id(0)
    n_used = n_used_ref[0]
    e = blk_e_ref[i]
    slot = run_ref[i] % 2
    new_run = jnp.logical_or(i == 0, e != blk_e_ref[jnp.maximum(i - 1, 0)])

    def weight_copies(expert, slot_):
        return [pltpu.make_async_copy(src.at[expert], dst.at[slot_], wsem.at[slot_])
                for src, dst in ((w1_hbm, wf1), (w3_hbm, wf3), (w2_hbm, wf2))]

    @pl.when(i == 0)
    def _():
        for cp in weight_copies(e, 0):
            cp.start()

    @pl.when(jnp.logical_and(i < n_used, new_run))
    def _():
        for cp in weight_copies(e, slot):
            cp.wait()
        nxt = next_e_ref[i]

        @pl.when(nxt >= 0)
        def _():
            for cp in weight_copies(nxt, 1 - slot):
                cp.start(priority=1)

        w1b[...] = wf1[slot].astype(BF16)
        w3b[...] = wf3[slot].astype(BF16)
        w2b[...] = wf2[slot].astype(BF16)

    @pl.when(i < n_used)
    def _():
        xb = _from_packed_token_major(xs_ref, 0, rows).astype(BF16)
        h1 = jnp.dot(xb, w1b[...], preferred_element_type=F32)
        h3 = jnp.dot(xb, w3b[...], preferred_element_type=F32)
        h = (jax.nn.silu(h1) * h3).astype(BF16)
        _to_packed_token_major(y_ref, jnp.dot(h, w2b[...], preferred_element_type=F32), 0, rows)

    @pl.when(i >= n_used)
    def _():
        y_ref[...] = jnp.zeros_like(y_ref)


def _experts(blk_e, n_used, run_idx, next_e, xs, w1, w3, w2):
    rows = MOE_ROWS
    n_blk = blk_e.shape[0]
    hbm = pl.BlockSpec(memory_space=pl.ANY)
    grid_spec = pltpu.PrefetchScalarGridSpec(
        num_scalar_prefetch=4,
        grid=(n_blk,),
        in_specs=[
            pl.BlockSpec((rows * PACKED_CHUNKS, LANES), lambda i, be, nu, *_: (jnp.minimum(i, nu[0] - 1), 0)),
            hbm, hbm, hbm,
        ],
        out_specs=pl.BlockSpec((rows * PACKED_CHUNKS, LANES), lambda i, *_: (i, 0)),
        scratch_shapes=[
            pltpu.VMEM((2, D_MODEL, D_EXPERT), F32),
            pltpu.VMEM((2, D_MODEL, D_EXPERT), F32),
            pltpu.VMEM((2, D_EXPERT, D_MODEL), F32),
            pltpu.SemaphoreType.DMA((2,)),
            pltpu.VMEM((D_MODEL, D_EXPERT), BF16),
            pltpu.VMEM((D_MODEL, D_EXPERT), BF16),
            pltpu.VMEM((D_EXPERT, D_MODEL), BF16),
        ],
    )
    return pl.pallas_call(
        functools.partial(_expert_kernel, rows=rows),
        grid_spec=grid_spec,
        out_shape=jax.ShapeDtypeStruct((n_blk * rows * PACKED_CHUNKS, LANES), U32),
        compiler_params=_params("arbitrary"),
        name="moe_experts",
    )(blk_e, n_used, run_idx, next_e, xs, w1, w3, w2)


def _combine_kernel(slot_ref, ys_hbm, x1_ref, rw_ref, g2_ref, b2_ref, o_ref, yg, sem, *, tm):
    i = pl.program_id(0)
    n = pl.num_programs(0)
    slot = i % 2
    slab = PACKED_CHUNKS
    half = tm * slab
    group = 8

    def start_gather(tile, slot_):
        def body(g, c):
            base = pl.multiple_of(g * (group * slab), group * slab)
            for t in range(group):
                for k in range(2):
                    src_row = slot_ref[tile * (2 * tm) + g * (2 * group) + 2 * t + k] * slab
                    pltpu.make_async_copy(
                        ys_hbm.at[pl.ds(pl.multiple_of(src_row, slab), slab), :],
                        yg.at[slot_, pl.ds(k * half + base + t * slab, slab), :],
                        sem.at[slot_]).start(priority=k)
            return c
        lax.fori_loop(0, tm // group, body, 0)

    @pl.when(i == 0)
    def _():
        start_gather(0, 0)

    @pl.when(i + 1 < n)
    def _():
        start_gather(i + 1, 1 - slot)

    pltpu.make_async_copy(ys_hbm.at[pl.ds(0, 2 * half), :], yg.at[slot], sem.at[slot]).wait()
    rw = rw_ref[...]
    buf = yg.at[slot]
    moe = (_from_packed_token_major(buf, 0, tm) * rw[:, 0:1]
           + _from_packed_token_major(buf, tm, tm) * rw[:, 1:2])
    o_ref[...] = _layer_norm_rows(ALPHA * x1_ref[...] + moe, g2_ref[...], b2_ref[...])


def _combine(slot_flat, ys, x1, rw, ln2_g, ln2_b):
    T = x1.shape[0]
    tm = TM_COMB
    grid_spec = pltpu.PrefetchScalarGridSpec(
        num_scalar_prefetch=1,
        grid=(T // tm,),
        in_specs=[
            pl.BlockSpec(memory_space=pl.ANY),
            pl.BlockSpec((tm, D_MODEL), lambda i, *_: (i, 0)),
            pl.BlockSpec((tm, LANES), lambda i, *_: (i, 0)),
            pl.BlockSpec((1, D_MODEL), lambda i, *_: (0, 0)),
            pl.BlockSpec((1, D_MODEL), lambda i, *_: (0, 0)),
        ],
        out_specs=pl.BlockSpec((tm, D_MODEL), lambda i, *_: (i, 0)),
        scratch_shapes=[
            pltpu.VMEM((2, 2 * tm * PACKED_CHUNKS, LANES), U32),
            pltpu.SemaphoreType.DMA((2,)),
        ],
    )
    return pl.pallas_call(
        functools.partial(_combine_kernel, tm=tm),
        grid_spec=grid_spec,
        out_shape=jax.ShapeDtypeStruct((T, D_MODEL), F32),
        compiler_params=_params("arbitrary"),
        name="moe_combine_ln",
    )(slot_flat, ys, x1, rw, ln2_g.reshape(1, D_MODEL).astype(F32), ln2_b.reshape(1, D_MODEL).astype(F32))


def _dispatch_plan(counts, pk_flat, n_tokens):
    rows = MOE_ROWS
    experts = jnp.arange(N_EXPERTS, dtype=jnp.int32)
    padded = (counts + rows - 1) // rows * rows
    pad_end = jnp.cumsum(padded).astype(jnp.int32)
    pad_start = (pad_end - padded).astype(jnp.int32)
    n_blk = 2 * n_tokens // rows + N_EXPERTS
    blk_start = jnp.arange(n_blk, dtype=jnp.int32) * rows
    blk_e = jnp.minimum(jnp.sum(blk_start[:, None] >= pad_end[None, :], axis=1), N_EXPERTS - 1).astype(jnp.int32)
    n_used = (pad_end[-1:] // rows).astype(jnp.int32)
    new_run = jnp.concatenate([jnp.ones((1,), jnp.int32), (blk_e[1:] != blk_e[:-1]).astype(jnp.int32)])
    run_idx = (jnp.cumsum(new_run) - 1).astype(jnp.int32)
    later_active = (experts[None, :] > experts[:, None]) & (counts[None, :] > 0)
    next_active = jnp.min(jnp.where(later_active, experts[None, :], N_EXPERTS), axis=1)
    next_active = jnp.where(next_active == N_EXPERTS, -1, next_active).astype(jnp.int32)
    next_e = next_active[blk_e]
    e_flat = pk_flat >> RANK_BITS
    start_of = jnp.sum(jnp.where(e_flat[:, None] == experts[None, :], pad_start[None, :], 0), axis=1)
    slot_flat = (start_of + (pk_flat & (RANK_SPAN - 1))).astype(jnp.int32)
    return slot_flat, pad_start, pad_end, blk_e, n_used, run_idx, next_e, n_blk


def kernel(x, w_in, conv_w, conv_b, lru_wa, lru_ba, lru_wx, lru_bx, lru_lambda, attn_norm_g, lru_norm_g,
           w_out, ln1_g, ln1_b, router_grp_w, router_grp_b, router_exp_w, router_exp_b, w1, w3, w2,
           ln2_g, ln2_b):
    B, S, D = x.shape
    assert D == D_MODEL and S % DILATED_PATTERNS[-1][0] == 0 and w_in.shape[0] == 1
    T = B * S
    x2d = x.reshape(T, D)

    qkv, rg = _in_proj(x, w_in[0].astype(BF16))
    attn = _attention(qkv, B, S)
    yrec = _recurrent(rg, conv_w[0], conv_b[0], lru_wa[0], lru_ba[0], lru_wx[0], lru_bx[0], lru_lambda[0],
                      lru_norm_g[0], B, S)

    n_r = N_GROUPS + N_EXPERTS
    w_router = jnp.zeros((D, LANES), F32).at[:, :n_r].set(
        jnp.concatenate([router_grp_w[0], router_exp_w[0]], axis=-1).astype(F32))
    b_router = jnp.zeros((1, LANES), F32).at[0, :n_r].set(
        jnp.concatenate([router_grp_b[0], router_exp_b[0]], axis=-1).astype(F32))
    x1, x1p, pk, rw, cnt = _out_router(attn, yrec, x2d, attn_norm_g[0], w_out[0].astype(BF16),
                                       ln1_g[0], ln1_b[0], w_router, b_router)

    pk_flat = pk[:, :2].reshape(-1)
    slot_flat, pad_start, pad_end, blk_e, n_used, run_idx, next_e, n_blk = _dispatch_plan(
        cnt[0, :N_EXPERTS], pk_flat, T)
    xs = _dispatch(slot_flat, pad_start, pad_end, n_used, x1p, n_blk)
    ys = _experts(blk_e, n_used, run_idx, next_e, xs, w1[0], w3[0], w2[0])
    out = _combine(slot_flat, ys, x1, rw, ln2_g[0], ln2_b[0])
    return out.reshape(B, S, D)
```

```python
import functools

import jax
import jax.numpy as jnp
import numpy as np
from jax import lax
from jax.experimental import pallas as pl
from jax.experimental.pallas import tpu as pltpu

F32 = jnp.float32
BF16 = jnp.bfloat16

D_MODEL = 2048
D_ATTN = 1024
D_LRU = 1024
HEAD_DIM = 64
N_HEADS = 16
LANES = 128
N_PLANES = D_ATTN // LANES
DILATED_PATTERNS = ((128, 1), (512, 4), (2048, 16))
SUB_WINDOW = 128
LRU_BLOCKS = 16
LRU_BLOCK_DIM = 64
CONV_WIDTH = 4
RG_C = 8.0
N_GROUPS = 4
EXPERTS_PER_GROUP = 8
N_EXPERTS = 32
D_EXPERT = 512
ALPHA = 2.0 ** 0.25
LN_EPS = 1e-5
RMS_EPS = 1e-6
MASKED = -1e30
LOG2E = 1.4426950408889634

VMEM_LIMIT = 56 * 1024 * 1024

TM_PROJ = 256
TS_LRU = 512
TM_OUT = 256
MOE_ROWS = 256
TM_COMB = 256
TM_DISP = 2048
CHUNKS = D_MODEL // LANES
PACKED_CHUNKS = CHUNKS // 2
RANK_BITS = 16
RANK_SPAN = 1 << RANK_BITS
ATTN_UNROLL = 32


def _params(*sem):
    return pltpu.CompilerParams(dimension_semantics=sem, vmem_limit_bytes=VMEM_LIMIT)


def _in_proj_kernel(x_ref, w_ref, qkv_ref, rg_ref, *, tm):
    xb = x_ref[...].astype(BF16)
    per_class = tm // CLASSES
    out_row = lax.broadcasted_iota(jnp.int32, (tm, tm), 0)
    src_row = (out_row % per_class) * CLASSES + out_row // per_class
    perm = (lax.broadcasted_iota(jnp.int32, (tm, tm), 1) == src_row).astype(BF16)
    xb_cm = jnp.dot(perm, xb, preferred_element_type=F32).astype(BF16)
    for c in range(5):
        lhs = xb_cm if c < 3 else xb
        acc = jnp.dot(lhs, w_ref[:, c * D_ATTN:(c + 1) * D_ATTN], preferred_element_type=F32)
        if c == 0:
            acc = acc * (HEAD_DIM ** -0.5 * LOG2E)
        for p in range(N_PLANES):
            plane = acc[:, p * LANES:(p + 1) * LANES]
            if c < 3:
                for r in range(CLASSES):
                    qkv_ref[c * N_PLANES + p, r] = plane[r * per_class:(r + 1) * per_class, :]
            else:
                rg_ref[(c - 3) * N_PLANES + p] = plane


def _in_proj(x2d, w_in_b, batch, seq):
    T = x2d.shape[0]
    tm = TM_PROJ
    tiles = seq // tm
    assert tm % CLASSES == 0 and seq % tm == 0
    return pl.pallas_call(
        functools.partial(_in_proj_kernel, tm=tm),
        grid=(T // tm,),
        in_specs=[
            pl.BlockSpec((tm, D_MODEL), lambda i: (i, 0)),
            pl.BlockSpec((D_MODEL, 5 * D_ATTN), lambda i: (0, 0), pipeline_mode=pl.Buffered(1)),
        ],
        out_specs=[
            pl.BlockSpec((3 * N_PLANES, None, CLASSES, tm // CLASSES, LANES),
                         lambda i: (0, i // tiles, 0, i % tiles, 0)),
            pl.BlockSpec((2 * N_PLANES, tm, LANES), lambda i: (0, i, 0)),
        ],
        out_shape=[
            jax.ShapeDtypeStruct((3 * N_PLANES, batch, CLASSES, seq // CLASSES, LANES), F32),
            jax.ShapeDtypeStruct((2 * N_PLANES, T, LANES), F32),
        ],
        compiler_params=_params("parallel"),
        name="in_proj",
    )(x2d, w_in_b)


CLASSES = 16


def _attn_kernel(slopes_ref, qs, ks, vs, ndf_ref, ndh_ref, o_ref,
                 os_, m_s, l_s, acc_s, bias_f, bias_h, *, seq):
    hp = pl.program_id(1)
    slope_a = slopes_ref[2 * hp] * LOG2E
    slope_b = slopes_ref[2 * hp + 1] * LOG2E
    w = SUB_WINDOW
    cl = seq // CLASSES
    n_pat = len(DILATED_PATTERNS)

    def gather(ref, starts, size):
        return jnp.concatenate([ref[pl.ds(s, size), :] for s in starts], axis=0)

    def scatter(ref, starts, size, val):
        for c, s in enumerate(starts):
            ref[pl.ds(s, size), :] = val[c * size:(c + 1) * size, :]

    for t in range(2 * n_pat):
        bias_f[t, 0:w, :] = slope_a * ndf_ref[t]
        bias_f[t, w:2 * w, :] = slope_b * ndf_ref[t]
    for t in range(n_pat):
        bias_h[t, 0:w, :] = slope_a * ndh_ref[t]
        bias_h[t, w:2 * w, :] = slope_b * ndh_ref[t]

    def block(c, n, d, pi, first, last, keys):
        qc = w * d // CLASSES
        aligned = lambda x: x if isinstance(x, int) else pl.multiple_of(x, 8)
        bases = [(c + d * j) * cl for j in range(CLASSES // d)]
        qstarts = [aligned(b + n * qc) for b in bases]
        if keys == "own":
            kstarts, kc, bias = qstarts, qc, bias_h[pi]
        elif keys == "prev+own":
            kstarts, kc, bias = [aligned(b + (n - 1) * qc) for b in bases], 2 * qc, bias_f[pi]
        else:
            kfirst = jnp.maximum(n - 1, 0) * qc
            kstarts, kc = [aligned(b + kfirst) for b in bases], 2 * qc
            bias = bias_f[jnp.where(n == 0, pi + n_pat, pi)]
        is_a = lax.broadcasted_iota(jnp.int32, (w, LANES), 1) < HEAD_DIM
        q = gather(qs, qstarts, qc)
        k = gather(ks, kstarts, kc).astype(BF16)
        v = gather(vs, kstarts, kc).astype(BF16)
        zero = jnp.zeros_like(q)
        q2 = jnp.concatenate([jnp.where(is_a, q, zero), jnp.where(is_a, zero, q)], axis=0).astype(BF16)
        s = lax.dot_general(q2, k, (((1,), (1,)), ((), ())), preferred_element_type=F32) + bias
        m = jnp.max(s, axis=-1, keepdims=True)
        p = jnp.exp2(s - m)
        l = jnp.sum(p, axis=-1, keepdims=True)
        o = jnp.dot(p.astype(BF16), v, preferred_element_type=F32)
        m_c = jnp.where(is_a, m[:w], m[w:])
        l_c = jnp.where(is_a, l[:w], l[w:])
        o_c = jnp.where(is_a, o[:w], o[w:])
        if first:
            scatter(m_s, qstarts, qc, m_c)
            scatter(l_s, qstarts, qc, l_c)
            scatter(acc_s, qstarts, qc, o_c)
            return
        m_o = gather(m_s, qstarts, qc)
        m_n = jnp.maximum(m_o, m_c)
        e_o = jnp.exp2(m_o - m_n)
        e_c = jnp.exp2(m_c - m_n)
        l_n = gather(l_s, qstarts, qc) * e_o + l_c * e_c
        a_n = gather(acc_s, qstarts, qc) * e_o + o_c * e_c
        if last:
            scatter(os_, qstarts, qc, a_n / l_n)
        else:
            scatter(m_s, qstarts, qc, m_n)
            scatter(l_s, qstarts, qc, l_n)
            scatter(acc_s, qstarts, qc, a_n)

    for pi, (window, d) in enumerate(DILATED_PATTERNS):
        assert window // d == w and CLASSES % d == 0
        first, last = pi == 0, pi == n_pat - 1
        nb = seq // (w * d)

        if nb >= 4:
            def any_block(i, carry, d=d, pi=pi, first=first, last=last):
                block(i % d, i // d, d, pi, first, last, "any")
                return carry
            lax.fori_loop(0, d * nb, any_block, 0, unroll=ATTN_UNROLL)
        else:
            def class_blocks(c, carry, d=d, pi=pi, first=first, last=last, nb=nb):
                block(c, 0, d, pi, first, last, "own")
                for n in range(1, nb):
                    block(c, n, d, pi, first, last, "prev+own")
                return carry
            lax.fori_loop(0, d, class_blocks, 0, unroll=max(ATTN_UNROLL // nb, 1))

    for r in range(CLASSES):
        o_ref[pl.ds(r, cl, stride=CLASSES), :] = os_[pl.ds(r * cl, cl), :]


def _neg_distance_tables():
    w = SUB_WINDOW
    prev, nxt, own = [], [], []
    for _, d in DILATED_PATTERNS:
        n_cls = CLASSES // d
        qc = w // n_cls
        def pos(chunk_rows):
            j, l = np.divmod(np.arange(n_cls * chunk_rows), chunk_rows)
            return n_cls * l + j
        sq, sk2, sk1 = pos(qc)[:, None], pos(2 * qc)[None, :], pos(qc)[None, :]
        for out, dist in ((prev, sq + w - sk2), (nxt, sq - sk2), (own, sq - sk1)):
            valid = (dist >= 0) & (dist <= w)
            out.append(np.where(valid, -(dist * d).astype(np.float32), np.float32(MASKED)))
    return jnp.asarray(np.stack(prev + nxt), dtype=F32), jnp.asarray(np.stack(own), dtype=F32)


def _attention(qkv, batch, seq):
    qkv4 = qkv.reshape(3 * N_PLANES, batch, seq, LANES)
    slopes = jnp.exp2(-8.0 * jnp.arange(1, N_HEADS + 1, dtype=F32) / N_HEADS)
    nd_full, nd_head = _neg_distance_tables()
    n_pat = len(DILATED_PATTERNS)
    plane = lambda off: pl.BlockSpec((None, None, seq, LANES), lambda b, h: (off + h, b, 0, 0))
    return pl.pallas_call(
        functools.partial(_attn_kernel, seq=seq),
        grid=(batch, N_PLANES),
        in_specs=[
            pl.BlockSpec(memory_space=pltpu.SMEM),
            plane(0), plane(N_PLANES), plane(2 * N_PLANES),
            pl.BlockSpec((2 * n_pat, SUB_WINDOW, 2 * SUB_WINDOW), lambda b, h: (0, 0, 0)),
            pl.BlockSpec((n_pat, SUB_WINDOW, SUB_WINDOW), lambda b, h: (0, 0, 0)),
        ],
        out_specs=pl.BlockSpec((None, None, seq, LANES), lambda b, h: (h, b, 0, 0)),
        scratch_shapes=[pltpu.VMEM((seq, LANES), F32)] * 4 + [
            pltpu.VMEM((2 * n_pat, 2 * SUB_WINDOW, 2 * SUB_WINDOW), F32),
            pltpu.VMEM((n_pat, 2 * SUB_WINDOW, SUB_WINDOW), F32),
        ],
        out_shape=jax.ShapeDtypeStruct((N_PLANES, batch, seq, LANES), F32),
        compiler_params=_params("parallel", "parallel"),
        name="dilated_attention",
    )(slopes, qkv4, qkv4, qkv4, nd_full, nd_head)


def _gelu_tanh(x):
    c = np.float32(np.sqrt(2.0 / np.pi))
    return 0.5 * x * (1.0 + jnp.tanh(c * (x + 0.044715 * (x * x * x))))


def _log1p(x):
    u = 1.0 + x
    return jnp.where(u == 1.0, x, jnp.log(u) * x / (u - 1.0))


def _softplus(z):
    return jnp.maximum(z, 0.0) + _log1p(jnp.exp(-jnp.abs(z)))


PHASES = 8


def _lru_kernel(xr_ref, xg_ref, cw_ref, cb_ref, wbd_ref, ba_ref, bx_ref, lam_ref, g_ref, y_ref,
                xc, hc, *, ts):
    i = pl.program_id(1)
    G = ts // PHASES
    first_group = lax.broadcasted_iota(jnp.int32, (G, LANES), 0) == 0
    sub = lax.broadcasted_iota(jnp.int32, (G, LANES), 0)

    @pl.when(i == 0)
    def _():
        xc[...] = jnp.zeros_like(xc)
        hc[...] = jnp.zeros_like(hc)

    def prev_group(cur, carry_row):
        return jnp.where(first_group, carry_row, pltpu.roll(cur, 1, 0))

    conv = []
    for l in range(N_PLANES):
        x = [xr_ref[l, pl.ds(s, G, stride=PHASES), :] for s in range(PHASES)]
        back = {s: prev_group(x[s], xc[l, s:s + 1, :]) for s in range(PHASES - CONV_WIDTH + 1, PHASES)}
        phases = []
        for s in range(PHASES):
            acc = jnp.broadcast_to(cb_ref[l:l + 1, :], (G, LANES))
            for j in range(CONV_WIDTH):
                q = s - (CONV_WIDTH - 1) + j
                acc = acc + (x[q] if q >= 0 else back[q + PHASES]) * cw_ref[j, l:l + 1, :]
            phases.append(acc)
        conv.append(jnp.concatenate(phases, axis=0))
        xc[l] = xr_ref[l, ts - PHASES:ts, :]

    gates_r, gates_i = [], []
    for j in range(N_PLANES // 2):
        yb = jnp.concatenate([conv[2 * j], conv[2 * j + 1]], axis=-1).astype(BF16)
        g = jnp.dot(yb, wbd_ref[j], preferred_element_type=F32)
        gates_r += [g[:, 0:LANES], g[:, LANES:2 * LANES]]
        gates_i += [g[:, 2 * LANES:3 * LANES], g[:, 3 * LANES:4 * LANES]]

    rec, sq = [], None
    for l in range(N_PLANES):
        y = conv[l]
        r = jax.nn.sigmoid(gates_r[l] + ba_ref[l:l + 1, :])
        ig = jax.nn.sigmoid(gates_i[l] + bx_ref[l:l + 1, :])
        log_a = (-RG_C * r) * _softplus(-lam_ref[l:l + 1, :])
        a = jnp.exp(log_a)
        z = 1.0 - a * a
        u = jnp.where(z > 0.0, z * lax.rsqrt(z), 0.0) * (ig * y)

        piece = lambda v, s: v[s * G:(s + 1) * G, :]
        pa, pb = [piece(a, 0)], [piece(u, 0)]
        for s in range(1, PHASES):
            pa.append(piece(a, s) * pa[-1])
            pb.append(piece(a, s) * pb[-1] + piece(u, s))
        ga, gb = pa[-1], pb[-1]
        sh = 1
        while sh < G:
            take = sub >= sh
            gb = jnp.where(take, ga * pltpu.roll(gb, sh, 0) + gb, gb)
            ga = jnp.where(take, ga * pltpu.roll(ga, sh, 0), ga)
            sh *= 2
        h0 = hc[l, 0:1, :]
        h_end = ga * h0 + gb
        h_in = prev_group(h_end, h0)
        hc[l] = jnp.broadcast_to(h_end[G - 1:G, :], (PHASES, LANES))
        h = jnp.concatenate([pa[s] * h_in + pb[s] for s in range(PHASES)], axis=0)

        xg = jnp.concatenate([xg_ref[l, pl.ds(s, G, stride=PHASES), :] for s in range(PHASES)], axis=0)
        rl = h * _gelu_tanh(xg)
        rec.append(rl)
        part = jnp.sum(rl * rl, axis=-1, keepdims=True)
        sq = part if sq is None else sq + part

    scale = lax.rsqrt(sq * (1.0 / D_LRU) + RMS_EPS)
    for l in range(N_PLANES):
        out = rec[l] * scale * g_ref[l:l + 1, :]
        for s in range(PHASES):
            y_ref[l, pl.ds(s, G, stride=PHASES), :] = out[s * G:(s + 1) * G, :]


def _block_diag_gates(wa, wx):
    def bd(wm):
        wm = wm.reshape(4, 4, LRU_BLOCK_DIM, LRU_BLOCK_DIM)
        eye = jnp.eye(4, dtype=wm.dtype)
        full = jnp.einsum("gide,ij->gidje", wm, eye)
        return full.reshape(4, 4 * LRU_BLOCK_DIM, 4 * LRU_BLOCK_DIM)
    return jnp.concatenate([bd(wa), bd(wx)], axis=-1).astype(BF16)


def _recurrent(rg, conv_w, conv_b, lru_wa, lru_ba, lru_wx, lru_bx, lru_lambda, lru_norm_g, batch, seq):
    ts = TS_LRU
    rg5 = rg.reshape(2, N_PLANES, batch, seq, LANES)
    wbd = _block_diag_gates(lru_wa, lru_wx)
    planes = lambda a: a.reshape(N_PLANES, LANES).astype(F32)
    vec = pl.BlockSpec((N_PLANES, LANES), lambda b, i: (0, 0))
    return pl.pallas_call(
        functools.partial(_lru_kernel, ts=ts),
        grid=(batch, seq // ts),
        in_specs=[
            pl.BlockSpec((None, N_PLANES, None, ts, LANES), lambda b, i: (0, 0, b, i, 0)),
            pl.BlockSpec((None, N_PLANES, None, ts, LANES), lambda b, i: (1, 0, b, i, 0)),
            pl.BlockSpec((CONV_WIDTH, N_PLANES, LANES), lambda b, i: (0, 0, 0)),
            vec,
            pl.BlockSpec((4, 4 * LRU_BLOCK_DIM, 8 * LRU_BLOCK_DIM), lambda b, i: (0, 0, 0)),
            vec, vec, vec, vec,
        ],
        out_specs=pl.BlockSpec((N_PLANES, None, ts, LANES), lambda b, i: (0, b, i, 0)),
        out_shape=jax.ShapeDtypeStruct((N_PLANES, batch, seq, LANES), F32),
        scratch_shapes=[
            pltpu.VMEM((N_PLANES, PHASES, LANES), F32),
            pltpu.VMEM((N_PLANES, PHASES, LANES), F32),
        ],
        compiler_params=_params("parallel", "arbitrary"),
        name="conv_rglru",
    )(rg5, rg5, conv_w.astype(F32).reshape(CONV_WIDTH, N_PLANES, LANES), planes(conv_b), wbd, planes(lru_ba),
      planes(lru_bx), planes(lru_lambda), planes(lru_norm_g))


def _layer_norm_rows(z, g, b):
    mu = jnp.mean(z, axis=-1, keepdims=True)
    zc = z - mu
    var = jnp.mean(zc * zc, axis=-1, keepdims=True)
    return zc * lax.rsqrt(var + LN_EPS) * g + b


U32 = jnp.uint32
HI_HALF = np.uint32(0xFFFF0000)


def _to_packed_token_major(ref, val, first_row, n_rows):
    bits = lambda a: lax.bitcast_convert_type(a.astype(BF16).astype(F32), U32)
    half = PACKED_CHUNKS * LANES
    for c in range(PACKED_CHUNKS):
        lo = bits(val[:, c * LANES:(c + 1) * LANES]) >> 16
        hi = bits(val[:, half + c * LANES:half + (c + 1) * LANES]) & HI_HALF
        ref[pl.ds(first_row * PACKED_CHUNKS + c, n_rows, stride=PACKED_CHUNKS), :] = lo | hi


def _from_packed_token_major(ref, first_row, n_rows):
    words = [ref[pl.ds(first_row * PACKED_CHUNKS + c, n_rows, stride=PACKED_CHUNKS), :]
             for c in range(PACKED_CHUNKS)]
    lo = [lax.bitcast_convert_type(wd << 16, F32) for wd in words]
    hi = [lax.bitcast_convert_type(wd & HI_HALF, F32) for wd in words]
    return jnp.concatenate(lo + hi, axis=-1)


def _out_router_kernel(*refs, tm):
    *io_refs, cnt_s, z_a, z_b = refs
    i = pl.program_id(0)

    @pl.when(i == 0)
    def _():
        cnt_s[...] = jnp.zeros_like(cnt_s)
        z_b[...] = jnp.zeros_like(z_b)

    @pl.when(i % 2 == 0)
    def _():
        _out_router_step(*io_refs, cnt_s, z_b, z_a, tm=tm)

    @pl.when(i % 2 == 1)
    def _():
        _out_router_step(*io_refs, cnt_s, z_a, z_b, tm=tm)


def _out_router_step(attn_ref, yrec_ref, x_ref, ga_ref, wo_ref, g1_ref, b1_ref, wr_ref, br_ref,
                     x1_ref, x1p_ref, pk_ref, rw_ref, cnt_ref, cnt_s, z_in, z_out, *, tm):
    i = pl.program_id(0)
    attn = jnp.concatenate([attn_ref[p] for p in range(N_PLANES)], axis=-1)
    ms = jnp.mean(attn * attn, axis=-1, keepdims=True)
    ya = (attn * lax.rsqrt(ms + RMS_EPS) * ga_ref[...]).astype(BF16)
    yr = jnp.concatenate([yrec_ref[p] for p in range(N_PLANES)], axis=-1).astype(BF16)
    n_col = 8
    cw = D_MODEL // n_col

    def project(j):
        cols = slice(j * cw, (j + 1) * cw)
        mix = jnp.dot(ya, wo_ref[0:D_ATTN, cols], preferred_element_type=F32)
        mix = mix + jnp.dot(yr, wo_ref[D_ATTN:, cols], preferred_element_type=F32)
        z_out[:, cols] = ALPHA * x_ref[:, cols] + mix

    def normalize(r0, n_rows):
        rows = slice(r0, r0 + n_rows)
        x1 = _layer_norm_rows(z_in[rows, :], g1_ref[...], b1_ref[...])
        x1_ref[rows, :] = x1
        _to_packed_token_major(x1p_ref, x1, r0, n_rows)
        return x1.astype(BF16)

    n_row = 4
    rh = tm // n_row
    parts = []
    for k in range(n_row):
        parts.append(normalize(k * rh, rh))
        project(k)

    xb = jnp.concatenate(parts, axis=0)
    logits = jnp.dot(xb, wr_ref[...].astype(BF16), preferred_element_type=F32) + br_ref[...]
    project(4)
    project(5)
    _route(logits, pk_ref, rw_ref, cnt_ref, cnt_s, i, tm)
    project(6)
    project(7)


def _route(logits, pk_ref, rw_ref, cnt_ref, cnt_s, i, tm):
    lane = lax.broadcasted_iota(jnp.int32, (tm, LANES), 1)
    big = jnp.int32(LANES)
    first_true = lambda c: jnp.min(jnp.where(c, lane, big), axis=-1, keepdims=True)

    in_g = lane < N_GROUPS
    gl = jnp.where(in_g, logits, MASKED)
    gmax = jnp.max(gl, axis=-1, keepdims=True)
    g_idx = first_true(gl == gmax)
    gsum = jnp.sum(jnp.where(in_g, jnp.exp(gl - gmax), 0.0), axis=-1, keepdims=True)
    g_gate = 1.0 / gsum

    lo = N_GROUPS + EXPERTS_PER_GROUP * g_idx
    in_e = (lane >= lo) & (lane < lo + EXPERTS_PER_GROUP)
    el = jnp.where(in_e, logits, MASKED)
    emax = jnp.max(el, axis=-1, keepdims=True)
    ee = jnp.where(in_e, jnp.exp(el - emax), 0.0)
    esum = jnp.sum(ee, axis=-1, keepdims=True)
    i1 = first_true(el == emax)
    rest = jnp.where(in_e & (lane != i1), ee, -1.0)
    e2max = jnp.max(rest, axis=-1, keepdims=True)
    i2 = first_true(rest == e2max)
    v1 = 1.0 / esum
    v2 = e2max / esum
    den = v1 + v2
    w1 = g_gate * v1 / den
    w2 = g_gate * v2 / den
    e1 = i1 - N_GROUPS
    e2 = i2 - N_GROUPS
    rw_ref[...] = jnp.where(lane == 0, w1, jnp.where(lane == 1, w2, 0.0))

    hot1 = lane == e1
    hot2 = lane == e2
    both = jnp.where((hot1 | hot2) & (i > 0), 1.0, 0.0)
    earlier = (lax.broadcasted_iota(jnp.int32, (tm, tm), 1) < lax.broadcasted_iota(jnp.int32, (tm, tm), 0))
    before = jnp.dot(earlier.astype(BF16), both.astype(BF16), preferred_element_type=F32) + cnt_s[0:1, :]
    r1 = jnp.sum(jnp.where(hot1, before, 0.0), axis=-1, keepdims=True).astype(jnp.int32)
    r2 = jnp.sum(jnp.where(hot2, before, 0.0), axis=-1, keepdims=True).astype(jnp.int32)
    pk_ref[...] = jnp.where(lane == 0, e1 * RANK_SPAN + r1, jnp.where(lane == 1, e2 * RANK_SPAN + r2, 0))
    cnt_s[...] = cnt_s[...] + jnp.sum(both, axis=0, keepdims=True)
    cnt_ref[...] = cnt_s[...].astype(jnp.int32)


def _out_router(attn, yrec, x2d, attn_norm_g, w_out_b, ln1_g, ln1_b, w_router, b_router):
    T = x2d.shape[0]
    tm = TM_OUT
    assert 2 * T <= RANK_SPAN
    attn3 = attn.reshape(N_PLANES, T, LANES)
    yrec3 = yrec.reshape(N_PLANES, T, LANES)
    const = lambda shape: pl.BlockSpec(shape, lambda i: (0,) * len(shape))
    n = T // tm
    stage1 = lambda i: jnp.minimum(i, n - 1)
    stage2 = lambda i: jnp.maximum(i - 1, 0)
    return pl.pallas_call(
        functools.partial(_out_router_kernel, tm=tm),
        grid=(n + 1,),
        in_specs=[
            pl.BlockSpec((N_PLANES, tm, LANES), lambda i: (0, stage1(i), 0)),
            pl.BlockSpec((N_PLANES, tm, LANES), lambda i: (0, stage1(i), 0)),
            pl.BlockSpec((tm, D_MODEL), lambda i: (stage1(i), 0)),
            const((1, D_ATTN)),
            pl.BlockSpec((D_MODEL, D_MODEL), lambda i: (0, 0), pipeline_mode=pl.Buffered(1)),
            const((1, D_MODEL)), const((1, D_MODEL)),
            const((D_MODEL, LANES)), const((1, LANES)),
        ],
        out_specs=[
            pl.BlockSpec((tm, D_MODEL), lambda i: (stage2(i), 0)),
            pl.BlockSpec((tm * PACKED_CHUNKS, LANES), lambda i: (stage2(i), 0)),
            pl.BlockSpec((tm, LANES), lambda i: (stage2(i), 0)),
            pl.BlockSpec((tm, LANES), lambda i: (stage2(i), 0)),
            const((8, LANES)),
        ],
        out_shape=[
            jax.ShapeDtypeStruct((T, D_MODEL), F32),
            jax.ShapeDtypeStruct((T * PACKED_CHUNKS, LANES), U32),
            jax.ShapeDtypeStruct((T, LANES), jnp.int32),
            jax.ShapeDtypeStruct((T, LANES), F32),
            jax.ShapeDtypeStruct((8, LANES), jnp.int32),
        ],
        scratch_shapes=[pltpu.VMEM((8, LANES), F32), pltpu.VMEM((tm, D_MODEL), F32),
                        pltpu.VMEM((tm, D_MODEL), F32)],
        compiler_params=_params("arbitrary"),
        name="out_proj_router",
    )(attn3, yrec3, x2d, attn_norm_g.reshape(1, D_ATTN).astype(F32), w_out_b,
      ln1_g.reshape(1, D_MODEL).astype(F32), ln1_b.reshape(1, D_MODEL).astype(F32), w_router, b_router)


def _dispatch_kernel(slot_ref, pstart_ref, pend_ref, nused_ref, x1p_ref, xs_hbm, zbuf, sem, zsem,
                     *, tm, rows, n_blk):
    i = pl.program_id(0)
    slab = PACKED_CHUNKS
    blk = rows * slab

    @pl.when(i == 0)
    def _():
        zbuf[...] = jnp.zeros_like(zbuf)
        n_used = nused_ref[0]

        def zero_copy(b):
            return pltpu.make_async_copy(zbuf, xs_hbm.at[pl.ds(pl.multiple_of(b * blk, blk), blk), :], zsem)

        def for_each_zeroed_block(fn):
            for e in range(N_EXPERTS):
                @pl.when(pend_ref[e] > pstart_ref[e])
                def _(e=e):
                    fn(pend_ref[e] // rows - 1)

            def tail(b, c):
                fn(b)
                return c
            lax.fori_loop(n_used, n_blk, tail, 0)

        for_each_zeroed_block(lambda b: zero_copy(b).start())
        for_each_zeroed_block(lambda b: zero_copy(b).wait())

    group = 8

    def push(g, c):
        for t in range(group):
            src = x1p_ref.at[pl.ds(pl.multiple_of(g * (group * slab), group * slab) + t * slab, slab), :]
            for k in range(2):
                row = slot_ref[i * (2 * tm) + g * (2 * group) + 2 * t + k] * slab
                dst = xs_hbm.at[pl.ds(pl.multiple_of(row, slab), slab), :]
                pltpu.make_async_copy(src, dst, sem).start(priority=k)
        return c
    lax.fori_loop(0, tm // group, push, 0)
    for _ in range(2):
        pltpu.make_async_copy(x1p_ref, xs_hbm.at[pl.ds(0, tm * slab), :], sem).wait()


def _dispatch(slot_flat, pad_start, pad_end, n_used, x1p, n_blk):
    tm, rows, slab = TM_DISP, MOE_ROWS, PACKED_CHUNKS
    n_tok = x1p.shape[0] // slab
    grid_spec = pltpu.PrefetchScalarGridSpec(
        num_scalar_prefetch=4,
        grid=(n_tok // tm,),
        in_specs=[pl.BlockSpec((tm * slab, LANES), lambda i, *_: (i, 0))],
        out_specs=pl.BlockSpec(memory_space=pl.ANY),
        scratch_shapes=[
            pltpu.VMEM((rows * slab, LANES), U32),
            pltpu.SemaphoreType.DMA(()),
            pltpu.SemaphoreType.DMA(()),
        ],
    )
    return pl.pallas_call(
        functools.partial(_dispatch_kernel, tm=tm, rows=rows, n_blk=n_blk),
        grid_spec=grid_spec,
        out_shape=jax.ShapeDtypeStruct((n_blk * rows * slab, LANES), U32),
        compiler_params=_params("arbitrary"),
        name="moe_dispatch",
    )(slot_flat, pad_start, pad_end, n_used, x1p)


def _expert_kernel(blk_e_ref, n_used_ref, run_ref, next_e_ref, xs_ref, w1_hbm, w3_hbm, w2_hbm, y_ref,
                   wf1, wf3, wf2, wsem, w1b, w3b, w2b, *, rows):
    i = pl.program_id(0)
    n_used = n_used_ref[0]
    e = blk_e_ref[i]
    slot = run_ref[i] % 2
    new_run = jnp.logical_or(i == 0, e != blk_e_ref[jnp.maximum(i - 1, 0)])

    def weight_copies(expert, slot_):
        return [pltpu.make_async_copy(src.at[expert], dst.at[slot_], wsem.at[slot_])
                for src, dst in ((w1_hbm, wf1), (w3_hbm, wf3), (w2_hbm, wf2))]

    @pl.when(i == 0)
    def _():
        for cp in weight_copies(e, 0):
            cp.start()

    @pl.when(jnp.logical_and(i < n_used, new_run))
    def _():
        for cp in weight_copies(e, slot):
            cp.wait()
        nxt = next_e_ref[i]

        @pl.when(nxt >= 0)
        def _():
            for cp in weight_copies(nxt, 1 - slot):
                cp.start(priority=1)

        w1b[...] = wf1[slot].astype(BF16)
        w3b[...] = wf3[slot].astype(BF16)
        w2b[...] = wf2[slot].astype(BF16)

    @pl.when(i < n_used)
    def _():
        xb = _from_packed_token_major(xs_ref, 0, rows).astype(BF16)
        h1 = jnp.dot(xb, w1b[...], preferred_element_type=F32)
        h3 = jnp.dot(xb, w3b[...], preferred_element_type=F32)
        h = (jax.nn.silu(h1) * h3).astype(BF16)
        _to_packed_token_major(y_ref, jnp.dot(h, w2b[...], preferred_element_type=F32), 0, rows)

    @pl.when(i >= n_used)
    def _():
        y_ref[...] = jnp.zeros_like(y_ref)


def _experts(blk_e, n_used, run_idx, next_e, xs, w1, w3, w2):
    rows = MOE_ROWS
    n_blk = blk_e.shape[0]
    hbm = pl.BlockSpec(memory_space=pl.ANY)
    grid_spec = pltpu.PrefetchScalarGridSpec(
        num_scalar_prefetch=4,
        grid=(n_blk,),
        in_specs=[
            pl.BlockSpec((rows * PACKED_CHUNKS, LANES), lambda i, be, nu, *_: (jnp.minimum(i, nu[0] - 1), 0)),
            hbm, hbm, hbm,
        ],
        out_specs=pl.BlockSpec((rows * PACKED_CHUNKS, LANES), lambda i, *_: (i, 0)),
        scratch_shapes=[
            pltpu.VMEM((2, D_MODEL, D_EXPERT), F32),
            pltpu.VMEM((2, D_MODEL, D_EXPERT), F32),
            pltpu.VMEM((2, D_EXPERT, D_MODEL), F32),
            pltpu.SemaphoreType.DMA((2,)),
            pltpu.VMEM((D_MODEL, D_EXPERT), BF16),
            pltpu.VMEM((D_MODEL, D_EXPERT), BF16),
            pltpu.VMEM((D_EXPERT, D_MODEL), BF16),
        ],
    )
    return pl.pallas_call(
        functools.partial(_expert_kernel, rows=rows),
        grid_spec=grid_spec,
        out_shape=jax.ShapeDtypeStruct((n_blk * rows * PACKED_CHUNKS, LANES), U32),
        compiler_params=_params("arbitrary"),
        name="moe_experts",
    )(blk_e, n_used, run_idx, next_e, xs, w1, w3, w2)


def _combine_kernel(slot_ref, ys_hbm, x1_ref, rw_ref, g2_ref, b2_ref, o_ref, yg, sem, *, tm):
    i = pl.program_id(0)
    n = pl.num_programs(0)
    slot = i % 2
    slab = PACKED_CHUNKS
    half = tm * slab
    group = 8

    def start_gather(tile, slot_):
        def body(g, c):
            base = pl.multiple_of(g * (group * slab), group * slab)
            for t in range(group):
                for k in range(2):
                    src_row = slot_ref[tile * (2 * tm) + g * (2 * group) + 2 * t + k] * slab
                    pltpu.make_async_copy(
                        ys_hbm.at[pl.ds(pl.multiple_of(src_row, slab), slab), :],
                        yg.at[slot_, pl.ds(k * half + base + t * slab, slab), :],
                        sem.at[slot_]).start(priority=k)
            return c
        lax.fori_loop(0, tm // group, body, 0)

    @pl.when(i == 0)
    def _():
        start_gather(0, 0)

    @pl.when(i + 1 < n)
    def _():
        start_gather(i + 1, 1 - slot)

    pltpu.make_async_copy(ys_hbm.at[pl.ds(0, 2 * half), :], yg.at[slot], sem.at[slot]).wait()
    rw = rw_ref[...]
    buf = yg.at[slot]
    moe = (_from_packed_token_major(buf, 0, tm) * rw[:, 0:1]
           + _from_packed_token_major(buf, tm, tm) * rw[:, 1:2])
    o_ref[...] = _layer_norm_rows(ALPHA * x1_ref[...] + moe, g2_ref[...], b2_ref[...])


def _combine(slot_flat, ys, x1, rw, ln2_g, ln2_b):
    T = x1.shape[0]
    tm = TM_COMB
    grid_spec = pltpu.PrefetchScalarGridSpec(
        num_scalar_prefetch=1,
        grid=(T // tm,),
        in_specs=[
            pl.BlockSpec(memory_space=pl.ANY),
            pl.BlockSpec((tm, D_MODEL), lambda i, *_: (i, 0)),
            pl.BlockSpec((tm, LANES), lambda i, *_: (i, 0)),
            pl.BlockSpec((1, D_MODEL), lambda i, *_: (0, 0)),
            pl.BlockSpec((1, D_MODEL), lambda i, *_: (0, 0)),
        ],
        out_specs=pl.BlockSpec((tm, D_MODEL), lambda i, *_: (i, 0)),
        scratch_shapes=[
            pltpu.VMEM((2, 2 * tm * PACKED_CHUNKS, LANES), U32),
            pltpu.SemaphoreType.DMA((2,)),
        ],
    )
    return pl.pallas_call(
        functools.partial(_combine_kernel, tm=tm),
        grid_spec=grid_spec,
        out_shape=jax.ShapeDtypeStruct((T, D_MODEL), F32),
        compiler_params=_params("arbitrary"),
        name="moe_combine_ln",
    )(slot_flat, ys, x1, rw, ln2_g.reshape(1, D_MODEL).astype(F32), ln2_b.reshape(1, D_MODEL).astype(F32))


def _dispatch_plan(counts, pk_flat, n_tokens):
    rows = MOE_ROWS
    experts = jnp.arange(N_EXPERTS, dtype=jnp.int32)
    padded = (counts + rows - 1) // rows * rows
    pad_end = jnp.cumsum(padded).astype(jnp.int32)
    pad_start = (pad_end - padded).astype(jnp.int32)
    n_blk = 2 * n_tokens // rows + N_EXPERTS
    blk_start = jnp.arange(n_blk, dtype=jnp.int32) * rows
    blk_e = jnp.minimum(jnp.sum(blk_start[:, None] >= pad_end[None, :], axis=1), N_EXPERTS - 1).astype(jnp.int32)
    n_used = (pad_end[-1:] // rows).astype(jnp.int32)
    new_run = jnp.concatenate([jnp.ones((1,), jnp.int32), (blk_e[1:] != blk_e[:-1]).astype(jnp.int32)])
    run_idx = (jnp.cumsum(new_run) - 1).astype(jnp.int32)
    later_active = (experts[None, :] > experts[:, None]) & (counts[None, :] > 0)
    next_active = jnp.min(jnp.where(later_active, experts[None, :], N_EXPERTS), axis=1)
    next_active = jnp.where(next_active == N_EXPERTS, -1, next_active).astype(jnp.int32)
    next_e = next_active[blk_e]
    e_flat = pk_flat >> RANK_BITS
    slot_flat = (pad_start[e_flat] + (pk_flat & (RANK_SPAN - 1))).astype(jnp.int32)
    return slot_flat, pad_start, pad_end, blk_e, n_used, run_idx, next_e, n_blk


def kernel(x, w_in, conv_w, conv_b, lru_wa, lru_ba, lru_wx, lru_bx, lru_lambda, attn_norm_g, lru_norm_g,
           w_out, ln1_g, ln1_b, router_grp_w, router_grp_b, router_exp_w, router_exp_b, w1, w3, w2,
           ln2_g, ln2_b):
    B, S, D = x.shape
    assert D == D_MODEL and S % DILATED_PATTERNS[-1][0] == 0 and w_in.shape[0] == 1
    T = B * S
    x2d = x.reshape(T, D)

    qkv, rg = _in_proj(x2d, w_in[0].astype(BF16), B, S)
    attn = _attention(qkv, B, S)
    yrec = _recurrent(rg, conv_w[0], conv_b[0], lru_wa[0], lru_ba[0], lru_wx[0], lru_bx[0], lru_lambda[0],
                      lru_norm_g[0], B, S)

    n_r = N_GROUPS + N_EXPERTS
    w_router = jnp.zeros((D, LANES), F32).at[:, :n_r].set(
        jnp.concatenate([router_grp_w[0], router_exp_w[0]], axis=-1).astype(F32))
    b_router = jnp.zeros((1, LANES), F32).at[0, :n_r].set(
        jnp.concatenate([router_grp_b[0], router_exp_b[0]], axis=-1).astype(F32))
    x1, x1p, pk, rw, cnt = _out_router(attn, yrec, x2d, attn_norm_g[0], w_out[0].astype(BF16),
                                       ln1_g[0], ln1_b[0], w_router, b_router)

    pk_flat = pk[:, :2].reshape(-1)
    slot_flat, pad_start, pad_end, blk_e, n_used, run_idx, next_e, n_blk = _dispatch_plan(
        cnt[0, :N_EXPERTS], pk_flat, T)
    xs = _dispatch(slot_flat, pad_start, pad_end, n_used, x1p, n_blk)
    ys = _experts(blk_e, n_used, run_idx, next_e, xs, w1[0], w3[0], w2[0])
    out = _combine(slot_flat, ys, x1, rw, ln2_g[0], ln2_b[0])
    return out.reshape(B, S, D)
```

```python
import functools

import jax
import jax.numpy as jnp
import numpy as np
from jax import lax
from jax.experimental import pallas as pl
from jax.experimental.pallas import tpu as pltpu

F32 = jnp.float32
BF16 = jnp.bfloat16

D_MODEL = 2048
D_ATTN = 1024
D_LRU = 1024
HEAD_DIM = 64
N_HEADS = 16
LANES = 128
N_PLANES = D_ATTN // LANES
DILATED_PATTERNS = ((128, 1), (512, 4), (2048, 16))
SUB_WINDOW = 128
LRU_BLOCKS = 16
LRU_BLOCK_DIM = 64
CONV_WIDTH = 4
RG_C = 8.0
N_GROUPS = 4
EXPERTS_PER_GROUP = 8
N_EXPERTS = 32
D_EXPERT = 512
ALPHA = 2.0 ** 0.25
LN_EPS = 1e-5
RMS_EPS = 1e-6
MASKED = -1e30
LOG2E = 1.4426950408889634

VMEM_LIMIT = 56 * 1024 * 1024

TM_PROJ = 256
TS_LRU = 512
TM_OUT = 256
MOE_ROWS = 256
TM_COMB = 256
TM_DISP = 2048
CHUNKS = D_MODEL // LANES
PACKED_CHUNKS = CHUNKS // 2
RANK_BITS = 16
RANK_SPAN = 1 << RANK_BITS
ATTN_UNROLL = 32


def _params(*sem):
    return pltpu.CompilerParams(dimension_semantics=sem, vmem_limit_bytes=VMEM_LIMIT)


def _in_proj_kernel(x_ref, w_ref, qkv_ref, rg_ref, *, tm):
    xb = x_ref[...].astype(BF16)
    per_class = tm // CLASSES
    out_row = lax.broadcasted_iota(jnp.int32, (tm, tm), 0)
    src_row = (out_row % per_class) * CLASSES + out_row // per_class
    perm = (lax.broadcasted_iota(jnp.int32, (tm, tm), 1) == src_row).astype(BF16)
    xb_cm = jnp.dot(perm, xb, preferred_element_type=F32).astype(BF16)
    for c in range(5):
        lhs = xb_cm if c < 3 else xb
        acc = jnp.dot(lhs, w_ref[:, c * D_ATTN:(c + 1) * D_ATTN], preferred_element_type=F32)
        if c == 0:
            acc = acc * (HEAD_DIM ** -0.5 * LOG2E)
        for p in range(N_PLANES):
            plane = acc[:, p * LANES:(p + 1) * LANES]
            if c < 3:
                for r in range(CLASSES):
                    qkv_ref[c * N_PLANES + p, r] = plane[r * per_class:(r + 1) * per_class, :]
            else:
                rg_ref[(c - 3) * N_PLANES + p] = plane


def _in_proj(x2d, w_in_b, batch, seq):
    T = x2d.shape[0]
    tm = TM_PROJ
    tiles = seq // tm
    assert tm % CLASSES == 0 and seq % tm == 0
    return pl.pallas_call(
        functools.partial(_in_proj_kernel, tm=tm),
        grid=(T // tm,),
        in_specs=[
            pl.BlockSpec((tm, D_MODEL), lambda i: (i, 0)),
            pl.BlockSpec((D_MODEL, 5 * D_ATTN), lambda i: (0, 0), pipeline_mode=pl.Buffered(1)),
        ],
        out_specs=[
            pl.BlockSpec((3 * N_PLANES, None, CLASSES, tm // CLASSES, LANES),
                         lambda i: (0, i // tiles, 0, i % tiles, 0)),
            pl.BlockSpec((2 * N_PLANES, tm, LANES), lambda i: (0, i, 0)),
        ],
        out_shape=[
            jax.ShapeDtypeStruct((3 * N_PLANES, batch, CLASSES, seq // CLASSES, LANES), F32),
            jax.ShapeDtypeStruct((2 * N_PLANES, T, LANES), F32),
        ],
        compiler_params=_params("parallel"),
        name="in_proj",
    )(x2d, w_in_b)


CLASSES = 16


def _attn_kernel(slopes_ref, qs, ks, vs, ndf_ref, ndh_ref, o_ref,
                 os_, m_s, l_s, acc_s, bias_f, bias_h, *, seq):
    hp = pl.program_id(1)
    slope_a = slopes_ref[2 * hp] * LOG2E
    slope_b = slopes_ref[2 * hp + 1] * LOG2E
    w = SUB_WINDOW
    cl = seq // CLASSES
    n_pat = len(DILATED_PATTERNS)

    def gather(ref, starts, size):
        return jnp.concatenate([ref[pl.ds(s, size), :] for s in starts], axis=0)

    def scatter(ref, starts, size, val):
        for c, s in enumerate(starts):
            ref[pl.ds(s, size), :] = val[c * size:(c + 1) * size, :]

    for t in range(2 * n_pat):
        bias_f[t, 0:w, :] = slope_a * ndf_ref[t]
        bias_f[t, w:2 * w, :] = slope_b * ndf_ref[t]
    for t in range(n_pat):
        bias_h[t, 0:w, :] = slope_a * ndh_ref[t]
        bias_h[t, w:2 * w, :] = slope_b * ndh_ref[t]

    def block(c, n, d, pi, first, last, keys):
        qc = w * d // CLASSES
        aligned = lambda x: x if isinstance(x, int) else pl.multiple_of(x, 8)
        bases = [(c + d * j) * cl for j in range(CLASSES // d)]
        qstarts = [aligned(b + n * qc) for b in bases]
        if keys == "own":
            kstarts, kc, bias = qstarts, qc, bias_h[pi]
        elif keys == "prev+own":
            kstarts, kc, bias = [aligned(b + (n - 1) * qc) for b in bases], 2 * qc, bias_f[pi]
        else:
            kfirst = jnp.maximum(n - 1, 0) * qc
            kstarts, kc = [aligned(b + kfirst) for b in bases], 2 * qc
            bias = bias_f[jnp.where(n == 0, pi + n_pat, pi)]
        is_a = lax.broadcasted_iota(jnp.int32, (w, LANES), 1) < HEAD_DIM
        q = gather(qs, qstarts, qc)
        k = gather(ks, kstarts, kc).astype(BF16)
        v = gather(vs, kstarts, kc).astype(BF16)
        zero = jnp.zeros_like(q)
        q2 = jnp.concatenate([jnp.where(is_a, q, zero), jnp.where(is_a, zero, q)], axis=0).astype(BF16)
        s = lax.dot_general(q2, k, (((1,), (1,)), ((), ())), preferred_element_type=F32) + bias
        m = jnp.max(s, axis=-1, keepdims=True)
        p = jnp.exp2(s - m)
        l = jnp.sum(p, axis=-1, keepdims=True)
        o = jnp.dot(p.astype(BF16), v, preferred_element_type=F32)
        m_c = jnp.where(is_a, m[:w], m[w:])
        l_c = jnp.where(is_a, l[:w], l[w:])
        o_c = jnp.where(is_a, o[:w], o[w:])
        if first:
            scatter(m_s, qstarts, qc, m_c)
            scatter(l_s, qstarts, qc, l_c)
            scatter(acc_s, qstarts, qc, o_c)
            return
        m_o = gather(m_s, qstarts, qc)
        m_n = jnp.maximum(m_o, m_c)
        e_o = jnp.exp2(m_o - m_n)
        e_c = jnp.exp2(m_c - m_n)
        l_n = gather(l_s, qstarts, qc) * e_o + l_c * e_c
        a_n = gather(acc_s, qstarts, qc) * e_o + o_c * e_c
        if last:
            scatter(os_, qstarts, qc, a_n / l_n)
        else:
            scatter(m_s, qstarts, qc, m_n)
            scatter(l_s, qstarts, qc, l_n)
            scatter(acc_s, qstarts, qc, a_n)

    for pi, (window, d) in enumerate(DILATED_PATTERNS):
        assert window // d == w and CLASSES % d == 0
        first, last = pi == 0, pi == n_pat - 1
        nb = seq // (w * d)

        if nb >= 4:
            def any_block(i, carry, d=d, pi=pi, first=first, last=last):
                block(i % d, i // d, d, pi, first, last, "any")
                return carry
            lax.fori_loop(0, d * nb, any_block, 0, unroll=ATTN_UNROLL)
        else:
            def class_blocks(c, carry, d=d, pi=pi, first=first, last=last, nb=nb):
                block(c, 0, d, pi, first, last, "own")
                for n in range(1, nb):
                    block(c, n, d, pi, first, last, "prev+own")
                return carry
            lax.fori_loop(0, d, class_blocks, 0, unroll=max(ATTN_UNROLL // nb, 1))

    for r in range(CLASSES):
        o_ref[pl.ds(r, cl, stride=CLASSES), :] = os_[pl.ds(r * cl, cl), :]


def _neg_distance_tables():
    w = SUB_WINDOW
    prev, nxt, own = [], [], []
    for _, d in DILATED_PATTERNS:
        n_cls = CLASSES // d
        qc = w // n_cls
        def pos(chunk_rows):
            j, l = np.divmod(np.arange(n_cls * chunk_rows), chunk_rows)
            return n_cls * l + j
        sq, sk2, sk1 = pos(qc)[:, None], pos(2 * qc)[None, :], pos(qc)[None, :]
        for out, dist in ((prev, sq + w - sk2), (nxt, sq - sk2), (own, sq - sk1)):
            valid = (dist >= 0) & (dist <= w)
            out.append(np.where(valid, -(dist * d).astype(np.float32), np.float32(MASKED)))
    return jnp.asarray(np.stack(prev + nxt), dtype=F32), jnp.asarray(np.stack(own), dtype=F32)


def _attention(qkv, batch, seq):
    qkv4 = qkv.reshape(3 * N_PLANES, batch, seq, LANES)
    slopes = jnp.exp2(-8.0 * jnp.arange(1, N_HEADS + 1, dtype=F32) / N_HEADS)
    nd_full, nd_head = _neg_distance_tables()
    n_pat = len(DILATED_PATTERNS)
    plane = lambda off: pl.BlockSpec((None, None, seq, LANES), lambda b, h: (off + h, b, 0, 0))
    return pl.pallas_call(
        functools.partial(_attn_kernel, seq=seq),
        grid=(batch, N_PLANES),
        in_specs=[
            pl.BlockSpec(memory_space=pltpu.SMEM),
            plane(0), plane(N_PLANES), plane(2 * N_PLANES),
            pl.BlockSpec((2 * n_pat, SUB_WINDOW, 2 * SUB_WINDOW), lambda b, h: (0, 0, 0)),
            pl.BlockSpec((n_pat, SUB_WINDOW, SUB_WINDOW), lambda b, h: (0, 0, 0)),
        ],
        out_specs=pl.BlockSpec((None, None, seq, LANES), lambda b, h: (h, b, 0, 0)),
        scratch_shapes=[pltpu.VMEM((seq, LANES), F32)] * 4 + [
            pltpu.VMEM((2 * n_pat, 2 * SUB_WINDOW, 2 * SUB_WINDOW), F32),
            pltpu.VMEM((n_pat, 2 * SUB_WINDOW, SUB_WINDOW), F32),
        ],
        out_shape=jax.ShapeDtypeStruct((N_PLANES, batch, seq, LANES), F32),
        compiler_params=_params("parallel", "parallel"),
        name="dilated_attention",
    )(slopes, qkv4, qkv4, qkv4, nd_full, nd_head)


def _gelu_tanh(x):
    c = np.float32(np.sqrt(2.0 / np.pi))
    return 0.5 * x * (1.0 + jnp.tanh(c * (x + 0.044715 * (x * x * x))))


def _log1p(x):
    u = 1.0 + x
    return jnp.where(u == 1.0, x, jnp.log(u) * x / (u - 1.0))


def _softplus(z):
    return jnp.maximum(z, 0.0) + _log1p(jnp.exp(-jnp.abs(z)))


PHASES = 8


def _lru_kernel(xr_ref, xg_ref, cw_ref, cb_ref, wbd_ref, ba_ref, bx_ref, lam_ref, g_ref, y_ref,
                xc, hc, *, ts):
    i = pl.program_id(1)
    G = ts // PHASES
    first_group = lax.broadcasted_iota(jnp.int32, (G, LANES), 0) == 0
    sub = lax.broadcasted_iota(jnp.int32, (G, LANES), 0)

    @pl.when(i == 0)
    def _():
        xc[...] = jnp.zeros_like(xc)
        hc[...] = jnp.zeros_like(hc)

    def prev_group(cur, carry_row):
        return jnp.where(first_group, carry_row, pltpu.roll(cur, 1, 0))

    conv = []
    for l in range(N_PLANES):
        x = [xr_ref[l, pl.ds(s, G, stride=PHASES), :] for s in range(PHASES)]
        back = {s: prev_group(x[s], xc[l, s:s + 1, :]) for s in range(PHASES - CONV_WIDTH + 1, PHASES)}
        phases = []
        for s in range(PHASES):
            acc = jnp.broadcast_to(cb_ref[l:l + 1, :], (G, LANES))
            for j in range(CONV_WIDTH):
                q = s - (CONV_WIDTH - 1) + j
                acc = acc + (x[q] if q >= 0 else back[q + PHASES]) * cw_ref[j, l:l + 1, :]
            phases.append(acc)
        conv.append(jnp.concatenate(phases, axis=0))
        xc[l] = xr_ref[l, ts - PHASES:ts, :]

    gates_r, gates_i = [], []
    for j in range(N_PLANES // 2):
        yb = jnp.concatenate([conv[2 * j], conv[2 * j + 1]], axis=-1).astype(BF16)
        g = jnp.dot(yb, wbd_ref[j], preferred_element_type=F32)
        gates_r += [g[:, 0:LANES], g[:, LANES:2 * LANES]]
        gates_i += [g[:, 2 * LANES:3 * LANES], g[:, 3 * LANES:4 * LANES]]

    rec, sq = [], None
    for l in range(N_PLANES):
        y = conv[l]
        r = jax.nn.sigmoid(gates_r[l] + ba_ref[l:l + 1, :])
        ig = jax.nn.sigmoid(gates_i[l] + bx_ref[l:l + 1, :])
        log_a = (-RG_C * r) * _softplus(-lam_ref[l:l + 1, :])
        a = jnp.exp(log_a)
        z = 1.0 - a * a
        u = jnp.where(z > 0.0, z * lax.rsqrt(z), 0.0) * (ig * y)

        piece = lambda v, s: v[s * G:(s + 1) * G, :]
        pa, pb = [piece(a, 0)], [piece(u, 0)]
        for s in range(1, PHASES):
            pa.append(piece(a, s) * pa[-1])
            pb.append(piece(a, s) * pb[-1] + piece(u, s))
        ga, gb = pa[-1], pb[-1]
        sh = 1
        while sh < G:
            take = sub >= sh
            gb = jnp.where(take, ga * pltpu.roll(gb, sh, 0) + gb, gb)
            ga = jnp.where(take, ga * pltpu.roll(ga, sh, 0), ga)
            sh *= 2
        h0 = hc[l, 0:1, :]
        h_end = ga * h0 + gb
        h_in = prev_group(h_end, h0)
        hc[l] = jnp.broadcast_to(h_end[G - 1:G, :], (PHASES, LANES))
        h = jnp.concatenate([pa[s] * h_in + pb[s] for s in range(PHASES)], axis=0)

        xg = jnp.concatenate([xg_ref[l, pl.ds(s, G, stride=PHASES), :] for s in range(PHASES)], axis=0)
        rl = h * _gelu_tanh(xg)
        rec.append(rl)
        part = jnp.sum(rl * rl, axis=-1, keepdims=True)
        sq = part if sq is None else sq + part

    scale = lax.rsqrt(sq * (1.0 / D_LRU) + RMS_EPS)
    for l in range(N_PLANES):
        out = rec[l] * scale * g_ref[l:l + 1, :]
        for s in range(PHASES):
            y_ref[l, pl.ds(s, G, stride=PHASES), :] = out[s * G:(s + 1) * G, :]


def _block_diag_gates(wa, wx):
    def bd(wm):
        wm = wm.reshape(4, 4, LRU_BLOCK_DIM, LRU_BLOCK_DIM)
        eye = jnp.eye(4, dtype=wm.dtype)
        full = jnp.einsum("gide,ij->gidje", wm, eye)
        return full.reshape(4, 4 * LRU_BLOCK_DIM, 4 * LRU_BLOCK_DIM)
    return jnp.concatenate([bd(wa), bd(wx)], axis=-1).astype(BF16)


def _recurrent(rg, conv_w, conv_b, lru_wa, lru_ba, lru_wx, lru_bx, lru_lambda, lru_norm_g, batch, seq):
    ts = TS_LRU
    rg5 = rg.reshape(2, N_PLANES, batch, seq, LANES)
    wbd = _block_diag_gates(lru_wa, lru_wx)
    planes = lambda a: a.reshape(N_PLANES, LANES).astype(F32)
    vec = pl.BlockSpec((N_PLANES, LANES), lambda b, i: (0, 0))
    return pl.pallas_call(
        functools.partial(_lru_kernel, ts=ts),
        grid=(batch, seq // ts),
        in_specs=[
            pl.BlockSpec((None, N_PLANES, None, ts, LANES), lambda b, i: (0, 0, b, i, 0)),
            pl.BlockSpec((None, N_PLANES, None, ts, LANES), lambda b, i: (1, 0, b, i, 0)),
            pl.BlockSpec((CONV_WIDTH, N_PLANES, LANES), lambda b, i: (0, 0, 0)),
            vec,
            pl.BlockSpec((4, 4 * LRU_BLOCK_DIM, 8 * LRU_BLOCK_DIM), lambda b, i: (0, 0, 0)),
            vec, vec, vec, vec,
        ],
        out_specs=pl.BlockSpec((N_PLANES, None, ts, LANES), lambda b, i: (0, b, i, 0)),
        out_shape=jax.ShapeDtypeStruct((N_PLANES, batch, seq, LANES), F32),
        scratch_shapes=[
            pltpu.VMEM((N_PLANES, PHASES, LANES), F32),
            pltpu.VMEM((N_PLANES, PHASES, LANES), F32),
        ],
        compiler_params=_params("parallel", "arbitrary"),
        name="conv_rglru",
    )(rg5, rg5, conv_w.astype(F32).reshape(CONV_WIDTH, N_PLANES, LANES), planes(conv_b), wbd, planes(lru_ba),
      planes(lru_bx), planes(lru_lambda), planes(lru_norm_g))


def _layer_norm_rows(z, g, b):
    mu = jnp.mean(z, axis=-1, keepdims=True)
    zc = z - mu
    var = jnp.mean(zc * zc, axis=-1, keepdims=True)
    return zc * lax.rsqrt(var + LN_EPS) * g + b


U32 = jnp.uint32
HI_HALF = np.uint32(0xFFFF0000)


def _to_packed_token_major(ref, val, first_row, n_rows):
    bits = lambda a: lax.bitcast_convert_type(a.astype(BF16).astype(F32), U32)
    half = PACKED_CHUNKS * LANES
    for c in range(PACKED_CHUNKS):
        lo = bits(val[:, c * LANES:(c + 1) * LANES]) >> 16
        hi = bits(val[:, half + c * LANES:half + (c + 1) * LANES]) & HI_HALF
        ref[pl.ds(first_row * PACKED_CHUNKS + c, n_rows, stride=PACKED_CHUNKS), :] = lo | hi


def _from_packed_token_major(ref, first_row, n_rows):
    words = [ref[pl.ds(first_row * PACKED_CHUNKS + c, n_rows, stride=PACKED_CHUNKS), :]
             for c in range(PACKED_CHUNKS)]
    lo = [lax.bitcast_convert_type(wd << 16, F32) for wd in words]
    hi = [lax.bitcast_convert_type(wd & HI_HALF, F32) for wd in words]
    return jnp.concatenate(lo + hi, axis=-1)


def _out_router_kernel(*refs, tm):
    *io_refs, cnt_s, z_a, z_b = refs
    i = pl.program_id(0)

    @pl.when(i == 0)
    def _():
        cnt_s[...] = jnp.zeros_like(cnt_s)
        z_b[...] = jnp.zeros_like(z_b)

    @pl.when(i % 2 == 0)
    def _():
        _out_router_step(*io_refs, cnt_s, z_b, z_a, tm=tm)

    @pl.when(i % 2 == 1)
    def _():
        _out_router_step(*io_refs, cnt_s, z_a, z_b, tm=tm)


def _out_router_step(attn_ref, yrec_ref, x_ref, ga_ref, wo_ref, g1_ref, b1_ref, wr_ref, br_ref,
                     x1_ref, x1p_ref, pk_ref, rw_ref, cnt_ref, cnt_s, z_in, z_out, *, tm):
    i = pl.program_id(0)
    attn = jnp.concatenate([attn_ref[p] for p in range(N_PLANES)], axis=-1)
    ms = jnp.mean(attn * attn, axis=-1, keepdims=True)
    ya = (attn * lax.rsqrt(ms + RMS_EPS) * ga_ref[...]).astype(BF16)
    yr = jnp.concatenate([yrec_ref[p] for p in range(N_PLANES)], axis=-1).astype(BF16)
    n_col = 8
    cw = D_MODEL // n_col

    def project(j):
        cols = slice(j * cw, (j + 1) * cw)
        mix = jnp.dot(ya, wo_ref[0:D_ATTN, cols], preferred_element_type=F32)
        mix = mix + jnp.dot(yr, wo_ref[D_ATTN:, cols], preferred_element_type=F32)
        z_out[:, cols] = ALPHA * x_ref[:, cols] + mix

    def normalize(r0, n_rows):
        rows = slice(r0, r0 + n_rows)
        x1 = _layer_norm_rows(z_in[rows, :], g1_ref[...], b1_ref[...])
        x1_ref[rows, :] = x1
        _to_packed_token_major(x1p_ref, x1, r0, n_rows)
        return x1.astype(BF16)

    n_row = 4
    rh = tm // n_row
    parts = []
    for k in range(n_row):
        parts.append(normalize(k * rh, rh))
        project(k)

    xb = jnp.concatenate(parts, axis=0)
    logits = jnp.dot(xb, wr_ref[...].astype(BF16), preferred_element_type=F32) + br_ref[...]
    project(4)
    project(5)
    _route(logits, pk_ref, rw_ref, cnt_ref, cnt_s, i, tm)
    project(6)
    project(7)


def _route(logits, pk_ref, rw_ref, cnt_ref, cnt_s, i, tm):
    lane = lax.broadcasted_iota(jnp.int32, (tm, LANES), 1)
    big = jnp.int32(LANES)
    first_true = lambda c: jnp.min(jnp.where(c, lane, big), axis=-1, keepdims=True)

    in_g = lane < N_GROUPS
    gl = jnp.where(in_g, logits, MASKED)
    gmax = jnp.max(gl, axis=-1, keepdims=True)
    g_idx = first_true(gl == gmax)
    gsum = jnp.sum(jnp.where(in_g, jnp.exp(gl - gmax), 0.0), axis=-1, keepdims=True)
    g_gate = 1.0 / gsum

    lo = N_GROUPS + EXPERTS_PER_GROUP * g_idx
    in_e = (lane >= lo) & (lane < lo + EXPERTS_PER_GROUP)
    el = jnp.where(in_e, logits, MASKED)
    emax = jnp.max(el, axis=-1, keepdims=True)
    ee = jnp.where(in_e, jnp.exp(el - emax), 0.0)
    esum = jnp.sum(ee, axis=-1, keepdims=True)
    i1 = first_true(el == emax)
    rest = jnp.where(in_e & (lane != i1), ee, -1.0)
    e2max = jnp.max(rest, axis=-1, keepdims=True)
    i2 = first_true(rest == e2max)
    v1 = 1.0 / esum
    v2 = e2max / esum
    den = v1 + v2
    w1 = g_gate * v1 / den
    w2 = g_gate * v2 / den
    e1 = i1 - N_GROUPS
    e2 = i2 - N_GROUPS
    rw_ref[...] = jnp.where(lane == 0, w1, jnp.where(lane == 1, w2, 0.0))

    hot1 = lane == e1
    hot2 = lane == e2
    both = jnp.where((hot1 | hot2) & (i > 0), 1.0, 0.0)
    earlier = (lax.broadcasted_iota(jnp.int32, (tm, tm), 1) < lax.broadcasted_iota(jnp.int32, (tm, tm), 0))
    before = jnp.dot(earlier.astype(BF16), both.astype(BF16), preferred_element_type=F32) + cnt_s[0:1, :]
    r1 = jnp.sum(jnp.where(hot1, before, 0.0), axis=-1, keepdims=True).astype(jnp.int32)
    r2 = jnp.sum(jnp.where(hot2, before, 0.0), axis=-1, keepdims=True).astype(jnp.int32)
    words = jnp.where(lane == 0, e1 * RANK_SPAN + r1, jnp.where(lane == 1, e2 * RANK_SPAN + r2, 0))
    words_t = jnp.transpose(lax.bitcast_convert_type(words, F32))
    pk_ref[...] = lax.bitcast_convert_type(words_t[0:8, :], jnp.int32)
    cnt_s[...] = cnt_s[...] + jnp.sum(both, axis=0, keepdims=True)
    cnt_ref[...] = cnt_s[...].astype(jnp.int32)


def _out_router(attn, yrec, x2d, attn_norm_g, w_out_b, ln1_g, ln1_b, w_router, b_router):
    T = x2d.shape[0]
    tm = TM_OUT
    assert 2 * T <= RANK_SPAN
    attn3 = attn.reshape(N_PLANES, T, LANES)
    yrec3 = yrec.reshape(N_PLANES, T, LANES)
    const = lambda shape: pl.BlockSpec(shape, lambda i: (0,) * len(shape))
    n = T // tm
    stage1 = lambda i: jnp.minimum(i, n - 1)
    stage2 = lambda i: jnp.maximum(i - 1, 0)
    return pl.pallas_call(
        functools.partial(_out_router_kernel, tm=tm),
        grid=(n + 1,),
        in_specs=[
            pl.BlockSpec((N_PLANES, tm, LANES), lambda i: (0, stage1(i), 0)),
            pl.BlockSpec((N_PLANES, tm, LANES), lambda i: (0, stage1(i), 0)),
            pl.BlockSpec((tm, D_MODEL), lambda i: (stage1(i), 0)),
            const((1, D_ATTN)),
            pl.BlockSpec((D_MODEL, D_MODEL), lambda i: (0, 0), pipeline_mode=pl.Buffered(1)),
            const((1, D_MODEL)), const((1, D_MODEL)),
            const((D_MODEL, LANES)), const((1, LANES)),
        ],
        out_specs=[
            pl.BlockSpec((tm, D_MODEL), lambda i: (stage2(i), 0)),
            pl.BlockSpec((tm * PACKED_CHUNKS, LANES), lambda i: (stage2(i), 0)),
            pl.BlockSpec((8, tm), lambda i: (0, stage2(i))),
            pl.BlockSpec((tm, LANES), lambda i: (stage2(i), 0)),
            const((8, LANES)),
        ],
        out_shape=[
            jax.ShapeDtypeStruct((T, D_MODEL), F32),
            jax.ShapeDtypeStruct((T * PACKED_CHUNKS, LANES), U32),
            jax.ShapeDtypeStruct((8, T), jnp.int32),
            jax.ShapeDtypeStruct((T, LANES), F32),
            jax.ShapeDtypeStruct((8, LANES), jnp.int32),
        ],
        scratch_shapes=[pltpu.VMEM((8, LANES), F32), pltpu.VMEM((tm, D_MODEL), F32),
                        pltpu.VMEM((tm, D_MODEL), F32)],
        compiler_params=_params("arbitrary"),
        name="out_proj_router",
    )(attn3, yrec3, x2d, attn_norm_g.reshape(1, D_ATTN).astype(F32), w_out_b,
      ln1_g.reshape(1, D_MODEL).astype(F32), ln1_b.reshape(1, D_MODEL).astype(F32), w_router, b_router)


def _dispatch_kernel(slot_ref, pstart_ref, pend_ref, nused_ref, x1p_ref, xs_hbm, zbuf, sem, zsem,
                     *, tm, rows, n_blk):
    i = pl.program_id(0)
    n_tok = pl.num_programs(0) * tm
    slab = PACKED_CHUNKS
    blk = rows * slab

    @pl.when(i == 0)
    def _():
        zbuf[...] = jnp.zeros_like(zbuf)
        n_used = nused_ref[0]

        def zero_copy(b):
            return pltpu.make_async_copy(zbuf, xs_hbm.at[pl.ds(pl.multiple_of(b * blk, blk), blk), :], zsem)

        def for_each_zeroed_block(fn):
            for e in range(N_EXPERTS):
                @pl.when(pend_ref[e] > pstart_ref[e])
                def _(e=e):
                    fn(pend_ref[e] // rows - 1)

            def tail(b, c):
                fn(b)
                return c
            lax.fori_loop(n_used, n_blk, tail, 0)

        for_each_zeroed_block(lambda b: zero_copy(b).start())
        for_each_zeroed_block(lambda b: zero_copy(b).wait())

    group = 8

    def push(g, c):
        for t in range(group):
            src = x1p_ref.at[pl.ds(pl.multiple_of(g * (group * slab), group * slab) + t * slab, slab), :]
            for k in range(2):
                row = slot_ref[k * n_tok + i * tm + g * group + t] * slab
                dst = xs_hbm.at[pl.ds(pl.multiple_of(row, slab), slab), :]
                pltpu.make_async_copy(src, dst, sem).start(priority=k)
        return c
    lax.fori_loop(0, tm // group, push, 0)
    for _ in range(2):
        pltpu.make_async_copy(x1p_ref, xs_hbm.at[pl.ds(0, tm * slab), :], sem).wait()


def _dispatch(slot_flat, pad_start, pad_end, n_used, x1p, n_blk):
    tm, rows, slab = TM_DISP, MOE_ROWS, PACKED_CHUNKS
    n_tok = x1p.shape[0] // slab
    grid_spec = pltpu.PrefetchScalarGridSpec(
        num_scalar_prefetch=4,
        grid=(n_tok // tm,),
        in_specs=[pl.BlockSpec((tm * slab, LANES), lambda i, *_: (i, 0))],
        out_specs=pl.BlockSpec(memory_space=pl.ANY),
        scratch_shapes=[
            pltpu.VMEM((rows * slab, LANES), U32),
            pltpu.SemaphoreType.DMA(()),
            pltpu.SemaphoreType.DMA(()),
        ],
    )
    return pl.pallas_call(
        functools.partial(_dispatch_kernel, tm=tm, rows=rows, n_blk=n_blk),
        grid_spec=grid_spec,
        out_shape=jax.ShapeDtypeStruct((n_blk * rows * slab, LANES), U32),
        compiler_params=_params("arbitrary"),
        name="moe_dispatch",
    )(slot_flat, pad_start, pad_end, n_used, x1p)


def _expert_kernel(blk_e_ref, n_used_ref, run_ref, next_e_ref, xs_ref, w1_hbm, w3_hbm, w2_hbm, y_ref,
                   wf1, wf3, wf2, wsem, w1b, w3b, w2b, *, rows):
    i = pl.program_id(0)
    n_used = n_used_ref[0]
    e = blk_e_ref[i]
    slot = run_ref[i] % 2
    new_run = jnp.logical_or(i == 0, e != blk_e_ref[jnp.maximum(i - 1, 0)])

    def weight_copies(expert, slot_):
        return [pltpu.make_async_copy(src.at[expert], dst.at[slot_], wsem.at[slot_])
                for src, dst in ((w1_hbm, wf1), (w3_hbm, wf3), (w2_hbm, wf2))]

    @pl.when(i == 0)
    def _():
        for cp in weight_copies(e, 0):
            cp.start()

    @pl.when(jnp.logical_and(i < n_used, new_run))
    def _():
        for cp in weight_copies(e, slot):
            cp.wait()
        nxt = next_e_ref[i]

        @pl.when(nxt >= 0)
        def _():
            for cp in weight_copies(nxt, 1 - slot):
                cp.start(priority=1)

        w1b[...] = wf1[slot].astype(BF16)
        w3b[...] = wf3[slot].astype(BF16)
        w2b[...] = wf2[slot].astype(BF16)

    @pl.when(i < n_used)
    def _():
        xb = _from_packed_token_major(xs_ref, 0, rows).astype(BF16)
        h1 = jnp.dot(xb, w1b[...], preferred_element_type=F32)
        h3 = jnp.dot(xb, w3b[...], preferred_element_type=F32)
        h = (jax.nn.silu(h1) * h3).astype(BF16)
        _to_packed_token_major(y_ref, jnp.dot(h, w2b[...], preferred_element_type=F32), 0, rows)

    @pl.when(i >= n_used)
    def _():
        y_ref[...] = jnp.zeros_like(y_ref)


def _experts(blk_e, n_used, run_idx, next_e, xs, w1, w3, w2):
    rows = MOE_ROWS
    n_blk = blk_e.shape[0]
    hbm = pl.BlockSpec(memory_space=pl.ANY)
    grid_spec = pltpu.PrefetchScalarGridSpec(
        num_scalar_prefetch=4,
        grid=(n_blk,),
        in_specs=[
            pl.BlockSpec((rows * PACKED_CHUNKS, LANES), lambda i, be, nu, *_: (jnp.minimum(i, nu[0] - 1), 0)),
            hbm, hbm, hbm,
        ],
        out_specs=pl.BlockSpec((rows * PACKED_CHUNKS, LANES), lambda i, *_: (i, 0)),
        scratch_shapes=[
            pltpu.VMEM((2, D_MODEL, D_EXPERT), F32),
            pltpu.VMEM((2, D_MODEL, D_EXPERT), F32),
            pltpu.VMEM((2, D_EXPERT, D_MODEL), F32),
            pltpu.SemaphoreType.DMA((2,)),
            pltpu.VMEM((D_MODEL, D_EXPERT), BF16),
            pltpu.VMEM((D_MODEL, D_EXPERT), BF16),
            pltpu.VMEM((D_EXPERT, D_MODEL), BF16),
        ],
    )
    return pl.pallas_call(
        functools.partial(_expert_kernel, rows=rows),
        grid_spec=grid_spec,
        out_shape=jax.ShapeDtypeStruct((n_blk * rows * PACKED_CHUNKS, LANES), U32),
        compiler_params=_params("arbitrary"),
        name="moe_experts",
    )(blk_e, n_used, run_idx, next_e, xs, w1, w3, w2)


def _combine_kernel(slot_ref, ys_hbm, x1_ref, rw_ref, g2_ref, b2_ref, o_ref, yg, sem, *, tm):
    i = pl.program_id(0)
    n = pl.num_programs(0)
    slot = i % 2
    slab = PACKED_CHUNKS
    half = tm * slab
    group = 8

    def start_gather(tile, slot_):
        def body(g, c):
            base = pl.multiple_of(g * (group * slab), group * slab)
            for t in range(group):
                for k in range(2):
                    src_row = slot_ref[k * (n * tm) + tile * tm + g * group + t] * slab
                    pltpu.make_async_copy(
                        ys_hbm.at[pl.ds(pl.multiple_of(src_row, slab), slab), :],
                        yg.at[slot_, pl.ds(k * half + base + t * slab, slab), :],
                        sem.at[slot_]).start(priority=k)
            return c
        lax.fori_loop(0, tm // group, body, 0)

    @pl.when(i == 0)
    def _():
        start_gather(0, 0)

    @pl.when(i + 1 < n)
    def _():
        start_gather(i + 1, 1 - slot)

    pltpu.make_async_copy(ys_hbm.at[pl.ds(0, 2 * half), :], yg.at[slot], sem.at[slot]).wait()
    rw = rw_ref[...]
    buf = yg.at[slot]
    moe = (_from_packed_token_major(buf, 0, tm) * rw[:, 0:1]
           + _from_packed_token_major(buf, tm, tm) * rw[:, 1:2])
    o_ref[...] = _layer_norm_rows(ALPHA * x1_ref[...] + moe, g2_ref[...], b2_ref[...])


def _combine(slot_flat, ys, x1, rw, ln2_g, ln2_b):
    T = x1.shape[0]
    tm = TM_COMB
    grid_spec = pltpu.PrefetchScalarGridSpec(
        num_scalar_prefetch=1,
        grid=(T // tm,),
        in_specs=[
            pl.BlockSpec(memory_space=pl.ANY),
            pl.BlockSpec((tm, D_MODEL), lambda i, *_: (i, 0)),
            pl.BlockSpec((tm, LANES), lambda i, *_: (i, 0)),
            pl.BlockSpec((1, D_MODEL), lambda i, *_: (0, 0)),
            pl.BlockSpec((1, D_MODEL), lambda i, *_: (0, 0)),
        ],
        out_specs=pl.BlockSpec((tm, D_MODEL), lambda i, *_: (i, 0)),
        scratch_shapes=[
            pltpu.VMEM((2, 2 * tm * PACKED_CHUNKS, LANES), U32),
            pltpu.SemaphoreType.DMA((2,)),
        ],
    )
    return pl.pallas_call(
        functools.partial(_combine_kernel, tm=tm),
        grid_spec=grid_spec,
        out_shape=jax.ShapeDtypeStruct((T, D_MODEL), F32),
        compiler_params=_params("arbitrary"),
        name="moe_combine_ln",
    )(slot_flat, ys, x1, rw, ln2_g.reshape(1, D_MODEL).astype(F32), ln2_b.reshape(1, D_MODEL).astype(F32))


def _slot_kernel(pstart_ref, pk_ref, slot_ref):
    chunk = 16 * LANES
    for c0 in range(0, pk_ref.shape[1], chunk):
        word = pk_ref[:, c0:c0 + chunk]
        e = word >> RANK_BITS
        start = jnp.zeros_like(word)
        for ex in range(N_EXPERTS):
            start = jnp.where(e == ex, pstart_ref[ex], start)
        slot_ref[:, c0:c0 + chunk] = start + (word & (RANK_SPAN - 1))


def _slots(pad_start, pk):
    grid_spec = pltpu.PrefetchScalarGridSpec(
        num_scalar_prefetch=1,
        grid=(1,),
        in_specs=[pl.BlockSpec(pk.shape, lambda i, *_: (0, 0))],
        out_specs=pl.BlockSpec(pk.shape, lambda i, *_: (0, 0)),
    )
    return pl.pallas_call(
        _slot_kernel,
        grid_spec=grid_spec,
        out_shape=jax.ShapeDtypeStruct(pk.shape, jnp.int32),
        compiler_params=_params("arbitrary"),
        name="moe_slots",
    )(pad_start, pk)


def _dispatch_plan(counts, pk, n_tokens):
    rows = MOE_ROWS
    experts = jnp.arange(N_EXPERTS, dtype=jnp.int32)
    padded = (counts + rows - 1) // rows * rows
    pad_end = jnp.cumsum(padded).astype(jnp.int32)
    pad_start = (pad_end - padded).astype(jnp.int32)
    n_blk = 2 * n_tokens // rows + N_EXPERTS
    blk_start = jnp.arange(n_blk, dtype=jnp.int32) * rows
    blk_e = jnp.minimum(jnp.sum(blk_start[:, None] >= pad_end[None, :], axis=1), N_EXPERTS - 1).astype(jnp.int32)
    n_used = (pad_end[-1:] // rows).astype(jnp.int32)
    new_run = jnp.concatenate([jnp.ones((1,), jnp.int32), (blk_e[1:] != blk_e[:-1]).astype(jnp.int32)])
    run_idx = (jnp.cumsum(new_run) - 1).astype(jnp.int32)
    later_active = (experts[None, :] > experts[:, None]) & (counts[None, :] > 0)
    next_active = jnp.min(jnp.where(later_active, experts[None, :], N_EXPERTS), axis=1)
    next_active = jnp.where(next_active == N_EXPERTS, -1, next_active).astype(jnp.int32)
    next_e = next_active[blk_e]
    slot_flat = _slots(pad_start, pk)[:2].reshape(-1)
    return slot_flat, pad_start, pad_end, blk_e, n_used, run_idx, next_e, n_blk


def kernel(x, w_in, conv_w, conv_b, lru_wa, lru_ba, lru_wx, lru_bx, lru_lambda, attn_norm_g, lru_norm_g,
           w_out, ln1_g, ln1_b, router_grp_w, router_grp_b, router_exp_w, router_exp_b, w1, w3, w2,
           ln2_g, ln2_b):
    B, S, D = x.shape
    assert D == D_MODEL and S % DILATED_PATTERNS[-1][0] == 0 and w_in.shape[0] == 1
    T = B * S
    x2d = x.reshape(T, D)

    qkv, rg = _in_proj(x2d, w_in[0].astype(BF16), B, S)
    attn = _attention(qkv, B, S)
    yrec = _recurrent(rg, conv_w[0], conv_b[0], lru_wa[0], lru_ba[0], lru_wx[0], lru_bx[0], lru_lambda[0],
                      lru_norm_g[0], B, S)

    n_r = N_GROUPS + N_EXPERTS
    w_router = jnp.zeros((D, LANES), F32).at[:, :n_r].set(
        jnp.concatenate([router_grp_w[0], router_exp_w[0]], axis=-1).astype(F32))
    b_router = jnp.zeros((1, LANES), F32).at[0, :n_r].set(
        jnp.concatenate([router_grp_b[0], router_exp_b[0]], axis=-1).astype(F32))
    x1, x1p, pk, rw, cnt = _out_router(attn, yrec, x2d, attn_norm_g[0], w_out[0].astype(BF16),
                                       ln1_g[0], ln1_b[0], w_router, b_router)

    slot_flat, pad_start, pad_end, blk_e, n_used, run_idx, next_e, n_blk = _dispatch_plan(
        cnt[0, :N_EXPERTS], pk, T)
    xs = _dispatch(slot_flat, pad_start, pad_end, n_used, x1p, n_blk)
    ys = _experts(blk_e, n_used, run_idx, next_e, xs, w1[0], w3[0], w2[0])
    out = _combine(slot_flat, ys, x1, rw, ln2_g[0], ln2_b[0])
    return out.reshape(B, S, D)
```

```python
import functools

import jax
import jax.numpy as jnp
import numpy as np
from jax import lax
from jax.experimental import pallas as pl
from jax.experimental.pallas import tpu as pltpu

F32 = jnp.float32
BF16 = jnp.bfloat16

D_MODEL = 2048
D_ATTN = 1024
D_LRU = 1024
HEAD_DIM = 64
N_HEADS = 16
LANES = 128
N_PLANES = D_ATTN // LANES
DILATED_PATTERNS = ((128, 1), (512, 4), (2048, 16))
SUB_WINDOW = 128
LRU_BLOCKS = 16
LRU_BLOCK_DIM = 64
CONV_WIDTH = 4
RG_C = 8.0
N_GROUPS = 4
EXPERTS_PER_GROUP = 8
N_EXPERTS = 32
D_EXPERT = 512
ALPHA = 2.0 ** 0.25
LN_EPS = 1e-5
RMS_EPS = 1e-6
MASKED = -1e30
LOG2E = 1.4426950408889634

VMEM_LIMIT = 56 * 1024 * 1024

TM_PROJ = 256
TS_LRU = 512
TM_OUT = 256
MOE_ROWS = 256
TM_COMB = 256
TM_DISP = 2048
CHUNKS = D_MODEL // LANES
PACKED_CHUNKS = CHUNKS // 2
RANK_BITS = 16
RANK_SPAN = 1 << RANK_BITS
ATTN_UNROLL = 32


def _params(*sem):
    return pltpu.CompilerParams(dimension_semantics=sem, vmem_limit_bytes=VMEM_LIMIT)


def _in_proj_kernel(x_ref, w_ref, qkv_ref, rg_ref, *, tm):
    xb = x_ref[...].astype(BF16)
    per_class = tm // CLASSES
    out_row = lax.broadcasted_iota(jnp.int32, (tm, tm), 0)
    src_row = (out_row % per_class) * CLASSES + out_row // per_class
    perm = (lax.broadcasted_iota(jnp.int32, (tm, tm), 1) == src_row).astype(BF16)
    xb_cm = jnp.dot(perm, xb, preferred_element_type=F32).astype(BF16)
    for c in range(5):
        lhs = xb_cm if c < 3 else xb
        acc = jnp.dot(lhs, w_ref[:, c * D_ATTN:(c + 1) * D_ATTN], preferred_element_type=F32)
        if c == 0:
            acc = acc * (HEAD_DIM ** -0.5 * LOG2E)
        for p in range(N_PLANES):
            plane = acc[:, p * LANES:(p + 1) * LANES]
            if c < 3:
                for r in range(CLASSES):
                    qkv_ref[c * N_PLANES + p, r] = plane[r * per_class:(r + 1) * per_class, :]
            else:
                rg_ref[(c - 3) * N_PLANES + p] = plane


def _in_proj(x2d, w_in_b, batch, seq):
    T = x2d.shape[0]
    tm = TM_PROJ
    tiles = seq // tm
    assert tm % CLASSES == 0 and seq % tm == 0
    return pl.pallas_call(
        functools.partial(_in_proj_kernel, tm=tm),
        grid=(T // tm,),
        in_specs=[
            pl.BlockSpec((tm, D_MODEL), lambda i: (i, 0)),
            pl.BlockSpec((D_MODEL, 5 * D_ATTN), lambda i: (0, 0), pipeline_mode=pl.Buffered(1)),
        ],
        out_specs=[
            pl.BlockSpec((3 * N_PLANES, None, CLASSES, tm // CLASSES, LANES),
                         lambda i: (0, i // tiles, 0, i % tiles, 0)),
            pl.BlockSpec((2 * N_PLANES, tm, LANES), lambda i: (0, i, 0)),
        ],
        out_shape=[
            jax.ShapeDtypeStruct((3 * N_PLANES, batch, CLASSES, seq // CLASSES, LANES), F32),
            jax.ShapeDtypeStruct((2 * N_PLANES, T, LANES), F32),
        ],
        compiler_params=_params("parallel"),
        name="in_proj",
    )(x2d, w_in_b)


CLASSES = 16


def _attn_kernel(slopes_ref, qs, ks, vs, ndf_ref, ndh_ref, o_ref,
                 os_, m_s, l_s, acc_s, bias_f, bias_h, *, seq):
    hp = pl.program_id(1)
    slope_a = slopes_ref[2 * hp] * LOG2E
    slope_b = slopes_ref[2 * hp + 1] * LOG2E
    w = SUB_WINDOW
    cl = seq // CLASSES
    n_pat = len(DILATED_PATTERNS)

    def gather(ref, starts, size):
        return jnp.concatenate([ref[pl.ds(s, size), :] for s in starts], axis=0)

    def scatter(ref, starts, size, val):
        for c, s in enumerate(starts):
            ref[pl.ds(s, size), :] = val[c * size:(c + 1) * size, :]

    for t in range(2 * n_pat):
        bias_f[t, 0:w, :] = slope_a * ndf_ref[t]
        bias_f[t, w:2 * w, :] = slope_b * ndf_ref[t]
    for t in range(n_pat):
        bias_h[t, 0:w, :] = slope_a * ndh_ref[t]
        bias_h[t, w:2 * w, :] = slope_b * ndh_ref[t]

    def block(c, n, d, pi, first, last, keys):
        qc = w * d // CLASSES
        aligned = lambda x: x if isinstance(x, int) else pl.multiple_of(x, 8)
        bases = [(c + d * j) * cl for j in range(CLASSES // d)]
        qstarts = [aligned(b + n * qc) for b in bases]
        if keys == "own":
            kstarts, kc, bias = qstarts, qc, bias_h[pi]
        elif keys == "prev+own":
            kstarts, kc, bias = [aligned(b + (n - 1) * qc) for b in bases], 2 * qc, bias_f[pi]
        else:
            kfirst = jnp.maximum(n - 1, 0) * qc
            kstarts, kc = [aligned(b + kfirst) for b in bases], 2 * qc
            bias = bias_f[jnp.where(n == 0, pi + n_pat, pi)]
        is_a = lax.broadcasted_iota(jnp.int32, (w, LANES), 1) < HEAD_DIM
        q = gather(qs, qstarts, qc)
        k = gather(ks, kstarts, kc).astype(BF16)
        v = gather(vs, kstarts, kc).astype(BF16)
        zero = jnp.zeros_like(q)
        q2 = jnp.concatenate([jnp.where(is_a, q, zero), jnp.where(is_a, zero, q)], axis=0).astype(BF16)
        s = lax.dot_general(q2, k, (((1,), (1,)), ((), ())), preferred_element_type=F32) + bias
        m = jnp.max(s, axis=-1, keepdims=True)
        p = jnp.exp2(s - m)
        l = jnp.sum(p, axis=-1, keepdims=True)
        o = jnp.dot(p.astype(BF16), v, preferred_element_type=F32)
        m_c = jnp.where(is_a, m[:w], m[w:])
        l_c = jnp.where(is_a, l[:w], l[w:])
        o_c = jnp.where(is_a, o[:w], o[w:])
        if first:
            scatter(m_s, qstarts, qc, m_c)
            scatter(l_s, qstarts, qc, l_c)
            scatter(acc_s, qstarts, qc, o_c)
            return
        m_o = gather(m_s, qstarts, qc)
        m_n = jnp.maximum(m_o, m_c)
        e_o = jnp.exp2(m_o - m_n)
        e_c = jnp.exp2(m_c - m_n)
        l_n = gather(l_s, qstarts, qc) * e_o + l_c * e_c
        a_n = gather(acc_s, qstarts, qc) * e_o + o_c * e_c
        if last:
            scatter(os_, qstarts, qc, a_n / l_n)
        else:
            scatter(m_s, qstarts, qc, m_n)
            scatter(l_s, qstarts, qc, l_n)
            scatter(acc_s, qstarts, qc, a_n)

    for pi, (window, d) in enumerate(DILATED_PATTERNS):
        assert window // d == w and CLASSES % d == 0
        first, last = pi == 0, pi == n_pat - 1
        nb = seq // (w * d)

        if nb >= 4:
            def any_block(i, carry, d=d, pi=pi, first=first, last=last):
                block(i % d, i // d, d, pi, first, last, "any")
                return carry
            lax.fori_loop(0, d * nb, any_block, 0, unroll=ATTN_UNROLL)
        else:
            def class_blocks(c, carry, d=d, pi=pi, first=first, last=last, nb=nb):
                block(c, 0, d, pi, first, last, "own")
                for n in range(1, nb):
                    block(c, n, d, pi, first, last, "prev+own")
                return carry
            lax.fori_loop(0, d, class_blocks, 0, unroll=max(ATTN_UNROLL // nb, 1))

    for r in range(CLASSES):
        o_ref[pl.ds(r, cl, stride=CLASSES), :] = os_[pl.ds(r * cl, cl), :]


def _neg_distance_tables():
    w = SUB_WINDOW
    prev, nxt, own = [], [], []
    for _, d in DILATED_PATTERNS:
        n_cls = CLASSES // d
        qc = w // n_cls
        def pos(chunk_rows):
            j, l = np.divmod(np.arange(n_cls * chunk_rows), chunk_rows)
            return n_cls * l + j
        sq, sk2, sk1 = pos(qc)[:, None], pos(2 * qc)[None, :], pos(qc)[None, :]
        for out, dist in ((prev, sq + w - sk2), (nxt, sq - sk2), (own, sq - sk1)):
            valid = (dist >= 0) & (dist <= w)
            out.append(np.where(valid, -(dist * d).astype(np.float32), np.float32(MASKED)))
    return jnp.asarray(np.stack(prev + nxt), dtype=F32), jnp.asarray(np.stack(own), dtype=F32)


def _attention(qkv, batch, seq):
    qkv4 = qkv.reshape(3 * N_PLANES, batch, seq, LANES)
    slopes = jnp.exp2(-8.0 * jnp.arange(1, N_HEADS + 1, dtype=F32) / N_HEADS)
    nd_full, nd_head = _neg_distance_tables()
    n_pat = len(DILATED_PATTERNS)
    plane = lambda off: pl.BlockSpec((None, None, seq, LANES), lambda b, h: (off + h, b, 0, 0))
    return pl.pallas_call(
        functools.partial(_attn_kernel, seq=seq),
        grid=(batch, N_PLANES),
        in_specs=[
            pl.BlockSpec(memory_space=pltpu.SMEM),
            plane(0), plane(N_PLANES), plane(2 * N_PLANES),
            pl.BlockSpec((2 * n_pat, SUB_WINDOW, 2 * SUB_WINDOW), lambda b, h: (0, 0, 0)),
            pl.BlockSpec((n_pat, SUB_WINDOW, SUB_WINDOW), lambda b, h: (0, 0, 0)),
        ],
        out_specs=pl.BlockSpec((None, None, seq, LANES), lambda b, h: (h, b, 0, 0)),
        scratch_shapes=[pltpu.VMEM((seq, LANES), F32)] * 4 + [
            pltpu.VMEM((2 * n_pat, 2 * SUB_WINDOW, 2 * SUB_WINDOW), F32),
            pltpu.VMEM((n_pat, 2 * SUB_WINDOW, SUB_WINDOW), F32),
        ],
        out_shape=jax.ShapeDtypeStruct((N_PLANES, batch, seq, LANES), F32),
        compiler_params=_params("parallel", "parallel"),
        name="dilated_attention",
    )(slopes, qkv4, qkv4, qkv4, nd_full, nd_head)


def _gelu_tanh(x):
    c = np.float32(np.sqrt(2.0 / np.pi))
    return 0.5 * x * (1.0 + jnp.tanh(c * (x + 0.044715 * (x * x * x))))


def _log1p(x):
    u = 1.0 + x
    return jnp.where(u == 1.0, x, jnp.log(u) * x / (u - 1.0))


def _softplus(z):
    return jnp.maximum(z, 0.0) + _log1p(jnp.exp(-jnp.abs(z)))


PHASES = 8


def _lru_kernel(xr_ref, xg_ref, cw_ref, cb_ref, wbd_ref, ba_ref, bx_ref, lam_ref, g_ref, y_ref,
                xc, hc, *, ts):
    i = pl.program_id(1)
    G = ts // PHASES
    first_group = lax.broadcasted_iota(jnp.int32, (G, LANES), 0) == 0
    sub = lax.broadcasted_iota(jnp.int32, (G, LANES), 0)

    @pl.when(i == 0)
    def _():
        xc[...] = jnp.zeros_like(xc)
        hc[...] = jnp.zeros_like(hc)

    def prev_group(cur, carry_row):
        return jnp.where(first_group, carry_row, pltpu.roll(cur, 1, 0))

    conv = []
    for l in range(N_PLANES):
        x = [xr_ref[l, pl.ds(s, G, stride=PHASES), :] for s in range(PHASES)]
        back = {s: prev_group(x[s], xc[l, s:s + 1, :]) for s in range(PHASES - CONV_WIDTH + 1, PHASES)}
        phases = []
        for s in range(PHASES):
            acc = jnp.broadcast_to(cb_ref[l:l + 1, :], (G, LANES))
            for j in range(CONV_WIDTH):
                q = s - (CONV_WIDTH - 1) + j
                acc = acc + (x[q] if q >= 0 else back[q + PHASES]) * cw_ref[j, l:l + 1, :]
            phases.append(acc)
        conv.append(jnp.concatenate(phases, axis=0))
        xc[l] = xr_ref[l, ts - PHASES:ts, :]

    gates_r, gates_i = [], []
    for j in range(N_PLANES // 2):
        yb = jnp.concatenate([conv[2 * j], conv[2 * j + 1]], axis=-1).astype(BF16)
        g = jnp.dot(yb, wbd_ref[j], preferred_element_type=F32)
        gates_r += [g[:, 0:LANES], g[:, LANES:2 * LANES]]
        gates_i += [g[:, 2 * LANES:3 * LANES], g[:, 3 * LANES:4 * LANES]]

    rec, sq = [], None
    for l in range(N_PLANES):
        y = conv[l]
        r = jax.nn.sigmoid(gates_r[l] + ba_ref[l:l + 1, :])
        ig = jax.nn.sigmoid(gates_i[l] + bx_ref[l:l + 1, :])
        log_a = (-RG_C * r) * _softplus(-lam_ref[l:l + 1, :])
        a = jnp.exp(log_a)
        z = 1.0 - a * a
        u = jnp.where(z > 0.0, z * lax.rsqrt(z), 0.0) * (ig * y)

        piece = lambda v, s: v[s * G:(s + 1) * G, :]
        pa, pb = [piece(a, 0)], [piece(u, 0)]
        for s in range(1, PHASES):
            pa.append(piece(a, s) * pa[-1])
            pb.append(piece(a, s) * pb[-1] + piece(u, s))
        ga, gb = pa[-1], pb[-1]
        sh = 1
        while sh < G:
            take = sub >= sh
            gb = jnp.where(take, ga * pltpu.roll(gb, sh, 0) + gb, gb)
            ga = jnp.where(take, ga * pltpu.roll(ga, sh, 0), ga)
            sh *= 2
        h0 = hc[l, 0:1, :]
        h_end = ga * h0 + gb
        h_in = prev_group(h_end, h0)
        hc[l] = jnp.broadcast_to(h_end[G - 1:G, :], (PHASES, LANES))
        h = jnp.concatenate([pa[s] * h_in + pb[s] for s in range(PHASES)], axis=0)

        xg = jnp.concatenate([xg_ref[l, pl.ds(s, G, stride=PHASES), :] for s in range(PHASES)], axis=0)
        rl = h * _gelu_tanh(xg)
        rec.append(rl)
        part = jnp.sum(rl * rl, axis=-1, keepdims=True)
        sq = part if sq is None else sq + part

    scale = lax.rsqrt(sq * (1.0 / D_LRU) + RMS_EPS)
    for l in range(N_PLANES):
        out = rec[l] * scale * g_ref[l:l + 1, :]
        for s in range(PHASES):
            y_ref[l, pl.ds(s, G, stride=PHASES), :] = out[s * G:(s + 1) * G, :]


def _block_diag_gates(wa, wx):
    def bd(wm):
        wm = wm.reshape(4, 4, LRU_BLOCK_DIM, LRU_BLOCK_DIM)
        eye = jnp.eye(4, dtype=wm.dtype)
        full = jnp.einsum("gide,ij->gidje", wm, eye)
        return full.reshape(4, 4 * LRU_BLOCK_DIM, 4 * LRU_BLOCK_DIM)
    return jnp.concatenate([bd(wa), bd(wx)], axis=-1).astype(BF16)


def _recurrent(rg, conv_w, conv_b, lru_wa, lru_ba, lru_wx, lru_bx, lru_lambda, lru_norm_g, batch, seq):
    ts = TS_LRU
    rg5 = rg.reshape(2, N_PLANES, batch, seq, LANES)
    wbd = _block_diag_gates(lru_wa, lru_wx)
    planes = lambda a: a.reshape(N_PLANES, LANES).astype(F32)
    vec = pl.BlockSpec((N_PLANES, LANES), lambda b, i: (0, 0))
    return pl.pallas_call(
        functools.partial(_lru_kernel, ts=ts),
        grid=(batch, seq // ts),
        in_specs=[
            pl.BlockSpec((None, N_PLANES, None, ts, LANES), lambda b, i: (0, 0, b, i, 0)),
            pl.BlockSpec((None, N_PLANES, None, ts, LANES), lambda b, i: (1, 0, b, i, 0)),
            pl.BlockSpec((CONV_WIDTH, N_PLANES, LANES), lambda b, i: (0, 0, 0)),
            vec,
            pl.BlockSpec((4, 4 * LRU_BLOCK_DIM, 8 * LRU_BLOCK_DIM), lambda b, i: (0, 0, 0)),
            vec, vec, vec, vec,
        ],
        out_specs=pl.BlockSpec((N_PLANES, None, ts, LANES), lambda b, i: (0, b, i, 0)),
        out_shape=jax.ShapeDtypeStruct((N_PLANES, batch, seq, LANES), F32),
        scratch_shapes=[
            pltpu.VMEM((N_PLANES, PHASES, LANES), F32),
            pltpu.VMEM((N_PLANES, PHASES, LANES), F32),
        ],
        compiler_params=_params("parallel", "arbitrary"),
        name="conv_rglru",
    )(rg5, rg5, conv_w.astype(F32).reshape(CONV_WIDTH, N_PLANES, LANES), planes(conv_b), wbd, planes(lru_ba),
      planes(lru_bx), planes(lru_lambda), planes(lru_norm_g))


def _layer_norm_rows(z, g, b):
    mu = jnp.mean(z, axis=-1, keepdims=True)
    zc = z - mu
    var = jnp.mean(zc * zc, axis=-1, keepdims=True)
    return zc * lax.rsqrt(var + LN_EPS) * g + b


U32 = jnp.uint32
HI_HALF = np.uint32(0xFFFF0000)


def _to_packed_token_major(ref, val, first_row, n_rows):
    bits = lambda a: lax.bitcast_convert_type(a.astype(BF16).astype(F32), U32)
    half = PACKED_CHUNKS * LANES
    for c in range(PACKED_CHUNKS):
        lo = bits(val[:, c * LANES:(c + 1) * LANES]) >> 16
        hi = bits(val[:, half + c * LANES:half + (c + 1) * LANES]) & HI_HALF
        ref[pl.ds(first_row * PACKED_CHUNKS + c, n_rows, stride=PACKED_CHUNKS), :] = lo | hi


def _from_packed_token_major(ref, first_row, n_rows):
    words = [ref[pl.ds(first_row * PACKED_CHUNKS + c, n_rows, stride=PACKED_CHUNKS), :]
             for c in range(PACKED_CHUNKS)]
    lo = [lax.bitcast_convert_type(wd << 16, F32) for wd in words]
    hi = [lax.bitcast_convert_type(wd & HI_HALF, F32) for wd in words]
    return jnp.concatenate(lo + hi, axis=-1)


def _out_router_kernel(*refs, tm):
    *io_refs, cnt_s, z_a, z_b = refs
    i = pl.program_id(0)

    @pl.when(i == 0)
    def _():
        cnt_s[...] = jnp.zeros_like(cnt_s)
        z_b[...] = jnp.zeros_like(z_b)

    @pl.when(i % 2 == 0)
    def _():
        _out_router_step(*io_refs, cnt_s, z_b, z_a, tm=tm)

    @pl.when(i % 2 == 1)
    def _():
        _out_router_step(*io_refs, cnt_s, z_a, z_b, tm=tm)


def _out_router_step(attn_ref, yrec_ref, x_ref, ga_ref, wo_ref, g1_ref, b1_ref, wr_ref, br_ref,
                     x1_ref, x1p_ref, pk_ref, rw_ref, cnt_ref, cnt_s, z_in, z_out, *, tm):
    i = pl.program_id(0)
    attn = jnp.concatenate([attn_ref[p] for p in range(N_PLANES)], axis=-1)
    ms = jnp.mean(attn * attn, axis=-1, keepdims=True)
    ya = (attn * lax.rsqrt(ms + RMS_EPS) * ga_ref[...]).astype(BF16)
    yr = jnp.concatenate([yrec_ref[p] for p in range(N_PLANES)], axis=-1).astype(BF16)
    n_col = 8
    cw = D_MODEL // n_col

    def project(j):
        cols = slice(j * cw, (j + 1) * cw)
        mix = jnp.dot(ya, wo_ref[0:D_ATTN, cols], preferred_element_type=F32)
        mix = mix + jnp.dot(yr, wo_ref[D_ATTN:, cols], preferred_element_type=F32)
        z_out[:, cols] = ALPHA * x_ref[:, cols] + mix

    def normalize(r0, n_rows):
        rows = slice(r0, r0 + n_rows)
        x1 = _layer_norm_rows(z_in[rows, :], g1_ref[...], b1_ref[...])
        x1_ref[rows, :] = x1
        _to_packed_token_major(x1p_ref, x1, r0, n_rows)
        return x1.astype(BF16)

    n_row = 4
    rh = tm // n_row
    parts = []
    for k in range(n_row):
        parts.append(normalize(k * rh, rh))
        project(k)

    xb = jnp.concatenate(parts, axis=0)
    logits = jnp.dot(xb, wr_ref[...].astype(BF16), preferred_element_type=F32) + br_ref[...]
    project(4)
    project(5)
    _route(logits, pk_ref, rw_ref, cnt_ref, cnt_s, i, tm)
    project(6)
    project(7)


def _route(logits, pk_ref, rw_ref, cnt_ref, cnt_s, i, tm):
    lane = lax.broadcasted_iota(jnp.int32, (tm, LANES), 1)
    big = jnp.int32(LANES)
    first_true = lambda c: jnp.min(jnp.where(c, lane, big), axis=-1, keepdims=True)

    in_g = lane < N_GROUPS
    gl = jnp.where(in_g, logits, MASKED)
    gmax = jnp.max(gl, axis=-1, keepdims=True)
    g_idx = first_true(gl == gmax)
    gsum = jnp.sum(jnp.where(in_g, jnp.exp(gl - gmax), 0.0), axis=-1, keepdims=True)
    g_gate = 1.0 / gsum

    lo = N_GROUPS + EXPERTS_PER_GROUP * g_idx
    in_e = (lane >= lo) & (lane < lo + EXPERTS_PER_GROUP)
    el = jnp.where(in_e, logits, MASKED)
    emax = jnp.max(el, axis=-1, keepdims=True)
    ee = jnp.where(in_e, jnp.exp(el - emax), 0.0)
    esum = jnp.sum(ee, axis=-1, keepdims=True)
    i1 = first_true(el == emax)
    rest = jnp.where(in_e & (lane != i1), ee, -1.0)
    e2max = jnp.max(rest, axis=-1, keepdims=True)
    i2 = first_true(rest == e2max)
    v1 = 1.0 / esum
    v2 = e2max / esum
    den = v1 + v2
    w1 = g_gate * v1 / den
    w2 = g_gate * v2 / den
    e1 = i1 - N_GROUPS
    e2 = i2 - N_GROUPS
    rw_ref[...] = jnp.where(lane == 0, w1, jnp.where(lane == 1, w2, 0.0))

    hot1 = lane == e1
    hot2 = lane == e2
    both = jnp.where((hot1 | hot2) & (i > 0), 1.0, 0.0)
    earlier = (lax.broadcasted_iota(jnp.int32, (tm, tm), 1) < lax.broadcasted_iota(jnp.int32, (tm, tm), 0))
    before = jnp.dot(earlier.astype(BF16), both.astype(BF16), preferred_element_type=F32) + cnt_s[0:1, :]
    r1 = jnp.sum(jnp.where(hot1, before, 0.0), axis=-1, keepdims=True).astype(jnp.int32)
    r2 = jnp.sum(jnp.where(hot2, before, 0.0), axis=-1, keepdims=True).astype(jnp.int32)
    words = jnp.where(lane == 0, e1 * RANK_SPAN + r1, jnp.where(lane == 1, e2 * RANK_SPAN + r2, 0))
    words_t = jnp.transpose(lax.bitcast_convert_type(words, F32))
    pk_ref[...] = lax.bitcast_convert_type(words_t[0:8, :], jnp.int32)
    cnt_s[...] = cnt_s[...] + jnp.sum(both, axis=0, keepdims=True)
    cnt_ref[...] = cnt_s[...].astype(jnp.int32)


def _out_router(attn, yrec, x2d, attn_norm_g, w_out_b, ln1_g, ln1_b, w_router, b_router):
    T = x2d.shape[0]
    tm = TM_OUT
    assert 2 * T <= RANK_SPAN
    attn3 = attn.reshape(N_PLANES, T, LANES)
    yrec3 = yrec.reshape(N_PLANES, T, LANES)
    const = lambda shape: pl.BlockSpec(shape, lambda i: (0,) * len(shape))
    n = T // tm
    stage1 = lambda i: jnp.minimum(i, n - 1)
    stage2 = lambda i: jnp.maximum(i - 1, 0)
    return pl.pallas_call(
        functools.partial(_out_router_kernel, tm=tm),
        grid=(n + 1,),
        in_specs=[
            pl.BlockSpec((N_PLANES, tm, LANES), lambda i: (0, stage1(i), 0)),
            pl.BlockSpec((N_PLANES, tm, LANES), lambda i: (0, stage1(i), 0)),
            pl.BlockSpec((tm, D_MODEL), lambda i: (stage1(i), 0)),
            const((1, D_ATTN)),
            pl.BlockSpec((D_MODEL, D_MODEL), lambda i: (0, 0), pipeline_mode=pl.Buffered(1)),
            const((1, D_MODEL)), const((1, D_MODEL)),
            const((D_MODEL, LANES)), const((1, LANES)),
        ],
        out_specs=[
            pl.BlockSpec((tm, D_MODEL), lambda i: (stage2(i), 0)),
            pl.BlockSpec((tm * PACKED_CHUNKS, LANES), lambda i: (stage2(i), 0)),
            pl.BlockSpec((8, tm), lambda i: (0, stage2(i))),
            pl.BlockSpec((tm, LANES), lambda i: (stage2(i), 0)),
            const((8, LANES)),
        ],
        out_shape=[
            jax.ShapeDtypeStruct((T, D_MODEL), F32),
            jax.ShapeDtypeStruct((T * PACKED_CHUNKS, LANES), U32),
            jax.ShapeDtypeStruct((8, T), jnp.int32),
            jax.ShapeDtypeStruct((T, LANES), F32),
            jax.ShapeDtypeStruct((8, LANES), jnp.int32),
        ],
        scratch_shapes=[pltpu.VMEM((8, LANES), F32), pltpu.VMEM((tm, D_MODEL), F32),
                        pltpu.VMEM((tm, D_MODEL), F32)],
        compiler_params=_params("arbitrary"),
        name="out_proj_router",
    )(attn3, yrec3, x2d, attn_norm_g.reshape(1, D_ATTN).astype(F32), w_out_b,
      ln1_g.reshape(1, D_MODEL).astype(F32), ln1_b.reshape(1, D_MODEL).astype(F32), w_router, b_router)


def _dispatch_kernel(slot_ref, pstart_ref, pend_ref, nused_ref, x1p_ref, xs_hbm, zbuf, sem, zsem,
                     *, tm, rows, n_blk):
    i = pl.program_id(0)
    n_tok = pl.num_programs(0) * tm
    slab = PACKED_CHUNKS
    blk = rows * slab

    @pl.when(i == 0)
    def _():
        zbuf[...] = jnp.zeros_like(zbuf)
        n_used = nused_ref[0]

        def zero_copy(b):
            return pltpu.make_async_copy(zbuf, xs_hbm.at[pl.ds(pl.multiple_of(b * blk, blk), blk), :], zsem)

        def for_each_zeroed_block(fn):
            for e in range(N_EXPERTS):
                @pl.when(pend_ref[e] > pstart_ref[e])
                def _(e=e):
                    fn(pend_ref[e] // rows - 1)

            def tail(b, c):
                fn(b)
                return c
            lax.fori_loop(n_used, n_blk, tail, 0)

        for_each_zeroed_block(lambda b: zero_copy(b).start())
        for_each_zeroed_block(lambda b: zero_copy(b).wait())

    group = 8

    def push(g, c):
        for t in range(group):
            src = x1p_ref.at[pl.ds(pl.multiple_of(g * (group * slab), group * slab) + t * slab, slab), :]
            for k in range(2):
                row = slot_ref[k * n_tok + i * tm + g * group + t] * slab
                dst = xs_hbm.at[pl.ds(pl.multiple_of(row, slab), slab), :]
                pltpu.make_async_copy(src, dst, sem).start(priority=k)
        return c
    lax.fori_loop(0, tm // group, push, 0)
    for _ in range(2):
        pltpu.make_async_copy(x1p_ref, xs_hbm.at[pl.ds(0, tm * slab), :], sem).wait()


def _dispatch(slot_flat, pad_start, pad_end, n_used, x1p, n_blk):
    tm, rows, slab = TM_DISP, MOE_ROWS, PACKED_CHUNKS
    n_tok = x1p.shape[0] // slab
    grid_spec = pltpu.PrefetchScalarGridSpec(
        num_scalar_prefetch=4,
        grid=(n_tok // tm,),
        in_specs=[pl.BlockSpec((tm * slab, LANES), lambda i, *_: (i, 0))],
        out_specs=pl.BlockSpec(memory_space=pl.ANY),
        scratch_shapes=[
            pltpu.VMEM((rows * slab, LANES), U32),
            pltpu.SemaphoreType.DMA(()),
            pltpu.SemaphoreType.DMA(()),
        ],
    )
    return pl.pallas_call(
        functools.partial(_dispatch_kernel, tm=tm, rows=rows, n_blk=n_blk),
        grid_spec=grid_spec,
        out_shape=jax.ShapeDtypeStruct((n_blk * rows * slab, LANES), U32),
        compiler_params=_params("arbitrary"),
        name="moe_dispatch",
    )(slot_flat, pad_start, pad_end, n_used, x1p)


def _expert_kernel(blk_e_ref, n_used_ref, run_ref, next_e_ref, xs_ref, w1_hbm, w3_hbm, w2_hbm, y_ref,
                   wf1, wf3, wf2, wsem, w1b, w3b, w2b, *, rows):
    i = pl.program_id(0)
    n_used = n_used_ref[0]
    e = blk_e_ref[i]
    slot = run_ref[i] % 2
    new_run = jnp.logical_or(i == 0, e != blk_e_ref[jnp.maximum(i - 1, 0)])

    def weight_copies(expert, slot_):
        return [pltpu.make_async_copy(src.at[expert], dst.at[slot_], wsem.at[slot_])
                for src, dst in ((w1_hbm, wf1), (w3_hbm, wf3), (w2_hbm, wf2))]

    @pl.when(i == 0)
    def _():
        for cp in weight_copies(e, 0):
            cp.start()

    @pl.when(jnp.logical_and(i < n_used, new_run))
    def _():
        for cp in weight_copies(e, slot):
            cp.wait()
        nxt = next_e_ref[i]

        @pl.when(nxt >= 0)
        def _():
            for cp in weight_copies(nxt, 1 - slot):
                cp.start(priority=1)

        w1b[...] = wf1[slot].astype(BF16)
        w3b[...] = wf3[slot].astype(BF16)
        w2b[...] = wf2[slot].astype(BF16)

    @pl.when(i < n_used)
    def _():
        xb = _from_packed_token_major(xs_ref, 0, rows).astype(BF16)
        h1 = jnp.dot(xb, w1b[...], preferred_element_type=F32)
        h3 = jnp.dot(xb, w3b[...], preferred_element_type=F32)
        h = (jax.nn.silu(h1) * h3).astype(BF16)
        _to_packed_token_major(y_ref, jnp.dot(h, w2b[...], preferred_element_type=F32), 0, rows)

    @pl.when(i >= n_used)
    def _():
        y_ref[...] = jnp.zeros_like(y_ref)


def _experts(blk_e, n_used, run_idx, next_e, xs, w1, w3, w2):
    rows = MOE_ROWS
    n_blk = blk_e.shape[0]
    hbm = pl.BlockSpec(memory_space=pl.ANY)
    grid_spec = pltpu.PrefetchScalarGridSpec(
        num_scalar_prefetch=4,
        grid=(n_blk,),
        in_specs=[
            pl.BlockSpec((rows * PACKED_CHUNKS, LANES), lambda i, be, nu, *_: (jnp.minimum(i, nu[0] - 1), 0)),
            hbm, hbm, hbm,
        ],
        out_specs=pl.BlockSpec((rows * PACKED_CHUNKS, LANES), lambda i, *_: (i, 0)),
        scratch_shapes=[
            pltpu.VMEM((2, D_MODEL, D_EXPERT), F32),
            pltpu.VMEM((2, D_MODEL, D_EXPERT), F32),
            pltpu.VMEM((2, D_EXPERT, D_MODEL), F32),
            pltpu.SemaphoreType.DMA((2,)),
            pltpu.VMEM((D_MODEL, D_EXPERT), BF16),
            pltpu.VMEM((D_MODEL, D_EXPERT), BF16),
            pltpu.VMEM((D_EXPERT, D_MODEL), BF16),
        ],
    )
    return pl.pallas_call(
        functools.partial(_expert_kernel, rows=rows),
        grid_spec=grid_spec,
        out_shape=jax.ShapeDtypeStruct((n_blk * rows * PACKED_CHUNKS, LANES), U32),
        compiler_params=_params("arbitrary"),
        name="moe_experts",
    )(blk_e, n_used, run_idx, next_e, xs, w1, w3, w2)


def _combine_kernel(slot_ref, ys_hbm, x1_ref, rw_ref, g2_ref, b2_ref, o_ref, yg, sem, *, tm):
    i = pl.program_id(0)
    n = pl.num_programs(0)
    slot = i % 2
    slab = PACKED_CHUNKS
    half = tm * slab
    group = 8

    def start_gather(tile, slot_):
        def body(g, c):
            base = pl.multiple_of(g * (group * slab), group * slab)
            for t in range(group):
                for k in range(2):
                    src_row = slot_ref[k * (n * tm) + tile * tm + g * group + t] * slab
                    pltpu.make_async_copy(
                        ys_hbm.at[pl.ds(pl.multiple_of(src_row, slab), slab), :],
                        yg.at[slot_, pl.ds(k * half + base + t * slab, slab), :],
                        sem.at[slot_]).start(priority=k)
            return c
        lax.fori_loop(0, tm // group, body, 0)

    @pl.when(i == 0)
    def _():
        start_gather(0, 0)

    @pl.when(i + 1 < n)
    def _():
        start_gather(i + 1, 1 - slot)

    pltpu.make_async_copy(ys_hbm.at[pl.ds(0, 2 * half), :], yg.at[slot], sem.at[slot]).wait()
    rw = rw_ref[...]
    buf = yg.at[slot]
    moe = (_from_packed_token_major(buf, 0, tm) * rw[:, 0:1]
           + _from_packed_token_major(buf, tm, tm) * rw[:, 1:2])
    o_ref[...] = _layer_norm_rows(ALPHA * x1_ref[...] + moe, g2_ref[...], b2_ref[...])


def _combine(slot_flat, ys, x1, rw, ln2_g, ln2_b):
    T = x1.shape[0]
    tm = TM_COMB
    grid_spec = pltpu.PrefetchScalarGridSpec(
        num_scalar_prefetch=1,
        grid=(T // tm,),
        in_specs=[
            pl.BlockSpec(memory_space=pl.ANY),
            pl.BlockSpec((tm, D_MODEL), lambda i, *_: (i, 0)),
            pl.BlockSpec((tm, LANES), lambda i, *_: (i, 0)),
            pl.BlockSpec((1, D_MODEL), lambda i, *_: (0, 0)),
            pl.BlockSpec((1, D_MODEL), lambda i, *_: (0, 0)),
        ],
        out_specs=pl.BlockSpec((tm, D_MODEL), lambda i, *_: (i, 0)),
        scratch_shapes=[
            pltpu.VMEM((2, 2 * tm * PACKED_CHUNKS, LANES), U32),
            pltpu.SemaphoreType.DMA((2,)),
        ],
    )
    return pl.pallas_call(
        functools.partial(_combine_kernel, tm=tm),
        grid_spec=grid_spec,
        out_shape=jax.ShapeDtypeStruct((T, D_MODEL), F32),
        compiler_params=_params("arbitrary"),
        name="moe_combine_ln",
    )(slot_flat, ys, x1, rw, ln2_g.reshape(1, D_MODEL).astype(F32), ln2_b.reshape(1, D_MODEL).astype(F32))


def _dispatch_plan(counts, pk_flat, n_tokens):
    rows = MOE_ROWS
    experts = jnp.arange(N_EXPERTS, dtype=jnp.int32)
    padded = (counts + rows - 1) // rows * rows
    pad_end = jnp.cumsum(padded).astype(jnp.int32)
    pad_start = (pad_end - padded).astype(jnp.int32)
    n_blk = 2 * n_tokens // rows + N_EXPERTS
    blk_start = jnp.arange(n_blk, dtype=jnp.int32) * rows
    blk_e = jnp.minimum(jnp.sum(blk_start[:, None] >= pad_end[None, :], axis=1), N_EXPERTS - 1).astype(jnp.int32)
    n_used = (pad_end[-1:] // rows).astype(jnp.int32)
    new_run = jnp.concatenate([jnp.ones((1,), jnp.int32), (blk_e[1:] != blk_e[:-1]).astype(jnp.int32)])
    run_idx = (jnp.cumsum(new_run) - 1).astype(jnp.int32)
    later_active = (experts[None, :] > experts[:, None]) & (counts[None, :] > 0)
    next_active = jnp.min(jnp.where(later_active, experts[None, :], N_EXPERTS), axis=1)
    next_active = jnp.where(next_active == N_EXPERTS, -1, next_active).astype(jnp.int32)
    next_e = jnp.sum(jnp.where(blk_e[:, None] == experts[None, :], next_active[None, :], 0), axis=1)
    e_flat = pk_flat >> RANK_BITS
    start_of = jnp.sum(jnp.where(e_flat[None, :] == experts[:, None], pad_start[:, None], 0), axis=0)
    slot_flat = (start_of + (pk_flat & (RANK_SPAN - 1))).astype(jnp.int32)
    return slot_flat, pad_start, pad_end, blk_e, n_used, run_idx, next_e, n_blk


def kernel(x, w_in, conv_w, conv_b, lru_wa, lru_ba, lru_wx, lru_bx, lru_lambda, attn_norm_g, lru_norm_g,
           w_out, ln1_g, ln1_b, router_grp_w, router_grp_b, router_exp_w, router_exp_b, w1, w3, w2,
           ln2_g, ln2_b):
    B, S, D = x.shape
    assert D == D_MODEL and S % DILATED_PATTERNS[-1][0] == 0 and w_in.shape[0] == 1
    T = B * S
    x2d = x.reshape(T, D)

    qkv, rg = _in_proj(x2d, w_in[0].astype(BF16), B, S)
    attn = _attention(qkv, B, S)
    yrec = _recurrent(rg, conv_w[0], conv_b[0], lru_wa[0], lru_ba[0], lru_wx[0], lru_bx[0], lru_lambda[0],
                      lru_norm_g[0], B, S)

    n_r = N_GROUPS + N_EXPERTS
    w_router = jnp.zeros((D, LANES), F32).at[:, :n_r].set(
        jnp.concatenate([router_grp_w[0], router_exp_w[0]], axis=-1).astype(F32))
    b_router = jnp.zeros((1, LANES), F32).at[0, :n_r].set(
        jnp.concatenate([router_grp_b[0], router_exp_b[0]], axis=-1).astype(F32))
    x1, x1p, pk, rw, cnt = _out_router(attn, yrec, x2d, attn_norm_g[0], w_out[0].astype(BF16),
                                       ln1_g[0], ln1_b[0], w_router, b_router)

    pk_flat = pk[:2].reshape(-1)
    slot_flat, pad_start, pad_end, blk_e, n_used, run_idx, next_e, n_blk = _dispatch_plan(
        cnt[0, :N_EXPERTS], pk_flat, T)
    xs = _dispatch(slot_flat, pad_start, pad_end, n_used, x1p, n_blk)
    ys = _experts(blk_e, n_used, run_idx, next_e, xs, w1[0], w3[0], w2[0])
    out = _combine(slot_flat, ys, x1, rw, ln2_g[0], ln2_b[0])
    return out.reshape(B, S, D)
```

```python
import functools

import jax
import jax.numpy as jnp
import numpy as np
from jax import lax
from jax.experimental import pallas as pl
from jax.experimental.pallas import tpu as pltpu

F32 = jnp.float32
BF16 = jnp.bfloat16

D_MODEL = 2048
D_ATTN = 1024
D_LRU = 1024
HEAD_DIM = 64
N_HEADS = 16
LANES = 128
N_PLANES = D_ATTN // LANES
DILATED_PATTERNS = ((128, 1), (512, 4), (2048, 16))
SUB_WINDOW = 128
LRU_BLOCKS = 16
LRU_BLOCK_DIM = 64
CONV_WIDTH = 4
RG_C = 8.0
N_GROUPS = 4
EXPERTS_PER_GROUP = 8
N_EXPERTS = 32
D_EXPERT = 512
ALPHA = 2.0 ** 0.25
LN_EPS = 1e-5
RMS_EPS = 1e-6
MASKED = -1e30
LOG2E = 1.4426950408889634

VMEM_LIMIT = 56 * 1024 * 1024

TM_PROJ = 256
TS_LRU = 512
TM_OUT = 256
MOE_ROWS = 256
TM_COMB = 256
TM_DISP = 2048
CHUNKS = D_MODEL // LANES
PACKED_CHUNKS = CHUNKS // 2
RANK_BITS = 16
RANK_SPAN = 1 << RANK_BITS
ATTN_UNROLL = 32


def _params(*sem):
    return pltpu.CompilerParams(dimension_semantics=sem, vmem_limit_bytes=VMEM_LIMIT)


def _in_proj_kernel(x_ref, w_ref, qkv_ref, rg_ref, *, tm):
    xb = x_ref[...].astype(BF16)
    per_class = tm // CLASSES
    out_row = lax.broadcasted_iota(jnp.int32, (tm, tm), 0)
    src_row = (out_row % per_class) * CLASSES + out_row // per_class
    perm = (lax.broadcasted_iota(jnp.int32, (tm, tm), 1) == src_row).astype(BF16)
    xb_cm = jnp.dot(perm, xb, preferred_element_type=F32).astype(BF16)
    for c in range(5):
        lhs = xb_cm if c < 3 else xb
        acc = jnp.dot(lhs, w_ref[:, c * D_ATTN:(c + 1) * D_ATTN], preferred_element_type=F32)
        if c == 0:
            acc = acc * (HEAD_DIM ** -0.5 * LOG2E)
        for p in range(N_PLANES):
            plane = acc[:, p * LANES:(p + 1) * LANES]
            if c < 3:
                for r in range(CLASSES):
                    qkv_ref[c * N_PLANES + p, 0, r] = plane[r * per_class:(r + 1) * per_class, :]
            else:
                rg_ref[(c - 3) * N_PLANES + p] = plane


def _in_proj_outer(x_hbm, w_hbm, qkv_hbm, rg_hbm, w_vmem, wsem, *, tm, tiles, n_tiles):
    cp = pltpu.make_async_copy(w_hbm, w_vmem, wsem)
    cp.start()
    cp.wait()

    def body(x_ref, qkv_ref, rg_ref):
        _in_proj_kernel(x_ref, w_vmem, qkv_ref, rg_ref, tm=tm)

    pltpu.emit_pipeline(
        body,
        grid=(n_tiles,),
        in_specs=[pl.BlockSpec((tm, D_MODEL), lambda i: (i, 0))],
        out_specs=[
            pl.BlockSpec((3 * N_PLANES, 1, CLASSES, tm // CLASSES, LANES),
                         lambda i: (0, i // tiles, 0, i % tiles, 0)),
            pl.BlockSpec((2 * N_PLANES, tm, LANES), lambda i: (0, i, 0)),
        ],
    )(x_hbm, qkv_hbm, rg_hbm)


def _in_proj(x2d, w_in_b, batch, seq):
    T = x2d.shape[0]
    tm = TM_PROJ
    tiles = seq // tm
    assert tm % CLASSES == 0 and seq % tm == 0
    hbm = pl.BlockSpec(memory_space=pl.ANY)
    return pl.pallas_call(
        functools.partial(_in_proj_outer, tm=tm, tiles=tiles, n_tiles=T // tm),
        in_specs=[hbm, hbm],
        out_specs=[hbm, hbm],
        out_shape=[
            jax.ShapeDtypeStruct((3 * N_PLANES, batch, CLASSES, seq // CLASSES, LANES), F32),
            jax.ShapeDtypeStruct((2 * N_PLANES, T, LANES), F32),
        ],
        scratch_shapes=[pltpu.VMEM((D_MODEL, 5 * D_ATTN), BF16), pltpu.SemaphoreType.DMA(())],
        compiler_params=pltpu.CompilerParams(vmem_limit_bytes=VMEM_LIMIT),
        name="in_proj",
    )(x2d, w_in_b)


CLASSES = 16


def _attn_kernel(slopes_ref, qs, ks, vs, ndf_ref, ndh_ref, o_ref,
                 os_, m_s, l_s, acc_s, bias_f, bias_h, *, seq):
    hp = pl.program_id(1)
    slope_a = slopes_ref[2 * hp] * LOG2E
    slope_b = slopes_ref[2 * hp + 1] * LOG2E
    w = SUB_WINDOW
    cl = seq // CLASSES
    n_pat = len(DILATED_PATTERNS)

    def gather(ref, starts, size):
        return jnp.concatenate([ref[pl.ds(s, size), :] for s in starts], axis=0)

    def scatter(ref, starts, size, val):
        for c, s in enumerate(starts):
            ref[pl.ds(s, size), :] = val[c * size:(c + 1) * size, :]

    for t in range(2 * n_pat):
        bias_f[t, 0:w, :] = slope_a * ndf_ref[t]
        bias_f[t, w:2 * w, :] = slope_b * ndf_ref[t]
    for t in range(n_pat):
        bias_h[t, 0:w, :] = slope_a * ndh_ref[t]
        bias_h[t, w:2 * w, :] = slope_b * ndh_ref[t]

    def block(c, n, d, pi, first, last, keys):
        qc = w * d // CLASSES
        aligned = lambda x: x if isinstance(x, int) else pl.multiple_of(x, 8)
        bases = [(c + d * j) * cl for j in range(CLASSES // d)]
        qstarts = [aligned(b + n * qc) for b in bases]
        if keys == "own":
            kstarts, kc, bias = qstarts, qc, bias_h[pi]
        elif keys == "prev+own":
            kstarts, kc, bias = [aligned(b + (n - 1) * qc) for b in bases], 2 * qc, bias_f[pi]
        else:
            kfirst = jnp.maximum(n - 1, 0) * qc
            kstarts, kc = [aligned(b + kfirst) for b in bases], 2 * qc
            bias = bias_f[jnp.where(n == 0, pi + n_pat, pi)]
        is_a = lax.broadcasted_iota(jnp.int32, (w, LANES), 1) < HEAD_DIM
        q = gather(qs, qstarts, qc)
        k = gather(ks, kstarts, kc).astype(BF16)
        v = gather(vs, kstarts, kc).astype(BF16)
        zero = jnp.zeros_like(q)
        q2 = jnp.concatenate([jnp.where(is_a, q, zero), jnp.where(is_a, zero, q)], axis=0).astype(BF16)
        s = lax.dot_general(q2, k, (((1,), (1,)), ((), ())), preferred_element_type=F32) + bias
        m = jnp.max(s, axis=-1, keepdims=True)
        p = jnp.exp2(s - m)
        l = jnp.sum(p, axis=-1, keepdims=True)
        o = jnp.dot(p.astype(BF16), v, preferred_element_type=F32)
        m_c = jnp.where(is_a, m[:w], m[w:])
        l_c = jnp.where(is_a, l[:w], l[w:])
        o_c = jnp.where(is_a, o[:w], o[w:])
        if first:
            scatter(m_s, qstarts, qc, m_c)
            scatter(l_s, qstarts, qc, l_c)
            scatter(acc_s, qstarts, qc, o_c)
            return
        m_o = gather(m_s, qstarts, qc)
        m_n = jnp.maximum(m_o, m_c)
        e_o = jnp.exp2(m_o - m_n)
        e_c = jnp.exp2(m_c - m_n)
        l_n = gather(l_s, qstarts, qc) * e_o + l_c * e_c
        a_n = gather(acc_s, qstarts, qc) * e_o + o_c * e_c
        if last:
            scatter(os_, qstarts, qc, a_n / l_n)
        else:
            scatter(m_s, qstarts, qc, m_n)
            scatter(l_s, qstarts, qc, l_n)
            scatter(acc_s, qstarts, qc, a_n)

    for pi, (window, d) in enumerate(DILATED_PATTERNS):
        assert window // d == w and CLASSES % d == 0
        first, last = pi == 0, pi == n_pat - 1
        nb = seq // (w * d)

        if nb >= 4:
            def any_block(i, carry, d=d, pi=pi, first=first, last=last):
                block(i % d, i // d, d, pi, first, last, "any")
                return carry
            lax.fori_loop(0, d * nb, any_block, 0, unroll=ATTN_UNROLL)
        else:
            def class_blocks(c, carry, d=d, pi=pi, first=first, last=last, nb=nb):
                block(c, 0, d, pi, first, last, "own")
                for n in range(1, nb):
                    block(c, n, d, pi, first, last, "prev+own")
                return carry
            lax.fori_loop(0, d, class_blocks, 0, unroll=max(ATTN_UNROLL // nb, 1))

    for r in range(CLASSES):
        o_ref[pl.ds(r, cl, stride=CLASSES), :] = os_[pl.ds(r * cl, cl), :]


def _neg_distance_tables():
    w = SUB_WINDOW
    prev, nxt, own = [], [], []
    for _, d in DILATED_PATTERNS:
        n_cls = CLASSES // d
        qc = w // n_cls
        def pos(chunk_rows):
            j, l = np.divmod(np.arange(n_cls * chunk_rows), chunk_rows)
            return n_cls * l + j
        sq, sk2, sk1 = pos(qc)[:, None], pos(2 * qc)[None, :], pos(qc)[None, :]
        for out, dist in ((prev, sq + w - sk2), (nxt, sq - sk2), (own, sq - sk1)):
            valid = (dist >= 0) & (dist <= w)
            out.append(np.where(valid, -(dist * d).astype(np.float32), np.float32(MASKED)))
    return jnp.asarray(np.stack(prev + nxt), dtype=F32), jnp.asarray(np.stack(own), dtype=F32)


def _attention(qkv, batch, seq):
    qkv4 = qkv.reshape(3 * N_PLANES, batch, seq, LANES)
    slopes = jnp.exp2(-8.0 * jnp.arange(1, N_HEADS + 1, dtype=F32) / N_HEADS)
    nd_full, nd_head = _neg_distance_tables()
    n_pat = len(DILATED_PATTERNS)
    plane = lambda off: pl.BlockSpec((None, None, seq, LANES), lambda b, h: (off + h, b, 0, 0))
    return pl.pallas_call(
        functools.partial(_attn_kernel, seq=seq),
        grid=(batch, N_PLANES),
        in_specs=[
            pl.BlockSpec(memory_space=pltpu.SMEM),
            plane(0), plane(N_PLANES), plane(2 * N_PLANES),
            pl.BlockSpec((2 * n_pat, SUB_WINDOW, 2 * SUB_WINDOW), lambda b, h: (0, 0, 0)),
            pl.BlockSpec((n_pat, SUB_WINDOW, SUB_WINDOW), lambda b, h: (0, 0, 0)),
        ],
        out_specs=pl.BlockSpec((None, None, seq, LANES), lambda b, h: (h, b, 0, 0)),
        scratch_shapes=[pltpu.VMEM((seq, LANES), F32)] * 4 + [
            pltpu.VMEM((2 * n_pat, 2 * SUB_WINDOW, 2 * SUB_WINDOW), F32),
            pltpu.VMEM((n_pat, 2 * SUB_WINDOW, SUB_WINDOW), F32),
        ],
        out_shape=jax.ShapeDtypeStruct((N_PLANES, batch, seq, LANES), F32),
        compiler_params=_params("parallel", "parallel"),
        name="dilated_attention",
    )(slopes, qkv4, qkv4, qkv4, nd_full, nd_head)


def _gelu_tanh(x):
    c = np.float32(np.sqrt(2.0 / np.pi))
    return 0.5 * x * (1.0 + jnp.tanh(c * (x + 0.044715 * (x * x * x))))


def _log1p(x):
    u = 1.0 + x
    return jnp.where(u == 1.0, x, jnp.log(u) * x / (u - 1.0))


def _softplus(z):
    return jnp.maximum(z, 0.0) + _log1p(jnp.exp(-jnp.abs(z)))


PHASES = 8


def _lru_kernel(xr_ref, xg_ref, cw_ref, cb_ref, wbd_ref, ba_ref, bx_ref, lam_ref, g_ref, y_ref,
                xc, hc, *, ts):
    i = pl.program_id(1)
    G = ts // PHASES
    first_group = lax.broadcasted_iota(jnp.int32, (G, LANES), 0) == 0
    sub = lax.broadcasted_iota(jnp.int32, (G, LANES), 0)

    @pl.when(i == 0)
    def _():
        xc[...] = jnp.zeros_like(xc)
        hc[...] = jnp.zeros_like(hc)

    def prev_group(cur, carry_row):
        return jnp.where(first_group, carry_row, pltpu.roll(cur, 1, 0))

    conv = []
    for l in range(N_PLANES):
        x = [xr_ref[l, pl.ds(s, G, stride=PHASES), :] for s in range(PHASES)]
        back = {s: prev_group(x[s], xc[l, s:s + 1, :]) for s in range(PHASES - CONV_WIDTH + 1, PHASES)}
        phases = []
        for s in range(PHASES):
            acc = jnp.broadcast_to(cb_ref[l:l + 1, :], (G, LANES))
            for j in range(CONV_WIDTH):
                q = s - (CONV_WIDTH - 1) + j
                acc = acc + (x[q] if q >= 0 else back[q + PHASES]) * cw_ref[j, l:l + 1, :]
            phases.append(acc)
        conv.append(jnp.concatenate(phases, axis=0))
        xc[l] = xr_ref[l, ts - PHASES:ts, :]

    gates_r, gates_i = [], []
    for j in range(N_PLANES // 2):
        yb = jnp.concatenate([conv[2 * j], conv[2 * j + 1]], axis=-1).astype(BF16)
        g = jnp.dot(yb, wbd_ref[j], preferred_element_type=F32)
        gates_r += [g[:, 0:LANES], g[:, LANES:2 * LANES]]
        gates_i += [g[:, 2 * LANES:3 * LANES], g[:, 3 * LANES:4 * LANES]]

    rec, sq = [], None
    for l in range(N_PLANES):
        y = conv[l]
        r = jax.nn.sigmoid(gates_r[l] + ba_ref[l:l + 1, :])
        ig = jax.nn.sigmoid(gates_i[l] + bx_ref[l:l + 1, :])
        log_a = (-RG_C * r) * _softplus(-lam_ref[l:l + 1, :])
        a = jnp.exp(log_a)
        z = 1.0 - a * a
        u = jnp.where(z > 0.0, z * lax.rsqrt(z), 0.0) * (ig * y)

        piece = lambda v, s: v[s * G:(s + 1) * G, :]
        pa, pb = [piece(a, 0)], [piece(u, 0)]
        for s in range(1, PHASES):
            pa.append(piece(a, s) * pa[-1])
            pb.append(piece(a, s) * pb[-1] + piece(u, s))
        ga, gb = pa[-1], pb[-1]
        sh = 1
        while sh < G:
            take = sub >= sh
            gb = jnp.where(take, ga * pltpu.roll(gb, sh, 0) + gb, gb)
            ga = jnp.where(take, ga * pltpu.roll(ga, sh, 0), ga)
            sh *= 2
        h0 = hc[l, 0:1, :]
        h_end = ga * h0 + gb
        h_in = prev_group(h_end, h0)
        hc[l] = jnp.broadcast_to(h_end[G - 1:G, :], (PHASES, LANES))
        h = jnp.concatenate([pa[s] * h_in + pb[s] for s in range(PHASES)], axis=0)

        xg = jnp.concatenate([xg_ref[l, pl.ds(s, G, stride=PHASES), :] for s in range(PHASES)], axis=0)
        rl = h * _gelu_tanh(xg)
        rec.append(rl)
        part = jnp.sum(rl * rl, axis=-1, keepdims=True)
        sq = part if sq is None else sq + part

    scale = lax.rsqrt(sq * (1.0 / D_LRU) + RMS_EPS)
    for l in range(N_PLANES):
        out = rec[l] * scale * g_ref[l:l + 1, :]
        for s in range(PHASES):
            y_ref[l, pl.ds(s, G, stride=PHASES), :] = out[s * G:(s + 1) * G, :]


def _block_diag_gates(wa, wx):
    def bd(wm):
        wm = wm.reshape(4, 4, LRU_BLOCK_DIM, LRU_BLOCK_DIM)
        eye = jnp.eye(4, dtype=wm.dtype)
        full = jnp.einsum("gide,ij->gidje", wm, eye)
        return full.reshape(4, 4 * LRU_BLOCK_DIM, 4 * LRU_BLOCK_DIM)
    return jnp.concatenate([bd(wa), bd(wx)], axis=-1).astype(BF16)


def _recurrent(rg, conv_w, conv_b, lru_wa, lru_ba, lru_wx, lru_bx, lru_lambda, lru_norm_g, batch, seq):
    ts = TS_LRU
    rg5 = rg.reshape(2, N_PLANES, batch, seq, LANES)
    wbd = _block_diag_gates(lru_wa, lru_wx)
    planes = lambda a: a.reshape(N_PLANES, LANES).astype(F32)
    vec = pl.BlockSpec((N_PLANES, LANES), lambda b, i: (0, 0))
    return pl.pallas_call(
        functools.partial(_lru_kernel, ts=ts),
        grid=(batch, seq // ts),
        in_specs=[
            pl.BlockSpec((None, N_PLANES, None, ts, LANES), lambda b, i: (0, 0, b, i, 0)),
            pl.BlockSpec((None, N_PLANES, None, ts, LANES), lambda b, i: (1, 0, b, i, 0)),
            pl.BlockSpec((CONV_WIDTH, N_PLANES, LANES), lambda b, i: (0, 0, 0)),
            vec,
            pl.BlockSpec((4, 4 * LRU_BLOCK_DIM, 8 * LRU_BLOCK_DIM), lambda b, i: (0, 0, 0)),
            vec, vec, vec, vec,
        ],
        out_specs=pl.BlockSpec((N_PLANES, None, ts, LANES), lambda b, i: (0, b, i, 0)),
        out_shape=jax.ShapeDtypeStruct((N_PLANES, batch, seq, LANES), F32),
        scratch_shapes=[
            pltpu.VMEM((N_PLANES, PHASES, LANES), F32),
            pltpu.VMEM((N_PLANES, PHASES, LANES), F32),
        ],
        compiler_params=_params("parallel", "arbitrary"),
        name="conv_rglru",
    )(rg5, rg5, conv_w.astype(F32).reshape(CONV_WIDTH, N_PLANES, LANES), planes(conv_b), wbd, planes(lru_ba),
      planes(lru_bx), planes(lru_lambda), planes(lru_norm_g))


def _layer_norm_rows(z, g, b):
    mu = jnp.mean(z, axis=-1, keepdims=True)
    zc = z - mu
    var = jnp.mean(zc * zc, axis=-1, keepdims=True)
    return zc * lax.rsqrt(var + LN_EPS) * g + b


U32 = jnp.uint32
HI_HALF = np.uint32(0xFFFF0000)


def _to_packed_token_major(ref, val, first_row, n_rows):
    bits = lambda a: lax.bitcast_convert_type(a.astype(BF16).astype(F32), U32)
    half = PACKED_CHUNKS * LANES
    for c in range(PACKED_CHUNKS):
        lo = bits(val[:, c * LANES:(c + 1) * LANES]) >> 16
        hi = bits(val[:, half + c * LANES:half + (c + 1) * LANES]) & HI_HALF
        ref[pl.ds(first_row * PACKED_CHUNKS + c, n_rows, stride=PACKED_CHUNKS), :] = lo | hi


def _from_packed_token_major(ref, first_row, n_rows):
    words = [ref[pl.ds(first_row * PACKED_CHUNKS + c, n_rows, stride=PACKED_CHUNKS), :]
             for c in range(PACKED_CHUNKS)]
    lo = [lax.bitcast_convert_type(wd << 16, F32) for wd in words]
    hi = [lax.bitcast_convert_type(wd & HI_HALF, F32) for wd in words]
    return jnp.concatenate(lo + hi, axis=-1)


def _out_router_kernel(*refs, tm):
    *io_refs, cnt_s, z_a, z_b = refs
    i = pl.program_id(0)

    @pl.when(i == 0)
    def _():
        cnt_s[...] = jnp.zeros_like(cnt_s)
        z_b[...] = jnp.zeros_like(z_b)

    @pl.when(i % 2 == 0)
    def _():
        _out_router_step(*io_refs, cnt_s, z_b, z_a, tm=tm)

    @pl.when(i % 2 == 1)
    def _():
        _out_router_step(*io_refs, cnt_s, z_a, z_b, tm=tm)


def _out_router_step(attn_ref, yrec_ref, x_ref, ga_ref, wo_ref, g1_ref, b1_ref, wr_ref, br_ref,
                     x1_ref, x1p_ref, pk_ref, rw_ref, cnt_ref, cnt_s, z_in, z_out, *, tm):
    i = pl.program_id(0)
    attn = jnp.concatenate([attn_ref[p] for p in range(N_PLANES)], axis=-1)
    ms = jnp.mean(attn * attn, axis=-1, keepdims=True)
    ya = (attn * lax.rsqrt(ms + RMS_EPS) * ga_ref[...]).astype(BF16)
    yr = jnp.concatenate([yrec_ref[p] for p in range(N_PLANES)], axis=-1).astype(BF16)
    n_col = 8
    cw = D_MODEL // n_col

    def project(j):
        cols = slice(j * cw, (j + 1) * cw)
        mix = jnp.dot(ya, wo_ref[0:D_ATTN, cols], preferred_element_type=F32)
        mix = mix + jnp.dot(yr, wo_ref[D_ATTN:, cols], preferred_element_type=F32)
        z_out[:, cols] = ALPHA * x_ref[:, cols] + mix

    def normalize(r0, n_rows):
        rows = slice(r0, r0 + n_rows)
        x1 = _layer_norm_rows(z_in[rows, :], g1_ref[...], b1_ref[...])
        x1_ref[rows, :] = x1
        _to_packed_token_major(x1p_ref, x1, r0, n_rows)
        return x1.astype(BF16)

    n_row = 4
    rh = tm // n_row
    parts = []
    for k in range(n_row):
        parts.append(normalize(k * rh, rh))
        project(k)

    xb = jnp.concatenate(parts, axis=0)
    logits = jnp.dot(xb, wr_ref[...].astype(BF16), preferred_element_type=F32) + br_ref[...]
    project(4)
    project(5)
    _route(logits, pk_ref, rw_ref, cnt_ref, cnt_s, i, tm)
    project(6)
    project(7)


def _route(logits, pk_ref, rw_ref, cnt_ref, cnt_s, i, tm):
    lane = lax.broadcasted_iota(jnp.int32, (tm, LANES), 1)
    big = jnp.int32(LANES)
    first_true = lambda c: jnp.min(jnp.where(c, lane, big), axis=-1, keepdims=True)

    in_g = lane < N_GROUPS
    gl = jnp.where(in_g, logits, MASKED)
    gmax = jnp.max(gl, axis=-1, keepdims=True)
    g_idx = first_true(gl == gmax)
    gsum = jnp.sum(jnp.where(in_g, jnp.exp(gl - gmax), 0.0), axis=-1, keepdims=True)
    g_gate = 1.0 / gsum

    lo = N_GROUPS + EXPERTS_PER_GROUP * g_idx
    in_e = (lane >= lo) & (lane < lo + EXPERTS_PER_GROUP)
    el = jnp.where(in_e, logits, MASKED)
    emax = jnp.max(el, axis=-1, keepdims=True)
    ee = jnp.where(in_e, jnp.exp(el - emax), 0.0)
    esum = jnp.sum(ee, axis=-1, keepdims=True)
    i1 = first_true(el == emax)
    rest = jnp.where(in_e & (lane != i1), ee, -1.0)
    e2max = jnp.max(rest, axis=-1, keepdims=True)
    i2 = first_true(rest == e2max)
    v1 = 1.0 / esum
    v2 = e2max / esum
    den = v1 + v2
    w1 = g_gate * v1 / den
    w2 = g_gate * v2 / den
    e1 = i1 - N_GROUPS
    e2 = i2 - N_GROUPS
    rw_ref[...] = jnp.where(lane == 0, w1, jnp.where(lane == 1, w2, 0.0))

    hot1 = lane == e1
    hot2 = lane == e2
    both = jnp.where((hot1 | hot2) & (i > 0), 1.0, 0.0)
    earlier = (lax.broadcasted_iota(jnp.int32, (tm, tm), 1) < lax.broadcasted_iota(jnp.int32, (tm, tm), 0))
    before = jnp.dot(earlier.astype(BF16), both.astype(BF16), preferred_element_type=F32) + cnt_s[0:1, :]
    r1 = jnp.sum(jnp.where(hot1, before, 0.0), axis=-1, keepdims=True).astype(jnp.int32)
    r2 = jnp.sum(jnp.where(hot2, before, 0.0), axis=-1, keepdims=True).astype(jnp.int32)
    words = jnp.where(lane == 0, e1 * RANK_SPAN + r1, jnp.where(lane == 1, e2 * RANK_SPAN + r2, 0))
    words_t = jnp.transpose(lax.bitcast_convert_type(words, F32))
    pk_ref[...] = lax.bitcast_convert_type(words_t[0:8, :], jnp.int32)
    cnt_s[...] = cnt_s[...] + jnp.sum(both, axis=0, keepdims=True)
    cnt_ref[...] = cnt_s[...].astype(jnp.int32)


def _out_router(attn, yrec, x2d, attn_norm_g, w_out_b, ln1_g, ln1_b, w_router, b_router):
    T = x2d.shape[0]
    tm = TM_OUT
    assert 2 * T <= RANK_SPAN
    attn3 = attn.reshape(N_PLANES, T, LANES)
    yrec3 = yrec.reshape(N_PLANES, T, LANES)
    const = lambda shape: pl.BlockSpec(shape, lambda i: (0,) * len(shape))
    n = T // tm
    stage1 = lambda i: jnp.minimum(i, n - 1)
    stage2 = lambda i: jnp.maximum(i - 1, 0)
    return pl.pallas_call(
        functools.partial(_out_router_kernel, tm=tm),
        grid=(n + 1,),
        in_specs=[
            pl.BlockSpec((N_PLANES, tm, LANES), lambda i: (0, stage1(i), 0)),
            pl.BlockSpec((N_PLANES, tm, LANES), lambda i: (0, stage1(i), 0)),
            pl.BlockSpec((tm, D_MODEL), lambda i: (stage1(i), 0)),
            const((1, D_ATTN)),
            pl.BlockSpec((D_MODEL, D_MODEL), lambda i: (0, 0), pipeline_mode=pl.Buffered(1)),
            const((1, D_MODEL)), const((1, D_MODEL)),
            const((D_MODEL, LANES)), const((1, LANES)),
        ],
        out_specs=[
            pl.BlockSpec((tm, D_MODEL), lambda i: (stage2(i), 0)),
            pl.BlockSpec((tm * PACKED_CHUNKS, LANES), lambda i: (stage2(i), 0)),
            pl.BlockSpec((8, tm), lambda i: (0, stage2(i))),
            pl.BlockSpec((tm, LANES), lambda i: (stage2(i), 0)),
            const((8, LANES)),
        ],
        out_shape=[
            jax.ShapeDtypeStruct((T, D_MODEL), F32),
            jax.ShapeDtypeStruct((T * PACKED_CHUNKS, LANES), U32),
            jax.ShapeDtypeStruct((8, T), jnp.int32),
            jax.ShapeDtypeStruct((T, LANES), F32),
            jax.ShapeDtypeStruct((8, LANES), jnp.int32),
        ],
        scratch_shapes=[pltpu.VMEM((8, LANES), F32), pltpu.VMEM((tm, D_MODEL), F32),
                        pltpu.VMEM((tm, D_MODEL), F32)],
        compiler_params=_params("arbitrary"),
        name="out_proj_router",
    )(attn3, yrec3, x2d, attn_norm_g.reshape(1, D_ATTN).astype(F32), w_out_b,
      ln1_g.reshape(1, D_MODEL).astype(F32), ln1_b.reshape(1, D_MODEL).astype(F32), w_router, b_router)


def _dispatch_kernel(slot_ref, pstart_ref, pend_ref, nused_ref, x1p_ref, xs_hbm, zbuf, sem, zsem,
                     *, tm, rows, n_blk):
    i = pl.program_id(0)
    n_tok = pl.num_programs(0) * tm
    slab = PACKED_CHUNKS
    blk = rows * slab

    @pl.when(i == 0)
    def _():
        zbuf[...] = jnp.zeros_like(zbuf)
        n_used = nused_ref[0]

        def zero_copy(b):
            return pltpu.make_async_copy(zbuf, xs_hbm.at[pl.ds(pl.multiple_of(b * blk, blk), blk), :], zsem)

        def for_each_zeroed_block(fn):
            for e in range(N_EXPERTS):
                @pl.when(pend_ref[e] > pstart_ref[e])
                def _(e=e):
                    fn(pend_ref[e] // rows - 1)

            def tail(b, c):
                fn(b)
                return c
            lax.fori_loop(n_used, n_blk, tail, 0)

        for_each_zeroed_block(lambda b: zero_copy(b).start())
        for_each_zeroed_block(lambda b: zero_copy(b).wait())

    group = 8

    def push(g, c):
        for t in range(group):
            src = x1p_ref.at[pl.ds(pl.multiple_of(g * (group * slab), group * slab) + t * slab, slab), :]
            for k in range(2):
                row = slot_ref[k * n_tok + i * tm + g * group + t] * slab
                dst = xs_hbm.at[pl.ds(pl.multiple_of(row, slab), slab), :]
                pltpu.make_async_copy(src, dst, sem).start(priority=k)
        return c
    lax.fori_loop(0, tm // group, push, 0)
    for _ in range(2):
        pltpu.make_async_copy(x1p_ref, xs_hbm.at[pl.ds(0, tm * slab), :], sem).wait()


def _dispatch(slot_flat, pad_start, pad_end, n_used, x1p, n_blk):
    tm, rows, slab = TM_DISP, MOE_ROWS, PACKED_CHUNKS
    n_tok = x1p.shape[0] // slab
    grid_spec = pltpu.PrefetchScalarGridSpec(
        num_scalar_prefetch=4,
        grid=(n_tok // tm,),
        in_specs=[pl.BlockSpec((tm * slab, LANES), lambda i, *_: (i, 0))],
        out_specs=pl.BlockSpec(memory_space=pl.ANY),
        scratch_shapes=[
            pltpu.VMEM((rows * slab, LANES), U32),
            pltpu.SemaphoreType.DMA(()),
            pltpu.SemaphoreType.DMA(()),
        ],
    )
    return pl.pallas_call(
        functools.partial(_dispatch_kernel, tm=tm, rows=rows, n_blk=n_blk),
        grid_spec=grid_spec,
        out_shape=jax.ShapeDtypeStruct((n_blk * rows * slab, LANES), U32),
        compiler_params=_params("arbitrary"),
        name="moe_dispatch",
    )(slot_flat, pad_start, pad_end, n_used, x1p)


def _expert_kernel(blk_e_ref, n_used_ref, run_ref, next_e_ref, xs_ref, w1_hbm, w3_hbm, w2_hbm, y_ref,
                   wf1, wf3, wf2, wsem, w1b, w3b, w2b, *, rows):
    i = pl.program_id(0)
    n_used = n_used_ref[0]
    e = blk_e_ref[i]
    slot = run_ref[i] % 2
    new_run = jnp.logical_or(i == 0, e != blk_e_ref[jnp.maximum(i - 1, 0)])

    def weight_copies(expert, slot_):
        return [pltpu.make_async_copy(src.at[expert], dst.at[slot_], wsem.at[slot_])
                for src, dst in ((w1_hbm, wf1), (w3_hbm, wf3), (w2_hbm, wf2))]

    @pl.when(i == 0)
    def _():
        for cp in weight_copies(e, 0):
            cp.start()

    @pl.when(jnp.logical_and(i < n_used, new_run))
    def _():
        for cp in weight_copies(e, slot):
            cp.wait()
        nxt = next_e_ref[i]

        @pl.when(nxt >= 0)
        def _():
            for cp in weight_copies(nxt, 1 - slot):
                cp.start(priority=1)

        w1b[...] = wf1[slot].astype(BF16)
        w3b[...] = wf3[slot].astype(BF16)
        w2b[...] = wf2[slot].astype(BF16)

    @pl.when(i < n_used)
    def _():
        xb = _from_packed_token_major(xs_ref, 0, rows).astype(BF16)
        h1 = jnp.dot(xb, w1b[...], preferred_element_type=F32)
        h3 = jnp.dot(xb, w3b[...], preferred_element_type=F32)
        h = (jax.nn.silu(h1) * h3).astype(BF16)
        _to_packed_token_major(y_ref, jnp.dot(h, w2b[...], preferred_element_type=F32), 0, rows)

    @pl.when(i >= n_used)
    def _():
        y_ref[...] = jnp.zeros_like(y_ref)


def _experts(blk_e, n_used, run_idx, next_e, xs, w1, w3, w2):
    rows = MOE_ROWS
    n_blk = blk_e.shape[0]
    hbm = pl.BlockSpec(memory_space=pl.ANY)
    grid_spec = pltpu.PrefetchScalarGridSpec(
        num_scalar_prefetch=4,
        grid=(n_blk,),
        in_specs=[
            pl.BlockSpec((rows * PACKED_CHUNKS, LANES), lambda i, be, nu, *_: (jnp.minimum(i, nu[0] - 1), 0)),
            hbm, hbm, hbm,
        ],
        out_specs=pl.BlockSpec((rows * PACKED_CHUNKS, LANES), lambda i, *_: (i, 0)),
        scratch_shapes=[
            pltpu.VMEM((2, D_MODEL, D_EXPERT), F32),
            pltpu.VMEM((2, D_MODEL, D_EXPERT), F32),
            pltpu.VMEM((2, D_EXPERT, D_MODEL), F32),
            pltpu.SemaphoreType.DMA((2,)),
            pltpu.VMEM((D_MODEL, D_EXPERT), BF16),
            pltpu.VMEM((D_MODEL, D_EXPERT), BF16),
            pltpu.VMEM((D_EXPERT, D_MODEL), BF16),
        ],
    )
    return pl.pallas_call(
        functools.partial(_expert_kernel, rows=rows),
        grid_spec=grid_spec,
        out_shape=jax.ShapeDtypeStruct((n_blk * rows * PACKED_CHUNKS, LANES), U32),
        compiler_params=_params("arbitrary"),
        name="moe_experts",
    )(blk_e, n_used, run_idx, next_e, xs, w1, w3, w2)


def _combine_kernel(slot_ref, ys_hbm, x1_ref, rw_ref, g2_ref, b2_ref, o_ref, yg, sem, *, tm):
    i = pl.program_id(0)
    n = pl.num_programs(0)
    slot = i % 2
    slab = PACKED_CHUNKS
    half = tm * slab
    group = 8

    def start_gather(tile, slot_):
        def body(g, c):
            base = pl.multiple_of(g * (group * slab), group * slab)
            for t in range(group):
                for k in range(2):
                    src_row = slot_ref[k * (n * tm) + tile * tm + g * group + t] * slab
                    pltpu.make_async_copy(
                        ys_hbm.at[pl.ds(pl.multiple_of(src_row, slab), slab), :],
                        yg.at[slot_, pl.ds(k * half + base + t * slab, slab), :],
                        sem.at[slot_]).start(priority=k)
            return c
        lax.fori_loop(0, tm // group, body, 0)

    @pl.when(i == 0)
    def _():
        start_gather(0, 0)

    @pl.when(i + 1 < n)
    def _():
        start_gather(i + 1, 1 - slot)

    pltpu.make_async_copy(ys_hbm.at[pl.ds(0, 2 * half), :], yg.at[slot], sem.at[slot]).wait()
    rw = rw_ref[...]
    buf = yg.at[slot]
    moe = (_from_packed_token_major(buf, 0, tm) * rw[:, 0:1]
           + _from_packed_token_major(buf, tm, tm) * rw[:, 1:2])
    o_ref[...] = _layer_norm_rows(ALPHA * x1_ref[...] + moe, g2_ref[...], b2_ref[...])


def _combine(slot_flat, ys, x1, rw, ln2_g, ln2_b):
    T = x1.shape[0]
    tm = TM_COMB
    grid_spec = pltpu.PrefetchScalarGridSpec(
        num_scalar_prefetch=1,
        grid=(T // tm,),
        in_specs=[
            pl.BlockSpec(memory_space=pl.ANY),
            pl.BlockSpec((tm, D_MODEL), lambda i, *_: (i, 0)),
            pl.BlockSpec((tm, LANES), lambda i, *_: (i, 0)),
            pl.BlockSpec((1, D_MODEL), lambda i, *_: (0, 0)),
            pl.BlockSpec((1, D_MODEL), lambda i, *_: (0, 0)),
        ],
        out_specs=pl.BlockSpec((tm, D_MODEL), lambda i, *_: (i, 0)),
        scratch_shapes=[
            pltpu.VMEM((2, 2 * tm * PACKED_CHUNKS, LANES), U32),
            pltpu.SemaphoreType.DMA((2,)),
        ],
    )
    return pl.pallas_call(
        functools.partial(_combine_kernel, tm=tm),
        grid_spec=grid_spec,
        out_shape=jax.ShapeDtypeStruct((T, D_MODEL), F32),
        compiler_params=_params("arbitrary"),
        name="moe_combine_ln",
    )(slot_flat, ys, x1, rw, ln2_g.reshape(1, D_MODEL).astype(F32), ln2_b.reshape(1, D_MODEL).astype(F32))


def _dispatch_plan(counts, pk_flat, n_tokens):
    rows = MOE_ROWS
    experts = jnp.arange(N_EXPERTS, dtype=jnp.int32)
    padded = (counts + rows - 1) // rows * rows
    pad_end = jnp.cumsum(padded).astype(jnp.int32)
    pad_start = (pad_end - padded).astype(jnp.int32)
    n_blk = 2 * n_tokens // rows + N_EXPERTS
    blk_start = jnp.arange(n_blk, dtype=jnp.int32) * rows
    blk_e = jnp.minimum(jnp.sum(blk_start[:, None] >= pad_end[None, :], axis=1), N_EXPERTS - 1).astype(jnp.int32)
    n_used = (pad_end[-1:] // rows).astype(jnp.int32)
    new_run = jnp.concatenate([jnp.ones((1,), jnp.int32), (blk_e[1:] != blk_e[:-1]).astype(jnp.int32)])
    run_idx = (jnp.cumsum(new_run) - 1).astype(jnp.int32)
    later_active = (experts[None, :] > experts[:, None]) & (counts[None, :] > 0)
    next_active = jnp.min(jnp.where(later_active, experts[None, :], N_EXPERTS), axis=1)
    next_active = jnp.where(next_active == N_EXPERTS, -1, next_active).astype(jnp.int32)
    next_e = jnp.sum(jnp.where(blk_e[:, None] == experts[None, :], next_active[None, :], 0), axis=1)
    e_flat = pk_flat >> RANK_BITS
    start_of = jnp.sum(jnp.where(e_flat[None, :] == experts[:, None], pad_start[:, None], 0), axis=0)
    slot_flat = (start_of + (pk_flat & (RANK_SPAN - 1))).astype(jnp.int32)
    return slot_flat, pad_start, pad_end, blk_e, n_used, run_idx, next_e, n_blk


def kernel(x, w_in, conv_w, conv_b, lru_wa, lru_ba, lru_wx, lru_bx, lru_lambda, attn_norm_g, lru_norm_g,
           w_out, ln1_g, ln1_b, router_grp_w, router_grp_b, router_exp_w, router_exp_b, w1, w3, w2,
           ln2_g, ln2_b):
    B, S, D = x.shape
    assert D == D_MODEL and S % DILATED_PATTERNS[-1][0] == 0 and w_in.shape[0] == 1
    T = B * S
    x2d = x.reshape(T, D)

    qkv, rg = _in_proj(x2d, w_in[0].astype(BF16), B, S)
    attn = _attention(qkv, B, S)
    yrec = _recurrent(rg, conv_w[0], conv_b[0], lru_wa[0], lru_ba[0], lru_wx[0], lru_bx[0], lru_lambda[0],
                      lru_norm_g[0], B, S)

    n_r = N_GROUPS + N_EXPERTS
    w_router = jnp.zeros((D, LANES), F32).at[:, :n_r].set(
        jnp.concatenate([router_grp_w[0], router_exp_w[0]], axis=-1).astype(F32))
    b_router = jnp.zeros((1, LANES), F32).at[0, :n_r].set(
        jnp.concatenate([router_grp_b[0], router_exp_b[0]], axis=-1).astype(F32))
    x1, x1p, pk, rw, cnt = _out_router(attn, yrec, x2d, attn_norm_g[0], w_out[0].astype(BF16),
                                       ln1_g[0], ln1_b[0], w_router, b_router)

    pk_flat = pk[:2].reshape(-1)
    slot_flat, pad_start, pad_end, blk_e, n_used, run_idx, next_e, n_blk = _dispatch_plan(
        cnt[0, :N_EXPERTS], pk_flat, T)
    xs = _dispatch(slot_flat, pad_start, pad_end, n_used, x1p, n_blk)
    ys = _experts(blk_e, n_used, run_idx, next_e, xs, w1[0], w3[0], w2[0])
    out = _combine(slot_flat, ys, x1, rw, ln2_g[0], ln2_b[0])
    return out.reshape(B, S, D)
```
